```python
import math
import jax, jax.numpy as jnp
from jax import lax
import numpy as np

D_MODEL = 1024
BATCH = 32
SEQ = 256
DEPTH = 4
DEC_BATCH = 8
DEC_SEQ = 1024
PAST_LEN = 512

GRID_W = 64
N_EVEN = (DEPTH + 1) // 2
N_ODD = DEPTH // 2
N_MOD = 6
EPS = 1e-6
A_GROUPS = 4
A_GROUP_DIM = D_MODEL // 8
A_WIDTH = A_GROUPS * A_GROUP_DIM
B_HEADS = 4
QK_DIM = D_MODEL // 16
V_DIM = 2 * QK_DIM
B_QK_WIDTH = B_HEADS * 2 * QK_DIM
B_V_WIDTH = B_HEADS * V_DIM
EVEN_IN = A_WIDTH + 2 * B_QK_WIDTH + B_V_WIDTH
EVEN_OUT = A_WIDTH + B_V_WIDTH
Q_BLOCK = 128
ROPE_BASE = 10000.0
CHUNK = 128
C_GROUPS = 8
C_WIDTH = D_MODEL
C_GROUP_DIM = C_WIDTH // C_GROUPS
N_EXPERTS = 32
TOP_K = 4
D_FF = D_MODEL
SWIGLU_LIMIT = 7.0
SWIGLU_ALPHA = 1.702
MOE_BLOCK = 128

kernel_name = 'hybrid_fourier_diffattn_gmlp_moe_denoise_step'


def rms_norm(x, g):
    xf = x.astype(jnp.float32)
    y = xf * lax.rsqrt(jnp.mean(xf * xf, axis=-1, keepdims=True) + EPS)
    return (y * g.astype(jnp.float32)).astype(x.dtype)


def ada_modulation(cond, w, b):
    m = jax.nn.silu(cond) @ w + b
    return [t[:, None, :] for t in jnp.split(m, N_MOD, axis=-1)]


def modulate(h, shift, scale):
    return h * (1 + scale) + shift


def axial_rope_angles(n_tokens):
    n_rows = n_tokens // GRID_W
    rows = jnp.repeat(jnp.arange(n_rows), GRID_W).astype(jnp.float32)
    cols = jnp.tile(jnp.arange(GRID_W), n_rows).astype(jnp.float32)
    n_freq = QK_DIM // 4
    inv = ROPE_BASE ** (-jnp.arange(n_freq, dtype=jnp.float32) / n_freq)
    return rows[:, None] * inv, cols[:, None] * inv


def rotate_half(x, ang):
    x1, x2 = jnp.split(x, 2, axis=-1)
    cos = jnp.cos(ang)[None, :, None, None, :]
    sin = jnp.sin(ang)[None, :, None, None, :]
    return jnp.concatenate([x1 * cos - x2 * sin, x1 * sin + x2 * cos], axis=-1)


def apply_axial_rope(x, ang_r, ang_c):
    xr, xc = jnp.split(x.astype(jnp.float32), 2, axis=-1)
    out = jnp.concatenate([rotate_half(xr, ang_r), rotate_half(xc, ang_c)], axis=-1)
    return out.astype(x.dtype)


def even_projection(h, w_in, qn_g, kn_g):
    b, n, _ = h.shape
    z = h @ w_in
    a, q, k, v = jnp.split(z, [A_WIDTH, A_WIDTH + B_QK_WIDTH, A_WIDTH + 2 * B_QK_WIDTH], axis=-1)
    a = a.reshape(b, n, A_GROUPS, A_GROUP_DIM)
    q = rms_norm(q.reshape(b, n, B_HEADS, 2, QK_DIM), qn_g)
    k = rms_norm(k.reshape(b, n, B_HEADS, 2, QK_DIM), kn_g)
    v = v.reshape(b, n, B_HEADS, V_DIM)
    return a, q, k, v


def fourier_mix(a):
    b, n = a.shape[:2]
    f = jnp.fft.fft2(a.astype(jnp.float32), axes=(1, 3), norm='ortho').real
    return f.astype(a.dtype).reshape(b, n, A_WIDTH)


def diff_lambda(lam_q, lam_k, lambda_init):
    dots = jnp.sum(lam_q.astype(jnp.float32) * lam_k.astype(jnp.float32), axis=-1)
    return jnp.exp(dots[0]) - jnp.exp(dots[1]) + lambda_init


def diff_attention(q, k, v, lam):
    b, nq = q.shape[:2]
    n_blk = nq // Q_BLOCK
    qb = jnp.moveaxis(q.reshape(b, n_blk, Q_BLOCK, B_HEADS, 2, QK_DIM), 1, 0)
    scale = QK_DIM ** -0.5

    def block(qblk):
        s = jnp.einsum('bqhmd,bkhmd->bhmqk', qblk, k).astype(jnp.float32) * scale
        p = jax.nn.softmax(s, axis=-1)
        p = p[:, :, 0] - lam * p[:, :, 1]
        return jnp.einsum('bhqk,bkhd->bqhd', p.astype(v.dtype), v)

    o = lax.map(block, qb)
    return jnp.moveaxis(o, 0, 1).reshape(b, nq, B_HEADS, V_DIM)


def diff_head_output(o, subln_g, lambda_init):
    b, n = o.shape[:2]
    return (rms_norm(o, subln_g) * (1.0 - lambda_init)).astype(o.dtype).reshape(b, n, B_V_WIDTH)


def chunk_mlp_mixer(h, w_in, v_norm_g, w_s, b_s, w_out):
    b, n, _ = h.shape
    z = jax.nn.gelu(h @ w_in, approximate=False)
    u, v = jnp.split(z, 2, axis=-1)
    v = rms_norm(v, v_norm_g).reshape(b, n // CHUNK, CHUNK, C_GROUPS, C_GROUP_DIM)
    sv = jnp.einsum('gpq,bcqgd->bcpgd', w_s, v) + jnp.transpose(b_s)[None, None, :, :, None]
    return (u * sv.reshape(b, n, C_WIDTH)) @ w_out


def moe_ffn(x, router_w, router_b, w_gate, b_gate, w_up, b_up, w_down, b_down):
    b, n, d = x.shape
    n_tok = b * n
    xt = x.reshape(n_tok, d)
    logits = xt @ router_w + router_b
    top_val, top_idx = lax.top_k(logits, TOP_K)
    gates = jax.nn.softmax(top_val.astype(jnp.float32), axis=-1)
    n_assign = n_tok * TOP_K
    flat_e = top_idx.reshape(n_assign)
    order = jnp.argsort(flat_e)
    sorted_e = flat_e[order]
    counts = jnp.bincount(flat_e, length=N_EXPERTS)
    padded = (counts + MOE_BLOCK - 1) // MOE_BLOCK * MOE_BLOCK
    pad_ends = jnp.cumsum(padded)
    pad_starts = pad_ends - padded
    starts = jnp.cumsum(counts) - counts
    slot_sorted = pad_starts[sorted_e] + (jnp.arange(n_assign) - starts[sorted_e])
    slot = jnp.zeros((n_assign,), jnp.int32).at[order].set(slot_sorted.astype(jnp.int32))
    n_blocks = -(-n_assign // MOE_BLOCK) + N_EXPERTS
    n_slots = n_blocks * MOE_BLOCK
    slot_token = jnp.zeros((n_slots,), jnp.int32).at[slot].set(jnp.arange(n_assign, dtype=jnp.int32) // TOP_K)
    block_expert = jnp.minimum(
        jnp.searchsorted(pad_ends, jnp.arange(n_blocks) * MOE_BLOCK, side='right'), N_EXPERTS - 1)
    xs = xt[slot_token].reshape(n_blocks, MOE_BLOCK, d)

    def expert_block(args):
        xb, e = args
        gt = jnp.minimum(xb @ w_gate[e] + b_gate[e], SWIGLU_LIMIT)
        up = jnp.clip(xb @ w_up[e] + b_up[e], -SWIGLU_LIMIT, SWIGLU_LIMIT)
        glu = gt * jax.nn.sigmoid(SWIGLU_ALPHA * gt)
        return ((up + 1) * glu) @ w_down[e] + b_down[e]

    ys = lax.map(expert_block, (xs, block_expert)).reshape(n_slots, d)
    y = jnp.einsum('nkd,nk->nd', ys[slot].reshape(n_tok, TOP_K, d), gates.astype(x.dtype))
    return y.reshape(b, n, d)


def setup_inputs(seed: int = 0) -> dict:
    key = jax.random.key(seed)
    ks = jax.random.split(key, 32)

    def nrm(k, shape, s):
        return s * jax.random.normal(k, shape, jnp.float32)

    def gain(k, shape):
        return 1.0 + nrm(k, shape, 0.02)

    return {
        'x_prompt': nrm(ks[0], (BATCH, SEQ, D_MODEL), 1.0),
        'x_sample': nrm(ks[1], (DEC_BATCH, DEC_SEQ, D_MODEL), 1.0),
        'cache_k': nrm(ks[2], (DEC_BATCH, N_EVEN, PAST_LEN, B_HEADS, 2, QK_DIM), 1.0),
        'cache_v': nrm(ks[3], (DEC_BATCH, N_EVEN, PAST_LEN, B_HEADS, V_DIM), 1.0),
        'c': nrm(ks[4], (DEC_BATCH, D_MODEL), 1.0),
        'c_ctx': nrm(ks[5], (D_MODEL,), 1.0),
        'ada_w': nrm(ks[6], (DEPTH, D_MODEL, N_MOD * D_MODEL), 0.5 * D_MODEL ** -0.5),
        'ada_b': nrm(ks[7], (DEPTH, N_MOD * D_MODEL), 0.02),
        'norm_mix_g': gain(ks[8], (DEPTH, D_MODEL)),
        'norm_ffn_g': gain(ks[9], (DEPTH, D_MODEL)),
        'w_in_even': nrm(ks[10], (N_EVEN, D_MODEL, EVEN_IN), D_MODEL ** -0.5),
        'w_out_even': nrm(ks[11], (N_EVEN, EVEN_OUT, D_MODEL), EVEN_OUT ** -0.5),
        'q_norm_g': gain(ks[12], (N_EVEN, QK_DIM)),
        'k_norm_g': gain(ks[13], (N_EVEN, QK_DIM)),
        'lambda_q': nrm(ks[14], (N_EVEN, 2, QK_DIM), 0.1),
        'lambda_k': nrm(ks[15], (N_EVEN, 2, QK_DIM), 0.1),
        'subln_g': gain(ks[16], (N_EVEN, V_DIM)),
        'w_in_odd': nrm(ks[17], (N_ODD, D_MODEL, 2 * C_WIDTH), D_MODEL ** -0.5),
        'v_norm_g': gain(ks[18], (N_ODD, C_WIDTH)),
        'w_spatial': nrm(ks[19], (N_ODD, C_GROUPS, CHUNK, CHUNK), CHUNK ** -0.5),
        'b_spatial': gain(ks[20], (N_ODD, C_GROUPS, CHUNK)),
        'w_out_odd': nrm(ks[21], (N_ODD, C_WIDTH, D_MODEL), C_WIDTH ** -0.5),
        'router_w': nrm(ks[22], (DEPTH, D_MODEL, N_EXPERTS), D_MODEL ** -0.5),
        'router_b': nrm(ks[23], (DEPTH, N_EXPERTS), 0.01),
        'w_gate': nrm(ks[24], (DEPTH, N_EXPERTS, D_MODEL, D_FF), D_MODEL ** -0.5),
        'b_gate': nrm(ks[25], (DEPTH, N_EXPERTS, D_FF), 0.02),
        'w_up': nrm(ks[26], (DEPTH, N_EXPERTS, D_MODEL, D_FF), D_MODEL ** -0.5),
        'b_up': nrm(ks[27], (DEPTH, N_EXPERTS, D_FF), 0.02),
        'w_down': nrm(ks[28], (DEPTH, N_EXPERTS, D_FF, D_MODEL), D_FF ** -0.5),
        'b_down': nrm(ks[29], (DEPTH, N_EXPERTS, D_MODEL), 0.02),
    }


def reference(x_prompt, x_sample, cache_k, cache_v, c, c_ctx, ada_w, ada_b, norm_mix_g, norm_ffn_g,
              w_in_even, w_out_even, q_norm_g, k_norm_g, lambda_q, lambda_k, subln_g,
              w_in_odd, v_norm_g, w_spatial, b_spatial, w_out_odd,
              router_w, router_b, w_gate, b_gate, w_up, b_up, w_down, b_down):
    ang_r, ang_c = axial_rope_angles(x_sample.shape[1])
    xc, xl = x_prompt, x_sample
    new_k, new_v = [], []
    for l in range(DEPTH):
        j = l // 2
        mc = ada_modulation(c_ctx[None, :], ada_w[l], ada_b[l])
        ml = ada_modulation(c, ada_w[l], ada_b[l])
        hc = modulate(rms_norm(xc, norm_mix_g[l]), mc[0], mc[1])
        hl = modulate(rms_norm(xl, norm_mix_g[l]), ml[0], ml[1])
        if l % 2 == 0:
            lambda_init = 0.8 - 0.6 * math.exp(-0.3 * l)
            lam = diff_lambda(lambda_q[j], lambda_k[j], lambda_init)
            a, q, k, v = even_projection(hc, w_in_even[j], q_norm_g[j], k_norm_g[j])
            o = diff_attention(q, k, v, lam)
            out_c = jnp.concatenate([fourier_mix(a), diff_head_output(o, subln_g[j], lambda_init)],
                                    axis=-1) @ w_out_even[j]
            new_k.append(k)
            new_v.append(v)
            a, q, k, v = even_projection(hl, w_in_even[j], q_norm_g[j], k_norm_g[j])
            q = apply_axial_rope(q, ang_r, ang_c)
            k = apply_axial_rope(k, ang_r, ang_c)
            k_all = jnp.concatenate([k, cache_k[:, j].astype(k.dtype)], axis=1)
            v_all = jnp.concatenate([v, cache_v[:, j].astype(v.dtype)], axis=1)
            o = diff_attention(q, k_all, v_all, lam)
            out_l = jnp.concatenate([fourier_mix(a), diff_head_output(o, subln_g[j], lambda_init)],
                                    axis=-1) @ w_out_even[j]
        else:
            out_c = chunk_mlp_mixer(hc, w_in_odd[j], v_norm_g[j], w_spatial[j], b_spatial[j], w_out_odd[j])
            out_l = chunk_mlp_mixer(hl, w_in_odd[j], v_norm_g[j], w_spatial[j], b_spatial[j], w_out_odd[j])
        xc = xc + mc[2] * out_c
        xl = xl + ml[2] * out_l
        hc = modulate(rms_norm(xc, norm_ffn_g[l]), mc[3], mc[4])
        hl = modulate(rms_norm(xl, norm_ffn_g[l]), ml[3], ml[4])
        xc = xc + mc[5] * moe_ffn(hc, router_w[l], router_b[l], w_gate[l], b_gate[l],
                                  w_up[l], b_up[l], w_down[l], b_down[l])
        xl = xl + ml[5] * moe_ffn(hl, router_w[l], router_b[l], w_gate[l], b_gate[l],
                                  w_up[l], b_up[l], w_down[l], b_down[l])
    new_cache_k = jnp.stack(new_k, axis=1)
    new_cache_v = jnp.stack(new_v, axis=1)
    return (xc, xl, new_cache_k, new_cache_v)
```

```python
import functools
import math

import numpy as np
import jax
import jax.numpy as jnp
from jax import lax
from jax.experimental import pallas as pl
from jax.experimental.pallas import tpu as pltpu

F32 = jnp.float32
BF16 = jnp.bfloat16

D_MODEL = 1024
DEPTH = 4
N_MOD = 6
EPS = 1e-6
CTX_BATCH, CTX_SEQ = 32, 256
LAT_BATCH, LAT_SEQ = 8, 1024
PAST_LEN = 512
GRID_W = 64
N_CTX = CTX_BATCH * CTX_SEQ
N_LAT = LAT_BATCH * LAT_SEQ
N_TOK = N_CTX + N_LAT
TM = 256
N_TILES = N_TOK // TM
CTX_TILES = N_CTX // TM
LAT_TILES_PER_BATCH = LAT_SEQ // TM
COND_ROWS = 16
GROUP = 128
N_GROUPS = 4
QK_DIM = 64
HALF = 512
CHUNK = 128
C_GROUPS = 8
N_EXPERTS = 32
TOP_K = 4
ROUTER_PAD = 128
SWIGLU_LIMIT = 7.0
SWIGLU_ALPHA = 1.702
MOE_BLOCK = 256
N_ASSIGN = N_TOK * TOP_K
MOE_BLOCKS = N_ASSIGN // MOE_BLOCK + N_EXPERTS
N_SLOTS = MOE_BLOCKS * MOE_BLOCK
ROPE_BASE = 10000.0
VMEM_LIMIT = 52 * 1024 * 1024


def _dot(a, b):
    return jnp.dot(a, b, preferred_element_type=F32)


def _rms(x, g):
    return x * lax.rsqrt(jnp.mean(x * x, axis=-1, keepdims=True) + EPS) * g


def _split_bf16(x):
    hi = x.astype(BF16)
    lo = (x - hi.astype(F32)).astype(BF16)
    return hi, lo


def _tile_mod_row(t):
    return jnp.where(t < CTX_TILES, 0, 1 + (t - CTX_TILES) // LAT_TILES_PER_BATCH)


def _mod_spec(layer, k, row_fn):
    return pl.BlockSpec((None, None, None, 1, D_MODEL),
                        lambda *g: (layer, k, row_fn(*g), 0, 0))


def _full_spec(shape):
    return pl.BlockSpec(shape, lambda *g: (0,) * len(shape))


def _ada_kernel(cond_ref, w_ref, b_ref, o_ref):
    c = cond_ref[...]
    s = (c * jax.nn.sigmoid(c)).astype(BF16)
    o_ref[...] = _dot(s, w_ref[...].astype(BF16)) + b_ref[...]


def _ada_modulation(cond, ada_w, ada_b):
    return pl.pallas_call(
        _ada_kernel,
        grid=(DEPTH, N_MOD),
        in_specs=[
            _full_spec((COND_ROWS, D_MODEL)),
            pl.BlockSpec((None, D_MODEL, D_MODEL), lambda l, n: (l, 0, n)),
            pl.BlockSpec((None, None, 1, D_MODEL), lambda l, n: (l, n, 0, 0)),
        ],
        out_specs=pl.BlockSpec((None, None, COND_ROWS, D_MODEL), lambda l, n: (l, n, 0, 0)),
        out_shape=jax.ShapeDtypeStruct((DEPTH, N_MOD, COND_ROWS, D_MODEL), F32),
        compiler_params=pltpu.CompilerParams(dimension_semantics=("arbitrary", "arbitrary"),
                                             vmem_limit_bytes=VMEM_LIMIT),
        name="ada_modulation",
    )(cond, ada_w, ada_b.reshape(DEPTH, N_MOD, 1, D_MODEL))


def _residual_in(has_prev, x_ref, y_ref, gate_ref, xo_ref):
    x = x_ref[...]
    if has_prev:
        x = x + gate_ref[...] * y_ref[...].astype(F32)
        xo_ref[...] = x
    return x


def _post_mixer(x, mix, gate_ref, gffn_ref, shift_ref, scale_ref, rwhi_ref, rwlo_ref, rb_ref,
                xo_ref, h_ref, lg_ref):
    xn = x + gate_ref[...] * mix
    xo_ref[...] = xn
    h2 = _rms(xn, gffn_ref[...]) * (1.0 + scale_ref[...]) + shift_ref[...]
    hi, lo = _split_bf16(h2)
    h_ref[...] = hi
    rwhi = rwhi_ref[...]
    lg_ref[...] = _dot(hi, rwhi) + _dot(lo, rwhi) + _dot(hi, rwlo_ref[...]) + rb_ref[...]


def _rope(x, c, sa, sb):
    outs = []
    for h in range(N_GROUPS):
        xs = x[:, GROUP * h:GROUP * (h + 1)]
        outs.append(xs * c + pltpu.roll(xs, GROUP - 16, 1) * sa + pltpu.roll(xs, 16, 1) * sb)
    return jnp.concatenate(outs, axis=1)


def _even_proj_kernel(has_prev, *refs):
    if has_prev:
        x_ref, y_ref, gprev_ref = refs[:3]
        refs = refs[3:]
    else:
        x_ref, y_ref, gprev_ref = refs[0], None, None
        refs = refs[1:]
    (g_ref, shift_ref, scale_ref, w_ref, bd_ref, qg_ref, kg_ref, cos_ref, sa_ref, sb_ref) = refs[:10]
    outs = refs[10:]
    if has_prev:
        xo_ref = outs[0]
        outs = outs[1:]
    else:
        xo_ref = None
    a_ref, q_ref, k_ref, v_ref, kf_ref, vf_ref = outs

    t = pl.program_id(0)
    x = _residual_in(has_prev, x_ref, y_ref, gprev_ref, xo_ref)
    h = (_rms(x, g_ref[...]) * (1.0 + scale_ref[...]) + shift_ref[...]).astype(BF16)

    a_ref[...] = _dot(h, w_ref[:, 0:HALF]).astype(BF16)
    zv = _dot(h, w_ref[:, 3 * HALF:4 * HALF])
    vf_ref[...] = zv
    v_ref[...] = zv.astype(BF16)

    bd = bd_ref[...]

    def qk_norm(z, gain):
        shi, slo = _split_bf16(z * z)
        ssq = _dot(shi, bd) + _dot(slo, bd)
        return z * lax.rsqrt(ssq * (1.0 / QK_DIM) + EPS) * gain

    qn = qk_norm(_dot(h, w_ref[:, HALF:2 * HALF]), qg_ref[...]) * (QK_DIM ** -0.5)
    kn = qk_norm(_dot(h, w_ref[:, 2 * HALF:3 * HALF]), kg_ref[...])
    kf_ref[...] = kn

    @pl.when(t < CTX_TILES)
    def _():
        q_ref[...] = qn.astype(BF16)
        k_ref[...] = kn.astype(BF16)

    @pl.when(t >= CTX_TILES)
    def _():
        c, sa, sb = cos_ref[...], sa_ref[...], sb_ref[...]
        q_ref[...] = _rope(qn, c, sa, sb).astype(BF16)
        k_ref[...] = _rope(kn, c, sa, sb).astype(BF16)


def _even_proj(layer, x, y_prev, mods, norm_g, w_in, blockdiag, qg, kg, rope_tabs):
    has_prev = y_prev is not None
    tile = lambda w: pl.BlockSpec((TM, w), lambda t: (t, 0))
    rope_spec = pl.BlockSpec(
        (TM, GROUP), lambda t: (jnp.where(t < CTX_TILES, 0, (t - CTX_TILES) % LAT_TILES_PER_BATCH), 0))
    in_specs = [tile(D_MODEL)]
    args = [x]
    if has_prev:
        in_specs += [tile(D_MODEL), _mod_spec(layer - 1, 5, _tile_mod_row)]
        args += [y_prev, mods]
    in_specs += [
        _full_spec((1, D_MODEL)),
        _mod_spec(layer, 0, _tile_mod_row), _mod_spec(layer, 1, _tile_mod_row),
        _full_spec((D_MODEL, 4 * HALF)), _full_spec((HALF, HALF)),
        _full_spec((1, HALF)), _full_spec((1, HALF)),
        rope_spec, rope_spec, rope_spec,
    ]
    args += [norm_g, mods, mods, w_in, blockdiag, qg, kg, *rope_tabs]
    out_specs = [tile(HALF)] * 6
    out_shape = [jax.ShapeDtypeStruct((N_TOK, HALF), BF16)] * 4 + \
                [jax.ShapeDtypeStruct((N_TOK, HALF), F32)] * 2
    if has_prev:
        out_specs = [tile(D_MODEL)] + out_specs
        out_shape = [jax.ShapeDtypeStruct((N_TOK, D_MODEL), F32)] + out_shape
    res = pl.pallas_call(
        functools.partial(_even_proj_kernel, has_prev),
        grid=(N_TILES,),
        in_specs=in_specs, out_specs=out_specs, out_shape=out_shape,
        compiler_params=pltpu.CompilerParams(dimension_semantics=("arbitrary",),
                                             vmem_limit_bytes=VMEM_LIMIT),
        name="even_proj",
    )(*args)
    if has_prev:
        return res[0], res[1:]
    return x, res


def _even_mix_kernel(has_cache, lambda_init, *refs):
    (x_ref, a_ref, q_ref, k_ref, v_ref) = refs[:5]
    refs = refs[5:]
    if has_cache:
        ck_ref, cv_ref = refs[:2]
        refs = refs[2:]
    (cn_ref, sn_ref, cs_ref, lq_ref, lk_ref, sg_ref, wo_ref,
     gate_ref, gffn_ref, shift_ref, scale_ref, rwhi_ref, rwlo_ref, rb_ref,
     xo_ref, h_ref, lg_ref) = refs

    a = a_ref[...]
    y1 = _dot(cn_ref[...], a).astype(BF16)
    y2 = _dot(sn_ref[...], a).astype(BF16)
    cs = cs_ref[...]
    pieces = []
    for g in range(N_GROUPS):
        sl = slice(GROUP * g, GROUP * (g + 1))
        pieces.append(_dot(jnp.concatenate([y1[:, sl], y2[:, sl]], axis=1), cs).astype(BF16))

    d = jnp.sum(lq_ref[...] * lk_ref[...], axis=1, keepdims=True)
    ed = jnp.exp(d)
    lam = ed[0:1, :] - ed[1:2, :] + lambda_init
    lane = lax.broadcasted_iota(jnp.int32, (TM, GROUP), 1)
    nt = (((1,), (1,)), ((), ()))
    for hd in range(N_GROUPS):
        sl = slice(GROUP * hd, GROUP * (hd + 1))
        qf = q_ref[:, sl].astype(F32)
        parts = [(k_ref[:, sl], v_ref[:, sl])]
        if has_cache:
            parts.append((ck_ref[:, sl].astype(BF16), cv_ref[:, sl].astype(BF16)))

        def probs(qm):
            ss = [lax.dot_general(qm, kk, nt, preferred_element_type=F32) for kk, _ in parts]
            m = functools.reduce(jnp.maximum, [jnp.max(s, axis=1, keepdims=True) for s in ss])
            es = [jnp.exp(s - m) for s in ss]
            den = functools.reduce(lambda u, w: u + w, [jnp.sum(e, axis=1, keepdims=True) for e in es])
            return es, 1.0 / den

        es0, inv0 = probs(jnp.where(lane < QK_DIM, qf, 0.0).astype(BF16))
        es1, inv1 = probs(jnp.where(lane >= QK_DIM, qf, 0.0).astype(BF16))
        linv1 = lam * inv1
        o = None
        for (e0, e1, (_, vv)) in zip(es0, es1, parts):
            po = _dot((e0 * inv0 - e1 * linv1).astype(BF16), vv)
            o = po if o is None else o + po
        pieces.append((_rms(o, sg_ref[...]) * (1.0 - lambda_init)).astype(BF16))

    mix = _dot(jnp.concatenate(pieces, axis=1), wo_ref[...])
    _post_mixer(x_ref[...], mix, gate_ref, gffn_ref, shift_ref, scale_ref, rwhi_ref, rwlo_ref, rb_ref,
                xo_ref, h_ref, lg_ref)


def _even_mix(layer, is_latent, x, a, q, k, v, cache_k, cache_v, dft, cs128, lam_q, lam_k, subln_g,
              w_out, mods, gffn, rw_hi, rw_lo, rb):
    j = layer // 2
    lambda_init = 0.8 - 0.6 * math.exp(-0.3 * layer)
    if is_latent:
        nb, seq, tile0, mod_row = LAT_BATCH, LAT_SEQ, CTX_TILES, (lambda b, qi: 1 + b)
    else:
        nb, seq, tile0, mod_row = CTX_BATCH, CTX_SEQ, 0, (lambda b, qi: 0)
    nq = seq // TM
    seq0 = (N_CTX // seq) if is_latent else 0
    qtile = lambda w: pl.BlockSpec((TM, w), lambda b, qi: (tile0 + b * nq + qi, 0))
    seqblk = pl.BlockSpec((seq, HALF), lambda b, qi: (seq0 + b, 0))
    in_specs = [qtile(D_MODEL), seqblk, qtile(HALF), seqblk, seqblk]
    args = [x, a, q, k, v]
    if is_latent:
        cspec = pl.BlockSpec((None, None, PAST_LEN, HALF), lambda b, qi: (b, j, 0, 0))
        in_specs += [cspec, cspec]
        args += [cache_k, cache_v]
    dft_spec = pl.BlockSpec((TM, seq), lambda b, qi: (qi, 0))
    in_specs += [
        dft_spec, dft_spec, _full_spec((2 * GROUP, GROUP)),
        _full_spec((2, QK_DIM)), _full_spec((2, QK_DIM)), _full_spec((1, GROUP)),
        _full_spec((D_MODEL, D_MODEL)),
        _mod_spec(layer, 2, mod_row), _full_spec((1, D_MODEL)),
        _mod_spec(layer, 3, mod_row), _mod_spec(layer, 4, mod_row),
        _full_spec((D_MODEL, ROUTER_PAD)), _full_spec((D_MODEL, ROUTER_PAD)), _full_spec((1, ROUTER_PAD)),
    ]
    args += [dft[0], dft[1], cs128, lam_q, lam_k, subln_g, w_out, mods, gffn, mods, mods, rw_hi, rw_lo, rb]
    n_rows = nb * seq
    otile = lambda w: pl.BlockSpec((TM, w), lambda b, qi: (b * nq + qi, 0))
    return pl.pallas_call(
        functools.partial(_even_mix_kernel, is_latent, lambda_init),
        grid=(nb, nq),
        in_specs=in_specs,
        out_specs=[otile(D_MODEL), otile(D_MODEL), otile(ROUTER_PAD)],
        out_shape=[jax.ShapeDtypeStruct((n_rows, D_MODEL), F32),
                   jax.ShapeDtypeStruct((n_rows, D_MODEL), BF16),
                   jax.ShapeDtypeStruct((n_rows, ROUTER_PAD), F32)],
        compiler_params=pltpu.CompilerParams(dimension_semantics=("arbitrary", "arbitrary"),
                                             vmem_limit_bytes=VMEM_LIMIT),
        name="even_mix_latent" if is_latent else "even_mix_ctx",
    )(*args)


def _odd_kernel(has_prev, *refs):
    if has_prev:
        x_ref, y_ref, gprev_ref = refs[:3]
        refs = refs[3:]
    else:
        x_ref, y_ref, gprev_ref = refs[0], None, None
        refs = refs[1:]
    (g_ref, shift_ref, scale_ref, w_ref, vg_ref, ws_ref, bs_ref, wo_ref,
     gate_ref, gffn_ref, shift2_ref, scale2_ref, rwhi_ref, rwlo_ref, rb_ref,
     xo_ref, h_ref, lg_ref) = refs

    x = x_ref[...]
    if has_prev:
        x = x + gprev_ref[...] * y_ref[...].astype(F32)
    h = (_rms(x, g_ref[...]) * (1.0 + scale_ref[...]) + shift_ref[...]).astype(BF16)

    def gelu(z):
        return 0.5 * z * (1.0 + lax.erf(z * (2.0 ** -0.5)))

    u = gelu(_dot(h, w_ref[:, 0:D_MODEL]))
    vn = _rms(gelu(_dot(h, w_ref[:, D_MODEL:2 * D_MODEL])), vg_ref[...]).astype(BF16)
    bs = bs_ref[...]
    rows = []
    for c in range(TM // CHUNK):
        cols = []
        for g in range(C_GROUPS):
            cols.append(_dot(ws_ref[g], vn[CHUNK * c:CHUNK * (c + 1), GROUP * g:GROUP * (g + 1)]))
        rows.append(jnp.concatenate(cols, axis=1) + bs)
    sv = jnp.concatenate(rows, axis=0)
    mix = _dot((u * sv).astype(BF16), wo_ref[...])
    _post_mixer(x, mix, gate_ref, gffn_ref, shift2_ref, scale2_ref, rwhi_ref, rwlo_ref, rb_ref,
                xo_ref, h_ref, lg_ref)


def _odd_layer(layer, x, y_prev, mods, norm_g, w_in, v_norm_g, w_s, b_s, w_out, gffn, rw_hi, rw_lo, rb):
    has_prev = y_prev is not None
    tile = lambda w: pl.BlockSpec((TM, w), lambda t: (t, 0))
    in_specs = [tile(D_MODEL)]
    args = [x]
    if has_prev:
        in_specs += [tile(D_MODEL), _mod_spec(layer - 1, 5, _tile_mod_row)]
        args += [y_prev, mods]
    in_specs += [
        _full_spec((1, D_MODEL)),
        _mod_spec(layer, 0, _tile_mod_row), _mod_spec(layer, 1, _tile_mod_row),
        _full_spec((D_MODEL, 2 * D_MODEL)), _full_spec((1, D_MODEL)),
        _full_spec((C_GROUPS, CHUNK, CHUNK)), _full_spec((CHUNK, D_MODEL)),
        _full_spec((D_MODEL, D_MODEL)),
        _mod_spec(layer, 2, _tile_mod_row), _full_spec((1, D_MODEL)),
        _mod_spec(layer, 3, _tile_mod_row), _mod_spec(layer, 4, _tile_mod_row),
        _full_spec((D_MODEL, ROUTER_PAD)), _full_spec((D_MODEL, ROUTER_PAD)), _full_spec((1, ROUTER_PAD)),
    ]
    args += [norm_g, mods, mods, w_in, v_norm_g, w_s, b_s, w_out, mods, gffn, mods, mods, rw_hi, rw_lo, rb]
    return pl.pallas_call(
        functools.partial(_odd_kernel, has_prev),
        grid=(N_TILES,),
        in_specs=in_specs,
        out_specs=[tile(D_MODEL), tile(D_MODEL), tile(ROUTER_PAD)],
        out_shape=[jax.ShapeDtypeStruct((N_TOK, D_MODEL), F32),
                   jax.ShapeDtypeStruct((N_TOK, D_MODEL), BF16),
                   jax.ShapeDtypeStruct((N_TOK, ROUTER_PAD), F32)],
        compiler_params=pltpu.CompilerParams(dimension_semantics=("arbitrary",),
                                             vmem_limit_bytes=VMEM_LIMIT),
        name="odd_layer",
    )(*args)


def _moe_kernel(be_ref, nused_ref, xs_ref, wg_ref, bg_ref, wu_ref, bu_ref, wd_ref, bd_ref, o_ref,
                wg_s, wu_s, wd_s):
    i = pl.program_id(0)
    prev = be_ref[jnp.maximum(i - 1, 0)]
    active = i < nused_ref[0]

    @pl.when(active & ((i == 0) | (be_ref[i] != prev)))
    def _():
        wg_s[...] = wg_ref[...].astype(BF16)
        wu_s[...] = wu_ref[...].astype(BF16)
        wd_s[...] = wd_ref[...].astype(BF16)

    @pl.when(active)
    def _():
        xb = xs_ref[...]
        gt = jnp.minimum(_dot(xb, wg_s[...]) + bg_ref[...], SWIGLU_LIMIT)
        up = jnp.clip(_dot(xb, wu_s[...]) + bu_ref[...], -SWIGLU_LIMIT, SWIGLU_LIMIT)
        glu = gt * jax.nn.sigmoid(SWIGLU_ALPHA * gt)
        hmid = ((up + 1.0) * glu).astype(BF16)
        o_ref[...] = (_dot(hmid, wd_s[...]) + bd_ref[...]).astype(BF16)

    @pl.when(jnp.logical_not(active))
    def _():
        o_ref[...] = jnp.zeros_like(o_ref)


def _moe_experts(layer, xs, block_expert, n_used, w_gate, b_gate, w_up, b_up, w_down, b_down):
    wspec = pl.BlockSpec((None, None, D_MODEL, D_MODEL), lambda i, be, nu: (layer, be[i], 0, 0))
    bspec = pl.BlockSpec((None, None, 1, D_MODEL), lambda i, be, nu: (layer, be[i], 0, 0))
    rows = pl.BlockSpec((MOE_BLOCK, D_MODEL), lambda i, be, nu: (i, 0))
    bias = lambda b: b.reshape(DEPTH, N_EXPERTS, 1, D_MODEL)
    return pl.pallas_call(
        _moe_kernel,
        grid_spec=pltpu.PrefetchScalarGridSpec(
            num_scalar_prefetch=2,
            grid=(MOE_BLOCKS,),
            in_specs=[rows, wspec, bspec, wspec, bspec, wspec, bspec],
            out_specs=rows,
            scratch_shapes=[pltpu.VMEM((D_MODEL, D_MODEL), BF16)] * 3,
        ),
        out_shape=jax.ShapeDtypeStruct((N_SLOTS, D_MODEL), BF16),
        compiler_params=pltpu.CompilerParams(dimension_semantics=("arbitrary",),
                                             vmem_limit_bytes=VMEM_LIMIT),
        name="moe_experts",
    )(block_expert, n_used, xs, w_gate, bias(b_gate), w_up, bias(b_up), w_down, bias(b_down))


def _route(logits):
    top_val, top_idx = lax.top_k(logits, TOP_K)
    gates = jax.nn.softmax(top_val, axis=-1)
    flat_e = top_idx.reshape(N_ASSIGN)
    onehot = (flat_e[:, None] == jnp.arange(N_EXPERTS, dtype=jnp.int32)[None, :]).astype(jnp.int32)
    csum = jnp.cumsum(onehot, axis=0)
    counts = csum[-1]
    rank = jnp.sum(csum * onehot, axis=1) - 1
    padded = (counts + MOE_BLOCK - 1) // MOE_BLOCK * MOE_BLOCK
    pad_ends = jnp.cumsum(padded)
    pad_starts = pad_ends - padded
    slot = (pad_starts[flat_e] + rank).astype(jnp.int32)
    slot_token = jnp.zeros((N_SLOTS,), jnp.int32).at[slot].set(
        jnp.arange(N_ASSIGN, dtype=jnp.int32) // TOP_K)
    block_expert = jnp.minimum(
        jnp.searchsorted(pad_ends, jnp.arange(MOE_BLOCKS, dtype=jnp.int32) * MOE_BLOCK, side='right'),
        N_EXPERTS - 1).astype(jnp.int32)
    n_used = (pad_ends[-1:] // MOE_BLOCK).astype(jnp.int32)
    return gates, slot, slot_token, block_expert, n_used


def _moe(layer, h2, logits, w_gate, b_gate, w_up, b_up, w_down, b_down):
    gates, slot, slot_token, block_expert, n_used = _route(logits[:, :N_EXPERTS])
    xs = h2[slot_token]
    ys = _moe_experts(layer, xs, block_expert, n_used, w_gate, b_gate, w_up, b_up, w_down, b_down)
    yk = ys[slot].reshape(N_TOK, TOP_K, D_MODEL).astype(F32)
    return jnp.sum(yk * gates[:, :, None], axis=1).astype(BF16)


def _final_kernel(x_ref, y_ref, gate_ref, o_ref):
    o_ref[...] = x_ref[...] + gate_ref[...] * y_ref[...].astype(F32)


def _final_residual(x, y, mods):
    tile = pl.BlockSpec((TM, D_MODEL), lambda t: (t, 0))
    return pl.pallas_call(
        _final_kernel,
        grid=(N_TILES,),
        in_specs=[tile, tile, _mod_spec(DEPTH - 1, 5, _tile_mod_row)],
        out_specs=tile,
        out_shape=jax.ShapeDtypeStruct((N_TOK, D_MODEL), F32),
        compiler_params=pltpu.CompilerParams(dimension_semantics=("arbitrary",)),
        name="final_residual",
    )(x, y, mods)


def _dft_pair(n):
    idx = np.arange(n)
    ang = 2.0 * np.pi * ((idx[:, None] * idx[None, :]) % n) / n
    return np.cos(ang) / np.sqrt(n), np.sin(ang) / np.sqrt(n)


def _rope_tables():
    lane = np.arange(GROUP)
    within = lane % QK_DIM
    axis = within // 32
    e = within % 32
    inv16 = ROPE_BASE ** (-jnp.arange(16, dtype=F32) / 16)
    pos = np.arange(LAT_SEQ)
    coord = np.where(axis[None, :] == 0, (pos // GRID_W)[:, None], (pos % GRID_W)[:, None])
    ang = jnp.asarray(coord, F32) * inv16[e % 16][None, :]
    first = jnp.asarray((e // 16) == 0)[None, :]
    cos, sin = jnp.cos(ang), jnp.sin(ang)
    return cos, jnp.where(first, -sin, 0.0), jnp.where(first, 0.0, sin)


def kernel(x_prompt, x_sample, cache_k, cache_v, c, c_ctx, ada_w, ada_b, norm_mix_g, norm_ffn_g,
           w_in_even, w_out_even, q_norm_g, k_norm_g, lambda_q, lambda_k, subln_g,
           w_in_odd, v_norm_g, w_spatial, b_spatial, w_out_odd,
           router_w, router_b, w_gate, b_gate, w_up, b_up, w_down, b_down):
    n_even = w_in_even.shape[0]
    x = jnp.concatenate([x_prompt.reshape(N_CTX, D_MODEL), x_sample.reshape(N_LAT, D_MODEL)], axis=0)
    cond = jnp.zeros((COND_ROWS, D_MODEL), F32).at[0].set(c_ctx).at[1:1 + LAT_BATCH].set(c)
    mods = _ada_modulation(cond, ada_w, ada_b).reshape(DEPTH, N_MOD, COND_ROWS, 1, D_MODEL)

    dft_ctx = [jnp.asarray(m, F32).astype(BF16) for m in _dft_pair(CTX_SEQ)]
    dft_lat = [jnp.asarray(m, F32).astype(BF16) for m in _dft_pair(LAT_SEQ)]
    c128, s128 = _dft_pair(GROUP)
    cs128 = jnp.asarray(np.concatenate([c128, -s128], axis=0), F32).astype(BF16)
    grp = np.arange(HALF) // QK_DIM
    blockdiag = jnp.asarray(grp[:, None] == grp[None, :], F32).astype(BF16)
    rope_tabs = _rope_tables()
    cache_k2 = cache_k.reshape(LAT_BATCH, n_even, PAST_LEN, HALF)
    cache_v2 = cache_v.reshape(LAT_BATCH, n_even, PAST_LEN, HALF)

    rw = jnp.pad(router_w, ((0, 0), (0, 0), (0, ROUTER_PAD - N_EXPERTS)))
    rw_hi = rw.astype(BF16)
    rw_lo = (rw - rw_hi.astype(F32)).astype(BF16)
    rb = jnp.pad(router_b, ((0, 0), (0, ROUTER_PAD - N_EXPERTS)))[:, None, :]

    y = None
    new_k, new_v = [], []
    for l in range(DEPTH):
        j = l // 2
        gmix = norm_mix_g[l][None, :]
        gffn = norm_ffn_g[l][None, :]
        if l % 2 == 0:
            x, (a, q, k, v, kf, vf) = _even_proj(
                l, x, y, mods, gmix, w_in_even[j].astype(BF16), blockdiag,
                jnp.tile(q_norm_g[j], HALF // QK_DIM)[None, :], jnp.tile(k_norm_g[j], HALF // QK_DIM)[None, :],
                rope_tabs)
            new_k.append(kf[:N_CTX].reshape(CTX_BATCH, CTX_SEQ, N_GROUPS, 2, QK_DIM))
            new_v.append(vf[:N_CTX].reshape(CTX_BATCH, CTX_SEQ, N_GROUPS, GROUP))
            common = (cs128, lambda_q[j], lambda_k[j], subln_g[j][None, :], w_out_even[j].astype(BF16),
                      mods, gffn, rw_hi[l], rw_lo[l], rb[l])
            oc = _even_mix(l, False, x, a, q, k, v, None, None, dft_ctx, *common)
            ol = _even_mix(l, True, x, a, q, k, v, cache_k2, cache_v2, dft_lat, *common)
            x, h2, logits = [jnp.concatenate([u, w], axis=0) for u, w in zip(oc, ol)]
        else:
            b_s = jnp.broadcast_to(jnp.transpose(b_spatial[j])[:, :, None],
                                   (CHUNK, C_GROUPS, GROUP)).reshape(CHUNK, D_MODEL)
            x, h2, logits = _odd_layer(
                l, x, y, mods, gmix, w_in_odd[j].astype(BF16), v_norm_g[j][None, :],
                w_spatial[j].astype(BF16), b_s, w_out_odd[j].astype(BF16), gffn, rw_hi[l], rw_lo[l], rb[l])
        y = _moe(l, h2, logits, w_gate, b_gate, w_up, b_up, w_down, b_down)
    x = _final_residual(x, y, mods)
    return (x[:N_CTX].reshape(CTX_BATCH, CTX_SEQ, D_MODEL),
            x[N_CTX:].reshape(LAT_BATCH, LAT_SEQ, D_MODEL),
            jnp.stack(new_k, axis=1),
            jnp.stack(new_v, axis=1))
```

```python
import functools
import math

import numpy as np
import jax
import jax.numpy as jnp
from jax import lax
from jax.experimental import pallas as pl
from jax.experimental.pallas import tpu as pltpu
from jax.experimental.pallas import tpu_sc as plsc

F32 = jnp.float32
BF16 = jnp.bfloat16
U32 = jnp.uint32
I32 = jnp.int32

D_MODEL = 1024
DEPTH = 4
N_MOD = 6
EPS = 1e-6
CTX_BATCH, CTX_SEQ = 32, 256
LAT_BATCH, LAT_SEQ = 8, 1024
PAST_LEN = 512
GRID_W = 64
N_CTX = CTX_BATCH * CTX_SEQ
N_LAT = LAT_BATCH * LAT_SEQ
N_TOK = N_CTX + N_LAT
TM = 256
N_TILES = N_TOK // TM
CTX_TILES = N_CTX // TM
LAT_TILES_PER_BATCH = LAT_SEQ // TM
COND_ROWS = 16
GROUP = 128
N_GROUPS = 4
QK_DIM = 64
HALF = 512
CHUNK = 128
C_GROUPS = 8
N_EXPERTS = 32
TOP_K = 4
LANES = 128
SUBLANES = 8
SWIGLU_LIMIT = 7.0
SWIGLU_ALPHA = 1.702
MOE_BLOCK = 256
N_ASSIGN = N_TOK * TOP_K
MOE_BLOCKS = N_ASSIGN // MOE_BLOCK + N_EXPERTS
N_SLOTS = MOE_BLOCKS * MOE_BLOCK
ROPE_BASE = 10000.0
VMEM_LIMIT = 52 * 1024 * 1024
SC_CORES = 2
SC_SUBCORES = 16
SC_WORKERS = SC_CORES * SC_SUBCORES
SC_ROWS = 64
HI_MASK = 0xFFFF0000


def _dot(a, b):
    return jnp.dot(a, b, preferred_element_type=F32)


def _rms(x, g):
    return x * lax.rsqrt(jnp.mean(x * x, axis=-1, keepdims=True) + EPS) * g


def _split_bf16(x):
    hi = x.astype(BF16)
    lo = (x - hi.astype(F32)).astype(BF16)
    return hi, lo


def _pack_halves(xf):
    b = lax.bitcast_convert_type(xf, U32)
    return (b[:, :HALF] >> 16) | (b[:, HALF:] & jnp.uint32(HI_MASK))


def _unpack_halves(w):
    lo = lax.bitcast_convert_type(w << 16, F32)
    hi = lax.bitcast_convert_type(w & jnp.uint32(HI_MASK), F32)
    return lo, hi


def _tile_mod_row(t):
    return jnp.where(t < CTX_TILES, 0, 1 + (t - CTX_TILES) // LAT_TILES_PER_BATCH)


def _mod_spec(layer, k, row_fn):
    return pl.BlockSpec((None, None, None, 1, D_MODEL),
                        lambda *g: (layer, k, row_fn(*g), 0, 0))


def _full_spec(shape):
    return pl.BlockSpec(shape, lambda *g: (0,) * len(shape))


def _ada_kernel(cond_ref, w_ref, b_ref, o_ref):
    c = cond_ref[...]
    s = (c * jax.nn.sigmoid(c)).astype(BF16)
    o_ref[...] = _dot(s, w_ref[...].astype(BF16)) + b_ref[...]


def _ada_modulation(cond, ada_w, ada_b):
    return pl.pallas_call(
        _ada_kernel,
        grid=(DEPTH, N_MOD),
        in_specs=[
            _full_spec((COND_ROWS, D_MODEL)),
            pl.BlockSpec((None, D_MODEL, D_MODEL), lambda l, n: (l, 0, n)),
            pl.BlockSpec((None, None, 1, D_MODEL), lambda l, n: (l, n, 0, 0)),
        ],
        out_specs=pl.BlockSpec((None, None, COND_ROWS, D_MODEL), lambda l, n: (l, n, 0, 0)),
        out_shape=jax.ShapeDtypeStruct((DEPTH, N_MOD, COND_ROWS, D_MODEL), F32),
        compiler_params=pltpu.CompilerParams(dimension_semantics=("arbitrary", "arbitrary"),
                                             vmem_limit_bytes=VMEM_LIMIT),
        name="ada_modulation",
    )(cond, ada_w, ada_b.reshape(DEPTH, N_MOD, 1, D_MODEL))


N_PREV = 6


def _prev_specs(layer, tile_fn):
    ys = [pl.BlockSpec((None, TM, HALF), functools.partial(lambda k, *g: (k, tile_fn(*g), 0), k))
          for k in range(TOP_K)]
    return ys + [pl.BlockSpec((TM, LANES), lambda *g: (tile_fn(*g), 0)),
                 _mod_spec(layer - 1, 5, _tile_mod_row)]


def _apply_prev(x, prev_refs):
    y_refs, gt_ref, gate_ref = prev_refs[:TOP_K], prev_refs[TOP_K], prev_refs[TOP_K + 1]
    gt = gt_ref[...]
    acc_lo = acc_hi = None
    for k in range(TOP_K):
        lo, hi = _unpack_halves(y_refs[k][...])
        g = gt[:, k:k + 1]
        acc_lo = g * lo if acc_lo is None else acc_lo + g * lo
        acc_hi = g * hi if acc_hi is None else acc_hi + g * hi
    return x + gate_ref[...] * jnp.concatenate([acc_lo, acc_hi], axis=1)


N_ROUTE_OUT = 5


def _route_out_specs(tile_fn):
    tile = lambda w: pl.BlockSpec((TM, w), lambda *g: (tile_fn(*g), 0))
    return [tile(D_MODEL), tile(HALF), tile(LANES), tile(LANES),
            pl.BlockSpec((SUBLANES, LANES), lambda *g: (tile_fn(*g), 0))]


_ROUTE_OUT_SHAPES = [
    jax.ShapeDtypeStruct((N_TOK, D_MODEL), F32),
    jax.ShapeDtypeStruct((N_TOK, HALF), U32),
    jax.ShapeDtypeStruct((N_TOK, LANES), F32),
    jax.ShapeDtypeStruct((N_TOK, LANES), I32),
    jax.ShapeDtypeStruct((N_TILES * SUBLANES, LANES), F32),
]


def _post_mixer(x, mix, gate_ref, gffn_ref, shift_ref, scale_ref, rwhi_ref, rwlo_ref, rb_ref,
                xo_ref, hp_ref, gt_ref, ei_ref, cnt_ref):
    xn = x + gate_ref[...] * mix
    xo_ref[...] = xn
    h2 = _rms(xn, gffn_ref[...]) * (1.0 + scale_ref[...]) + shift_ref[...]
    hi = h2.astype(BF16)
    hif = hi.astype(F32)
    lo = (h2 - hif).astype(BF16)
    hp_ref[...] = _pack_halves(hif)
    rwhi = rwhi_ref[...]
    logits = _dot(hi, rwhi) + _dot(lo, rwhi) + _dot(hi, rwlo_ref[...]) + rb_ref[...]

    lane = lax.broadcasted_iota(I32, (TM, LANES), 1)
    lane_f = lane.astype(F32)
    work = jnp.where(lane < N_EXPERTS, logits, -jnp.inf)
    member = jnp.zeros((TM, LANES), F32)
    gates = jnp.zeros((TM, LANES), F32)
    ids = jnp.zeros((TM, LANES), F32)
    den = None
    top = None
    for k in range(TOP_K):
        m = jnp.max(work, axis=1, keepdims=True)
        idx = jnp.min(jnp.where(work == m, lane_f, float(LANES)), axis=1, keepdims=True)
        onehot = lane_f == idx
        work = jnp.where(onehot, -jnp.inf, work)
        member = member + onehot.astype(F32)
        if k == 0:
            top = m
        e = jnp.exp(m - top)
        den = e if den is None else den + e
        gates = gates + jnp.where(lane == k, e, 0.0)
        ids = ids + jnp.where(lane == k, idx, 0.0)
    gt_ref[...] = gates * (1.0 / den)
    ei_ref[...] = ids.astype(I32)
    cnt_ref[...] = jnp.broadcast_to(jnp.sum(member, axis=0, keepdims=True), (SUBLANES, LANES))


def _rope(x, c, sa, sb):
    outs = []
    for h in range(N_GROUPS):
        xs = x[:, GROUP * h:GROUP * (h + 1)]
        outs.append(xs * c + pltpu.roll(xs, GROUP - 16, 1) * sa + pltpu.roll(xs, 16, 1) * sb)
    return jnp.concatenate(outs, axis=1)


def _even_proj_kernel(has_prev, *refs):
    x_ref = refs[0]
    refs = refs[1:]
    if has_prev:
        prev_refs = refs[:N_PREV]
        refs = refs[N_PREV:]
    (g_ref, shift_ref, scale_ref, w_ref, bd_ref, qg_ref, kg_ref, cos_ref, sa_ref, sb_ref) = refs[:10]
    outs = refs[10:]
    if has_prev:
        xo_ref = outs[0]
        outs = outs[1:]
    a_ref, q_ref, k_ref, v_ref, kf_ref, vf_ref = outs

    t = pl.program_id(0)
    x = x_ref[...]
    if has_prev:
        x = _apply_prev(x, prev_refs)
        xo_ref[...] = x
    h = (_rms(x, g_ref[...]) * (1.0 + scale_ref[...]) + shift_ref[...]).astype(BF16)

    a_ref[...] = _dot(h, w_ref[:, 0:HALF]).astype(BF16)
    zv = _dot(h, w_ref[:, 3 * HALF:4 * HALF])
    vf_ref[...] = zv
    v_ref[...] = zv.astype(BF16)

    bd = bd_ref[...]

    def qk_norm(z, gain):
        shi, slo = _split_bf16(z * z)
        ssq = _dot(shi, bd) + _dot(slo, bd)
        return z * lax.rsqrt(ssq * (1.0 / QK_DIM) + EPS) * gain

    qn = qk_norm(_dot(h, w_ref[:, HALF:2 * HALF]), qg_ref[...]) * (QK_DIM ** -0.5)
    kn = qk_norm(_dot(h, w_ref[:, 2 * HALF:3 * HALF]), kg_ref[...])
    kf_ref[...] = kn

    @pl.when(t < CTX_TILES)
    def _():
        q_ref[...] = qn.astype(BF16)
        k_ref[...] = kn.astype(BF16)

    @pl.when(t >= CTX_TILES)
    def _():
        c, sa, sb = cos_ref[...], sa_ref[...], sb_ref[...]
        q_ref[...] = _rope(qn, c, sa, sb).astype(BF16)
        k_ref[...] = _rope(kn, c, sa, sb).astype(BF16)


def _even_proj(layer, x, prev, mods, norm_g, w_in, blockdiag, qg, kg, rope_tabs):
    has_prev = prev is not None
    tile = lambda w: pl.BlockSpec((TM, w), lambda t: (t, 0))
    rope_spec = pl.BlockSpec(
        (TM, GROUP), lambda t: (jnp.where(t < CTX_TILES, 0, (t - CTX_TILES) % LAT_TILES_PER_BATCH), 0))
    in_specs = [tile(D_MODEL)]
    args = [x]
    if has_prev:
        yg, gates = prev
        in_specs += _prev_specs(layer, lambda t: t)
        args += [yg] * TOP_K + [gates, mods]
    in_specs += [
        _full_spec((1, D_MODEL)),
        _mod_spec(layer, 0, _tile_mod_row), _mod_spec(layer, 1, _tile_mod_row),
        _full_spec((D_MODEL, 4 * HALF)), _full_spec((HALF, HALF)),
        _full_spec((1, HALF)), _full_spec((1, HALF)),
        rope_spec, rope_spec, rope_spec,
    ]
    args += [norm_g, mods, mods, w_in, blockdiag, qg, kg, *rope_tabs]
    out_specs = [tile(HALF)] * 6
    out_shape = [jax.ShapeDtypeStruct((N_TOK, HALF), BF16)] * 4 + \
                [jax.ShapeDtypeStruct((N_TOK, HALF), F32)] * 2
    if has_prev:
        out_specs = [tile(D_MODEL)] + out_specs
        out_shape = [jax.ShapeDtypeStruct((N_TOK, D_MODEL), F32)] + out_shape
    res = pl.pallas_call(
        functools.partial(_even_proj_kernel, has_prev),
        grid=(N_TILES,),
        in_specs=in_specs, out_specs=out_specs, out_shape=out_shape,
        compiler_params=pltpu.CompilerParams(dimension_semantics=("arbitrary",),
                                             vmem_limit_bytes=VMEM_LIMIT),
        name="even_proj",
    )(*args)
    if has_prev:
        return res[0], res[1:]
    return x, res


def _even_mix_kernel(lambda_init, *refs):
    (x_ref, q_ref, ac_ref, kc_ref, vc_ref, al_ref, kl_ref, vl_ref, ck_ref, cv_ref,
     cnc_ref, snc_ref, cnl_ref, snl_ref) = refs[:14]
    shared = refs[14:]
    t = pl.program_id(0)

    @pl.when(t < CTX_TILES)
    def _():
        _even_mix_body(lambda_init, x_ref, q_ref, ac_ref, [(kc_ref, vc_ref, False)],
                       cnc_ref, snc_ref, *shared)

    @pl.when(t >= CTX_TILES)
    def _():
        _even_mix_body(lambda_init, x_ref, q_ref, al_ref, [(kl_ref, vl_ref, False), (ck_ref, cv_ref, True)],
                       cnl_ref, snl_ref, *shared)


def _even_mix_body(lambda_init, x_ref, q_ref, a_ref, kv_refs, cn_ref, sn_ref, *refs):
    (cs_ref, lq_ref, lk_ref, sg_ref, wo_ref,
     gate_ref, gffn_ref, shift_ref, scale_ref, rwhi_ref, rwlo_ref, rb_ref) = refs[:12]
    out_refs = refs[12:]

    a = a_ref[...]
    y1 = _dot(cn_ref[...], a).astype(BF16)
    y2 = _dot(sn_ref[...], a).astype(BF16)
    cs = cs_ref[...]
    pieces = []
    for g in range(N_GROUPS):
        sl = slice(GROUP * g, GROUP * (g + 1))
        pieces.append(_dot(jnp.concatenate([y1[:, sl], y2[:, sl]], axis=1), cs).astype(BF16))

    d = jnp.sum(lq_ref[...] * lk_ref[...], axis=1, keepdims=True)
    ed = jnp.exp(d)
    lam = ed[0:1, :] - ed[1:2, :] + lambda_init
    lane = lax.broadcasted_iota(I32, (TM, GROUP), 1)
    nt = (((1,), (1,)), ((), ()))
    for hd in range(N_GROUPS):
        sl = slice(GROUP * hd, GROUP * (hd + 1))
        qf = q_ref[:, sl].astype(F32)
        parts = []
        for k_ref, v_ref, is_f32 in kv_refs:
            kk, vv = k_ref[:, sl], v_ref[:, sl]
            parts.append((kk.astype(BF16), vv.astype(BF16)) if is_f32 else (kk, vv))

        def probs(qm):
            ss = [lax.dot_general(qm, kk, nt, preferred_element_type=F32) for kk, _ in parts]
            m = functools.reduce(jnp.maximum, [jnp.max(s, axis=1, keepdims=True) for s in ss])
            es = [jnp.exp(s - m) for s in ss]
            den = functools.reduce(lambda u, w: u + w, [jnp.sum(e, axis=1, keepdims=True) for e in es])
            return es, 1.0 / den

        es0, inv0 = probs(jnp.where(lane < QK_DIM, qf, 0.0).astype(BF16))
        es1, inv1 = probs(jnp.where(lane >= QK_DIM, qf, 0.0).astype(BF16))
        linv1 = lam * inv1
        o = None
        for (e0, e1, (_, vv)) in zip(es0, es1, parts):
            po = _dot((e0 * inv0 - e1 * linv1).astype(BF16), vv)
            o = po if o is None else o + po
        pieces.append((_rms(o, sg_ref[...]) * (1.0 - lambda_init)).astype(BF16))

    mix = _dot(jnp.concatenate(pieces, axis=1), wo_ref[...])
    _post_mixer(x_ref[...], mix, gate_ref, gffn_ref, shift_ref, scale_ref, rwhi_ref, rwlo_ref, rb_ref,
                *out_refs)


def _even_mix(layer, x, a, q, k, v, cache_k, cache_v, dft_ctx, dft_lat, cs128, lam_q, lam_k, subln_g,
              w_out, mods, gffn, rw_hi, rw_lo, rb):
    j = layer // 2
    lambda_init = 0.8 - 0.6 * math.exp(-0.3 * layer)
    lat = lambda t: jnp.maximum(t - CTX_TILES, 0)
    tile = lambda w: pl.BlockSpec((TM, w), lambda t: (t, 0))
    ctx_seq = pl.BlockSpec((CTX_SEQ, HALF), lambda t: (jnp.minimum(t, CTX_TILES - 1), 0))
    lat_seq = pl.BlockSpec((LAT_SEQ, HALF),
                           lambda t: (N_CTX // LAT_SEQ + lat(t) // LAT_TILES_PER_BATCH, 0))
    cache = pl.BlockSpec((None, None, PAST_LEN, HALF),
                         lambda t: (lat(t) // LAT_TILES_PER_BATCH, j, 0, 0))
    dft_lat_spec = pl.BlockSpec((TM, LAT_SEQ), lambda t: (lat(t) % LAT_TILES_PER_BATCH, 0))
    in_specs = [
        tile(D_MODEL), tile(HALF), ctx_seq, ctx_seq, ctx_seq, lat_seq, lat_seq, lat_seq, cache, cache,
        _full_spec((CTX_SEQ, CTX_SEQ)), _full_spec((CTX_SEQ, CTX_SEQ)), dft_lat_spec, dft_lat_spec,
        _full_spec((2 * GROUP, GROUP)),
        _full_spec((2, QK_DIM)), _full_spec((2, QK_DIM)), _full_spec((1, GROUP)),
        _full_spec((D_MODEL, D_MODEL)),
        _mod_spec(layer, 2, _tile_mod_row), _full_spec((1, D_MODEL)),
        _mod_spec(layer, 3, _tile_mod_row), _mod_spec(layer, 4, _tile_mod_row),
        _full_spec((D_MODEL, LANES)), _full_spec((D_MODEL, LANES)), _full_spec((1, LANES)),
    ]
    args = [x, q, a, k, v, a, k, v, cache_k, cache_v, dft_ctx[0], dft_ctx[1], dft_lat[0], dft_lat[1],
            cs128, lam_q, lam_k, subln_g, w_out, mods, gffn, mods, mods, rw_hi, rw_lo, rb]
    return pl.pallas_call(
        functools.partial(_even_mix_kernel, lambda_init),
        grid=(N_TILES,),
        in_specs=in_specs,
        out_specs=_route_out_specs(lambda t: t),
        out_shape=_ROUTE_OUT_SHAPES,
        compiler_params=pltpu.CompilerParams(dimension_semantics=("arbitrary",),
                                             vmem_limit_bytes=VMEM_LIMIT),
        name="even_mix",
    )(*args)


def _odd_kernel(has_prev, *refs):
    x_ref = refs[0]
    refs = refs[1:]
    if has_prev:
        prev_refs = refs[:N_PREV]
        refs = refs[N_PREV:]
    (g_ref, shift_ref, scale_ref, w_ref, vg_ref, ws_ref, bs_ref, wo_ref,
     gate_ref, gffn_ref, shift2_ref, scale2_ref, rwhi_ref, rwlo_ref, rb_ref) = refs[:15]
    out_refs = refs[15:]

    x = x_ref[...]
    if has_prev:
        x = _apply_prev(x, prev_refs)
    h = (_rms(x, g_ref[...]) * (1.0 + scale_ref[...]) + shift_ref[...]).astype(BF16)

    def gelu(z):
        return 0.5 * z * (1.0 + lax.erf(z * (2.0 ** -0.5)))

    u = gelu(_dot(h, w_ref[:, 0:D_MODEL]))
    vn = _rms(gelu(_dot(h, w_ref[:, D_MODEL:2 * D_MODEL])), vg_ref[...]).astype(BF16)
    bs = bs_ref[...]
    rows = []
    for c in range(TM // CHUNK):
        cols = []
        for g in range(C_GROUPS):
            cols.append(_dot(ws_ref[g], vn[CHUNK * c:CHUNK * (c + 1), GROUP * g:GROUP * (g + 1)]))
        rows.append(jnp.concatenate(cols, axis=1) + bs)
    sv = jnp.concatenate(rows, axis=0)
    mix = _dot((u * sv).astype(BF16), wo_ref[...])
    _post_mixer(x, mix, gate_ref, gffn_ref, shift2_ref, scale2_ref, rwhi_ref, rwlo_ref, rb_ref,
                *out_refs)


def _odd_layer(layer, x, prev, mods, norm_g, w_in, v_norm_g, w_s, b_s, w_out, gffn, rw_hi, rw_lo, rb):
    has_prev = prev is not None
    in_specs = [pl.BlockSpec((TM, D_MODEL), lambda t: (t, 0))]
    args = [x]
    if has_prev:
        yg, gates = prev
        in_specs += _prev_specs(layer, lambda t: t)
        args += [yg] * TOP_K + [gates, mods]
    in_specs += [
        _full_spec((1, D_MODEL)),
        _mod_spec(layer, 0, _tile_mod_row), _mod_spec(layer, 1, _tile_mod_row),
        _full_spec((D_MODEL, 2 * D_MODEL)), _full_spec((1, D_MODEL)),
        _full_spec((C_GROUPS, CHUNK, CHUNK)), _full_spec((CHUNK, D_MODEL)),
        _full_spec((D_MODEL, D_MODEL)),
        _mod_spec(layer, 2, _tile_mod_row), _full_spec((1, D_MODEL)),
        _mod_spec(layer, 3, _tile_mod_row), _mod_spec(layer, 4, _tile_mod_row),
        _full_spec((D_MODEL, LANES)), _full_spec((D_MODEL, LANES)), _full_spec((1, LANES)),
    ]
    args += [norm_g, mods, mods, w_in, v_norm_g, w_s, b_s, w_out, mods, gffn, mods, mods, rw_hi, rw_lo, rb]
    return pl.pallas_call(
        functools.partial(_odd_kernel, has_prev),
        grid=(N_TILES,),
        in_specs=in_specs,
        out_specs=_route_out_specs(lambda t: t),
        out_shape=_ROUTE_OUT_SHAPES,
        compiler_params=pltpu.CompilerParams(dimension_semantics=("arbitrary",),
                                             vmem_limit_bytes=VMEM_LIMIT),
        name="odd_layer",
    )(*args)


def _slot_kernel(ei_ref, base_ref, tri_ref, o_ref):
    lane = lax.broadcasted_iota(I32, (TM, LANES), 1)
    ei = ei_ref[...]
    onehots = [lane == ei[:, k:k + 1] for k in range(TOP_K)]
    member = functools.reduce(lambda u, w: u + w, [o.astype(F32) for o in onehots])
    before = _dot(tri_ref[...], member.astype(BF16)) + base_ref[0:1, :]
    slots = jnp.zeros((TM, LANES), F32)
    for k in range(TOP_K):
        s = jnp.sum(jnp.where(onehots[k], before, 0.0), axis=1, keepdims=True)
        slots = slots + jnp.where(lane == k, s, 0.0)
    o_ref[...] = slots.T[0:SUBLANES, :].astype(I32)


def _slots(eidx, tile_base, tri):
    return pl.pallas_call(
        _slot_kernel,
        grid=(N_TILES,),
        in_specs=[pl.BlockSpec((TM, LANES), lambda t: (t, 0)),
                  pl.BlockSpec((SUBLANES, LANES), lambda t: (t, 0)),
                  _full_spec((TM, TM))],
        out_specs=pl.BlockSpec((None, SUBLANES, TM), lambda t: (t, 0, 0)),
        out_shape=jax.ShapeDtypeStruct((N_TILES, SUBLANES, TM), I32),
        compiler_params=pltpu.CompilerParams(dimension_semantics=("arbitrary",)),
        name="moe_slots",
    )(eidx, tile_base, tri)


def _sc_mesh():
    return plsc.VectorSubcoreMesh(core_axis_name="c", subcore_axis_name="s")


def _sc_worker():
    return lax.axis_index("s") * SC_CORES + lax.axis_index("c")


def _dispatch(hp, idx):
    n_chunks = N_TOK // SC_WORKERS // SC_ROWS

    @functools.partial(
        pl.kernel, mesh=_sc_mesh(),
        out_type=jax.ShapeDtypeStruct((N_SLOTS, HALF), U32),
        scratch_types=[pltpu.VMEM((TOP_K, SC_ROWS), I32), pltpu.VMEM((SC_ROWS, HALF), U32),
                       pltpu.SemaphoreType.DMA],
        name="moe_dispatch",
    )
    def k(x_hbm, idx_hbm, out_hbm, idx_v, rows_v, sem):
        wid = _sc_worker()

        @pl.loop(0, n_chunks)
        def _(j):
            pltpu.sync_copy(idx_hbm.at[wid, j], idx_v)
            pltpu.sync_copy(x_hbm.at[pl.ds((wid * n_chunks + j) * SC_ROWS, SC_ROWS)], rows_v)
            copies = [pltpu.async_copy(rows_v, out_hbm.at[idx_v.at[kk]], sem) for kk in range(TOP_K)]
            for cp in copies:
                cp.wait()

    return k(hp, idx)


def _combine_gather(ys, idx):
    n_chunks = N_ASSIGN // SC_WORKERS // SC_ROWS

    @functools.partial(
        pl.kernel, mesh=_sc_mesh(),
        out_type=jax.ShapeDtypeStruct((N_ASSIGN, HALF), U32),
        scratch_types=[pltpu.VMEM((SC_ROWS,), I32), pltpu.VMEM((SC_ROWS, HALF), U32),
                       pltpu.SemaphoreType.DMA],
        name="moe_combine",
    )
    def k(ys_hbm, idx_hbm, out_hbm, idx_v, rows_v, sem):
        wid = _sc_worker()

        @pl.loop(0, n_chunks)
        def _(j):
            pltpu.sync_copy(idx_hbm.at[wid, j], idx_v)
            pltpu.async_copy(ys_hbm.at[idx_v], rows_v, sem).wait()
            pltpu.sync_copy(rows_v, out_hbm.at[pl.ds((wid * n_chunks + j) * SC_ROWS, SC_ROWS)])

    return k(ys, idx)


def _moe_kernel(be_ref, nused_ref, xs_ref, wg_ref, bg_ref, wu_ref, bu_ref, wd_ref, bd_ref, o_ref,
                wg_s, wu_s, wd_s):
    i = pl.program_id(0)
    prev = be_ref[jnp.maximum(i - 1, 0)]
    active = i < nused_ref[0]

    @pl.when(active & ((i == 0) | (be_ref[i] != prev)))
    def _():
        wg_s[...] = wg_ref[...].astype(BF16)
        wu_s[...] = wu_ref[...].astype(BF16)
        wd_s[...] = wd_ref[...].astype(BF16)

    @pl.when(active)
    def _():
        lo, hi = _unpack_halves(xs_ref[...])
        xb = jnp.concatenate([lo.astype(BF16), hi.astype(BF16)], axis=1)
        gt = jnp.minimum(_dot(xb, wg_s[...]) + bg_ref[...], SWIGLU_LIMIT)
        up = jnp.clip(_dot(xb, wu_s[...]) + bu_ref[...], -SWIGLU_LIMIT, SWIGLU_LIMIT)
        glu = gt * jax.nn.sigmoid(SWIGLU_ALPHA * gt)
        hmid = ((up + 1.0) * glu).astype(BF16)
        out = _dot(hmid, wd_s[...]) + bd_ref[...]
        o_ref[...] = _pack_halves(out.astype(BF16).astype(F32))

    @pl.when(jnp.logical_not(active))
    def _():
        o_ref[...] = jnp.zeros_like(o_ref)


def _moe_experts(layer, xs, block_expert, n_used, w_gate, b_gate, w_up, b_up, w_down, b_down):
    wspec = pl.BlockSpec((None, None, D_MODEL, D_MODEL), lambda i, be, nu: (layer, be[i], 0, 0))
    bspec = pl.BlockSpec((None, None, 1, D_MODEL), lambda i, be, nu: (layer, be[i], 0, 0))
    rows = pl.BlockSpec((MOE_BLOCK, HALF), lambda i, be, nu: (i, 0))
    bias = lambda b: b.reshape(DEPTH, N_EXPERTS, 1, D_MODEL)
    return pl.pallas_call(
        _moe_kernel,
        grid_spec=pltpu.PrefetchScalarGridSpec(
            num_scalar_prefetch=2,
            grid=(MOE_BLOCKS,),
            in_specs=[rows, wspec, bspec, wspec, bspec, wspec, bspec],
            out_specs=rows,
            scratch_shapes=[pltpu.VMEM((D_MODEL, D_MODEL), BF16)] * 3,
        ),
        out_shape=jax.ShapeDtypeStruct((N_SLOTS, HALF), U32),
        compiler_params=pltpu.CompilerParams(dimension_semantics=("arbitrary",),
                                             vmem_limit_bytes=VMEM_LIMIT),
        name="moe_experts",
    )(block_expert, n_used, xs, w_gate, bias(b_gate), w_up, bias(b_up), w_down, bias(b_down))


def _moe(layer, hp, eidx, cnt, tri, w_gate, b_gate, w_up, b_up, w_down, b_down):
    tile_cnt = cnt[::SUBLANES, :N_EXPERTS].astype(I32)
    counts = jnp.sum(tile_cnt, axis=0)
    padded = (counts + MOE_BLOCK - 1) // MOE_BLOCK * MOE_BLOCK
    pad_ends = jnp.cumsum(padded)
    pad_starts = pad_ends - padded
    tile_base = pad_starts[None, :] + jnp.cumsum(tile_cnt, axis=0) - tile_cnt
    tile_base = jnp.pad(tile_base.astype(F32), ((0, 0), (0, LANES - N_EXPERTS)))
    tile_base = jnp.repeat(tile_base, SUBLANES, axis=0)
    block_expert = jnp.minimum(
        jnp.searchsorted(pad_ends, jnp.arange(MOE_BLOCKS, dtype=I32) * MOE_BLOCK, side='right'),
        N_EXPERTS - 1).astype(I32)
    n_used = (pad_ends[-1:] // MOE_BLOCK).astype(I32)

    slots = _slots(eidx, tile_base, tri)[:, :TOP_K, :]
    per_tile = TM // SC_ROWS
    d_idx = slots.reshape(N_TILES, TOP_K, per_tile, SC_ROWS).transpose(0, 2, 1, 3)
    d_idx = d_idx.reshape(SC_WORKERS, N_TOK // SC_WORKERS // SC_ROWS, TOP_K, SC_ROWS)
    c_idx = slots.transpose(1, 0, 2).reshape(SC_WORKERS, N_ASSIGN // SC_WORKERS // SC_ROWS, SC_ROWS)

    xs = _dispatch(hp, d_idx)
    ys = _moe_experts(layer, xs, block_expert, n_used, w_gate, b_gate, w_up, b_up, w_down, b_down)
    return _combine_gather(ys, c_idx).reshape(TOP_K, N_TOK, HALF)


def _final_kernel(x_ref, *refs):
    o_ref = refs[N_PREV]
    o_ref[...] = _apply_prev(x_ref[...], refs[:N_PREV])


def _final_residual(x, prev, mods):
    tile = pl.BlockSpec((TM, D_MODEL), lambda t: (t, 0))
    yg, gates = prev
    return pl.pallas_call(
        _final_kernel,
        grid=(N_TILES,),
        in_specs=[tile] + _prev_specs(DEPTH, lambda t: t),
        out_specs=tile,
        out_shape=jax.ShapeDtypeStruct((N_TOK, D_MODEL), F32),
        compiler_params=pltpu.CompilerParams(dimension_semantics=("arbitrary",)),
        name="final_residual",
    )(x, *([yg] * TOP_K), gates, mods)


def _dft_pair(n):
    idx = np.arange(n)
    ang = 2.0 * np.pi * ((idx[:, None] * idx[None, :]) % n) / n
    return np.cos(ang) / np.sqrt(n), np.sin(ang) / np.sqrt(n)


def _rope_tables():
    lane = np.arange(GROUP)
    within = lane % QK_DIM
    axis = within // 32
    e = within % 32
    inv16 = ROPE_BASE ** (-jnp.arange(16, dtype=F32) / 16)
    pos = np.arange(LAT_SEQ)
    coord = np.where(axis[None, :] == 0, (pos // GRID_W)[:, None], (pos % GRID_W)[:, None])
    ang = jnp.asarray(coord, F32) * inv16[e % 16][None, :]
    first = jnp.asarray((e // 16) == 0)[None, :]
    cos, sin = jnp.cos(ang), jnp.sin(ang)
    return cos, jnp.where(first, -sin, 0.0), jnp.where(first, 0.0, sin)


def kernel(x_prompt, x_sample, cache_k, cache_v, c, c_ctx, ada_w, ada_b, norm_mix_g, norm_ffn_g,
           w_in_even, w_out_even, q_norm_g, k_norm_g, lambda_q, lambda_k, subln_g,
           w_in_odd, v_norm_g, w_spatial, b_spatial, w_out_odd,
           router_w, router_b, w_gate, b_gate, w_up, b_up, w_down, b_down):
    n_even = w_in_even.shape[0]
    x = jnp.concatenate([x_prompt.reshape(N_CTX, D_MODEL), x_sample.reshape(N_LAT, D_MODEL)], axis=0)
    cond = jnp.zeros((COND_ROWS, D_MODEL), F32).at[0].set(c_ctx).at[1:1 + LAT_BATCH].set(c)
    mods = _ada_modulation(cond, ada_w, ada_b).reshape(DEPTH, N_MOD, COND_ROWS, 1, D_MODEL)

    dft_ctx = [jnp.asarray(m, F32).astype(BF16) for m in _dft_pair(CTX_SEQ)]
    dft_lat = [jnp.asarray(m, F32).astype(BF16) for m in _dft_pair(LAT_SEQ)]
    c128, s128 = _dft_pair(GROUP)
    cs128 = jnp.asarray(np.concatenate([c128, -s128], axis=0), F32).astype(BF16)
    grp = np.arange(HALF) // QK_DIM
    blockdiag = jnp.asarray(grp[:, None] == grp[None, :], F32).astype(BF16)
    tri = jnp.asarray(np.arange(TM)[:, None] > np.arange(TM)[None, :], F32).astype(BF16)
    rope_tabs = _rope_tables()
    cache_k2 = cache_k.reshape(LAT_BATCH, n_even, PAST_LEN, HALF)
    cache_v2 = cache_v.reshape(LAT_BATCH, n_even, PAST_LEN, HALF)

    rw = jnp.pad(router_w, ((0, 0), (0, 0), (0, LANES - N_EXPERTS)))
    rw_hi = rw.astype(BF16)
    rw_lo = (rw - rw_hi.astype(F32)).astype(BF16)
    rb = jnp.pad(router_b, ((0, 0), (0, LANES - N_EXPERTS)))[:, None, :]

    prev = None
    new_k, new_v = [], []
    for l in range(DEPTH):
        j = l // 2
        gmix = norm_mix_g[l][None, :]
        gffn = norm_ffn_g[l][None, :]
        if l % 2 == 0:
            x, (a, q, k, v, kf, vf) = _even_proj(
                l, x, prev, mods, gmix, w_in_even[j].astype(BF16), blockdiag,
                jnp.tile(q_norm_g[j], HALF // QK_DIM)[None, :], jnp.tile(k_norm_g[j], HALF // QK_DIM)[None, :],
                rope_tabs)
            new_k.append(kf[:N_CTX].reshape(CTX_BATCH, CTX_SEQ, N_GROUPS, 2, QK_DIM))
            new_v.append(vf[:N_CTX].reshape(CTX_BATCH, CTX_SEQ, N_GROUPS, GROUP))
            common = (cs128, lambda_q[j], lambda_k[j], subln_g[j][None, :], w_out_even[j].astype(BF16),
                      mods, gffn, rw_hi[l], rw_lo[l], rb[l])
            x, hp, gates, eidx, cnt = _even_mix(l, x, a, q, k, v, cache_k2, cache_v2, dft_ctx, dft_lat, *common)
        else:
            b_s = jnp.broadcast_to(jnp.transpose(b_spatial[j])[:, :, None],
                                   (CHUNK, C_GROUPS, GROUP)).reshape(CHUNK, D_MODEL)
            x, hp, gates, eidx, cnt = _odd_layer(
                l, x, prev, mods, gmix, w_in_odd[j].astype(BF16), v_norm_g[j][None, :],
                w_spatial[j].astype(BF16), b_s, w_out_odd[j].astype(BF16), gffn, rw_hi[l], rw_lo[l], rb[l])
        yg = _moe(l, hp, eidx, cnt, tri, w_gate, b_gate, w_up, b_up, w_down, b_down)
        prev = (yg, gates)
    x = _final_residual(x, prev, mods)
    return (x[:N_CTX].reshape(CTX_BATCH, CTX_SEQ, D_MODEL),
            x[N_CTX:].reshape(LAT_BATCH, LAT_SEQ, D_MODEL),
            jnp.stack(new_k, axis=1),
            jnp.stack(new_v, axis=1))
```

```python
import functools
import math

import numpy as np
import jax
import jax.numpy as jnp
from jax import lax
from jax.experimental import pallas as pl
from jax.experimental.pallas import tpu as pltpu
from jax.experimental.pallas import tpu_sc as plsc

F32 = jnp.float32
BF16 = jnp.bfloat16
U32 = jnp.uint32
I32 = jnp.int32

D_MODEL = 1024
DEPTH = 4
N_MOD = 6
EPS = 1e-6
CTX_BATCH, CTX_SEQ = 32, 256
LAT_BATCH, LAT_SEQ = 8, 1024
PAST_LEN = 512
GRID_W = 64
N_CTX = CTX_BATCH * CTX_SEQ
N_LAT = LAT_BATCH * LAT_SEQ
N_TOK = N_CTX + N_LAT
TM = 256
N_TILES = N_TOK // TM
CTX_TILES = N_CTX // TM
LAT_TILES_PER_BATCH = LAT_SEQ // TM
COND_ROWS = 16
GROUP = 128
N_GROUPS = 4
QK_DIM = 64
HALF = 512
CHUNK = 128
C_GROUPS = 8
N_EXPERTS = 32
TOP_K = 4
LANES = 128
SUBLANES = 8
SWIGLU_LIMIT = 7.0
SWIGLU_ALPHA = 1.702
MOE_BLOCK = 256
N_ASSIGN = N_TOK * TOP_K
MOE_BLOCKS = N_ASSIGN // MOE_BLOCK + N_EXPERTS
N_SLOTS = MOE_BLOCKS * MOE_BLOCK
ROPE_BASE = 10000.0
VMEM_LIMIT = 52 * 1024 * 1024
SC_CORES = 2
SC_SUBCORES = 16
SC_WORKERS = SC_CORES * SC_SUBCORES
SC_ROWS = 64
SC_BUFS = 2
HI_MASK = 0xFFFF0000
LOG2E = math.log2(math.e)


def _dot(a, b):
    return jnp.dot(a, b, preferred_element_type=F32)


def _rms(x, g):
    return x * lax.rsqrt(jnp.mean(x * x, axis=-1, keepdims=True) + EPS) * g


def _split_bf16(x):
    hi = x.astype(BF16)
    lo = (x - hi.astype(F32)).astype(BF16)
    return hi, lo


def _pack_halves(xf):
    b = lax.bitcast_convert_type(xf, U32)
    return (b[:, :HALF] >> 16) | (b[:, HALF:] & jnp.uint32(HI_MASK))


def _unpack_halves(w):
    lo = lax.bitcast_convert_type(w << 16, F32)
    hi = lax.bitcast_convert_type(w & jnp.uint32(HI_MASK), F32)
    return lo, hi


def _tile_mod_row(t):
    return jnp.where(t < CTX_TILES, 0, 1 + (t - CTX_TILES) // LAT_TILES_PER_BATCH)


def _mod_spec(layer, k, row_fn):
    return pl.BlockSpec((None, None, None, 1, D_MODEL),
                        lambda *g: (layer, k, row_fn(*g), 0, 0))


def _full_spec(shape):
    return pl.BlockSpec(shape, lambda *g: (0,) * len(shape))


def _ada_kernel(cond_ref, w_ref, b_ref, o_ref):
    c = cond_ref[...]
    s = (c * jax.nn.sigmoid(c)).astype(BF16)
    o_ref[...] = _dot(s, w_ref[...].astype(BF16)) + b_ref[...]


def _ada_modulation(cond, ada_w, ada_b):
    return pl.pallas_call(
        _ada_kernel,
        grid=(DEPTH, N_MOD),
        in_specs=[
            _full_spec((COND_ROWS, D_MODEL)),
            pl.BlockSpec((None, D_MODEL, D_MODEL), lambda l, n: (l, 0, n)),
            pl.BlockSpec((None, None, 1, D_MODEL), lambda l, n: (l, n, 0, 0)),
        ],
        out_specs=pl.BlockSpec((None, None, COND_ROWS, D_MODEL), lambda l, n: (l, n, 0, 0)),
        out_shape=jax.ShapeDtypeStruct((DEPTH, N_MOD, COND_ROWS, D_MODEL), F32),
        compiler_params=pltpu.CompilerParams(dimension_semantics=("arbitrary", "arbitrary"),
                                             vmem_limit_bytes=VMEM_LIMIT),
        name="ada_modulation",
    )(cond, ada_w, ada_b.reshape(DEPTH, N_MOD, 1, D_MODEL))


N_PREV = 6


def _prev_specs(layer, tile_fn):
    ys = [pl.BlockSpec((None, TM, HALF), functools.partial(lambda k, *g: (k, tile_fn(*g), 0), k))
          for k in range(TOP_K)]
    return ys + [pl.BlockSpec((TM, LANES), lambda *g: (tile_fn(*g), 0)),
                 _mod_spec(layer - 1, 5, lambda *g: _tile_mod_row(tile_fn(*g)))]


def _apply_prev(x, prev_refs):
    y_refs, gt_ref, gate_ref = prev_refs[:TOP_K], prev_refs[TOP_K], prev_refs[TOP_K + 1]
    gt = gt_ref[...]
    acc_lo = acc_hi = None
    for k in range(TOP_K):
        lo, hi = _unpack_halves(y_refs[k][...])
        g = gt[:, k:k + 1]
        acc_lo = g * lo if acc_lo is None else acc_lo + g * lo
        acc_hi = g * hi if acc_hi is None else acc_hi + g * hi
    return x + gate_ref[...] * jnp.concatenate([acc_lo, acc_hi], axis=1)


N_ROUTE_OUT = 5


def _route_out_specs(tile_fn):
    tile = lambda w: pl.BlockSpec((TM, w), lambda *g: (tile_fn(*g), 0))
    return [tile(D_MODEL), tile(HALF), tile(LANES), tile(LANES),
            pl.BlockSpec((SUBLANES, LANES), lambda *g: (tile_fn(*g), 0))]


_ROUTE_OUT_SHAPES = [
    jax.ShapeDtypeStruct((N_TOK, D_MODEL), F32),
    jax.ShapeDtypeStruct((N_TOK, HALF), U32),
    jax.ShapeDtypeStruct((N_TOK, LANES), F32),
    jax.ShapeDtypeStruct((N_TOK, LANES), I32),
    jax.ShapeDtypeStruct((N_TILES * SUBLANES, LANES), F32),
]


def _post_mixer(x, mix, gate_ref, gffn_ref, shift_ref, scale_ref, rwhi_ref, rwlo_ref, rb_ref,
                xo_ref, hp_ref, gt_ref, ei_ref, cnt_ref):
    xn = x + gate_ref[...] * mix
    xo_ref[...] = xn
    h2 = _rms(xn, gffn_ref[...]) * (1.0 + scale_ref[...]) + shift_ref[...]
    hi = h2.astype(BF16)
    hif = hi.astype(F32)
    lo = (h2 - hif).astype(BF16)
    hp_ref[...] = _pack_halves(hif)
    rwhi = rwhi_ref[...]
    logits = _dot(hi, rwhi) + _dot(lo, rwhi) + _dot(hi, rwlo_ref[...]) + rb_ref[...]

    lane = lax.broadcasted_iota(I32, (TM, LANES), 1)
    lane_f = lane.astype(F32)
    work = jnp.where(lane < N_EXPERTS, logits, -jnp.inf)
    member = jnp.zeros((TM, LANES), F32)
    gates = jnp.zeros((TM, LANES), F32)
    ids = jnp.zeros((TM, LANES), F32)
    den = None
    top = None
    for k in range(TOP_K):
        m = jnp.max(work, axis=1, keepdims=True)
        idx = jnp.min(jnp.where(work == m, lane_f, float(LANES)), axis=1, keepdims=True)
        onehot = lane_f == idx
        work = jnp.where(onehot, -jnp.inf, work)
        member = member + onehot.astype(F32)
        if k == 0:
            top = m
        e = jnp.exp(m - top)
        den = e if den is None else den + e
        gates = gates + jnp.where(lane == k, e, 0.0)
        ids = ids + jnp.where(lane == k, idx, 0.0)
    gt_ref[...] = gates * (1.0 / den)
    ei_ref[...] = ids.astype(I32)
    cnt_ref[...] = jnp.broadcast_to(jnp.sum(member, axis=0, keepdims=True), (SUBLANES, LANES))


def _rope(x, c, sa, sb):
    outs = []
    for h in range(N_GROUPS):
        xs = x[:, GROUP * h:GROUP * (h + 1)]
        outs.append(xs * c + pltpu.roll(xs, GROUP - 16, 1) * sa + pltpu.roll(xs, 16, 1) * sb)
    return jnp.concatenate(outs, axis=1)


def _even_proj_kernel(has_prev, *refs):
    x_ref = refs[0]
    refs = refs[1:]
    if has_prev:
        prev_refs = refs[:N_PREV]
        refs = refs[N_PREV:]
    (g_ref, shift_ref, scale_ref, w_ref, bd_ref, qg_ref, kg_ref, cos_ref, sa_ref, sb_ref) = refs[:10]
    outs = refs[10:]
    if has_prev:
        xo_ref = outs[0]
        outs = outs[1:]
    a_ref, q_ref, k_ref, v_ref, kf_ref, vf_ref = outs

    t = pl.program_id(0)
    x = x_ref[...]
    if has_prev:
        x = _apply_prev(x, prev_refs)
        xo_ref[...] = x
    h = (_rms(x, g_ref[...]) * (1.0 + scale_ref[...]) + shift_ref[...]).astype(BF16)

    a_ref[...] = _dot(h, w_ref[:, 0:HALF]).astype(BF16)
    zv = _dot(h, w_ref[:, 3 * HALF:4 * HALF])
    v_ref[...] = zv.astype(BF16)

    bd = bd_ref[...]

    def qk_norm(z, gain):
        shi, slo = _split_bf16(z * z)
        ssq = _dot(shi, bd) + _dot(slo, bd)
        return z * lax.rsqrt(ssq * (1.0 / QK_DIM) + EPS) * gain

    qn = qk_norm(_dot(h, w_ref[:, HALF:2 * HALF]), qg_ref[...]) * (QK_DIM ** -0.5 * LOG2E)
    kn = qk_norm(_dot(h, w_ref[:, 2 * HALF:3 * HALF]), kg_ref[...])

    @pl.when(t < CTX_TILES)
    def _():
        kf_ref[...] = kn
        vf_ref[...] = zv
        q_ref[...] = qn.astype(BF16)
        k_ref[...] = kn.astype(BF16)

    @pl.when(t >= CTX_TILES)
    def _():
        c, sa, sb = cos_ref[...], sa_ref[...], sb_ref[...]
        q_ref[...] = _rope(qn, c, sa, sb).astype(BF16)
        k_ref[...] = _rope(kn, c, sa, sb).astype(BF16)


def _even_proj(layer, x, prev, mods, norm_g, w_in, blockdiag, qg, kg, rope_tabs):
    has_prev = prev is not None
    tile = lambda w: pl.BlockSpec((TM, w), lambda t: (t, 0))
    rope_spec = pl.BlockSpec(
        (TM, GROUP), lambda t: (jnp.where(t < CTX_TILES, 0, (t - CTX_TILES) % LAT_TILES_PER_BATCH), 0))
    in_specs = [tile(D_MODEL)]
    args = [x]
    if has_prev:
        yg, gates = prev
        in_specs += _prev_specs(layer, lambda t: t)
        args += [yg] * TOP_K + [gates, mods]
    in_specs += [
        _full_spec((1, D_MODEL)),
        _mod_spec(layer, 0, _tile_mod_row), _mod_spec(layer, 1, _tile_mod_row),
        _full_spec((D_MODEL, 4 * HALF)), _full_spec((HALF, HALF)),
        _full_spec((1, HALF)), _full_spec((1, HALF)),
        rope_spec, rope_spec, rope_spec,
    ]
    args += [norm_g, mods, mods, w_in, blockdiag, qg, kg, *rope_tabs]
    cache_tile = pl.BlockSpec((TM, HALF), lambda t: (jnp.minimum(t, CTX_TILES - 1), 0))
    out_specs = [tile(HALF)] * 4 + [cache_tile] * 2
    out_shape = [jax.ShapeDtypeStruct((N_TOK, HALF), BF16)] * 4 + \
                [jax.ShapeDtypeStruct((N_CTX, HALF), F32)] * 2
    if has_prev:
        out_specs = [tile(D_MODEL)] + out_specs
        out_shape = [jax.ShapeDtypeStruct((N_TOK, D_MODEL), F32)] + out_shape
    res = pl.pallas_call(
        functools.partial(_even_proj_kernel, has_prev),
        grid=(N_TILES,),
        in_specs=in_specs, out_specs=out_specs, out_shape=out_shape,
        compiler_params=pltpu.CompilerParams(dimension_semantics=("arbitrary",),
                                             vmem_limit_bytes=VMEM_LIMIT),
        name="even_proj",
    )(*args)
    if has_prev:
        return res[0], res[1:]
    return x, res


def _even_mix_kernel(lambda_init, *refs):
    (x_ref, q_ref, ac_ref, kc_ref, vc_ref, al_ref, kl_ref, vl_ref, ck_ref, cv_ref,
     cnc_ref, snc_ref, cnl_ref, snl_ref) = refs[:14]
    shared = refs[14:]
    t = pl.program_id(0)

    @pl.when(t < CTX_TILES)
    def _():
        _even_mix_body(lambda_init, x_ref, q_ref, ac_ref, [(kc_ref, vc_ref, False)],
                       cnc_ref, snc_ref, *shared)

    @pl.when(t >= CTX_TILES)
    def _():
        _even_mix_body(lambda_init, x_ref, q_ref, al_ref, [(kl_ref, vl_ref, False), (ck_ref, cv_ref, True)],
                       cnl_ref, snl_ref, *shared)


def _even_mix_body(lambda_init, x_ref, q_ref, a_ref, kv_refs, cn_ref, sn_ref, *refs):
    (cs_ref, lq_ref, lk_ref, sg_ref, wo_ref,
     gate_ref, gffn_ref, shift_ref, scale_ref, rwhi_ref, rwlo_ref, rb_ref) = refs[:12]
    out_refs = refs[12:]

    a = a_ref[...]
    y1 = _dot(cn_ref[...], a).astype(BF16)
    y2 = _dot(sn_ref[...], a).astype(BF16)
    cs = cs_ref[...]
    pieces = []
    for g in range(N_GROUPS):
        sl = slice(GROUP * g, GROUP * (g + 1))
        pieces.append(_dot(jnp.concatenate([y1[:, sl], y2[:, sl]], axis=1), cs).astype(BF16))

    d = jnp.sum(lq_ref[...] * lk_ref[...], axis=1, keepdims=True)
    ed = jnp.exp(d)
    lam = ed[0:1, :] - ed[1:2, :] + lambda_init
    lane = lax.broadcasted_iota(I32, (TM, GROUP), 1)
    nt = (((1,), (1,)), ((), ()))
    for hd in range(N_GROUPS):
        sl = slice(GROUP * hd, GROUP * (hd + 1))
        qf = q_ref[:, sl].astype(F32)
        parts = []
        for k_ref, v_ref, is_f32 in kv_refs:
            kk, vv = k_ref[:, sl], v_ref[:, sl]
            parts.append((kk.astype(BF16), vv.astype(BF16)) if is_f32 else (kk, vv))

        def probs(qm):
            ss = [lax.dot_general(qm, kk, nt, preferred_element_type=F32) for kk, _ in parts]
            m = functools.reduce(jnp.maximum, [jnp.max(s, axis=1, keepdims=True) for s in ss])
            es = [jnp.exp2(s - m) for s in ss]
            den = functools.reduce(lambda u, w: u + w, [jnp.sum(e, axis=1, keepdims=True) for e in es])
            return es, 1.0 / den

        es0, inv0 = probs(jnp.where(lane < QK_DIM, qf, 0.0).astype(BF16))
        es1, inv1 = probs(jnp.where(lane >= QK_DIM, qf, 0.0).astype(BF16))
        o0 = functools.reduce(lambda u, w: u + w,
                              [_dot(e.astype(BF16), vv) for e, (_, vv) in zip(es0, parts)])
        o1 = functools.reduce(lambda u, w: u + w,
                              [_dot(e.astype(BF16), vv) for e, (_, vv) in zip(es1, parts)])
        o = o0 * inv0 - o1 * (lam * inv1)
        pieces.append((_rms(o, sg_ref[...]) * (1.0 - lambda_init)).astype(BF16))

    mix = _dot(jnp.concatenate(pieces, axis=1), wo_ref[...])
    _post_mixer(x_ref[...], mix, gate_ref, gffn_ref, shift_ref, scale_ref, rwhi_ref, rwlo_ref, rb_ref,
                *out_refs)


def _even_mix(layer, x, a, q, k, v, cache_k, cache_v, dft_ctx, dft_lat, cs128, lam_q, lam_k, subln_g,
              w_out, mods, gffn, rw_hi, rw_lo, rb):
    j = layer // 2
    lambda_init = 0.8 - 0.6 * math.exp(-0.3 * layer)
    lat = lambda t: jnp.maximum(t - CTX_TILES, 0)
    tile = lambda w: pl.BlockSpec((TM, w), lambda t: (t, 0))
    ctx_seq = pl.BlockSpec((CTX_SEQ, HALF), lambda t: (jnp.minimum(t, CTX_TILES - 1), 0))
    lat_seq = pl.BlockSpec((LAT_SEQ, HALF),
                           lambda t: (N_CTX // LAT_SEQ + lat(t) // LAT_TILES_PER_BATCH, 0))
    cache = pl.BlockSpec((None, None, PAST_LEN, HALF),
                         lambda t: (lat(t) // LAT_TILES_PER_BATCH, j, 0, 0))
    dft_lat_spec = pl.BlockSpec((TM, LAT_SEQ), lambda t: (lat(t) % LAT_TILES_PER_BATCH, 0))
    in_specs = [
        tile(D_MODEL), tile(HALF), ctx_seq, ctx_seq, ctx_seq, lat_seq, lat_seq, lat_seq, cache, cache,
        _full_spec((CTX_SEQ, CTX_SEQ)), _full_spec((CTX_SEQ, CTX_SEQ)), dft_lat_spec, dft_lat_spec,
        _full_spec((2 * GROUP, GROUP)),
        _full_spec((2, QK_DIM)), _full_spec((2, QK_DIM)), _full_spec((1, GROUP)),
        _full_spec((D_MODEL, D_MODEL)),
        _mod_spec(layer, 2, _tile_mod_row), _full_spec((1, D_MODEL)),
        _mod_spec(layer, 3, _tile_mod_row), _mod_spec(layer, 4, _tile_mod_row),
        _full_spec((D_MODEL, LANES)), _full_spec((D_MODEL, LANES)), _full_spec((1, LANES)),
    ]
    args = [x, q, a, k, v, a, k, v, cache_k, cache_v, dft_ctx[0], dft_ctx[1], dft_lat[0], dft_lat[1],
            cs128, lam_q, lam_k, subln_g, w_out, mods, gffn, mods, mods, rw_hi, rw_lo, rb]
    return pl.pallas_call(
        functools.partial(_even_mix_kernel, lambda_init),
        grid=(N_TILES,),
        in_specs=in_specs,
        out_specs=_route_out_specs(lambda t: t),
        out_shape=_ROUTE_OUT_SHAPES,
        compiler_params=pltpu.CompilerParams(dimension_semantics=("arbitrary",),
                                             vmem_limit_bytes=VMEM_LIMIT),
        name="even_mix",
    )(*args)


def _odd_kernel(has_prev, *refs):
    x_ref = refs[0]
    refs = refs[1:]
    if has_prev:
        prev_refs = refs[:N_PREV]
        refs = refs[N_PREV:]
    (g_ref, shift_ref, scale_ref, w_ref, vg_ref, ws_ref, bs_ref, wo_ref,
     gate_ref, gffn_ref, shift2_ref, scale2_ref, rwhi_ref, rwlo_ref, rb_ref) = refs[:15]
    out_refs = refs[15:]

    x = x_ref[...]
    if has_prev:
        x = _apply_prev(x, prev_refs)
    h = (_rms(x, g_ref[...]) * (1.0 + scale_ref[...]) + shift_ref[...]).astype(BF16)

    def gelu(z):
        return 0.5 * z * (1.0 + lax.erf(z * (2.0 ** -0.5)))

    u = gelu(_dot(h, w_ref[:, 0:D_MODEL]))
    vn = _rms(gelu(_dot(h, w_ref[:, D_MODEL:2 * D_MODEL])), vg_ref[...]).astype(BF16)
    bs = bs_ref[...]
    rows = []
    for c in range(TM // CHUNK):
        cols = []
        for g in range(C_GROUPS):
            cols.append(_dot(ws_ref[g], vn[CHUNK * c:CHUNK * (c + 1), GROUP * g:GROUP * (g + 1)]))
        rows.append(jnp.concatenate(cols, axis=1) + bs)
    sv = jnp.concatenate(rows, axis=0)
    mix = _dot((u * sv).astype(BF16), wo_ref[...])
    _post_mixer(x, mix, gate_ref, gffn_ref, shift2_ref, scale2_ref, rwhi_ref, rwlo_ref, rb_ref,
                *out_refs)


def _odd_layer(layer, x, prev, mods, norm_g, w_in, v_norm_g, w_s, b_s, w_out, gffn, rw_hi, rw_lo, rb):
    has_prev = prev is not None
    in_specs = [pl.BlockSpec((TM, D_MODEL), lambda t: (t, 0))]
    args = [x]
    if has_prev:
        yg, gates = prev
        in_specs += _prev_specs(layer, lambda t: t)
        args += [yg] * TOP_K + [gates, mods]
    in_specs += [
        _full_spec((1, D_MODEL)),
        _mod_spec(layer, 0, _tile_mod_row), _mod_spec(layer, 1, _tile_mod_row),
        _full_spec((D_MODEL, 2 * D_MODEL)), _full_spec((1, D_MODEL)),
        _full_spec((C_GROUPS, CHUNK, CHUNK)), _full_spec((CHUNK, D_MODEL)),
        _full_spec((D_MODEL, D_MODEL)),
        _mod_spec(layer, 2, _tile_mod_row), _full_spec((1, D_MODEL)),
        _mod_spec(layer, 3, _tile_mod_row), _mod_spec(layer, 4, _tile_mod_row),
        _full_spec((D_MODEL, LANES)), _full_spec((D_MODEL, LANES)), _full_spec((1, LANES)),
    ]
    args += [norm_g, mods, mods, w_in, v_norm_g, w_s, b_s, w_out, mods, gffn, mods, mods, rw_hi, rw_lo, rb]
    return pl.pallas_call(
        functools.partial(_odd_kernel, has_prev),
        grid=(N_TILES,),
        in_specs=in_specs,
        out_specs=_route_out_specs(lambda t: t),
        out_shape=_ROUTE_OUT_SHAPES,
        compiler_params=pltpu.CompilerParams(dimension_semantics=("arbitrary",),
                                             vmem_limit_bytes=VMEM_LIMIT),
        name="odd_layer",
    )(*args)


def _slot_kernel(ei_ref, base_ref, tri_ref, o_ref):
    lane = lax.broadcasted_iota(I32, (TM, LANES), 1)
    ei = ei_ref[...]
    onehots = [lane == ei[:, k:k + 1] for k in range(TOP_K)]
    member = functools.reduce(lambda u, w: u + w, [o.astype(F32) for o in onehots])
    before = _dot(tri_ref[...], member.astype(BF16)) + base_ref[0:1, :]
    slots = jnp.zeros((TM, LANES), F32)
    for k in range(TOP_K):
        s = jnp.sum(jnp.where(onehots[k], before, 0.0), axis=1, keepdims=True)
        slots = slots + jnp.where(lane == k, s, 0.0)
    o_ref[...] = slots.T[0:SUBLANES, :].astype(I32)


def _slots(eidx, tile_base, tri):
    return pl.pallas_call(
        _slot_kernel,
        grid=(N_TILES,),
        in_specs=[pl.BlockSpec((TM, LANES), lambda t: (t, 0)),
                  pl.BlockSpec((SUBLANES, LANES), lambda t: (t, 0)),
                  _full_spec((TM, TM))],
        out_specs=pl.BlockSpec((None, SUBLANES, TM), lambda t: (t, 0, 0)),
        out_shape=jax.ShapeDtypeStruct((N_TILES, SUBLANES, TM), I32),
        compiler_params=pltpu.CompilerParams(dimension_semantics=("arbitrary",)),
        name="moe_slots",
    )(eidx, tile_base, tri)


def _sc_mesh():
    return plsc.VectorSubcoreMesh(core_axis_name="c", subcore_axis_name="s")


def _sc_worker():
    return lax.axis_index("s") * SC_CORES + lax.axis_index("c")


def _dispatch(hp, idx):
    n_chunks = N_TOK // SC_WORKERS // SC_ROWS

    @functools.partial(
        pl.kernel, mesh=_sc_mesh(),
        out_type=jax.ShapeDtypeStruct((N_SLOTS, HALF), U32),
        scratch_types=[pltpu.VMEM((n_chunks, TOP_K, SC_ROWS), I32), pltpu.VMEM((SC_BUFS, SC_ROWS, HALF), U32),
                       pltpu.SemaphoreType.DMA((SC_BUFS,)), pltpu.SemaphoreType.DMA((SC_BUFS,))],
        name="moe_dispatch",
    )
    def k(x_hbm, idx_hbm, out_hbm, idx_v, rows_v, read_sem, scat_sem):
        wid = _sc_worker()
        base = wid * n_chunks
        pltpu.sync_copy(idx_hbm.at[wid], idx_v)

        def read(j, b):
            return pltpu.make_async_copy(x_hbm.at[pl.ds((base + j) * SC_ROWS, SC_ROWS)], rows_v.at[b],
                                         read_sem.at[b])

        def scatter(j, b, kk):
            return pltpu.make_async_copy(rows_v.at[b], out_hbm.at[idx_v.at[j, kk]], scat_sem.at[b])

        def drain(j, b):
            for kk in range(TOP_K):
                scatter(j, b, kk).wait()

        read(0, 0).start()

        @pl.loop(0, n_chunks, step=SC_BUFS)
        def _(j):
            for b in range(SC_BUFS):
                jj = j + b
                other = (b + 1) % SC_BUFS
                read(jj, b).wait()

                @pl.when(jj >= 1)
                def _():
                    drain(jj - 1, other)

                @pl.when(jj + 1 < n_chunks)
                def _():
                    read(jj + 1, other).start()

                for kk in range(TOP_K):
                    scatter(jj, b, kk).start()

        drain(n_chunks - 1, (n_chunks - 1) % SC_BUFS)

    return k(hp, idx)


def _combine_gather(ys, idx):
    n_chunks = N_ASSIGN // SC_WORKERS // SC_ROWS

    @functools.partial(
        pl.kernel, mesh=_sc_mesh(),
        out_type=jax.ShapeDtypeStruct((N_ASSIGN, HALF), U32),
        scratch_types=[pltpu.VMEM((n_chunks, SC_ROWS), I32), pltpu.VMEM((SC_BUFS, SC_ROWS, HALF), U32),
                       pltpu.SemaphoreType.DMA((SC_BUFS,)), pltpu.SemaphoreType.DMA((SC_BUFS,))],
        name="moe_combine",
    )
    def k(ys_hbm, idx_hbm, out_hbm, idx_v, rows_v, gather_sem, write_sem):
        wid = _sc_worker()
        base = wid * n_chunks
        pltpu.sync_copy(idx_hbm.at[wid], idx_v)

        def gather(j, b):
            return pltpu.make_async_copy(ys_hbm.at[idx_v.at[j]], rows_v.at[b], gather_sem.at[b])

        def write(j, b):
            return pltpu.make_async_copy(rows_v.at[b], out_hbm.at[pl.ds((base + j) * SC_ROWS, SC_ROWS)],
                                         write_sem.at[b])

        gather(0, 0).start()

        @pl.loop(0, n_chunks, step=SC_BUFS)
        def _(j):
            for b in range(SC_BUFS):
                jj = j + b
                other = (b + 1) % SC_BUFS
                gather(jj, b).wait()

                @pl.when(jj >= 1)
                def _():
                    write(jj - 1, other).wait()

                @pl.when(jj + 1 < n_chunks)
                def _():
                    gather(jj + 1, other).start()

                write(jj, b).start()

        write(n_chunks - 1, (n_chunks - 1) % SC_BUFS).wait()

    return k(ys, idx)


def _moe_kernel(be_ref, nused_ref, xs_ref, wg_ref, bg_ref, wu_ref, bu_ref, wd_ref, bd_ref, o_ref,
                wg_s, wu_s, wd_s):
    i = pl.program_id(0)
    prev = be_ref[jnp.maximum(i - 1, 0)]
    active = i < nused_ref[0]

    @pl.when(active & ((i == 0) | (be_ref[i] != prev)))
    def _():
        wg_s[...] = wg_ref[...].astype(BF16)
        wu_s[...] = wu_ref[...].astype(BF16)
        wd_s[...] = wd_ref[...].astype(BF16)

    @pl.when(active)
    def _():
        lo, hi = _unpack_halves(xs_ref[...])
        xb = jnp.concatenate([lo.astype(BF16), hi.astype(BF16)], axis=1)
        gt = jnp.minimum(_dot(xb, wg_s[...]) + bg_ref[...], SWIGLU_LIMIT)
        up = jnp.clip(_dot(xb, wu_s[...]) + bu_ref[...], -SWIGLU_LIMIT, SWIGLU_LIMIT)
        glu = gt * jax.nn.sigmoid(SWIGLU_ALPHA * gt)
        hmid = ((up + 1.0) * glu).astype(BF16)
        out = _dot(hmid, wd_s[...]) + bd_ref[...]
        o_ref[...] = _pack_halves(out.astype(BF16).astype(F32))

    @pl.when(jnp.logical_not(active))
    def _():
        o_ref[...] = jnp.zeros_like(o_ref)


def _moe_experts(layer, xs, block_expert, n_used, w_gate, b_gate, w_up, b_up, w_down, b_down):
    wspec = pl.BlockSpec((None, None, D_MODEL, D_MODEL), lambda i, be, nu: (layer, be[i], 0, 0))
    bspec = pl.BlockSpec((None, None, 1, D_MODEL), lambda i, be, nu: (layer, be[i], 0, 0))
    rows = pl.BlockSpec((MOE_BLOCK, HALF), lambda i, be, nu: (i, 0))
    bias = lambda b: b.reshape(DEPTH, N_EXPERTS, 1, D_MODEL)
    return pl.pallas_call(
        _moe_kernel,
        grid_spec=pltpu.PrefetchScalarGridSpec(
            num_scalar_prefetch=2,
            grid=(MOE_BLOCKS,),
            in_specs=[rows, wspec, bspec, wspec, bspec, wspec, bspec],
            out_specs=rows,
            scratch_shapes=[pltpu.VMEM((D_MODEL, D_MODEL), BF16)] * 3,
        ),
        out_shape=jax.ShapeDtypeStruct((N_SLOTS, HALF), U32),
        compiler_params=pltpu.CompilerParams(dimension_semantics=("arbitrary",),
                                             vmem_limit_bytes=VMEM_LIMIT),
        name="moe_experts",
    )(block_expert, n_used, xs, w_gate, bias(b_gate), w_up, bias(b_up), w_down, bias(b_down))


def _moe(layer, hp, eidx, cnt, tri, w_gate, b_gate, w_up, b_up, w_down, b_down):
    tile_cnt = cnt[::SUBLANES, :N_EXPERTS].astype(I32)
    counts = jnp.sum(tile_cnt, axis=0)
    padded = (counts + MOE_BLOCK - 1) // MOE_BLOCK * MOE_BLOCK
    pad_ends = jnp.cumsum(padded)
    pad_starts = pad_ends - padded
    tile_base = pad_starts[None, :] + jnp.cumsum(tile_cnt, axis=0) - tile_cnt
    tile_base = jnp.pad(tile_base.astype(F32), ((0, 0), (0, LANES - N_EXPERTS)))
    tile_base = jnp.repeat(tile_base, SUBLANES, axis=0)
    block_start = jnp.arange(MOE_BLOCKS, dtype=I32)[:, None] * MOE_BLOCK
    block_expert = jnp.minimum(jnp.sum((pad_ends[None, :] <= block_start).astype(I32), axis=1),
                               N_EXPERTS - 1)
    n_used = (pad_ends[-1:] // MOE_BLOCK).astype(I32)

    slots = _slots(eidx, tile_base, tri)[:, :TOP_K, :]
    per_tile = TM // SC_ROWS
    d_idx = slots.reshape(N_TILES, TOP_K, per_tile, SC_ROWS).transpose(0, 2, 1, 3)
    d_idx = d_idx.reshape(SC_WORKERS, N_TOK // SC_WORKERS // SC_ROWS, TOP_K, SC_ROWS)
    c_idx = slots.transpose(1, 0, 2).reshape(SC_WORKERS, N_ASSIGN // SC_WORKERS // SC_ROWS, SC_ROWS)

    xs = _dispatch(hp, d_idx)
    ys = _moe_experts(layer, xs, block_expert, n_used, w_gate, b_gate, w_up, b_up, w_down, b_down)
    return _combine_gather(ys, c_idx).reshape(TOP_K, N_TOK, HALF)


def _final_kernel(x_ref, *refs):
    o_ref = refs[N_PREV]
    o_ref[...] = _apply_prev(x_ref[...], refs[:N_PREV])


def _final_residual(x, prev, mods, first_tile, n_tiles):
    tile_fn = lambda t: first_tile + t
    yg, gates = prev
    return pl.pallas_call(
        _final_kernel,
        grid=(n_tiles,),
        in_specs=[pl.BlockSpec((TM, D_MODEL), lambda t: (tile_fn(t), 0))] + _prev_specs(DEPTH, tile_fn),
        out_specs=pl.BlockSpec((TM, D_MODEL), lambda t: (t, 0)),
        out_shape=jax.ShapeDtypeStruct((n_tiles * TM, D_MODEL), F32),
        compiler_params=pltpu.CompilerParams(dimension_semantics=("arbitrary",)),
        name="final_residual",
    )(x, *([yg] * TOP_K), gates, mods)


def _dft_pair(n):
    idx = np.arange(n)
    ang = 2.0 * np.pi * ((idx[:, None] * idx[None, :]) % n) / n
    return np.cos(ang) / np.sqrt(n), np.sin(ang) / np.sqrt(n)


def _rope_tables():
    lane = np.arange(GROUP)
    within = lane % QK_DIM
    axis = within // 32
    e = within % 32
    inv16 = ROPE_BASE ** (-jnp.arange(16, dtype=F32) / 16)
    pos = np.arange(LAT_SEQ)
    coord = np.where(axis[None, :] == 0, (pos // GRID_W)[:, None], (pos % GRID_W)[:, None])
    ang = jnp.asarray(coord, F32) * inv16[e % 16][None, :]
    first = jnp.asarray((e // 16) == 0)[None, :]
    cos, sin = jnp.cos(ang), jnp.sin(ang)
    return cos, jnp.where(first, -sin, 0.0), jnp.where(first, 0.0, sin)


def kernel(x_prompt, x_sample, cache_k, cache_v, c, c_ctx, ada_w, ada_b, norm_mix_g, norm_ffn_g,
           w_in_even, w_out_even, q_norm_g, k_norm_g, lambda_q, lambda_k, subln_g,
           w_in_odd, v_norm_g, w_spatial, b_spatial, w_out_odd,
           router_w, router_b, w_gate, b_gate, w_up, b_up, w_down, b_down):
    n_even = w_in_even.shape[0]
    x = jnp.concatenate([x_prompt.reshape(N_CTX, D_MODEL), x_sample.reshape(N_LAT, D_MODEL)], axis=0)
    cond = jnp.zeros((COND_ROWS, D_MODEL), F32).at[0].set(c_ctx).at[1:1 + LAT_BATCH].set(c)
    mods = _ada_modulation(cond, ada_w, ada_b).reshape(DEPTH, N_MOD, COND_ROWS, 1, D_MODEL)

    dft_ctx = [jnp.asarray(m, F32).astype(BF16) for m in _dft_pair(CTX_SEQ)]
    dft_lat = [jnp.asarray(m, F32).astype(BF16) for m in _dft_pair(LAT_SEQ)]
    c128, s128 = _dft_pair(GROUP)
    cs128 = jnp.asarray(np.concatenate([c128, -s128], axis=0), F32).astype(BF16)
    grp = np.arange(HALF) // QK_DIM
    blockdiag = jnp.asarray(grp[:, None] == grp[None, :], F32).astype(BF16)
    tri = jnp.asarray(np.arange(TM)[:, None] > np.arange(TM)[None, :], F32).astype(BF16)
    rope_tabs = _rope_tables()
    cache_k2 = cache_k.reshape(LAT_BATCH, n_even, PAST_LEN, HALF)
    cache_v2 = cache_v.reshape(LAT_BATCH, n_even, PAST_LEN, HALF)

    rw = jnp.pad(router_w, ((0, 0), (0, 0), (0, LANES - N_EXPERTS)))
    rw_hi = rw.astype(BF16)
    rw_lo = (rw - rw_hi.astype(F32)).astype(BF16)
    rb = jnp.pad(router_b, ((0, 0), (0, LANES - N_EXPERTS)))[:, None, :]

    prev = None
    new_k, new_v = [], []
    for l in range(DEPTH):
        j = l // 2
        gmix = norm_mix_g[l][None, :]
        gffn = norm_ffn_g[l][None, :]
        if l % 2 == 0:
            x, (a, q, k, v, kf, vf) = _even_proj(
                l, x, prev, mods, gmix, w_in_even[j].astype(BF16), blockdiag,
                jnp.tile(q_norm_g[j], HALF // QK_DIM)[None, :], jnp.tile(k_norm_g[j], HALF // QK_DIM)[None, :],
                rope_tabs)
            new_k.append(kf.reshape(CTX_BATCH, CTX_SEQ, N_GROUPS, 2, QK_DIM))
            new_v.append(vf.reshape(CTX_BATCH, CTX_SEQ, N_GROUPS, GROUP))
            common = (cs128, lambda_q[j], lambda_k[j], subln_g[j][None, :], w_out_even[j].astype(BF16),
                      mods, gffn, rw_hi[l], rw_lo[l], rb[l])
            x, hp, gates, eidx, cnt = _even_mix(l, x, a, q, k, v, cache_k2, cache_v2, dft_ctx, dft_lat, *common)
        else:
            b_s = jnp.broadcast_to(jnp.transpose(b_spatial[j])[:, :, None],
                                   (CHUNK, C_GROUPS, GROUP)).reshape(CHUNK, D_MODEL)
            x, hp, gates, eidx, cnt = _odd_layer(
                l, x, prev, mods, gmix, w_in_odd[j].astype(BF16), v_norm_g[j][None, :],
                w_spatial[j].astype(BF16), b_s, w_out_odd[j].astype(BF16), gffn, rw_hi[l], rw_lo[l], rb[l])
        yg = _moe(l, hp, eidx, cnt, tri, w_gate, b_gate, w_up, b_up, w_down, b_down)
        prev = (yg, gates)
    y_ctx = _final_residual(x, prev, mods, 0, CTX_TILES)
    y_lat = _final_residual(x, prev, mods, CTX_TILES, N_TILES - CTX_TILES)
    return (y_ctx.reshape(CTX_BATCH, CTX_SEQ, D_MODEL),
            y_lat.reshape(LAT_BATCH, LAT_SEQ, D_MODEL),
            jnp.stack(new_k, axis=1),
            jnp.stack(new_v, axis=1))
```

```python
import functools
import math

import numpy as np
import jax
import jax.numpy as jnp
from jax import lax
from jax.experimental import pallas as pl
from jax.experimental.pallas import tpu as pltpu
from jax.experimental.pallas import tpu_sc as plsc

F32 = jnp.float32
BF16 = jnp.bfloat16
U32 = jnp.uint32
I32 = jnp.int32

D_MODEL = 1024
DEPTH = 4
N_MOD = 6
EPS = 1e-6
CTX_BATCH, CTX_SEQ = 32, 256
LAT_BATCH, LAT_SEQ = 8, 1024
PAST_LEN = 512
GRID_W = 64
N_CTX = CTX_BATCH * CTX_SEQ
N_LAT = LAT_BATCH * LAT_SEQ
N_TOK = N_CTX + N_LAT
TM = 256
N_TILES = N_TOK // TM
CTX_TILES = N_CTX // TM
LAT_TILES_PER_BATCH = LAT_SEQ // TM
COND_ROWS = 16
GROUP = 128
N_GROUPS = 4
QK_DIM = 64
HALF = 512
CHUNK = 128
C_GROUPS = 8
N_EXPERTS = 32
TOP_K = 4
LANES = 128
SUBLANES = 8
SWIGLU_LIMIT = 7.0
SWIGLU_ALPHA = 1.702
MOE_BLOCK = 256
N_ASSIGN = N_TOK * TOP_K
MOE_BLOCKS = N_ASSIGN // MOE_BLOCK + N_EXPERTS
N_SLOTS = MOE_BLOCKS * MOE_BLOCK
ROPE_BASE = 10000.0
VMEM_LIMIT = 52 * 1024 * 1024
SC_CORES = 2
SC_SUBCORES = 16
SC_WORKERS = SC_CORES * SC_SUBCORES
SC_ROWS = 64
SC_BUFS = 2
HI_MASK = 0xFFFF0000
LOG2E = math.log2(math.e)


def _dot(a, b):
    return jnp.dot(a, b, preferred_element_type=F32)


def _rms(x, g):
    return x * lax.rsqrt(jnp.mean(x * x, axis=-1, keepdims=True) + EPS) * g


def _split_bf16(x):
    hi = x.astype(BF16)
    lo = (x - hi.astype(F32)).astype(BF16)
    return hi, lo


def _pack_halves(xf):
    b = lax.bitcast_convert_type(xf, U32)
    return (b[:, :HALF] >> 16) | (b[:, HALF:] & jnp.uint32(HI_MASK))


def _unpack_halves(w):
    lo = lax.bitcast_convert_type(w << 16, F32)
    hi = lax.bitcast_convert_type(w & jnp.uint32(HI_MASK), F32)
    return lo, hi


def _tile_mod_row(t):
    return jnp.where(t < CTX_TILES, 0, 1 + (t - CTX_TILES) // LAT_TILES_PER_BATCH)


def _mod_spec(layer, k, row_fn):
    return pl.BlockSpec((None, None, None, 1, D_MODEL),
                        lambda *g: (layer, k, row_fn(*g), 0, 0))


def _full_spec(shape):
    return pl.BlockSpec(shape, lambda *g: (0,) * len(shape))


def _ada_kernel(cond_ref, w_ref, b_ref, o_ref):
    c = cond_ref[...]
    s = (c * jax.nn.sigmoid(c)).astype(BF16)
    o_ref[...] = _dot(s, w_ref[...].astype(BF16)) + b_ref[...]


def _ada_modulation(cond, ada_w, ada_b):
    return pl.pallas_call(
        _ada_kernel,
        grid=(DEPTH, N_MOD),
        in_specs=[
            _full_spec((COND_ROWS, D_MODEL)),
            pl.BlockSpec((None, D_MODEL, D_MODEL), lambda l, n: (l, 0, n)),
            pl.BlockSpec((None, None, 1, D_MODEL), lambda l, n: (l, n, 0, 0)),
        ],
        out_specs=pl.BlockSpec((None, None, COND_ROWS, D_MODEL), lambda l, n: (l, n, 0, 0)),
        out_shape=jax.ShapeDtypeStruct((DEPTH, N_MOD, COND_ROWS, D_MODEL), F32),
        compiler_params=pltpu.CompilerParams(dimension_semantics=("arbitrary", "arbitrary"),
                                             vmem_limit_bytes=VMEM_LIMIT),
        name="ada_modulation",
    )(cond, ada_w, ada_b.reshape(DEPTH, N_MOD, 1, D_MODEL))


N_PREV = 6


def _prev_specs(layer, tile_fn):
    ys = [pl.BlockSpec((None, TM, HALF), functools.partial(lambda k, *g: (k, tile_fn(*g), 0), k))
          for k in range(TOP_K)]
    return ys + [pl.BlockSpec((TM, LANES), lambda *g: (tile_fn(*g), 0)),
                 _mod_spec(layer - 1, 5, lambda *g: _tile_mod_row(tile_fn(*g)))]


def _apply_prev(x, prev_refs):
    y_refs, gt_ref, gate_ref = prev_refs[:TOP_K], prev_refs[TOP_K], prev_refs[TOP_K + 1]
    gt = gt_ref[...]
    acc_lo = acc_hi = None
    for k in range(TOP_K):
        lo, hi = _unpack_halves(y_refs[k][...])
        g = gt[:, k:k + 1]
        acc_lo = g * lo if acc_lo is None else acc_lo + g * lo
        acc_hi = g * hi if acc_hi is None else acc_hi + g * hi
    return x + gate_ref[...] * jnp.concatenate([acc_lo, acc_hi], axis=1)


N_ROUTE_OUT = 5


def _route_out_specs(tile_fn):
    tile = lambda w: pl.BlockSpec((TM, w), lambda *g: (tile_fn(*g), 0))
    return [tile(D_MODEL), tile(HALF), tile(LANES), tile(LANES),
            pl.BlockSpec((SUBLANES, LANES), lambda *g: (tile_fn(*g), 0))]


_ROUTE_OUT_SHAPES = [
    jax.ShapeDtypeStruct((N_TOK, D_MODEL), F32),
    jax.ShapeDtypeStruct((N_TOK, HALF), U32),
    jax.ShapeDtypeStruct((N_TOK, LANES), F32),
    jax.ShapeDtypeStruct((N_TOK, LANES), I32),
    jax.ShapeDtypeStruct((N_TILES * SUBLANES, LANES), F32),
]


def _post_mixer(x, mix, gate_ref, gffn_ref, shift_ref, scale_ref, rwhi_ref, rwlo_ref, rb_ref,
                xo_ref, hp_ref, gt_ref, ei_ref, cnt_ref):
    xn = x + gate_ref[...] * mix
    xo_ref[...] = xn
    h2 = _rms(xn, gffn_ref[...]) * (1.0 + scale_ref[...]) + shift_ref[...]
    hi = h2.astype(BF16)
    hif = hi.astype(F32)
    lo = (h2 - hif).astype(BF16)
    hp_ref[...] = _pack_halves(hif)
    rwhi = rwhi_ref[...]
    logits = _dot(hi, rwhi) + _dot(lo, rwhi) + _dot(hi, rwlo_ref[...]) + rb_ref[...]

    lane = lax.broadcasted_iota(I32, (TM, LANES), 1)
    lane_f = lane.astype(F32)
    work = jnp.where(lane < N_EXPERTS, logits, -jnp.inf)
    member = jnp.zeros((TM, LANES), F32)
    gates = jnp.zeros((TM, LANES), F32)
    ids = jnp.zeros((TM, LANES), F32)
    den = None
    top = None
    for k in range(TOP_K):
        m = jnp.max(work, axis=1, keepdims=True)
        idx = jnp.min(jnp.where(work == m, lane_f, float(LANES)), axis=1, keepdims=True)
        onehot = lane_f == idx
        work = jnp.where(onehot, -jnp.inf, work)
        member = member + onehot.astype(F32)
        if k == 0:
            top = m
        e = jnp.exp(m - top)
        den = e if den is None else den + e
        gates = gates + jnp.where(lane == k, e, 0.0)
        ids = ids + jnp.where(lane == k, idx, 0.0)
    gt_ref[...] = gates * (1.0 / den)
    ei_ref[...] = ids.astype(I32)
    cnt_ref[...] = jnp.broadcast_to(jnp.sum(member, axis=0, keepdims=True), (SUBLANES, LANES))


def _rope(x, c, sa, sb):
    outs = []
    for h in range(N_GROUPS):
        xs = x[:, GROUP * h:GROUP * (h + 1)]
        outs.append(xs * c + pltpu.roll(xs, GROUP - 16, 1) * sa + pltpu.roll(xs, 16, 1) * sb)
    return jnp.concatenate(outs, axis=1)


def _even_proj_kernel(has_prev, *refs):
    x_ref = refs[0]
    refs = refs[1:]
    if has_prev:
        prev_refs = refs[:N_PREV]
        refs = refs[N_PREV:]
    (g_ref, shift_ref, scale_ref, w_ref, bd_ref, qg_ref, kg_ref, cos_ref, sa_ref, sb_ref) = refs[:10]
    outs = refs[10:]
    if has_prev:
        xo_ref = outs[0]
        outs = outs[1:]
    a_ref, q_ref, k_ref, v_ref, kf_ref, vf_ref = outs

    t = pl.program_id(0)
    x = x_ref[...]
    if has_prev:
        x = _apply_prev(x, prev_refs)
        xo_ref[...] = x
    h = (_rms(x, g_ref[...]) * (1.0 + scale_ref[...]) + shift_ref[...]).astype(BF16)

    a_ref[...] = _dot(h, w_ref[:, 0:HALF]).astype(BF16)
    zv = _dot(h, w_ref[:, 3 * HALF:4 * HALF])
    v_ref[...] = zv.astype(BF16)

    bd = bd_ref[...]

    def qk_norm(z, gain):
        shi, slo = _split_bf16(z * z)
        ssq = _dot(shi, bd) + _dot(slo, bd)
        return z * lax.rsqrt(ssq * (1.0 / QK_DIM) + EPS) * gain

    qn = qk_norm(_dot(h, w_ref[:, HALF:2 * HALF]), qg_ref[...]) * (QK_DIM ** -0.5 * LOG2E)
    kn = qk_norm(_dot(h, w_ref[:, 2 * HALF:3 * HALF]), kg_ref[...])

    @pl.when(t < CTX_TILES)
    def _():
        kf_ref[...] = kn
        vf_ref[...] = zv
        q_ref[...] = qn.astype(BF16)
        k_ref[...] = kn.astype(BF16)

    @pl.when(t >= CTX_TILES)
    def _():
        c, sa, sb = cos_ref[...], sa_ref[...], sb_ref[...]
        q_ref[...] = _rope(qn, c, sa, sb).astype(BF16)
        k_ref[...] = _rope(kn, c, sa, sb).astype(BF16)


def _even_proj(layer, x, prev, mods, norm_g, w_in, blockdiag, qg, kg, rope_tabs):
    has_prev = prev is not None
    tile = lambda w: pl.BlockSpec((TM, w), lambda t: (t, 0))
    rope_spec = pl.BlockSpec(
        (TM, GROUP), lambda t: (jnp.where(t < CTX_TILES, 0, (t - CTX_TILES) % LAT_TILES_PER_BATCH), 0))
    in_specs = [tile(D_MODEL)]
    args = [x]
    if has_prev:
        yg, gates = prev
        in_specs += _prev_specs(layer, lambda t: t)
        args += [yg] * TOP_K + [gates, mods]
    in_specs += [
        _full_spec((1, D_MODEL)),
        _mod_spec(layer, 0, _tile_mod_row), _mod_spec(layer, 1, _tile_mod_row),
        _full_spec((D_MODEL, 4 * HALF)), _full_spec((HALF, HALF)),
        _full_spec((1, HALF)), _full_spec((1, HALF)),
        rope_spec, rope_spec, rope_spec,
    ]
    args += [norm_g, mods, mods, w_in, blockdiag, qg, kg, *rope_tabs]
    cache_tile = pl.BlockSpec((TM, HALF), lambda t: (jnp.minimum(t, CTX_TILES - 1), 0))
    out_specs = [tile(HALF)] * 4 + [cache_tile] * 2
    out_shape = [jax.ShapeDtypeStruct((N_TOK, HALF), BF16)] * 4 + \
                [jax.ShapeDtypeStruct((N_CTX, HALF), F32)] * 2
    if has_prev:
        out_specs = [tile(D_MODEL)] + out_specs
        out_shape = [jax.ShapeDtypeStruct((N_TOK, D_MODEL), F32)] + out_shape
    res = pl.pallas_call(
        functools.partial(_even_proj_kernel, has_prev),
        grid=(N_TILES,),
        in_specs=in_specs, out_specs=out_specs, out_shape=out_shape,
        compiler_params=pltpu.CompilerParams(dimension_semantics=("arbitrary",),
                                             vmem_limit_bytes=VMEM_LIMIT),
        name="even_proj",
    )(*args)
    if has_prev:
        return res[0], res[1:]
    return x, res


def _even_mix_kernel(lambda_init, *refs):
    (x_ref, q_ref, ac_ref, kc_ref, vc_ref, al_ref, kl_ref, vl_ref, ck_ref, cv_ref,
     cnc_ref, snc_ref, cnl_ref, snl_ref) = refs[:14]
    shared = refs[14:]
    t = pl.program_id(0)

    @pl.when(t < CTX_TILES)
    def _():
        _even_mix_body(lambda_init, x_ref, q_ref, ac_ref, [(kc_ref, vc_ref, False)],
                       cnc_ref, snc_ref, *shared)

    @pl.when(t >= CTX_TILES)
    def _():
        _even_mix_body(lambda_init, x_ref, q_ref, al_ref, [(kl_ref, vl_ref, False), (ck_ref, cv_ref, True)],
                       cnl_ref, snl_ref, *shared)


def _even_mix_body(lambda_init, x_ref, q_ref, a_ref, kv_refs, cn_ref, sn_ref, *refs):
    (cs_ref, lq_ref, lk_ref, sg_ref, wo_ref,
     gate_ref, gffn_ref, shift_ref, scale_ref, rwhi_ref, rwlo_ref, rb_ref) = refs[:12]
    out_refs = refs[12:]

    a = a_ref[...]
    y1 = _dot(cn_ref[...], a).astype(BF16)
    y2 = _dot(sn_ref[...], a).astype(BF16)
    cs = cs_ref[...]
    pieces = []
    for g in range(N_GROUPS):
        sl = slice(GROUP * g, GROUP * (g + 1))
        pieces.append(_dot(jnp.concatenate([y1[:, sl], y2[:, sl]], axis=1), cs).astype(BF16))

    d = jnp.sum(lq_ref[...] * lk_ref[...], axis=1, keepdims=True)
    ed = jnp.exp(d)
    lam = ed[0:1, :] - ed[1:2, :] + lambda_init
    lane = lax.broadcasted_iota(I32, (TM, GROUP), 1)
    nt = (((1,), (1,)), ((), ()))
    for hd in range(N_GROUPS):
        sl = slice(GROUP * hd, GROUP * (hd + 1))
        qf = q_ref[:, sl].astype(F32)
        parts = []
        for k_ref, v_ref, is_f32 in kv_refs:
            kk, vv = k_ref[:, sl], v_ref[:, sl]
            parts.append((kk.astype(BF16), vv.astype(BF16)) if is_f32 else (kk, vv))

        def probs(qm):
            ss = [lax.dot_general(qm, kk, nt, preferred_element_type=F32) for kk, _ in parts]
            m = functools.reduce(jnp.maximum, [jnp.max(s, axis=1, keepdims=True) for s in ss])
            es = [jnp.exp2(s - m) for s in ss]
            den = functools.reduce(lambda u, w: u + w, [jnp.sum(e, axis=1, keepdims=True) for e in es])
            return es, 1.0 / den

        es0, inv0 = probs(jnp.where(lane < QK_DIM, qf, 0.0).astype(BF16))
        es1, inv1 = probs(jnp.where(lane >= QK_DIM, qf, 0.0).astype(BF16))
        o0 = functools.reduce(lambda u, w: u + w,
                              [_dot(e.astype(BF16), vv) for e, (_, vv) in zip(es0, parts)])
        o1 = functools.reduce(lambda u, w: u + w,
                              [_dot(e.astype(BF16), vv) for e, (_, vv) in zip(es1, parts)])
        o = o0 * inv0 - o1 * (lam * inv1)
        pieces.append((_rms(o, sg_ref[...]) * (1.0 - lambda_init)).astype(BF16))

    mix = _dot(jnp.concatenate(pieces, axis=1), wo_ref[...])
    _post_mixer(x_ref[...], mix, gate_ref, gffn_ref, shift_ref, scale_ref, rwhi_ref, rwlo_ref, rb_ref,
                *out_refs)


def _even_mix(layer, x, a, q, k, v, cache_k, cache_v, dft_ctx, dft_lat, cs128, lam_q, lam_k, subln_g,
              w_out, mods, gffn, rw_hi, rw_lo, rb):
    j = layer // 2
    lambda_init = 0.8 - 0.6 * math.exp(-0.3 * layer)
    lat = lambda t: jnp.maximum(t - CTX_TILES, 0)
    tile = lambda w: pl.BlockSpec((TM, w), lambda t: (t, 0))
    ctx_seq = pl.BlockSpec((CTX_SEQ, HALF), lambda t: (jnp.minimum(t, CTX_TILES - 1), 0))
    lat_seq = pl.BlockSpec((LAT_SEQ, HALF),
                           lambda t: (N_CTX // LAT_SEQ + lat(t) // LAT_TILES_PER_BATCH, 0))
    cache = pl.BlockSpec((None, None, PAST_LEN, HALF),
                         lambda t: (lat(t) // LAT_TILES_PER_BATCH, j, 0, 0))
    dft_lat_spec = pl.BlockSpec((TM, LAT_SEQ), lambda t: (lat(t) % LAT_TILES_PER_BATCH, 0))
    in_specs = [
        tile(D_MODEL), tile(HALF), ctx_seq, ctx_seq, ctx_seq, lat_seq, lat_seq, lat_seq, cache, cache,
        _full_spec((CTX_SEQ, CTX_SEQ)), _full_spec((CTX_SEQ, CTX_SEQ)), dft_lat_spec, dft_lat_spec,
        _full_spec((2 * GROUP, GROUP)),
        _full_spec((2, QK_DIM)), _full_spec((2, QK_DIM)), _full_spec((1, GROUP)),
        _full_spec((D_MODEL, D_MODEL)),
        _mod_spec(layer, 2, _tile_mod_row), _full_spec((1, D_MODEL)),
        _mod_spec(layer, 3, _tile_mod_row), _mod_spec(layer, 4, _tile_mod_row),
        _full_spec((D_MODEL, LANES)), _full_spec((D_MODEL, LANES)), _full_spec((1, LANES)),
    ]
    args = [x, q, a, k, v, a, k, v, cache_k, cache_v, dft_ctx[0], dft_ctx[1], dft_lat[0], dft_lat[1],
            cs128, lam_q, lam_k, subln_g, w_out, mods, gffn, mods, mods, rw_hi, rw_lo, rb]
    return pl.pallas_call(
        functools.partial(_even_mix_kernel, lambda_init),
        grid=(N_TILES,),
        in_specs=in_specs,
        out_specs=_route_out_specs(lambda t: t),
        out_shape=_ROUTE_OUT_SHAPES,
        compiler_params=pltpu.CompilerParams(dimension_semantics=("arbitrary",),
                                             vmem_limit_bytes=VMEM_LIMIT),
        name="even_mix",
    )(*args)


def _odd_kernel(has_prev, *refs):
    x_ref = refs[0]
    refs = refs[1:]
    if has_prev:
        prev_refs = refs[:N_PREV]
        refs = refs[N_PREV:]
    (g_ref, shift_ref, scale_ref, w_ref, vg_ref, ws_ref, bs_ref, wo_ref,
     gate_ref, gffn_ref, shift2_ref, scale2_ref, rwhi_ref, rwlo_ref, rb_ref) = refs[:15]
    out_refs = refs[15:]

    x = x_ref[...]
    if has_prev:
        x = _apply_prev(x, prev_refs)
    h = (_rms(x, g_ref[...]) * (1.0 + scale_ref[...]) + shift_ref[...]).astype(BF16)

    def gelu(z):
        return 0.5 * z * (1.0 + lax.erf(z * (2.0 ** -0.5)))

    u = gelu(_dot(h, w_ref[:, 0:D_MODEL]))
    vn = _rms(gelu(_dot(h, w_ref[:, D_MODEL:2 * D_MODEL])), vg_ref[...]).astype(BF16)
    bs = bs_ref[...]
    rows = []
    for c in range(TM // CHUNK):
        cols = []
        for g in range(C_GROUPS):
            cols.append(_dot(ws_ref[g], vn[CHUNK * c:CHUNK * (c + 1), GROUP * g:GROUP * (g + 1)]))
        rows.append(jnp.concatenate(cols, axis=1) + bs)
    sv = jnp.concatenate(rows, axis=0)
    mix = _dot((u * sv).astype(BF16), wo_ref[...])
    _post_mixer(x, mix, gate_ref, gffn_ref, shift2_ref, scale2_ref, rwhi_ref, rwlo_ref, rb_ref,
                *out_refs)


def _odd_layer(layer, x, prev, mods, norm_g, w_in, v_norm_g, w_s, b_s, w_out, gffn, rw_hi, rw_lo, rb):
    has_prev = prev is not None
    in_specs = [pl.BlockSpec((TM, D_MODEL), lambda t: (t, 0))]
    args = [x]
    if has_prev:
        yg, gates = prev
        in_specs += _prev_specs(layer, lambda t: t)
        args += [yg] * TOP_K + [gates, mods]
    in_specs += [
        _full_spec((1, D_MODEL)),
        _mod_spec(layer, 0, _tile_mod_row), _mod_spec(layer, 1, _tile_mod_row),
        _full_spec((D_MODEL, 2 * D_MODEL)), _full_spec((1, D_MODEL)),
        _full_spec((C_GROUPS, CHUNK, CHUNK)), _full_spec((CHUNK, D_MODEL)),
        _full_spec((D_MODEL, D_MODEL)),
        _mod_spec(layer, 2, _tile_mod_row), _full_spec((1, D_MODEL)),
        _mod_spec(layer, 3, _tile_mod_row), _mod_spec(layer, 4, _tile_mod_row),
        _full_spec((D_MODEL, LANES)), _full_spec((D_MODEL, LANES)), _full_spec((1, LANES)),
    ]
    args += [norm_g, mods, mods, w_in, v_norm_g, w_s, b_s, w_out, mods, gffn, mods, mods, rw_hi, rw_lo, rb]
    return pl.pallas_call(
        functools.partial(_odd_kernel, has_prev),
        grid=(N_TILES,),
        in_specs=in_specs,
        out_specs=_route_out_specs(lambda t: t),
        out_shape=_ROUTE_OUT_SHAPES,
        compiler_params=pltpu.CompilerParams(dimension_semantics=("arbitrary",),
                                             vmem_limit_bytes=VMEM_LIMIT),
        name="odd_layer",
    )(*args)


def _slot_kernel(ei_ref, base_ref, tri_ref, o_ref):
    lane = lax.broadcasted_iota(I32, (TM, LANES), 1)
    ei = ei_ref[...]
    onehots = [lane == ei[:, k:k + 1] for k in range(TOP_K)]
    member = functools.reduce(lambda u, w: u + w, [o.astype(F32) for o in onehots])
    before = _dot(tri_ref[...], member.astype(BF16)) + base_ref[0:1, :]
    slots = jnp.zeros((TM, LANES), F32)
    for k in range(TOP_K):
        s = jnp.sum(jnp.where(onehots[k], before, 0.0), axis=1, keepdims=True)
        slots = slots + jnp.where(lane == k, s, 0.0)
    o_ref[...] = slots.T[0:SUBLANES, :].astype(I32)


def _slots(eidx, tile_base, tri):
    return pl.pallas_call(
        _slot_kernel,
        grid=(N_TILES,),
        in_specs=[pl.BlockSpec((TM, LANES), lambda t: (t, 0)),
                  pl.BlockSpec((SUBLANES, LANES), lambda t: (t, 0)),
                  _full_spec((TM, TM))],
        out_specs=pl.BlockSpec((None, SUBLANES, TM), lambda t: (t, 0, 0)),
        out_shape=jax.ShapeDtypeStruct((N_TILES, SUBLANES, TM), I32),
        compiler_params=pltpu.CompilerParams(dimension_semantics=("arbitrary",)),
        name="moe_slots",
    )(eidx, tile_base, tri)


def _sc_mesh():
    return plsc.VectorSubcoreMesh(core_axis_name="c", subcore_axis_name="s")


def _sc_worker():
    return lax.axis_index("s") * SC_CORES + lax.axis_index("c")


def _dispatch(hp, idx):
    n_chunks = N_TOK // SC_WORKERS // SC_ROWS

    @functools.partial(
        pl.kernel, mesh=_sc_mesh(),
        out_type=jax.ShapeDtypeStruct((N_SLOTS, HALF), U32),
        scratch_types=[pltpu.VMEM((n_chunks, TOP_K, SC_ROWS), I32), pltpu.VMEM((SC_BUFS, SC_ROWS, HALF), U32),
                       pltpu.SemaphoreType.DMA((SC_BUFS,)), pltpu.SemaphoreType.DMA((SC_BUFS,))],
        name="moe_dispatch",
    )
    def k(x_hbm, idx_hbm, out_hbm, idx_v, rows_v, read_sem, scat_sem):
        wid = _sc_worker()
        base = wid * n_chunks
        pltpu.sync_copy(idx_hbm.at[wid], idx_v)

        def read(j, b):
            return pltpu.make_async_copy(x_hbm.at[pl.ds((base + j) * SC_ROWS, SC_ROWS)], rows_v.at[b],
                                         read_sem.at[b])

        def scatter(j, b, kk):
            return pltpu.make_async_copy(rows_v.at[b], out_hbm.at[idx_v.at[j, kk]], scat_sem.at[b])

        def drain(j, b):
            for kk in range(TOP_K):
                scatter(j, b, kk).wait()

        read(0, 0).start()

        @pl.loop(0, n_chunks, step=SC_BUFS)
        def _(j):
            for b in range(SC_BUFS):
                jj = j + b
                other = (b + 1) % SC_BUFS
                read(jj, b).wait()

                @pl.when(jj >= 1)
                def _():
                    drain(jj - 1, other)

                @pl.when(jj + 1 < n_chunks)
                def _():
                    read(jj + 1, other).start()

                for kk in range(TOP_K):
                    scatter(jj, b, kk).start()

        drain(n_chunks - 1, (n_chunks - 1) % SC_BUFS)

    return k(hp, idx)


def _combine_gather(ys, idx):
    n_chunks = N_ASSIGN // SC_WORKERS // SC_ROWS

    @functools.partial(
        pl.kernel, mesh=_sc_mesh(),
        out_type=jax.ShapeDtypeStruct((N_ASSIGN, HALF), U32),
        scratch_types=[pltpu.VMEM((n_chunks, SC_ROWS), I32), pltpu.VMEM((SC_BUFS, SC_ROWS, HALF), U32),
                       pltpu.SemaphoreType.DMA((SC_BUFS,)), pltpu.SemaphoreType.DMA((SC_BUFS,))],
        name="moe_combine",
    )
    def k(ys_hbm, idx_hbm, out_hbm, idx_v, rows_v, gather_sem, write_sem):
        wid = _sc_worker()
        base = wid * n_chunks
        pltpu.sync_copy(idx_hbm.at[wid], idx_v)

        def gather(j, b):
            return pltpu.make_async_copy(ys_hbm.at[idx_v.at[j]], rows_v.at[b], gather_sem.at[b])

        def write(j, b):
            return pltpu.make_async_copy(rows_v.at[b], out_hbm.at[pl.ds((base + j) * SC_ROWS, SC_ROWS)],
                                         write_sem.at[b])

        gather(0, 0).start()

        @pl.loop(0, n_chunks, step=SC_BUFS)
        def _(j):
            for b in range(SC_BUFS):
                jj = j + b
                other = (b + 1) % SC_BUFS
                gather(jj, b).wait()

                @pl.when(jj >= 1)
                def _():
                    write(jj - 1, other).wait()

                @pl.when(jj + 1 < n_chunks)
                def _():
                    gather(jj + 1, other).start()

                write(jj, b).start()

        write(n_chunks - 1, (n_chunks - 1) % SC_BUFS).wait()

    return k(ys, idx)


N_MATS = 3


def _moe_kernel(layer, first_ref, nblk_ref, xs_hbm, wg_hbm, wu_hbm, wd_hbm, bg_ref, bu_ref, bd_ref,
                ys_hbm, wf32, wbf, xbuf, obuf, wsem, xsem, osem):
    e = pl.program_id(0)
    slot = e % 2
    w_hbm = (wg_hbm, wu_hbm, wd_hbm)
    nb = nblk_ref[e]
    row0 = first_ref[e] * MOE_BLOCK

    def w_copy(ee, s, m):
        return pltpu.make_async_copy(w_hbm[m].at[layer, ee], wf32.at[s, m], wsem.at[s, m])

    def x_copy(b, s):
        return pltpu.make_async_copy(xs_hbm.at[pl.ds(row0 + b * MOE_BLOCK, MOE_BLOCK)], xbuf.at[s], xsem.at[s])

    def o_copy(b, s):
        return pltpu.make_async_copy(obuf.at[s], ys_hbm.at[pl.ds(row0 + b * MOE_BLOCK, MOE_BLOCK)], osem.at[s])

    @pl.when(e == 0)
    def _():
        for m in range(N_MATS):
            w_copy(0, 0, m).start()

    @pl.when(nb > 0)
    def _():
        x_copy(0, 0).start()

    for m in range(N_MATS):
        w_copy(e, slot, m).wait()

    @pl.when(e + 1 < N_EXPERTS)
    def _():
        for m in range(N_MATS):
            w_copy(e + 1, 1 - slot, m).start()

    for m in range(N_MATS):
        wbf[m] = wf32[slot, m].astype(BF16)

    def block(b, carry):
        s = b % 2
        x_copy(b, s).wait()

        @pl.when(b + 1 < nb)
        def _():
            x_copy(b + 1, 1 - s).start()

        @pl.when(b >= 2)
        def _():
            o_copy(b - 2, s).wait()

        lo, hi = _unpack_halves(xbuf[s])
        xb = jnp.concatenate([lo.astype(BF16), hi.astype(BF16)], axis=1)
        gt = jnp.minimum(_dot(xb, wbf[0]) + bg_ref[...], SWIGLU_LIMIT)
        up = jnp.clip(_dot(xb, wbf[1]) + bu_ref[...], -SWIGLU_LIMIT, SWIGLU_LIMIT)
        glu = gt * jax.nn.sigmoid(SWIGLU_ALPHA * gt)
        hmid = ((up + 1.0) * glu).astype(BF16)
        out = _dot(hmid, wbf[2]) + bd_ref[...]
        obuf[s] = _pack_halves(out.astype(BF16).astype(F32))
        o_copy(b, s).start()
        return carry

    lax.fori_loop(0, nb, block, 0)

    @pl.when(nb >= 2)
    def _():
        o_copy(nb - 2, nb % 2).wait()

    @pl.when(nb >= 1)
    def _():
        o_copy(nb - 1, (nb - 1) % 2).wait()


def _moe_experts(layer, xs, first_block, n_blocks, w_gate, b_gate, w_up, b_up, w_down, b_down):
    hbm = pl.BlockSpec(memory_space=pl.ANY)
    bspec = pl.BlockSpec((None, None, 1, D_MODEL), lambda e, fb, nb: (layer, e, 0, 0))
    bias = lambda b: b.reshape(DEPTH, N_EXPERTS, 1, D_MODEL)
    return pl.pallas_call(
        functools.partial(_moe_kernel, layer),
        grid_spec=pltpu.PrefetchScalarGridSpec(
            num_scalar_prefetch=2,
            grid=(N_EXPERTS,),
            in_specs=[hbm, hbm, hbm, hbm, bspec, bspec, bspec],
            out_specs=hbm,
            scratch_shapes=[
                pltpu.VMEM((2, N_MATS, D_MODEL, D_MODEL), F32),
                pltpu.VMEM((N_MATS, D_MODEL, D_MODEL), BF16),
                pltpu.VMEM((2, MOE_BLOCK, HALF), U32),
                pltpu.VMEM((2, MOE_BLOCK, HALF), U32),
                pltpu.SemaphoreType.DMA((2, N_MATS)),
                pltpu.SemaphoreType.DMA((2,)),
                pltpu.SemaphoreType.DMA((2,)),
            ],
        ),
        out_shape=jax.ShapeDtypeStruct((N_SLOTS, HALF), U32),
        compiler_params=pltpu.CompilerParams(dimension_semantics=("arbitrary",),
                                             vmem_limit_bytes=VMEM_LIMIT),
        name="moe_experts",
    )(first_block, n_blocks, xs, w_gate, w_up, w_down, bias(b_gate), bias(b_up), bias(b_down))


def _moe(layer, hp, eidx, cnt, tri, w_gate, b_gate, w_up, b_up, w_down, b_down):
    tile_cnt = cnt[::SUBLANES, :N_EXPERTS].astype(I32)
    counts = jnp.sum(tile_cnt, axis=0)
    padded = (counts + MOE_BLOCK - 1) // MOE_BLOCK * MOE_BLOCK
    pad_ends = jnp.cumsum(padded)
    pad_starts = pad_ends - padded
    tile_base = pad_starts[None, :] + jnp.cumsum(tile_cnt, axis=0) - tile_cnt
    tile_base = jnp.pad(tile_base.astype(F32), ((0, 0), (0, LANES - N_EXPERTS)))
    tile_base = jnp.repeat(tile_base, SUBLANES, axis=0)
    first_block = (pad_starts // MOE_BLOCK).astype(I32)
    n_blocks = (padded // MOE_BLOCK).astype(I32)

    slots = _slots(eidx, tile_base, tri)[:, :TOP_K, :]
    per_tile = TM // SC_ROWS
    d_idx = slots.reshape(N_TILES, TOP_K, per_tile, SC_ROWS).transpose(0, 2, 1, 3)
    d_idx = d_idx.reshape(SC_WORKERS, N_TOK // SC_WORKERS // SC_ROWS, TOP_K, SC_ROWS)
    c_idx = slots.transpose(1, 0, 2).reshape(SC_WORKERS, N_ASSIGN // SC_WORKERS // SC_ROWS, SC_ROWS)

    xs = _dispatch(hp, d_idx)
    ys = _moe_experts(layer, xs, first_block, n_blocks, w_gate, b_gate, w_up, b_up, w_down, b_down)
    return _combine_gather(ys, c_idx).reshape(TOP_K, N_TOK, HALF)


def _final_kernel(x_ref, *refs):
    o_ref = refs[N_PREV]
    o_ref[...] = _apply_prev(x_ref[...], refs[:N_PREV])


def _final_residual(x, prev, mods, first_tile, n_tiles):
    tile_fn = lambda t: first_tile + t
    yg, gates = prev
    return pl.pallas_call(
        _final_kernel,
        grid=(n_tiles,),
        in_specs=[pl.BlockSpec((TM, D_MODEL), lambda t: (tile_fn(t), 0))] + _prev_specs(DEPTH, tile_fn),
        out_specs=pl.BlockSpec((TM, D_MODEL), lambda t: (t, 0)),
        out_shape=jax.ShapeDtypeStruct((n_tiles * TM, D_MODEL), F32),
        compiler_params=pltpu.CompilerParams(dimension_semantics=("arbitrary",)),
        name="final_residual",
    )(x, *([yg] * TOP_K), gates, mods)


def _dft_pair(n):
    idx = np.arange(n)
    ang = 2.0 * np.pi * ((idx[:, None] * idx[None, :]) % n) / n
    return np.cos(ang) / np.sqrt(n), np.sin(ang) / np.sqrt(n)


def _rope_tables():
    lane = np.arange(GROUP)
    within = lane % QK_DIM
    axis = within // 32
    e = within % 32
    inv16 = ROPE_BASE ** (-jnp.arange(16, dtype=F32) / 16)
    pos = np.arange(LAT_SEQ)
    coord = np.where(axis[None, :] == 0, (pos // GRID_W)[:, None], (pos % GRID_W)[:, None])
    ang = jnp.asarray(coord, F32) * inv16[e % 16][None, :]
    first = jnp.asarray((e // 16) == 0)[None, :]
    cos, sin = jnp.cos(ang), jnp.sin(ang)
    return cos, jnp.where(first, -sin, 0.0), jnp.where(first, 0.0, sin)


def kernel(x_prompt, x_sample, cache_k, cache_v, c, c_ctx, ada_w, ada_b, norm_mix_g, norm_ffn_g,
           w_in_even, w_out_even, q_norm_g, k_norm_g, lambda_q, lambda_k, subln_g,
           w_in_odd, v_norm_g, w_spatial, b_spatial, w_out_odd,
           router_w, router_b, w_gate, b_gate, w_up, b_up, w_down, b_down):
    n_even = w_in_even.shape[0]
    x = jnp.concatenate([x_prompt.reshape(N_CTX, D_MODEL), x_sample.reshape(N_LAT, D_MODEL)], axis=0)
    cond = jnp.zeros((COND_ROWS, D_MODEL), F32).at[0].set(c_ctx).at[1:1 + LAT_BATCH].set(c)
    mods = _ada_modulation(cond, ada_w, ada_b).reshape(DEPTH, N_MOD, COND_ROWS, 1, D_MODEL)

    dft_ctx = [jnp.asarray(m, F32).astype(BF16) for m in _dft_pair(CTX_SEQ)]
    dft_lat = [jnp.asarray(m, F32).astype(BF16) for m in _dft_pair(LAT_SEQ)]
    c128, s128 = _dft_pair(GROUP)
    cs128 = jnp.asarray(np.concatenate([c128, -s128], axis=0), F32).astype(BF16)
    grp = np.arange(HALF) // QK_DIM
    blockdiag = jnp.asarray(grp[:, None] == grp[None, :], F32).astype(BF16)
    tri = jnp.asarray(np.arange(TM)[:, None] > np.arange(TM)[None, :], F32).astype(BF16)
    rope_tabs = _rope_tables()
    cache_k2 = cache_k.reshape(LAT_BATCH, n_even, PAST_LEN, HALF)
    cache_v2 = cache_v.reshape(LAT_BATCH, n_even, PAST_LEN, HALF)

    rw = jnp.pad(router_w, ((0, 0), (0, 0), (0, LANES - N_EXPERTS)))
    rw_hi = rw.astype(BF16)
    rw_lo = (rw - rw_hi.astype(F32)).astype(BF16)
    rb = jnp.pad(router_b, ((0, 0), (0, LANES - N_EXPERTS)))[:, None, :]

    prev = None
    new_k, new_v = [], []
    for l in range(DEPTH):
        j = l // 2
        gmix = norm_mix_g[l][None, :]
        gffn = norm_ffn_g[l][None, :]
        if l % 2 == 0:
            x, (a, q, k, v, kf, vf) = _even_proj(
                l, x, prev, mods, gmix, w_in_even[j].astype(BF16), blockdiag,
                jnp.tile(q_norm_g[j], HALF // QK_DIM)[None, :], jnp.tile(k_norm_g[j], HALF // QK_DIM)[None, :],
                rope_tabs)
            new_k.append(kf.reshape(CTX_BATCH, CTX_SEQ, N_GROUPS, 2, QK_DIM))
            new_v.append(vf.reshape(CTX_BATCH, CTX_SEQ, N_GROUPS, GROUP))
            common = (cs128, lambda_q[j], lambda_k[j], subln_g[j][None, :], w_out_even[j].astype(BF16),
                      mods, gffn, rw_hi[l], rw_lo[l], rb[l])
            x, hp, gates, eidx, cnt = _even_mix(l, x, a, q, k, v, cache_k2, cache_v2, dft_ctx, dft_lat, *common)
        else:
            b_s = jnp.broadcast_to(jnp.transpose(b_spatial[j])[:, :, None],
                                   (CHUNK, C_GROUPS, GROUP)).reshape(CHUNK, D_MODEL)
            x, hp, gates, eidx, cnt = _odd_layer(
                l, x, prev, mods, gmix, w_in_odd[j].astype(BF16), v_norm_g[j][None, :],
                w_spatial[j].astype(BF16), b_s, w_out_odd[j].astype(BF16), gffn, rw_hi[l], rw_lo[l], rb[l])
        yg = _moe(l, hp, eidx, cnt, tri, w_gate, b_gate, w_up, b_up, w_down, b_down)
        prev = (yg, gates)
    y_ctx = _final_residual(x, prev, mods, 0, CTX_TILES)
    y_lat = _final_residual(x, prev, mods, CTX_TILES, N_TILES - CTX_TILES)
    return (y_ctx.reshape(CTX_BATCH, CTX_SEQ, D_MODEL),
            y_lat.reshape(LAT_BATCH, LAT_SEQ, D_MODEL),
            jnp.stack(new_k, axis=1),
            jnp.stack(new_v, axis=1))
```

```python
import functools
import math

import numpy as np
import jax
import jax.numpy as jnp
from jax import lax
from jax.experimental import pallas as pl
from jax.experimental.pallas import tpu as pltpu
from jax.experimental.pallas import tpu_sc as plsc

F32 = jnp.float32
BF16 = jnp.bfloat16
U32 = jnp.uint32
I32 = jnp.int32

D_MODEL = 1024
DEPTH = 4
N_MOD = 6
EPS = 1e-6
CTX_BATCH, CTX_SEQ = 32, 256
LAT_BATCH, LAT_SEQ = 8, 1024
PAST_LEN = 512
GRID_W = 64
N_CTX = CTX_BATCH * CTX_SEQ
N_LAT = LAT_BATCH * LAT_SEQ
N_TOK = N_CTX + N_LAT
TM = 256
N_TILES = N_TOK // TM
CTX_TILES = N_CTX // TM
LAT_TILES_PER_BATCH = LAT_SEQ // TM
COND_ROWS = 16
GROUP = 128
N_GROUPS = 4
QK_DIM = 64
HALF = 512
CHUNK = 128
C_GROUPS = 8
N_EXPERTS = 32
TOP_K = 4
LANES = 128
SUBLANES = 8
SWIGLU_LIMIT = 7.0
SWIGLU_ALPHA = 1.702
MOE_BLOCK = 256
BLOCKS_PER_CHUNK = 2
MOE_CHUNK = BLOCKS_PER_CHUNK * MOE_BLOCK
N_ASSIGN = N_TOK * TOP_K
MOE_BLOCKS = N_ASSIGN // MOE_BLOCK + N_EXPERTS
N_SLOTS = MOE_BLOCKS * MOE_BLOCK
ROPE_BASE = 10000.0
VMEM_LIMIT = 52 * 1024 * 1024
SC_CORES = 2
SC_SUBCORES = 16
SC_WORKERS = SC_CORES * SC_SUBCORES
SC_ROWS = 64
SC_BUFS = 2
HI_MASK = 0xFFFF0000
LOG2E = math.log2(math.e)


def _dot(a, b):
    return jnp.dot(a, b, preferred_element_type=F32)


def _rms(x, g):
    return x * lax.rsqrt(jnp.mean(x * x, axis=-1, keepdims=True) + EPS) * g


def _split_bf16(x):
    hi = x.astype(BF16)
    lo = (x - hi.astype(F32)).astype(BF16)
    return hi, lo


def _pack_halves(xf):
    b = lax.bitcast_convert_type(xf, U32)
    return (b[:, :HALF] >> 16) | (b[:, HALF:] & jnp.uint32(HI_MASK))


def _unpack_halves(w):
    lo = lax.bitcast_convert_type(w << 16, F32)
    hi = lax.bitcast_convert_type(w & jnp.uint32(HI_MASK), F32)
    return lo, hi


def _tile_mod_row(t):
    return jnp.where(t < CTX_TILES, 0, 1 + (t - CTX_TILES) // LAT_TILES_PER_BATCH)


def _mod_spec(layer, k, row_fn):
    return pl.BlockSpec((None, None, None, 1, D_MODEL),
                        lambda *g: (layer, k, row_fn(*g), 0, 0))


def _full_spec(shape):
    return pl.BlockSpec(shape, lambda *g: (0,) * len(shape))


def _ada_kernel(cond_ref, w_ref, b_ref, o_ref):
    c = cond_ref[...]
    s = (c * jax.nn.sigmoid(c)).astype(BF16)
    o_ref[...] = _dot(s, w_ref[...].astype(BF16)) + b_ref[...]


def _ada_modulation(cond, ada_w, ada_b):
    return pl.pallas_call(
        _ada_kernel,
        grid=(DEPTH, N_MOD),
        in_specs=[
            _full_spec((COND_ROWS, D_MODEL)),
            pl.BlockSpec((None, D_MODEL, D_MODEL), lambda l, n: (l, 0, n)),
            pl.BlockSpec((None, None, 1, D_MODEL), lambda l, n: (l, n, 0, 0)),
        ],
        out_specs=pl.BlockSpec((None, None, COND_ROWS, D_MODEL), lambda l, n: (l, n, 0, 0)),
        out_shape=jax.ShapeDtypeStruct((DEPTH, N_MOD, COND_ROWS, D_MODEL), F32),
        compiler_params=pltpu.CompilerParams(dimension_semantics=("arbitrary", "arbitrary"),
                                             vmem_limit_bytes=VMEM_LIMIT),
        name="ada_modulation",
    )(cond, ada_w, ada_b.reshape(DEPTH, N_MOD, 1, D_MODEL))


N_PREV = 6


def _prev_specs(layer, tile_fn):
    ys = [pl.BlockSpec((None, TM, HALF), functools.partial(lambda k, *g: (k, tile_fn(*g), 0), k))
          for k in range(TOP_K)]
    return ys + [pl.BlockSpec((TM, LANES), lambda *g: (tile_fn(*g), 0)),
                 _mod_spec(layer - 1, 5, lambda *g: _tile_mod_row(tile_fn(*g)))]


def _apply_prev(x, prev_refs):
    y_refs, gt_ref, gate_ref = prev_refs[:TOP_K], prev_refs[TOP_K], prev_refs[TOP_K + 1]
    gt = gt_ref[...]
    acc_lo = acc_hi = None
    for k in range(TOP_K):
        lo, hi = _unpack_halves(y_refs[k][...])
        g = gt[:, k:k + 1]
        acc_lo = g * lo if acc_lo is None else acc_lo + g * lo
        acc_hi = g * hi if acc_hi is None else acc_hi + g * hi
    return x + gate_ref[...] * jnp.concatenate([acc_lo, acc_hi], axis=1)


N_ROUTE_OUT = 5


def _route_out_specs(tile_fn):
    tile = lambda w: pl.BlockSpec((TM, w), lambda *g: (tile_fn(*g), 0))
    return [tile(D_MODEL), tile(HALF), tile(LANES), tile(LANES),
            pl.BlockSpec((SUBLANES, LANES), lambda *g: (tile_fn(*g), 0))]


_ROUTE_OUT_SHAPES = [
    jax.ShapeDtypeStruct((N_TOK, D_MODEL), F32),
    jax.ShapeDtypeStruct((N_TOK, HALF), U32),
    jax.ShapeDtypeStruct((N_TOK, LANES), F32),
    jax.ShapeDtypeStruct((N_TOK, LANES), I32),
    jax.ShapeDtypeStruct((N_TILES * SUBLANES, LANES), F32),
]


def _post_mixer(x, mix, gate_ref, gffn_ref, shift_ref, scale_ref, rwhi_ref, rwlo_ref, rb_ref,
                xo_ref, hp_ref, gt_ref, ei_ref, cnt_ref):
    xn = x + gate_ref[...] * mix
    xo_ref[...] = xn
    h2 = _rms(xn, gffn_ref[...]) * (1.0 + scale_ref[...]) + shift_ref[...]
    hi = h2.astype(BF16)
    hif = hi.astype(F32)
    lo = (h2 - hif).astype(BF16)
    hp_ref[...] = _pack_halves(hif)
    rwhi = rwhi_ref[...]
    logits = _dot(hi, rwhi) + _dot(lo, rwhi) + _dot(hi, rwlo_ref[...]) + rb_ref[...]

    lane = lax.broadcasted_iota(I32, (TM, LANES), 1)
    lane_f = lane.astype(F32)
    work = jnp.where(lane < N_EXPERTS, logits, -jnp.inf)
    member = jnp.zeros((TM, LANES), F32)
    gates = jnp.zeros((TM, LANES), F32)
    ids = jnp.zeros((TM, LANES), F32)
    den = None
    top = None
    for k in range(TOP_K):
        m = jnp.max(work, axis=1, keepdims=True)
        idx = jnp.min(jnp.where(work == m, lane_f, float(LANES)), axis=1, keepdims=True)
        onehot = lane_f == idx
        work = jnp.where(onehot, -jnp.inf, work)
        member = member + onehot.astype(F32)
        if k == 0:
            top = m
        e = jnp.exp(m - top)
        den = e if den is None else den + e
        gates = gates + jnp.where(lane == k, e, 0.0)
        ids = ids + jnp.where(lane == k, idx, 0.0)
    gt_ref[...] = gates * (1.0 / den)
    ei_ref[...] = ids.astype(I32)
    cnt_ref[...] = jnp.broadcast_to(jnp.sum(member, axis=0, keepdims=True), (SUBLANES, LANES))


def _rope(x, c, sa, sb):
    outs = []
    for h in range(N_GROUPS):
        xs = x[:, GROUP * h:GROUP * (h + 1)]
        outs.append(xs * c + pltpu.roll(xs, GROUP - 16, 1) * sa + pltpu.roll(xs, 16, 1) * sb)
    return jnp.concatenate(outs, axis=1)


def _even_proj_kernel(has_prev, *refs):
    x_ref = refs[0]
    refs = refs[1:]
    if has_prev:
        prev_refs = refs[:N_PREV]
        refs = refs[N_PREV:]
    (g_ref, shift_ref, scale_ref, w_ref, bd_ref, qg_ref, kg_ref, cos_ref, sa_ref, sb_ref) = refs[:10]
    outs = refs[10:]
    if has_prev:
        xo_ref = outs[0]
        outs = outs[1:]
    a_ref, q_ref, k_ref, v_ref, kf_ref, vf_ref = outs

    t = pl.program_id(0)
    x = x_ref[...]
    if has_prev:
        x = _apply_prev(x, prev_refs)
        xo_ref[...] = x
    h = (_rms(x, g_ref[...]) * (1.0 + scale_ref[...]) + shift_ref[...]).astype(BF16)

    a_ref[...] = _dot(h, w_ref[:, 0:HALF]).astype(BF16)
    zv = _dot(h, w_ref[:, 3 * HALF:4 * HALF])
    v_ref[...] = zv.astype(BF16)

    bd = bd_ref[...]

    def qk_norm(z, gain):
        shi, slo = _split_bf16(z * z)
        ssq = _dot(shi, bd) + _dot(slo, bd)
        return z * lax.rsqrt(ssq * (1.0 / QK_DIM) + EPS) * gain

    qn = qk_norm(_dot(h, w_ref[:, HALF:2 * HALF]), qg_ref[...]) * (QK_DIM ** -0.5 * LOG2E)
    kn = qk_norm(_dot(h, w_ref[:, 2 * HALF:3 * HALF]), kg_ref[...])

    @pl.when(t < CTX_TILES)
    def _():
        kf_ref[...] = kn
        vf_ref[...] = zv
        q_ref[...] = qn.astype(BF16)
        k_ref[...] = kn.astype(BF16)

    @pl.when(t >= CTX_TILES)
    def _():
        c, sa, sb = cos_ref[...], sa_ref[...], sb_ref[...]
        q_ref[...] = _rope(qn, c, sa, sb).astype(BF16)
        k_ref[...] = _rope(kn, c, sa, sb).astype(BF16)


def _even_proj(layer, x, prev, mods, norm_g, w_in, blockdiag, qg, kg, rope_tabs):
    has_prev = prev is not None
    tile = lambda w: pl.BlockSpec((TM, w), lambda t: (t, 0))
    rope_spec = pl.BlockSpec(
        (TM, GROUP), lambda t: (jnp.where(t < CTX_TILES, 0, (t - CTX_TILES) % LAT_TILES_PER_BATCH), 0))
    in_specs = [tile(D_MODEL)]
    args = [x]
    if has_prev:
        yg, gates = prev
        in_specs += _prev_specs(layer, lambda t: t)
        args += [yg] * TOP_K + [gates, mods]
    in_specs += [
        _full_spec((1, D_MODEL)),
        _mod_spec(layer, 0, _tile_mod_row), _mod_spec(layer, 1, _tile_mod_row),
        _full_spec((D_MODEL, 4 * HALF)), _full_spec((HALF, HALF)),
        _full_spec((1, HALF)), _full_spec((1, HALF)),
        rope_spec, rope_spec, rope_spec,
    ]
    args += [norm_g, mods, mods, w_in, blockdiag, qg, kg, *rope_tabs]
    cache_tile = pl.BlockSpec((TM, HALF), lambda t: (jnp.minimum(t, CTX_TILES - 1), 0))
    out_specs = [tile(HALF)] * 4 + [cache_tile] * 2
    out_shape = [jax.ShapeDtypeStruct((N_TOK, HALF), BF16)] * 4 + \
                [jax.ShapeDtypeStruct((N_CTX, HALF), F32)] * 2
    if has_prev:
        out_specs = [tile(D_MODEL)] + out_specs
        out_shape = [jax.ShapeDtypeStruct((N_TOK, D_MODEL), F32)] + out_shape
    res = pl.pallas_call(
        functools.partial(_even_proj_kernel, has_prev),
        grid=(N_TILES,),
        in_specs=in_specs, out_specs=out_specs, out_shape=out_shape,
        compiler_params=pltpu.CompilerParams(dimension_semantics=("arbitrary",),
                                             vmem_limit_bytes=VMEM_LIMIT),
        name="even_proj",
    )(*args)
    if has_prev:
        return res[0], res[1:]
    return x, res


def _even_mix_kernel(lambda_init, *refs):
    (x_ref, q_ref, ac_ref, kc_ref, vc_ref, al_ref, kl_ref, vl_ref, ck_ref, cv_ref,
     cnc_ref, snc_ref, cnl_ref, snl_ref) = refs[:14]
    shared = refs[14:]
    t = pl.program_id(0)

    @pl.when(t < CTX_TILES)
    def _():
        _even_mix_body(lambda_init, x_ref, q_ref, ac_ref, [(kc_ref, vc_ref, False)],
                       cnc_ref, snc_ref, *shared)

    @pl.when(t >= CTX_TILES)
    def _():
        _even_mix_body(lambda_init, x_ref, q_ref, al_ref, [(kl_ref, vl_ref, False), (ck_ref, cv_ref, True)],
                       cnl_ref, snl_ref, *shared)


def _even_mix_body(lambda_init, x_ref, q_ref, a_ref, kv_refs, cn_ref, sn_ref, *refs):
    (cs_ref, lq_ref, lk_ref, sg_ref, wo_ref,
     gate_ref, gffn_ref, shift_ref, scale_ref, rwhi_ref, rwlo_ref, rb_ref) = refs[:12]
    out_refs = refs[12:]

    a = a_ref[...]
    y1 = _dot(cn_ref[...], a).astype(BF16)
    y2 = _dot(sn_ref[...], a).astype(BF16)
    cs = cs_ref[...]
    pieces = []
    for g in range(N_GROUPS):
        sl = slice(GROUP * g, GROUP * (g + 1))
        pieces.append(_dot(jnp.concatenate([y1[:, sl], y2[:, sl]], axis=1), cs).astype(BF16))

    d = jnp.sum(lq_ref[...] * lk_ref[...], axis=1, keepdims=True)
    ed = jnp.exp(d)
    lam = ed[0:1, :] - ed[1:2, :] + lambda_init
    lane = lax.broadcasted_iota(I32, (TM, GROUP), 1)
    nt = (((1,), (1,)), ((), ()))
    for hd in range(N_GROUPS):
        sl = slice(GROUP * hd, GROUP * (hd + 1))
        qf = q_ref[:, sl].astype(F32)
        parts = []
        for k_ref, v_ref, is_f32 in kv_refs:
            kk, vv = k_ref[:, sl], v_ref[:, sl]
            parts.append((kk.astype(BF16), vv.astype(BF16)) if is_f32 else (kk, vv))

        def probs(qm):
            ss = [lax.dot_general(qm, kk, nt, preferred_element_type=F32) for kk, _ in parts]
            m = functools.reduce(jnp.maximum, [jnp.max(s, axis=1, keepdims=True) for s in ss])
            es = [jnp.exp2(s - m) for s in ss]
            den = functools.reduce(lambda u, w: u + w, [jnp.sum(e, axis=1, keepdims=True) for e in es])
            return es, 1.0 / den

        es0, inv0 = probs(jnp.where(lane < QK_DIM, qf, 0.0).astype(BF16))
        es1, inv1 = probs(jnp.where(lane >= QK_DIM, qf, 0.0).astype(BF16))
        o0 = functools.reduce(lambda u, w: u + w,
                              [_dot(e.astype(BF16), vv) for e, (_, vv) in zip(es0, parts)])
        o1 = functools.reduce(lambda u, w: u + w,
                              [_dot(e.astype(BF16), vv) for e, (_, vv) in zip(es1, parts)])
        o = o0 * inv0 - o1 * (lam * inv1)
        pieces.append((_rms(o, sg_ref[...]) * (1.0 - lambda_init)).astype(BF16))

    mix = _dot(jnp.concatenate(pieces, axis=1), wo_ref[...])
    _post_mixer(x_ref[...], mix, gate_ref, gffn_ref, shift_ref, scale_ref, rwhi_ref, rwlo_ref, rb_ref,
                *out_refs)


def _even_mix(layer, x, a, q, k, v, cache_k, cache_v, dft_ctx, dft_lat, cs128, lam_q, lam_k, subln_g,
              w_out, mods, gffn, rw_hi, rw_lo, rb):
    j = layer // 2
    lambda_init = 0.8 - 0.6 * math.exp(-0.3 * layer)
    lat = lambda t: jnp.maximum(t - CTX_TILES, 0)
    tile = lambda w: pl.BlockSpec((TM, w), lambda t: (t, 0))
    ctx_seq = pl.BlockSpec((CTX_SEQ, HALF), lambda t: (jnp.minimum(t, CTX_TILES - 1), 0))
    lat_seq = pl.BlockSpec((LAT_SEQ, HALF),
                           lambda t: (N_CTX // LAT_SEQ + lat(t) // LAT_TILES_PER_BATCH, 0))
    cache = pl.BlockSpec((None, None, PAST_LEN, HALF),
                         lambda t: (lat(t) // LAT_TILES_PER_BATCH, j, 0, 0))
    dft_lat_spec = pl.BlockSpec((TM, LAT_SEQ), lambda t: (lat(t) % LAT_TILES_PER_BATCH, 0))
    in_specs = [
        tile(D_MODEL), tile(HALF), ctx_seq, ctx_seq, ctx_seq, lat_seq, lat_seq, lat_seq, cache, cache,
        _full_spec((CTX_SEQ, CTX_SEQ)), _full_spec((CTX_SEQ, CTX_SEQ)), dft_lat_spec, dft_lat_spec,
        _full_spec((2 * GROUP, GROUP)),
        _full_spec((2, QK_DIM)), _full_spec((2, QK_DIM)), _full_spec((1, GROUP)),
        _full_spec((D_MODEL, D_MODEL)),
        _mod_spec(layer, 2, _tile_mod_row), _full_spec((1, D_MODEL)),
        _mod_spec(layer, 3, _tile_mod_row), _mod_spec(layer, 4, _tile_mod_row),
        _full_spec((D_MODEL, LANES)), _full_spec((D_MODEL, LANES)), _full_spec((1, LANES)),
    ]
    args = [x, q, a, k, v, a, k, v, cache_k, cache_v, dft_ctx[0], dft_ctx[1], dft_lat[0], dft_lat[1],
            cs128, lam_q, lam_k, subln_g, w_out, mods, gffn, mods, mods, rw_hi, rw_lo, rb]
    return pl.pallas_call(
        functools.partial(_even_mix_kernel, lambda_init),
        grid=(N_TILES,),
        in_specs=in_specs,
        out_specs=_route_out_specs(lambda t: t),
        out_shape=_ROUTE_OUT_SHAPES,
        compiler_params=pltpu.CompilerParams(dimension_semantics=("arbitrary",),
                                             vmem_limit_bytes=VMEM_LIMIT),
        name="even_mix",
    )(*args)


def _odd_kernel(has_prev, *refs):
    x_ref = refs[0]
    refs = refs[1:]
    if has_prev:
        prev_refs = refs[:N_PREV]
        refs = refs[N_PREV:]
    (g_ref, shift_ref, scale_ref, w_ref, vg_ref, ws_ref, bs_ref, wo_ref,
     gate_ref, gffn_ref, shift2_ref, scale2_ref, rwhi_ref, rwlo_ref, rb_ref) = refs[:15]
    out_refs = refs[15:]

    x = x_ref[...]
    if has_prev:
        x = _apply_prev(x, prev_refs)
    h = (_rms(x, g_ref[...]) * (1.0 + scale_ref[...]) + shift_ref[...]).astype(BF16)

    def gelu(z):
        return 0.5 * z * (1.0 + lax.erf(z * (2.0 ** -0.5)))

    u = gelu(_dot(h, w_ref[:, 0:D_MODEL]))
    vn = _rms(gelu(_dot(h, w_ref[:, D_MODEL:2 * D_MODEL])), vg_ref[...]).astype(BF16)
    bs = bs_ref[...]
    rows = []
    for c in range(TM // CHUNK):
        cols = []
        for g in range(C_GROUPS):
            cols.append(_dot(ws_ref[g], vn[CHUNK * c:CHUNK * (c + 1), GROUP * g:GROUP * (g + 1)]))
        rows.append(jnp.concatenate(cols, axis=1) + bs)
    sv = jnp.concatenate(rows, axis=0)
    mix = _dot((u * sv).astype(BF16), wo_ref[...])
    _post_mixer(x, mix, gate_ref, gffn_ref, shift2_ref, scale2_ref, rwhi_ref, rwlo_ref, rb_ref,
                *out_refs)


def _odd_layer(layer, x, prev, mods, norm_g, w_in, v_norm_g, w_s, b_s, w_out, gffn, rw_hi, rw_lo, rb):
    has_prev = prev is not None
    in_specs = [pl.BlockSpec((TM, D_MODEL), lambda t: (t, 0))]
    args = [x]
    if has_prev:
        yg, gates = prev
        in_specs += _prev_specs(layer, lambda t: t)
        args += [yg] * TOP_K + [gates, mods]
    in_specs += [
        _full_spec((1, D_MODEL)),
        _mod_spec(layer, 0, _tile_mod_row), _mod_spec(layer, 1, _tile_mod_row),
        _full_spec((D_MODEL, 2 * D_MODEL)), _full_spec((1, D_MODEL)),
        _full_spec((C_GROUPS, CHUNK, CHUNK)), _full_spec((CHUNK, D_MODEL)),
        _full_spec((D_MODEL, D_MODEL)),
        _mod_spec(layer, 2, _tile_mod_row), _full_spec((1, D_MODEL)),
        _mod_spec(layer, 3, _tile_mod_row), _mod_spec(layer, 4, _tile_mod_row),
        _full_spec((D_MODEL, LANES)), _full_spec((D_MODEL, LANES)), _full_spec((1, LANES)),
    ]
    args += [norm_g, mods, mods, w_in, v_norm_g, w_s, b_s, w_out, mods, gffn, mods, mods, rw_hi, rw_lo, rb]
    return pl.pallas_call(
        functools.partial(_odd_kernel, has_prev),
        grid=(N_TILES,),
        in_specs=in_specs,
        out_specs=_route_out_specs(lambda t: t),
        out_shape=_ROUTE_OUT_SHAPES,
        compiler_params=pltpu.CompilerParams(dimension_semantics=("arbitrary",),
                                             vmem_limit_bytes=VMEM_LIMIT),
        name="odd_layer",
    )(*args)


def _slot_kernel(ei_ref, base_ref, tri_ref, o_ref):
    lane = lax.broadcasted_iota(I32, (TM, LANES), 1)
    ei = ei_ref[...]
    onehots = [lane == ei[:, k:k + 1] for k in range(TOP_K)]
    member = functools.reduce(lambda u, w: u + w, [o.astype(F32) for o in onehots])
    before = _dot(tri_ref[...], member.astype(BF16)) + base_ref[0:1, :]
    slots = jnp.zeros((TM, LANES), F32)
    for k in range(TOP_K):
        s = jnp.sum(jnp.where(onehots[k], before, 0.0), axis=1, keepdims=True)
        slots = slots + jnp.where(lane == k, s, 0.0)
    o_ref[...] = slots.T[0:SUBLANES, :].astype(I32)


def _slots(eidx, tile_base, tri):
    return pl.pallas_call(
        _slot_kernel,
        grid=(N_TILES,),
        in_specs=[pl.BlockSpec((TM, LANES), lambda t: (t, 0)),
                  pl.BlockSpec((SUBLANES, LANES), lambda t: (t, 0)),
                  _full_spec((TM, TM))],
        out_specs=pl.BlockSpec((None, SUBLANES, TM), lambda t: (t, 0, 0)),
        out_shape=jax.ShapeDtypeStruct((N_TILES, SUBLANES, TM), I32),
        compiler_params=pltpu.CompilerParams(dimension_semantics=("arbitrary",)),
        name="moe_slots",
    )(eidx, tile_base, tri)


def _sc_mesh():
    return plsc.VectorSubcoreMesh(core_axis_name="c", subcore_axis_name="s")


def _sc_worker():
    return lax.axis_index("s") * SC_CORES + lax.axis_index("c")


def _dispatch(hp, idx):
    n_chunks = N_TOK // SC_WORKERS // SC_ROWS

    @functools.partial(
        pl.kernel, mesh=_sc_mesh(),
        out_type=jax.ShapeDtypeStruct((N_SLOTS, HALF), U32),
        scratch_types=[pltpu.VMEM((n_chunks, TOP_K, SC_ROWS), I32), pltpu.VMEM((SC_BUFS, SC_ROWS, HALF), U32),
                       pltpu.SemaphoreType.DMA((SC_BUFS,)), pltpu.SemaphoreType.DMA((SC_BUFS,))],
        name="moe_dispatch",
    )
    def k(x_hbm, idx_hbm, out_hbm, idx_v, rows_v, read_sem, scat_sem):
        wid = _sc_worker()
        base = wid * n_chunks
        pltpu.sync_copy(idx_hbm.at[wid], idx_v)

        def read(j, b):
            return pltpu.make_async_copy(x_hbm.at[pl.ds((base + j) * SC_ROWS, SC_ROWS)], rows_v.at[b],
                                         read_sem.at[b])

        def scatter(j, b, kk):
            return pltpu.make_async_copy(rows_v.at[b], out_hbm.at[idx_v.at[j, kk]], scat_sem.at[b])

        def drain(j, b):
            for kk in range(TOP_K):
                scatter(j, b, kk).wait()

        read(0, 0).start()

        @pl.loop(0, n_chunks, step=SC_BUFS)
        def _(j):
            for b in range(SC_BUFS):
                jj = j + b
                other = (b + 1) % SC_BUFS
                read(jj, b).wait()

                @pl.when(jj >= 1)
                def _():
                    drain(jj - 1, other)

                @pl.when(jj + 1 < n_chunks)
                def _():
                    read(jj + 1, other).start()

                for kk in range(TOP_K):
                    scatter(jj, b, kk).start()

        drain(n_chunks - 1, (n_chunks - 1) % SC_BUFS)

    return k(hp, idx)


def _combine_gather(ys, idx):
    n_chunks = N_ASSIGN // SC_WORKERS // SC_ROWS

    @functools.partial(
        pl.kernel, mesh=_sc_mesh(),
        out_type=jax.ShapeDtypeStruct((N_ASSIGN, HALF), U32),
        scratch_types=[pltpu.VMEM((n_chunks, SC_ROWS), I32), pltpu.VMEM((SC_BUFS, SC_ROWS, HALF), U32),
                       pltpu.SemaphoreType.DMA((SC_BUFS,)), pltpu.SemaphoreType.DMA((SC_BUFS,))],
        name="moe_combine",
    )
    def k(ys_hbm, idx_hbm, out_hbm, idx_v, rows_v, gather_sem, write_sem):
        wid = _sc_worker()
        base = wid * n_chunks
        pltpu.sync_copy(idx_hbm.at[wid], idx_v)

        def gather(j, b):
            return pltpu.make_async_copy(ys_hbm.at[idx_v.at[j]], rows_v.at[b], gather_sem.at[b])

        def write(j, b):
            return pltpu.make_async_copy(rows_v.at[b], out_hbm.at[pl.ds((base + j) * SC_ROWS, SC_ROWS)],
                                         write_sem.at[b])

        gather(0, 0).start()

        @pl.loop(0, n_chunks, step=SC_BUFS)
        def _(j):
            for b in range(SC_BUFS):
                jj = j + b
                other = (b + 1) % SC_BUFS
                gather(jj, b).wait()

                @pl.when(jj >= 1)
                def _():
                    write(jj - 1, other).wait()

                @pl.when(jj + 1 < n_chunks)
                def _():
                    gather(jj + 1, other).start()

                write(jj, b).start()

        write(n_chunks - 1, (n_chunks - 1) % SC_BUFS).wait()

    return k(ys, idx)


N_MATS = 3


def _moe_kernel(layer, first_ref, nblk_ref, xs_hbm, wg_hbm, wu_hbm, wd_hbm, bg_ref, bu_ref, bd_ref,
                ys_hbm, wf32, wbf, xbuf, obuf, wsem, xsem, osem):
    e = pl.program_id(0)
    slot = e % 2
    w_hbm = (wg_hbm, wu_hbm, wd_hbm)
    nb = nblk_ref[e]
    row0 = first_ref[e] * MOE_BLOCK
    n_chunks = nb // BLOCKS_PER_CHUNK
    has_tail = nb % BLOCKS_PER_CHUNK == 1
    tail_row = row0 + n_chunks * MOE_CHUNK
    tail_slot = n_chunks % 2

    def w_copy(ee, s, m):
        return pltpu.make_async_copy(w_hbm[m].at[layer, ee], wf32.at[s, m], wsem.at[s, m])

    def x_copy(row, rows, s):
        return pltpu.make_async_copy(xs_hbm.at[pl.ds(row, rows)], xbuf.at[s, pl.ds(0, rows)], xsem.at[s])

    def o_copy(row, rows, s):
        return pltpu.make_async_copy(obuf.at[s, pl.ds(0, rows)], ys_hbm.at[pl.ds(row, rows)], osem.at[s])

    def expert(xp):
        lo, hi = _unpack_halves(xp)
        xb = jnp.concatenate([lo.astype(BF16), hi.astype(BF16)], axis=1)
        gt = jnp.minimum(_dot(xb, wbf[0]) + bg_ref[...], SWIGLU_LIMIT)
        up = jnp.clip(_dot(xb, wbf[1]) + bu_ref[...], -SWIGLU_LIMIT, SWIGLU_LIMIT)
        glu = gt * jax.nn.sigmoid(SWIGLU_ALPHA * gt)
        hmid = ((up + 1.0) * glu).astype(BF16)
        out = _dot(hmid, wbf[2]) + bd_ref[...]
        return _pack_halves(out.astype(BF16).astype(F32))

    @pl.when(e == 0)
    def _():
        for m in range(N_MATS):
            w_copy(0, 0, m).start()

    @pl.when(n_chunks > 0)
    def _():
        x_copy(row0, MOE_CHUNK, 0).start()

    @pl.when((n_chunks == 0) & has_tail)
    def _():
        x_copy(row0, MOE_BLOCK, 0).start()

    for m in range(N_MATS):
        w_copy(e, slot, m).wait()

    @pl.when(e + 1 < N_EXPERTS)
    def _():
        for m in range(N_MATS):
            w_copy(e + 1, 1 - slot, m).start()

    for m in range(N_MATS):
        wbf[m] = wf32[slot, m].astype(BF16)

    def chunk(c, carry):
        s = c % 2
        row = row0 + c * MOE_CHUNK
        x_copy(row, MOE_CHUNK, s).wait()

        @pl.when(c + 1 < n_chunks)
        def _():
            x_copy(row + MOE_CHUNK, MOE_CHUNK, 1 - s).start()

        @pl.when((c + 1 == n_chunks) & has_tail)
        def _():
            x_copy(tail_row, MOE_BLOCK, 1 - s).start()

        obuf[s] = expert(xbuf[s])

        @pl.when(c >= 1)
        def _():
            o_copy(row - MOE_CHUNK, MOE_CHUNK, 1 - s).wait()

        o_copy(row, MOE_CHUNK, s).start()
        return carry

    lax.fori_loop(0, n_chunks, chunk, 0)

    @pl.when(has_tail)
    def _():
        x_copy(tail_row, MOE_BLOCK, tail_slot).wait()
        obuf[tail_slot, 0:MOE_BLOCK, :] = expert(xbuf[tail_slot, 0:MOE_BLOCK, :])

        @pl.when(n_chunks >= 1)
        def _():
            o_copy(tail_row - MOE_CHUNK, MOE_CHUNK, 1 - tail_slot).wait()

        o_copy(tail_row, MOE_BLOCK, tail_slot).start()
        o_copy(tail_row, MOE_BLOCK, tail_slot).wait()

    @pl.when(jnp.logical_not(has_tail) & (n_chunks >= 1))
    def _():
        o_copy(tail_row - MOE_CHUNK, MOE_CHUNK, 1 - tail_slot).wait()


def _moe_experts(layer, xs, first_block, n_blocks, w_gate, b_gate, w_up, b_up, w_down, b_down):
    hbm = pl.BlockSpec(memory_space=pl.ANY)
    bspec = pl.BlockSpec((None, None, 1, D_MODEL), lambda e, fb, nb: (layer, e, 0, 0))
    bias = lambda b: b.reshape(DEPTH, N_EXPERTS, 1, D_MODEL)
    return pl.pallas_call(
        functools.partial(_moe_kernel, layer),
        grid_spec=pltpu.PrefetchScalarGridSpec(
            num_scalar_prefetch=2,
            grid=(N_EXPERTS,),
            in_specs=[hbm, hbm, hbm, hbm, bspec, bspec, bspec],
            out_specs=hbm,
            scratch_shapes=[
                pltpu.VMEM((2, N_MATS, D_MODEL, D_MODEL), F32),
                pltpu.VMEM((N_MATS, D_MODEL, D_MODEL), BF16),
                pltpu.VMEM((2, MOE_CHUNK, HALF), U32),
                pltpu.VMEM((2, MOE_CHUNK, HALF), U32),
                pltpu.SemaphoreType.DMA((2, N_MATS)),
                pltpu.SemaphoreType.DMA((2,)),
                pltpu.SemaphoreType.DMA((2,)),
            ],
        ),
        out_shape=jax.ShapeDtypeStruct((N_SLOTS, HALF), U32),
        compiler_params=pltpu.CompilerParams(dimension_semantics=("arbitrary",),
                                             vmem_limit_bytes=VMEM_LIMIT),
        name="moe_experts",
    )(first_block, n_blocks, xs, w_gate, w_up, w_down, bias(b_gate), bias(b_up), bias(b_down))


def _moe(layer, hp, eidx, cnt, tri, w_gate, b_gate, w_up, b_up, w_down, b_down):
    tile_cnt = cnt[::SUBLANES, :N_EXPERTS].astype(I32)
    counts = jnp.sum(tile_cnt, axis=0)
    padded = (counts + MOE_BLOCK - 1) // MOE_BLOCK * MOE_BLOCK
    pad_ends = jnp.cumsum(padded)
    pad_starts = pad_ends - padded
    tile_base = pad_starts[None, :] + jnp.cumsum(tile_cnt, axis=0) - tile_cnt
    tile_base = jnp.pad(tile_base.astype(F32), ((0, 0), (0, LANES - N_EXPERTS)))
    tile_base = jnp.repeat(tile_base, SUBLANES, axis=0)
    first_block = (pad_starts // MOE_BLOCK).astype(I32)
    n_blocks = (padded // MOE_BLOCK).astype(I32)

    slots = _slots(eidx, tile_base, tri)[:, :TOP_K, :]
    per_tile = TM // SC_ROWS
    d_idx = slots.reshape(N_TILES, TOP_K, per_tile, SC_ROWS).transpose(0, 2, 1, 3)
    d_idx = d_idx.reshape(SC_WORKERS, N_TOK // SC_WORKERS // SC_ROWS, TOP_K, SC_ROWS)
    c_idx = slots.transpose(1, 0, 2).reshape(SC_WORKERS, N_ASSIGN // SC_WORKERS // SC_ROWS, SC_ROWS)

    xs = _dispatch(hp, d_idx)
    ys = _moe_experts(layer, xs, first_block, n_blocks, w_gate, b_gate, w_up, b_up, w_down, b_down)
    return _combine_gather(ys, c_idx).reshape(TOP_K, N_TOK, HALF)


def _final_kernel(x_ref, *refs):
    o_ref = refs[N_PREV]
    o_ref[...] = _apply_prev(x_ref[...], refs[:N_PREV])


def _final_residual(x, prev, mods, first_tile, n_tiles):
    tile_fn = lambda t: first_tile + t
    yg, gates = prev
    return pl.pallas_call(
        _final_kernel,
        grid=(n_tiles,),
        in_specs=[pl.BlockSpec((TM, D_MODEL), lambda t: (tile_fn(t), 0))] + _prev_specs(DEPTH, tile_fn),
        out_specs=pl.BlockSpec((TM, D_MODEL), lambda t: (t, 0)),
        out_shape=jax.ShapeDtypeStruct((n_tiles * TM, D_MODEL), F32),
        compiler_params=pltpu.CompilerParams(dimension_semantics=("arbitrary",)),
        name="final_residual",
    )(x, *([yg] * TOP_K), gates, mods)


def _dft_pair(n):
    idx = np.arange(n)
    ang = 2.0 * np.pi * ((idx[:, None] * idx[None, :]) % n) / n
    return np.cos(ang) / np.sqrt(n), np.sin(ang) / np.sqrt(n)


def _rope_tables():
    lane = np.arange(GROUP)
    within = lane % QK_DIM
    axis = within // 32
    e = within % 32
    inv16 = ROPE_BASE ** (-jnp.arange(16, dtype=F32) / 16)
    pos = np.arange(LAT_SEQ)
    coord = np.where(axis[None, :] == 0, (pos // GRID_W)[:, None], (pos % GRID_W)[:, None])
    ang = jnp.asarray(coord, F32) * inv16[e % 16][None, :]
    first = jnp.asarray((e // 16) == 0)[None, :]
    cos, sin = jnp.cos(ang), jnp.sin(ang)
    return cos, jnp.where(first, -sin, 0.0), jnp.where(first, 0.0, sin)


def kernel(x_prompt, x_sample, cache_k, cache_v, c, c_ctx, ada_w, ada_b, norm_mix_g, norm_ffn_g,
           w_in_even, w_out_even, q_norm_g, k_norm_g, lambda_q, lambda_k, subln_g,
           w_in_odd, v_norm_g, w_spatial, b_spatial, w_out_odd,
           router_w, router_b, w_gate, b_gate, w_up, b_up, w_down, b_down):
    n_even = w_in_even.shape[0]
    x = jnp.concatenate([x_prompt.reshape(N_CTX, D_MODEL), x_sample.reshape(N_LAT, D_MODEL)], axis=0)
    cond = jnp.zeros((COND_ROWS, D_MODEL), F32).at[0].set(c_ctx).at[1:1 + LAT_BATCH].set(c)
    mods = _ada_modulation(cond, ada_w, ada_b).reshape(DEPTH, N_MOD, COND_ROWS, 1, D_MODEL)

    dft_ctx = [jnp.asarray(m, F32).astype(BF16) for m in _dft_pair(CTX_SEQ)]
    dft_lat = [jnp.asarray(m, F32).astype(BF16) for m in _dft_pair(LAT_SEQ)]
    c128, s128 = _dft_pair(GROUP)
    cs128 = jnp.asarray(np.concatenate([c128, -s128], axis=0), F32).astype(BF16)
    grp = np.arange(HALF) // QK_DIM
    blockdiag = jnp.asarray(grp[:, None] == grp[None, :], F32).astype(BF16)
    tri = jnp.asarray(np.arange(TM)[:, None] > np.arange(TM)[None, :], F32).astype(BF16)
    rope_tabs = _rope_tables()
    cache_k2 = cache_k.reshape(LAT_BATCH, n_even, PAST_LEN, HALF)
    cache_v2 = cache_v.reshape(LAT_BATCH, n_even, PAST_LEN, HALF)

    rw = jnp.pad(router_w, ((0, 0), (0, 0), (0, LANES - N_EXPERTS)))
    rw_hi = rw.astype(BF16)
    rw_lo = (rw - rw_hi.astype(F32)).astype(BF16)
    rb = jnp.pad(router_b, ((0, 0), (0, LANES - N_EXPERTS)))[:, None, :]

    prev = None
    new_k, new_v = [], []
    for l in range(DEPTH):
        j = l // 2
        gmix = norm_mix_g[l][None, :]
        gffn = norm_ffn_g[l][None, :]
        if l % 2 == 0:
            x, (a, q, k, v, kf, vf) = _even_proj(
                l, x, prev, mods, gmix, w_in_even[j].astype(BF16), blockdiag,
                jnp.tile(q_norm_g[j], HALF // QK_DIM)[None, :], jnp.tile(k_norm_g[j], HALF // QK_DIM)[None, :],
                rope_tabs)
            new_k.append(kf.reshape(CTX_BATCH, CTX_SEQ, N_GROUPS, 2, QK_DIM))
            new_v.append(vf.reshape(CTX_BATCH, CTX_SEQ, N_GROUPS, GROUP))
            common = (cs128, lambda_q[j], lambda_k[j], subln_g[j][None, :], w_out_even[j].astype(BF16),
                      mods, gffn, rw_hi[l], rw_lo[l], rb[l])
            x, hp, gates, eidx, cnt = _even_mix(l, x, a, q, k, v, cache_k2, cache_v2, dft_ctx, dft_lat, *common)
        else:
            b_s = jnp.broadcast_to(jnp.transpose(b_spatial[j])[:, :, None],
                                   (CHUNK, C_GROUPS, GROUP)).reshape(CHUNK, D_MODEL)
            x, hp, gates, eidx, cnt = _odd_layer(
                l, x, prev, mods, gmix, w_in_odd[j].astype(BF16), v_norm_g[j][None, :],
                w_spatial[j].astype(BF16), b_s, w_out_odd[j].astype(BF16), gffn, rw_hi[l], rw_lo[l], rb[l])
        yg = _moe(l, hp, eidx, cnt, tri, w_gate, b_gate, w_up, b_up, w_down, b_down)
        prev = (yg, gates)
    y_ctx = _final_residual(x, prev, mods, 0, CTX_TILES)
    y_lat = _final_residual(x, prev, mods, CTX_TILES, N_TILES - CTX_TILES)
    return (y_ctx.reshape(CTX_BATCH, CTX_SEQ, D_MODEL),
            y_lat.reshape(LAT_BATCH, LAT_SEQ, D_MODEL),
            jnp.stack(new_k, axis=1),
            jnp.stack(new_v, axis=1))
```

```python
import functools
import math

import numpy as np
import jax
import jax.numpy as jnp
from jax import lax
from jax.experimental import pallas as pl
from jax.experimental.pallas import tpu as pltpu
from jax.experimental.pallas import tpu_sc as plsc

F32 = jnp.float32
BF16 = jnp.bfloat16
U32 = jnp.uint32
I32 = jnp.int32

D_MODEL = 1024
DEPTH = 4
N_MOD = 6
EPS = 1e-6
CTX_BATCH, CTX_SEQ = 32, 256
LAT_BATCH, LAT_SEQ = 8, 1024
PAST_LEN = 512
GRID_W = 64
N_CTX = CTX_BATCH * CTX_SEQ
N_LAT = LAT_BATCH * LAT_SEQ
N_TOK = N_CTX + N_LAT
TM = 256
N_TILES = N_TOK // TM
CTX_TILES = N_CTX // TM
LAT_TILES_PER_BATCH = LAT_SEQ // TM
COND_ROWS = 16
GROUP = 128
N_GROUPS = 4
QK_DIM = 64
HALF = 512
CHUNK = 128
C_GROUPS = 8
N_EXPERTS = 32
TOP_K = 4
LANES = 128
SUBLANES = 8
SWIGLU_LIMIT = 7.0
SWIGLU_ALPHA = 1.702
MOE_BLOCK = 256
BLOCKS_PER_CHUNK = 2
MOE_CHUNK = BLOCKS_PER_CHUNK * MOE_BLOCK
N_ASSIGN = N_TOK * TOP_K
MOE_BLOCKS = N_ASSIGN // MOE_BLOCK + N_EXPERTS
N_SLOTS = MOE_BLOCKS * MOE_BLOCK
ROPE_BASE = 10000.0
VMEM_LIMIT = 52 * 1024 * 1024
SC_CORES = 2
SC_SUBCORES = 16
SC_WORKERS = SC_CORES * SC_SUBCORES
SC_ROWS = 64
SC_BUFS = 2
HI_MASK = 0xFFFF0000
LOG2E = math.log2(math.e)


def _dot(a, b):
    return jnp.dot(a, b, preferred_element_type=F32)


def _rms(x, g):
    return x * lax.rsqrt(jnp.mean(x * x, axis=-1, keepdims=True) + EPS) * g


def _split_bf16(x):
    hi = x.astype(BF16)
    lo = (x - hi.astype(F32)).astype(BF16)
    return hi, lo


def _pack_halves(xf):
    b = lax.bitcast_convert_type(xf, U32)
    return (b[:, :HALF] >> 16) | (b[:, HALF:] & jnp.uint32(HI_MASK))


def _unpack_halves(w):
    lo = lax.bitcast_convert_type(w << 16, F32)
    hi = lax.bitcast_convert_type(w & jnp.uint32(HI_MASK), F32)
    return lo, hi


def _tile_mod_row(t):
    return jnp.where(t < CTX_TILES, 0, 1 + (t - CTX_TILES) // LAT_TILES_PER_BATCH)


def _mod_spec(layer, k, row_fn):
    return pl.BlockSpec((None, None, None, 1, D_MODEL),
                        lambda *g: (layer, k, row_fn(*g), 0, 0))


def _full_spec(shape):
    return pl.BlockSpec(shape, lambda *g: (0,) * len(shape))


def _ada_kernel(cond_ref, w_ref, b_ref, o_ref):
    c = cond_ref[...]
    s = (c * jax.nn.sigmoid(c)).astype(BF16)
    o_ref[...] = _dot(s, w_ref[...].astype(BF16)) + b_ref[...]


def _ada_modulation(cond, ada_w, ada_b):
    return pl.pallas_call(
        _ada_kernel,
        grid=(DEPTH, N_MOD),
        in_specs=[
            _full_spec((COND_ROWS, D_MODEL)),
            pl.BlockSpec((None, D_MODEL, D_MODEL), lambda l, n: (l, 0, n)),
            pl.BlockSpec((None, None, 1, D_MODEL), lambda l, n: (l, n, 0, 0)),
        ],
        out_specs=pl.BlockSpec((None, None, COND_ROWS, D_MODEL), lambda l, n: (l, n, 0, 0)),
        out_shape=jax.ShapeDtypeStruct((DEPTH, N_MOD, COND_ROWS, D_MODEL), F32),
        compiler_params=pltpu.CompilerParams(dimension_semantics=("arbitrary", "arbitrary"),
                                             vmem_limit_bytes=VMEM_LIMIT),
        name="ada_modulation",
    )(cond, ada_w, ada_b.reshape(DEPTH, N_MOD, 1, D_MODEL))


N_PREV = 6


def _prev_specs(layer, tile_fn, tiles_per_block=1):
    rows = tiles_per_block * TM
    ys = [pl.BlockSpec((None, rows, HALF), functools.partial(lambda k, *g: (k, tile_fn(*g), 0), k))
          for k in range(TOP_K)]
    return ys + [pl.BlockSpec((rows, LANES), lambda *g: (tile_fn(*g), 0)),
                 _mod_spec(layer - 1, 5, lambda *g: _tile_mod_row(tile_fn(*g) * tiles_per_block))]


def _apply_prev(x, prev_refs):
    y_refs, gt_ref, gate_ref = prev_refs[:TOP_K], prev_refs[TOP_K], prev_refs[TOP_K + 1]
    gt = gt_ref[...]
    acc_lo = acc_hi = None
    for k in range(TOP_K):
        lo, hi = _unpack_halves(y_refs[k][...])
        g = gt[:, k:k + 1]
        acc_lo = g * lo if acc_lo is None else acc_lo + g * lo
        acc_hi = g * hi if acc_hi is None else acc_hi + g * hi
    return x + gate_ref[...] * jnp.concatenate([acc_lo, acc_hi], axis=1)


N_ROUTE_OUT = 5


def _route_out_specs(tile_fn):
    tile = lambda w: pl.BlockSpec((TM, w), lambda *g: (tile_fn(*g), 0))
    return [tile(D_MODEL), tile(HALF), tile(LANES), tile(LANES),
            pl.BlockSpec((SUBLANES, LANES), lambda *g: (tile_fn(*g), 0))]


_ROUTE_OUT_SHAPES = [
    jax.ShapeDtypeStruct((N_TOK, D_MODEL), F32),
    jax.ShapeDtypeStruct((N_TOK, HALF), U32),
    jax.ShapeDtypeStruct((N_TOK, LANES), F32),
    jax.ShapeDtypeStruct((N_TOK, LANES), I32),
    jax.ShapeDtypeStruct((N_TILES * SUBLANES, LANES), F32),
]


def _post_mixer(x, mix, gate_ref, gffn_ref, shift_ref, scale_ref, rwhi_ref, rwlo_ref, rb_ref,
                xo_ref, hp_ref, gt_ref, ei_ref, cnt_ref):
    xn = x + gate_ref[...] * mix
    xo_ref[...] = xn
    h2 = _rms(xn, gffn_ref[...]) * (1.0 + scale_ref[...]) + shift_ref[...]
    hi = h2.astype(BF16)
    hif = hi.astype(F32)
    lo = (h2 - hif).astype(BF16)
    hp_ref[...] = _pack_halves(hif)
    rwhi = rwhi_ref[...]
    logits = _dot(hi, rwhi) + _dot(lo, rwhi) + _dot(hi, rwlo_ref[...]) + rb_ref[...]

    lane = lax.broadcasted_iota(I32, (TM, LANES), 1)
    lane_f = lane.astype(F32)
    work = jnp.where(lane < N_EXPERTS, logits, -jnp.inf)
    member = jnp.zeros((TM, LANES), F32)
    gates = jnp.zeros((TM, LANES), F32)
    ids = jnp.zeros((TM, LANES), F32)
    den = None
    top = None
    for k in range(TOP_K):
        m = jnp.max(work, axis=1, keepdims=True)
        idx = jnp.min(jnp.where(work == m, lane_f, float(LANES)), axis=1, keepdims=True)
        onehot = lane_f == idx
        work = jnp.where(onehot, -jnp.inf, work)
        member = member + onehot.astype(F32)
        if k == 0:
            top = m
        e = jnp.exp(m - top)
        den = e if den is None else den + e
        gates = gates + jnp.where(lane == k, e, 0.0)
        ids = ids + jnp.where(lane == k, idx, 0.0)
    gt_ref[...] = gates * (1.0 / den)
    ei_ref[...] = ids.astype(I32)
    cnt_ref[...] = jnp.broadcast_to(jnp.sum(member, axis=0, keepdims=True), (SUBLANES, LANES))


def _rope(x, c, sa, sb):
    outs = []
    for h in range(N_GROUPS):
        xs = x[:, GROUP * h:GROUP * (h + 1)]
        outs.append(xs * c + pltpu.roll(xs, GROUP - 16, 1) * sa + pltpu.roll(xs, 16, 1) * sb)
    return jnp.concatenate(outs, axis=1)


def _even_proj_kernel(has_prev, *refs):
    x_ref = refs[0]
    refs = refs[1:]
    if has_prev:
        prev_refs = refs[:N_PREV]
        refs = refs[N_PREV:]
    (g_ref, shift_ref, scale_ref, w_ref, bd_ref, qg_ref, kg_ref, cos_ref, sa_ref, sb_ref) = refs[:10]
    outs = refs[10:]
    if has_prev:
        xo_ref = outs[0]
        outs = outs[1:]
    a_ref, q_ref, k_ref, v_ref, kf_ref, vf_ref = outs

    t = pl.program_id(0)
    x = x_ref[...]
    if has_prev:
        x = _apply_prev(x, prev_refs)
        xo_ref[...] = x
    h = (_rms(x, g_ref[...]) * (1.0 + scale_ref[...]) + shift_ref[...]).astype(BF16)

    a_ref[...] = _dot(h, w_ref[:, 0:HALF]).astype(BF16)
    zv = _dot(h, w_ref[:, 3 * HALF:4 * HALF])
    v_ref[...] = zv.astype(BF16)

    bd = bd_ref[...]

    def qk_norm(z, gain):
        shi, slo = _split_bf16(z * z)
        ssq = _dot(shi, bd) + _dot(slo, bd)
        return z * lax.rsqrt(ssq * (1.0 / QK_DIM) + EPS) * gain

    qn = qk_norm(_dot(h, w_ref[:, HALF:2 * HALF]), qg_ref[...]) * (QK_DIM ** -0.5 * LOG2E)
    kn = qk_norm(_dot(h, w_ref[:, 2 * HALF:3 * HALF]), kg_ref[...])

    @pl.when(t < CTX_TILES)
    def _():
        kf_ref[...] = kn
        vf_ref[...] = zv
        q_ref[...] = qn.astype(BF16)
        k_ref[...] = kn.astype(BF16)

    @pl.when(t >= CTX_TILES)
    def _():
        c, sa, sb = cos_ref[...], sa_ref[...], sb_ref[...]
        q_ref[...] = _rope(qn, c, sa, sb).astype(BF16)
        k_ref[...] = _rope(kn, c, sa, sb).astype(BF16)


def _even_proj(layer, x, prev, mods, norm_g, w_in, blockdiag, qg, kg, rope_tabs):
    has_prev = prev is not None
    tile = lambda w: pl.BlockSpec((TM, w), lambda t: (t, 0))
    rope_spec = pl.BlockSpec(
        (TM, GROUP), lambda t: (jnp.where(t < CTX_TILES, 0, (t - CTX_TILES) % LAT_TILES_PER_BATCH), 0))
    in_specs = [tile(D_MODEL)]
    args = [x]
    if has_prev:
        yg, gates = prev
        in_specs += _prev_specs(layer, lambda t: t)
        args += [yg] * TOP_K + [gates, mods]
    in_specs += [
        _full_spec((1, D_MODEL)),
        _mod_spec(layer, 0, _tile_mod_row), _mod_spec(layer, 1, _tile_mod_row),
        _full_spec((D_MODEL, 4 * HALF)), _full_spec((HALF, HALF)),
        _full_spec((1, HALF)), _full_spec((1, HALF)),
        rope_spec, rope_spec, rope_spec,
    ]
    args += [norm_g, mods, mods, w_in, blockdiag, qg, kg, *rope_tabs]
    cache_tile = pl.BlockSpec((TM, HALF), lambda t: (jnp.minimum(t, CTX_TILES - 1), 0))
    out_specs = [tile(HALF)] * 4 + [cache_tile] * 2
    out_shape = [jax.ShapeDtypeStruct((N_TOK, HALF), BF16)] * 4 + \
                [jax.ShapeDtypeStruct((N_CTX, HALF), F32)] * 2
    if has_prev:
        out_specs = [tile(D_MODEL)] + out_specs
        out_shape = [jax.ShapeDtypeStruct((N_TOK, D_MODEL), F32)] + out_shape
    res = pl.pallas_call(
        functools.partial(_even_proj_kernel, has_prev),
        grid=(N_TILES,),
        in_specs=in_specs, out_specs=out_specs, out_shape=out_shape,
        compiler_params=pltpu.CompilerParams(dimension_semantics=("arbitrary",),
                                             vmem_limit_bytes=VMEM_LIMIT),
        name="even_proj",
    )(*args)
    if has_prev:
        return res[0], res[1:]
    return x, res


def _even_mix_kernel(lambda_init, *refs):
    (x_ref, q_ref, ac_ref, kc_ref, vc_ref, al_ref, kl_ref, vl_ref, ck_ref, cv_ref,
     cnc_ref, snc_ref, cnl_ref, snl_ref) = refs[:14]
    shared = refs[14:]
    t = pl.program_id(0)

    @pl.when(t < CTX_TILES)
    def _():
        _even_mix_body(lambda_init, x_ref, q_ref, ac_ref, [(kc_ref, vc_ref, False)],
                       cnc_ref, snc_ref, *shared)

    @pl.when(t >= CTX_TILES)
    def _():
        _even_mix_body(lambda_init, x_ref, q_ref, al_ref, [(kl_ref, vl_ref, False), (ck_ref, cv_ref, True)],
                       cnl_ref, snl_ref, *shared)


def _even_mix_body(lambda_init, x_ref, q_ref, a_ref, kv_refs, cn_ref, sn_ref, *refs):
    (cs_ref, lq_ref, lk_ref, sg_ref, wo_ref,
     gate_ref, gffn_ref, shift_ref, scale_ref, rwhi_ref, rwlo_ref, rb_ref) = refs[:12]
    out_refs = refs[12:]

    a = a_ref[...]
    y1 = _dot(cn_ref[...], a).astype(BF16)
    y2 = _dot(sn_ref[...], a).astype(BF16)
    cs = cs_ref[...]
    pieces = []
    for g in range(N_GROUPS):
        sl = slice(GROUP * g, GROUP * (g + 1))
        pieces.append(_dot(jnp.concatenate([y1[:, sl], y2[:, sl]], axis=1), cs).astype(BF16))

    d = jnp.sum(lq_ref[...] * lk_ref[...], axis=1, keepdims=True)
    ed = jnp.exp(d)
    lam = ed[0:1, :] - ed[1:2, :] + lambda_init
    lane = lax.broadcasted_iota(I32, (TM, GROUP), 1)
    nt = (((1,), (1,)), ((), ()))
    for hd in range(N_GROUPS):
        sl = slice(GROUP * hd, GROUP * (hd + 1))
        qf = q_ref[:, sl].astype(F32)
        parts = []
        for k_ref, v_ref, is_f32 in kv_refs:
            kk, vv = k_ref[:, sl], v_ref[:, sl]
            parts.append((kk.astype(BF16), vv.astype(BF16)) if is_f32 else (kk, vv))

        def probs(qm):
            ss = [lax.dot_general(qm, kk, nt, preferred_element_type=F32) for kk, _ in parts]
            m = functools.reduce(jnp.maximum, [jnp.max(s, axis=1, keepdims=True) for s in ss])
            es = [jnp.exp2(s - m) for s in ss]
            den = functools.reduce(lambda u, w: u + w, [jnp.sum(e, axis=1, keepdims=True) for e in es])
            return es, 1.0 / den

        es0, inv0 = probs(jnp.where(lane < QK_DIM, qf, 0.0).astype(BF16))
        es1, inv1 = probs(jnp.where(lane >= QK_DIM, qf, 0.0).astype(BF16))
        o0 = functools.reduce(lambda u, w: u + w,
                              [_dot(e.astype(BF16), vv) for e, (_, vv) in zip(es0, parts)])
        o1 = functools.reduce(lambda u, w: u + w,
                              [_dot(e.astype(BF16), vv) for e, (_, vv) in zip(es1, parts)])
        o = o0 * inv0 - o1 * (lam * inv1)
        pieces.append((_rms(o, sg_ref[...]) * (1.0 - lambda_init)).astype(BF16))

    mix = _dot(jnp.concatenate(pieces, axis=1), wo_ref[...])
    _post_mixer(x_ref[...], mix, gate_ref, gffn_ref, shift_ref, scale_ref, rwhi_ref, rwlo_ref, rb_ref,
                *out_refs)


def _even_mix(layer, x, a, q, k, v, cache_k, cache_v, dft_ctx, dft_lat, cs128, lam_q, lam_k, subln_g,
              w_out, mods, gffn, rw_hi, rw_lo, rb):
    j = layer // 2
    lambda_init = 0.8 - 0.6 * math.exp(-0.3 * layer)
    lat = lambda t: jnp.maximum(t - CTX_TILES, 0)
    tile = lambda w: pl.BlockSpec((TM, w), lambda t: (t, 0))
    ctx_seq = pl.BlockSpec((CTX_SEQ, HALF), lambda t: (jnp.minimum(t, CTX_TILES - 1), 0))
    lat_seq = pl.BlockSpec((LAT_SEQ, HALF),
                           lambda t: (N_CTX // LAT_SEQ + lat(t) // LAT_TILES_PER_BATCH, 0))
    cache = pl.BlockSpec((None, None, PAST_LEN, HALF),
                         lambda t: (lat(t) // LAT_TILES_PER_BATCH, j, 0, 0))
    dft_lat_spec = pl.BlockSpec((TM, LAT_SEQ), lambda t: (lat(t) % LAT_TILES_PER_BATCH, 0))
    in_specs = [
        tile(D_MODEL), tile(HALF), ctx_seq, ctx_seq, ctx_seq, lat_seq, lat_seq, lat_seq, cache, cache,
        _full_spec((CTX_SEQ, CTX_SEQ)), _full_spec((CTX_SEQ, CTX_SEQ)), dft_lat_spec, dft_lat_spec,
        _full_spec((2 * GROUP, GROUP)),
        _full_spec((2, QK_DIM)), _full_spec((2, QK_DIM)), _full_spec((1, GROUP)),
        _full_spec((D_MODEL, D_MODEL)),
        _mod_spec(layer, 2, _tile_mod_row), _full_spec((1, D_MODEL)),
        _mod_spec(layer, 3, _tile_mod_row), _mod_spec(layer, 4, _tile_mod_row),
        _full_spec((D_MODEL, LANES)), _full_spec((D_MODEL, LANES)), _full_spec((1, LANES)),
    ]
    args = [x, q, a, k, v, a, k, v, cache_k, cache_v, dft_ctx[0], dft_ctx[1], dft_lat[0], dft_lat[1],
            cs128, lam_q, lam_k, subln_g, w_out, mods, gffn, mods, mods, rw_hi, rw_lo, rb]
    return pl.pallas_call(
        functools.partial(_even_mix_kernel, lambda_init),
        grid=(N_TILES,),
        in_specs=in_specs,
        out_specs=_route_out_specs(lambda t: t),
        out_shape=_ROUTE_OUT_SHAPES,
        compiler_params=pltpu.CompilerParams(dimension_semantics=("arbitrary",),
                                             vmem_limit_bytes=VMEM_LIMIT),
        name="even_mix",
    )(*args)


def _odd_kernel(has_prev, *refs):
    x_ref = refs[0]
    refs = refs[1:]
    if has_prev:
        prev_refs = refs[:N_PREV]
        refs = refs[N_PREV:]
    (g_ref, shift_ref, scale_ref, w_ref, vg_ref, ws_ref, bs_ref, wo_ref,
     gate_ref, gffn_ref, shift2_ref, scale2_ref, rwhi_ref, rwlo_ref, rb_ref) = refs[:15]
    out_refs = refs[15:]

    x = x_ref[...]
    if has_prev:
        x = _apply_prev(x, prev_refs)
    h = (_rms(x, g_ref[...]) * (1.0 + scale_ref[...]) + shift_ref[...]).astype(BF16)

    def gelu(z):
        return 0.5 * z * (1.0 + lax.erf(z * (2.0 ** -0.5)))

    u = gelu(_dot(h, w_ref[:, 0:D_MODEL]))
    vn = _rms(gelu(_dot(h, w_ref[:, D_MODEL:2 * D_MODEL])), vg_ref[...]).astype(BF16)
    bs = bs_ref[...]
    rows = []
    for c in range(TM // CHUNK):
        cols = []
        for g in range(C_GROUPS):
            cols.append(_dot(ws_ref[g], vn[CHUNK * c:CHUNK * (c + 1), GROUP * g:GROUP * (g + 1)]))
        rows.append(jnp.concatenate(cols, axis=1) + bs)
    sv = jnp.concatenate(rows, axis=0)
    mix = _dot((u * sv).astype(BF16), wo_ref[...])
    _post_mixer(x, mix, gate_ref, gffn_ref, shift2_ref, scale2_ref, rwhi_ref, rwlo_ref, rb_ref,
                *out_refs)


def _odd_layer(layer, x, prev, mods, norm_g, w_in, v_norm_g, w_s, b_s, w_out, gffn, rw_hi, rw_lo, rb):
    has_prev = prev is not None
    in_specs = [pl.BlockSpec((TM, D_MODEL), lambda t: (t, 0))]
    args = [x]
    if has_prev:
        yg, gates = prev
        in_specs += _prev_specs(layer, lambda t: t)
        args += [yg] * TOP_K + [gates, mods]
    in_specs += [
        _full_spec((1, D_MODEL)),
        _mod_spec(layer, 0, _tile_mod_row), _mod_spec(layer, 1, _tile_mod_row),
        _full_spec((D_MODEL, 2 * D_MODEL)), _full_spec((1, D_MODEL)),
        _full_spec((C_GROUPS, CHUNK, CHUNK)), _full_spec((CHUNK, D_MODEL)),
        _full_spec((D_MODEL, D_MODEL)),
        _mod_spec(layer, 2, _tile_mod_row), _full_spec((1, D_MODEL)),
        _mod_spec(layer, 3, _tile_mod_row), _mod_spec(layer, 4, _tile_mod_row),
        _full_spec((D_MODEL, LANES)), _full_spec((D_MODEL, LANES)), _full_spec((1, LANES)),
    ]
    args += [norm_g, mods, mods, w_in, v_norm_g, w_s, b_s, w_out, mods, gffn, mods, mods, rw_hi, rw_lo, rb]
    return pl.pallas_call(
        functools.partial(_odd_kernel, has_prev),
        grid=(N_TILES,),
        in_specs=in_specs,
        out_specs=_route_out_specs(lambda t: t),
        out_shape=_ROUTE_OUT_SHAPES,
        compiler_params=pltpu.CompilerParams(dimension_semantics=("arbitrary",),
                                             vmem_limit_bytes=VMEM_LIMIT),
        name="odd_layer",
    )(*args)


def _slot_kernel(ei_ref, base_ref, tri_ref, o_ref):
    lane = lax.broadcasted_iota(I32, (TM, LANES), 1)
    tri = tri_ref[...]
    for i in range(SLOT_TILES):
        ei = ei_ref[TM * i:TM * (i + 1), :]
        onehots = [lane == ei[:, k:k + 1] for k in range(TOP_K)]
        member = functools.reduce(lambda u, w: u + w, [o.astype(F32) for o in onehots])
        before = _dot(tri, member.astype(BF16)) + base_ref[SUBLANES * i:SUBLANES * i + 1, :]
        slots = jnp.zeros((TM, LANES), F32)
        for k in range(TOP_K):
            s = jnp.sum(jnp.where(onehots[k], before, 0.0), axis=1, keepdims=True)
            slots = slots + jnp.where(lane == k, s, 0.0)
        o_ref[i] = slots.T[0:SUBLANES, :].astype(I32)


SLOT_TILES = 4


def _slots(eidx, tile_base, tri):
    return pl.pallas_call(
        _slot_kernel,
        grid=(N_TILES // SLOT_TILES,),
        in_specs=[pl.BlockSpec((SLOT_TILES * TM, LANES), lambda t: (t, 0)),
                  pl.BlockSpec((SLOT_TILES * SUBLANES, LANES), lambda t: (t, 0)),
                  _full_spec((TM, TM))],
        out_specs=pl.BlockSpec((SLOT_TILES, SUBLANES, TM), lambda t: (t, 0, 0)),
        out_shape=jax.ShapeDtypeStruct((N_TILES, SUBLANES, TM), I32),
        compiler_params=pltpu.CompilerParams(dimension_semantics=("arbitrary",)),
        name="moe_slots",
    )(eidx, tile_base, tri)


def _sc_mesh():
    return plsc.VectorSubcoreMesh(core_axis_name="c", subcore_axis_name="s")


def _sc_worker():
    return lax.axis_index("s") * SC_CORES + lax.axis_index("c")


def _dispatch(hp, idx):
    n_chunks = N_TOK // SC_WORKERS // SC_ROWS

    @functools.partial(
        pl.kernel, mesh=_sc_mesh(),
        out_type=jax.ShapeDtypeStruct((N_SLOTS, HALF), U32),
        scratch_types=[pltpu.VMEM((n_chunks, TOP_K, SC_ROWS), I32), pltpu.VMEM((SC_BUFS, SC_ROWS, HALF), U32),
                       pltpu.SemaphoreType.DMA((SC_BUFS,)), pltpu.SemaphoreType.DMA((SC_BUFS,))],
        name="moe_dispatch",
    )
    def k(x_hbm, idx_hbm, out_hbm, idx_v, rows_v, read_sem, scat_sem):
        wid = _sc_worker()
        base = wid * n_chunks
        pltpu.sync_copy(idx_hbm.at[wid], idx_v)

        def read(j, b):
            return pltpu.make_async_copy(x_hbm.at[pl.ds((base + j) * SC_ROWS, SC_ROWS)], rows_v.at[b],
                                         read_sem.at[b])

        def scatter(j, b, kk):
            return pltpu.make_async_copy(rows_v.at[b], out_hbm.at[idx_v.at[j, kk]], scat_sem.at[b])

        def drain(j, b):
            for kk in range(TOP_K):
                scatter(j, b, kk).wait()

        read(0, 0).start()

        @pl.loop(0, n_chunks, step=SC_BUFS)
        def _(j):
            for b in range(SC_BUFS):
                jj = j + b
                other = (b + 1) % SC_BUFS
                read(jj, b).wait()

                @pl.when(jj >= 1)
                def _():
                    drain(jj - 1, other)

                @pl.when(jj + 1 < n_chunks)
                def _():
                    read(jj + 1, other).start()

                for kk in range(TOP_K):
                    scatter(jj, b, kk).start()

        drain(n_chunks - 1, (n_chunks - 1) % SC_BUFS)

    return k(hp, idx)


def _combine_gather(ys, idx):
    n_chunks = N_ASSIGN // SC_WORKERS // SC_ROWS

    @functools.partial(
        pl.kernel, mesh=_sc_mesh(),
        out_type=jax.ShapeDtypeStruct((N_ASSIGN, HALF), U32),
        scratch_types=[pltpu.VMEM((n_chunks, SC_ROWS), I32), pltpu.VMEM((SC_BUFS, SC_ROWS, HALF), U32),
                       pltpu.SemaphoreType.DMA((SC_BUFS,)), pltpu.SemaphoreType.DMA((SC_BUFS,))],
        name="moe_combine",
    )
    def k(ys_hbm, idx_hbm, out_hbm, idx_v, rows_v, gather_sem, write_sem):
        wid = _sc_worker()
        base = wid * n_chunks
        pltpu.sync_copy(idx_hbm.at[wid], idx_v)

        def gather(j, b):
            return pltpu.make_async_copy(ys_hbm.at[idx_v.at[j]], rows_v.at[b], gather_sem.at[b])

        def write(j, b):
            return pltpu.make_async_copy(rows_v.at[b], out_hbm.at[pl.ds((base + j) * SC_ROWS, SC_ROWS)],
                                         write_sem.at[b])

        gather(0, 0).start()

        @pl.loop(0, n_chunks, step=SC_BUFS)
        def _(j):
            for b in range(SC_BUFS):
                jj = j + b
                other = (b + 1) % SC_BUFS
                gather(jj, b).wait()

                @pl.when(jj >= 1)
                def _():
                    write(jj - 1, other).wait()

                @pl.when(jj + 1 < n_chunks)
                def _():
                    gather(jj + 1, other).start()

                write(jj, b).start()

        write(n_chunks - 1, (n_chunks - 1) % SC_BUFS).wait()

    return k(ys, idx)


N_MATS = 3


def _moe_kernel(layer, first_ref, nblk_ref, xs_hbm, wg_hbm, wu_hbm, wd_hbm, bg_ref, bu_ref, bd_ref,
                ys_hbm, wf32, wbf, xbuf, obuf, wsem, xsem, osem):
    e = pl.program_id(0)
    slot = e % 2
    w_hbm = (wg_hbm, wu_hbm, wd_hbm)
    nb = nblk_ref[e]
    row0 = first_ref[e] * MOE_BLOCK
    n_chunks = nb // BLOCKS_PER_CHUNK
    has_tail = nb % BLOCKS_PER_CHUNK == 1
    tail_row = row0 + n_chunks * MOE_CHUNK
    tail_slot = n_chunks % 2

    def w_copy(ee, s, m):
        return pltpu.make_async_copy(w_hbm[m].at[layer, ee], wf32.at[s, m], wsem.at[s, m])

    def x_copy(row, rows, s):
        return pltpu.make_async_copy(xs_hbm.at[pl.ds(row, rows)], xbuf.at[s, pl.ds(0, rows)], xsem.at[s])

    def o_copy(row, rows, s):
        return pltpu.make_async_copy(obuf.at[s, pl.ds(0, rows)], ys_hbm.at[pl.ds(row, rows)], osem.at[s])

    def expert(xp):
        lo, hi = _unpack_halves(xp)
        xb = jnp.concatenate([lo.astype(BF16), hi.astype(BF16)], axis=1)
        gt = jnp.minimum(_dot(xb, wbf[0]) + bg_ref[...], SWIGLU_LIMIT)
        up = jnp.clip(_dot(xb, wbf[1]) + bu_ref[...], -SWIGLU_LIMIT, SWIGLU_LIMIT)
        glu = gt * jax.nn.sigmoid(SWIGLU_ALPHA * gt)
        hmid = ((up + 1.0) * glu).astype(BF16)
        out = _dot(hmid, wbf[2]) + bd_ref[...]
        return _pack_halves(out.astype(BF16).astype(F32))

    @pl.when(e == 0)
    def _():
        for m in range(N_MATS):
            w_copy(0, 0, m).start()

    @pl.when(n_chunks > 0)
    def _():
        x_copy(row0, MOE_CHUNK, 0).start()

    @pl.when((n_chunks == 0) & has_tail)
    def _():
        x_copy(row0, MOE_BLOCK, 0).start()

    for m in range(N_MATS):
        w_copy(e, slot, m).wait()

    @pl.when(e + 1 < N_EXPERTS)
    def _():
        for m in range(N_MATS):
            w_copy(e + 1, 1 - slot, m).start()

    for m in range(N_MATS):
        wbf[m] = wf32[slot, m].astype(BF16)

    def chunk(c, carry):
        s = c % 2
        row = row0 + c * MOE_CHUNK
        x_copy(row, MOE_CHUNK, s).wait()

        @pl.when(c + 1 < n_chunks)
        def _():
            x_copy(row + MOE_CHUNK, MOE_CHUNK, 1 - s).start()

        @pl.when((c + 1 == n_chunks) & has_tail)
        def _():
            x_copy(tail_row, MOE_BLOCK, 1 - s).start()

        obuf[s] = expert(xbuf[s])

        @pl.when(c >= 1)
        def _():
            o_copy(row - MOE_CHUNK, MOE_CHUNK, 1 - s).wait()

        o_copy(row, MOE_CHUNK, s).start()
        return carry

    lax.fori_loop(0, n_chunks, chunk, 0)

    @pl.when(has_tail)
    def _():
        x_copy(tail_row, MOE_BLOCK, tail_slot).wait()
        obuf[tail_slot, 0:MOE_BLOCK, :] = expert(xbuf[tail_slot, 0:MOE_BLOCK, :])

        @pl.when(n_chunks >= 1)
        def _():
            o_copy(tail_row - MOE_CHUNK, MOE_CHUNK, 1 - tail_slot).wait()

        o_copy(tail_row, MOE_BLOCK, tail_slot).start()
        o_copy(tail_row, MOE_BLOCK, tail_slot).wait()

    @pl.when(jnp.logical_not(has_tail) & (n_chunks >= 1))
    def _():
        o_copy(tail_row - MOE_CHUNK, MOE_CHUNK, 1 - tail_slot).wait()


def _moe_experts(layer, xs, first_block, n_blocks, w_gate, b_gate, w_up, b_up, w_down, b_down):
    hbm = pl.BlockSpec(memory_space=pl.ANY)
    bspec = pl.BlockSpec((None, None, 1, D_MODEL), lambda e, fb, nb: (layer, e, 0, 0))
    bias = lambda b: b.reshape(DEPTH, N_EXPERTS, 1, D_MODEL)
    return pl.pallas_call(
        functools.partial(_moe_kernel, layer),
        grid_spec=pltpu.PrefetchScalarGridSpec(
            num_scalar_prefetch=2,
            grid=(N_EXPERTS,),
            in_specs=[hbm, hbm, hbm, hbm, bspec, bspec, bspec],
            out_specs=hbm,
            scratch_shapes=[
                pltpu.VMEM((2, N_MATS, D_MODEL, D_MODEL), F32),
                pltpu.VMEM((N_MATS, D_MODEL, D_MODEL), BF16),
                pltpu.VMEM((2, MOE_CHUNK, HALF), U32),
                pltpu.VMEM((2, MOE_CHUNK, HALF), U32),
                pltpu.SemaphoreType.DMA((2, N_MATS)),
                pltpu.SemaphoreType.DMA((2,)),
                pltpu.SemaphoreType.DMA((2,)),
            ],
        ),
        out_shape=jax.ShapeDtypeStruct((N_SLOTS, HALF), U32),
        compiler_params=pltpu.CompilerParams(dimension_semantics=("arbitrary",),
                                             vmem_limit_bytes=VMEM_LIMIT),
        name="moe_experts",
    )(first_block, n_blocks, xs, w_gate, w_up, w_down, bias(b_gate), bias(b_up), bias(b_down))


def _moe(layer, hp, eidx, cnt, tri, w_gate, b_gate, w_up, b_up, w_down, b_down):
    tile_cnt = cnt[::SUBLANES, :N_EXPERTS].astype(I32)
    counts = jnp.sum(tile_cnt, axis=0)
    padded = (counts + MOE_BLOCK - 1) // MOE_BLOCK * MOE_BLOCK
    pad_ends = jnp.cumsum(padded)
    pad_starts = pad_ends - padded
    tile_base = pad_starts[None, :] + jnp.cumsum(tile_cnt, axis=0) - tile_cnt
    tile_base = jnp.pad(tile_base.astype(F32), ((0, 0), (0, LANES - N_EXPERTS)))
    tile_base = jnp.repeat(tile_base, SUBLANES, axis=0)
    first_block = (pad_starts // MOE_BLOCK).astype(I32)
    n_blocks = (padded // MOE_BLOCK).astype(I32)

    slots = _slots(eidx, tile_base, tri)[:, :TOP_K, :]
    per_tile = TM // SC_ROWS
    d_idx = slots.reshape(N_TILES, TOP_K, per_tile, SC_ROWS).transpose(0, 2, 1, 3)
    d_idx = d_idx.reshape(SC_WORKERS, N_TOK // SC_WORKERS // SC_ROWS, TOP_K, SC_ROWS)
    c_idx = slots.transpose(1, 0, 2).reshape(SC_WORKERS, N_ASSIGN // SC_WORKERS // SC_ROWS, SC_ROWS)

    xs = _dispatch(hp, d_idx)
    ys = _moe_experts(layer, xs, first_block, n_blocks, w_gate, b_gate, w_up, b_up, w_down, b_down)
    return _combine_gather(ys, c_idx).reshape(TOP_K, N_TOK, HALF)


FINAL_TILES = 2


def _final_kernel(x_ref, *refs):
    o_ref = refs[N_PREV]
    o_ref[...] = _apply_prev(x_ref[...], refs[:N_PREV])


def _final_residual(x, prev, mods, first_tile, n_tiles):
    rows = FINAL_TILES * TM
    block_fn = lambda t: first_tile // FINAL_TILES + t
    yg, gates = prev
    return pl.pallas_call(
        _final_kernel,
        grid=(n_tiles // FINAL_TILES,),
        in_specs=[pl.BlockSpec((rows, D_MODEL), lambda t: (block_fn(t), 0))]
        + _prev_specs(DEPTH, block_fn, FINAL_TILES),
        out_specs=pl.BlockSpec((rows, D_MODEL), lambda t: (t, 0)),
        out_shape=jax.ShapeDtypeStruct((n_tiles * TM, D_MODEL), F32),
        compiler_params=pltpu.CompilerParams(dimension_semantics=("arbitrary",)),
        name="final_residual",
    )(x, *([yg] * TOP_K), gates, mods)


def _dft_pair(n):
    idx = np.arange(n)
    ang = 2.0 * np.pi * ((idx[:, None] * idx[None, :]) % n) / n
    return np.cos(ang) / np.sqrt(n), np.sin(ang) / np.sqrt(n)


def _rope_tables():
    lane = np.arange(GROUP)
    within = lane % QK_DIM
    axis = within // 32
    e = within % 32
    inv16 = ROPE_BASE ** (-jnp.arange(16, dtype=F32) / 16)
    pos = np.arange(LAT_SEQ)
    coord = np.where(axis[None, :] == 0, (pos // GRID_W)[:, None], (pos % GRID_W)[:, None])
    ang = jnp.asarray(coord, F32) * inv16[e % 16][None, :]
    first = jnp.asarray((e // 16) == 0)[None, :]
    cos, sin = jnp.cos(ang), jnp.sin(ang)
    return cos, jnp.where(first, -sin, 0.0), jnp.where(first, 0.0, sin)


def kernel(x_prompt, x_sample, cache_k, cache_v, c, c_ctx, ada_w, ada_b, norm_mix_g, norm_ffn_g,
           w_in_even, w_out_even, q_norm_g, k_norm_g, lambda_q, lambda_k, subln_g,
           w_in_odd, v_norm_g, w_spatial, b_spatial, w_out_odd,
           router_w, router_b, w_gate, b_gate, w_up, b_up, w_down, b_down):
    n_even = w_in_even.shape[0]
    x = jnp.concatenate([x_prompt.reshape(N_CTX, D_MODEL), x_sample.reshape(N_LAT, D_MODEL)], axis=0)
    cond = jnp.zeros((COND_ROWS, D_MODEL), F32).at[0].set(c_ctx).at[1:1 + LAT_BATCH].set(c)
    mods = _ada_modulation(cond, ada_w, ada_b).reshape(DEPTH, N_MOD, COND_ROWS, 1, D_MODEL)

    dft_ctx = [jnp.asarray(m, F32).astype(BF16) for m in _dft_pair(CTX_SEQ)]
    dft_lat = [jnp.asarray(m, F32).astype(BF16) for m in _dft_pair(LAT_SEQ)]
    c128, s128 = _dft_pair(GROUP)
    cs128 = jnp.asarray(np.concatenate([c128, -s128], axis=0), F32).astype(BF16)
    grp = np.arange(HALF) // QK_DIM
    blockdiag = jnp.asarray(grp[:, None] == grp[None, :], F32).astype(BF16)
    tri = jnp.asarray(np.arange(TM)[:, None] > np.arange(TM)[None, :], F32).astype(BF16)
    rope_tabs = _rope_tables()
    cache_k2 = cache_k.reshape(LAT_BATCH, n_even, PAST_LEN, HALF)
    cache_v2 = cache_v.reshape(LAT_BATCH, n_even, PAST_LEN, HALF)

    rw = jnp.pad(router_w, ((0, 0), (0, 0), (0, LANES - N_EXPERTS)))
    rw_hi = rw.astype(BF16)
    rw_lo = (rw - rw_hi.astype(F32)).astype(BF16)
    rb = jnp.pad(router_b, ((0, 0), (0, LANES - N_EXPERTS)))[:, None, :]

    prev = None
    new_k, new_v = [], []
    for l in range(DEPTH):
        j = l // 2
        gmix = norm_mix_g[l][None, :]
        gffn = norm_ffn_g[l][None, :]
        if l % 2 == 0:
            x, (a, q, k, v, kf, vf) = _even_proj(
                l, x, prev, mods, gmix, w_in_even[j].astype(BF16), blockdiag,
                jnp.tile(q_norm_g[j], HALF // QK_DIM)[None, :], jnp.tile(k_norm_g[j], HALF // QK_DIM)[None, :],
                rope_tabs)
            new_k.append(kf.reshape(CTX_BATCH, CTX_SEQ, N_GROUPS, 2, QK_DIM))
            new_v.append(vf.reshape(CTX_BATCH, CTX_SEQ, N_GROUPS, GROUP))
            common = (cs128, lambda_q[j], lambda_k[j], subln_g[j][None, :], w_out_even[j].astype(BF16),
                      mods, gffn, rw_hi[l], rw_lo[l], rb[l])
            x, hp, gates, eidx, cnt = _even_mix(l, x, a, q, k, v, cache_k2, cache_v2, dft_ctx, dft_lat, *common)
        else:
            b_s = jnp.broadcast_to(jnp.transpose(b_spatial[j])[:, :, None],
                                   (CHUNK, C_GROUPS, GROUP)).reshape(CHUNK, D_MODEL)
            x, hp, gates, eidx, cnt = _odd_layer(
                l, x, prev, mods, gmix, w_in_odd[j].astype(BF16), v_norm_g[j][None, :],
                w_spatial[j].astype(BF16), b_s, w_out_odd[j].astype(BF16), gffn, rw_hi[l], rw_lo[l], rb[l])
        yg = _moe(l, hp, eidx, cnt, tri, w_gate, b_gate, w_up, b_up, w_down, b_down)
        prev = (yg, gates)
    y_ctx = _final_residual(x, prev, mods, 0, CTX_TILES)
    y_lat = _final_residual(x, prev, mods, CTX_TILES, N_TILES - CTX_TILES)
    return (y_ctx.reshape(CTX_BATCH, CTX_SEQ, D_MODEL),
            y_lat.reshape(LAT_BATCH, LAT_SEQ, D_MODEL),
            jnp.stack(new_k, axis=1),
            jnp.stack(new_v, axis=1))
```

```python
import functools
import math

import numpy as np
import jax
import jax.numpy as jnp
from jax import lax
from jax.experimental import pallas as pl
from jax.experimental.pallas import tpu as pltpu
from jax.experimental.pallas import tpu_sc as plsc

F32 = jnp.float32
BF16 = jnp.bfloat16
U32 = jnp.uint32
I32 = jnp.int32

D_MODEL = 1024
DEPTH = 4
N_MOD = 6
EPS = 1e-6
CTX_BATCH, CTX_SEQ = 32, 256
LAT_BATCH, LAT_SEQ = 8, 1024
PAST_LEN = 512
GRID_W = 64
N_CTX = CTX_BATCH * CTX_SEQ
N_LAT = LAT_BATCH * LAT_SEQ
N_TOK = N_CTX + N_LAT
TM = 256
N_TILES = N_TOK // TM
CTX_TILES = N_CTX // TM
LAT_TILES_PER_BATCH = LAT_SEQ // TM
COND_ROWS = 16
GROUP = 128
N_GROUPS = 4
QK_DIM = 64
HALF = 512
CHUNK = 128
C_GROUPS = 8
N_EXPERTS = 32
TOP_K = 4
LANES = 128
SUBLANES = 8
SWIGLU_LIMIT = 7.0
SWIGLU_ALPHA = 1.702
MOE_BLOCK = 256
BLOCKS_PER_CHUNK = 2
MOE_CHUNK = BLOCKS_PER_CHUNK * MOE_BLOCK
N_ASSIGN = N_TOK * TOP_K
MOE_BLOCKS = N_ASSIGN // MOE_BLOCK + N_EXPERTS
N_SLOTS = MOE_BLOCKS * MOE_BLOCK
ROPE_BASE = 10000.0
VMEM_LIMIT = 52 * 1024 * 1024
SC_CORES = 2
SC_SUBCORES = 16
SC_WORKERS = SC_CORES * SC_SUBCORES
SC_ROWS = 64
SC_BUFS = 2
HI_MASK = 0xFFFF0000
LOG2E = math.log2(math.e)


def _dot(a, b):
    return jnp.dot(a, b, preferred_element_type=F32)


def _rms(x, g):
    return x * lax.rsqrt(jnp.mean(x * x, axis=-1, keepdims=True) + EPS) * g


def _split_bf16(x):
    hi = x.astype(BF16)
    lo = (x - hi.astype(F32)).astype(BF16)
    return hi, lo


def _pack_halves(xf):
    b = lax.bitcast_convert_type(xf, U32)
    return (b[:, :HALF] >> 16) | (b[:, HALF:] & jnp.uint32(HI_MASK))


def _unpack_halves(w):
    lo = lax.bitcast_convert_type(w << 16, F32)
    hi = lax.bitcast_convert_type(w & jnp.uint32(HI_MASK), F32)
    return lo, hi


def _tile_mod_row(t):
    return jnp.where(t < CTX_TILES, 0, 1 + (t - CTX_TILES) // LAT_TILES_PER_BATCH)


def _mod_spec(layer, k, row_fn):
    return pl.BlockSpec((None, None, None, 1, D_MODEL),
                        lambda *g: (layer, k, row_fn(*g), 0, 0))


def _full_spec(shape):
    return pl.BlockSpec(shape, lambda *g: (0,) * len(shape))


def _ada_kernel(cond_ref, w_ref, b_ref, o_ref):
    c = cond_ref[...]
    s = (c * jax.nn.sigmoid(c)).astype(BF16)
    o_ref[...] = _dot(s, w_ref[...].astype(BF16)) + b_ref[...]


def _ada_modulation(cond, ada_w, ada_b):
    return pl.pallas_call(
        _ada_kernel,
        grid=(DEPTH, N_MOD),
        in_specs=[
            _full_spec((COND_ROWS, D_MODEL)),
            pl.BlockSpec((None, D_MODEL, D_MODEL), lambda l, n: (l, 0, n)),
            pl.BlockSpec((None, None, 1, D_MODEL), lambda l, n: (l, n, 0, 0)),
        ],
        out_specs=pl.BlockSpec((None, None, COND_ROWS, D_MODEL), lambda l, n: (l, n, 0, 0)),
        out_shape=jax.ShapeDtypeStruct((DEPTH, N_MOD, COND_ROWS, D_MODEL), F32),
        compiler_params=pltpu.CompilerParams(dimension_semantics=("arbitrary", "arbitrary"),
                                             vmem_limit_bytes=VMEM_LIMIT),
        name="ada_modulation",
    )(cond, ada_w, ada_b.reshape(DEPTH, N_MOD, 1, D_MODEL))


N_PREV = 6


def _prev_specs(layer, tile_fn, tiles_per_block=1):
    rows = tiles_per_block * TM
    ys = [pl.BlockSpec((None, rows, HALF), functools.partial(lambda k, *g: (k, tile_fn(*g), 0), k))
          for k in range(TOP_K)]
    return ys + [pl.BlockSpec((rows, LANES), lambda *g: (tile_fn(*g), 0)),
                 _mod_spec(layer - 1, 5, lambda *g: _tile_mod_row(tile_fn(*g) * tiles_per_block))]


def _apply_prev(x, prev_refs):
    y_refs, gt_ref, gate_ref = prev_refs[:TOP_K], prev_refs[TOP_K], prev_refs[TOP_K + 1]
    gt = gt_ref[...]
    acc_lo = acc_hi = None
    for k in range(TOP_K):
        lo, hi = _unpack_halves(y_refs[k][...])
        g = gt[:, k:k + 1]
        acc_lo = g * lo if acc_lo is None else acc_lo + g * lo
        acc_hi = g * hi if acc_hi is None else acc_hi + g * hi
    return x + gate_ref[...] * jnp.concatenate([acc_lo, acc_hi], axis=1)


N_ROUTE_OUT = 5


def _route_out_specs(tile_fn, tiles_per_block=1):
    tile = lambda w: pl.BlockSpec((tiles_per_block * TM, w), lambda *g: (tile_fn(*g), 0))
    return [tile(D_MODEL), tile(HALF), tile(LANES), tile(LANES),
            pl.BlockSpec((tiles_per_block * SUBLANES, LANES), lambda *g: (tile_fn(*g), 0))]


_ROUTE_OUT_SHAPES = [
    jax.ShapeDtypeStruct((N_TOK, D_MODEL), F32),
    jax.ShapeDtypeStruct((N_TOK, HALF), U32),
    jax.ShapeDtypeStruct((N_TOK, LANES), F32),
    jax.ShapeDtypeStruct((N_TOK, LANES), I32),
    jax.ShapeDtypeStruct((N_TILES * SUBLANES, LANES), F32),
]


def _post_mixer(x, mix, gate_ref, gffn_ref, shift_ref, scale_ref, rwhi_ref, rwlo_ref, rb_ref,
                xo_ref, hp_ref, gt_ref, ei_ref, cnt_ref):
    xn = x + gate_ref[...] * mix
    xo_ref[...] = xn
    h2 = _rms(xn, gffn_ref[...]) * (1.0 + scale_ref[...]) + shift_ref[...]
    hi = h2.astype(BF16)
    hif = hi.astype(F32)
    lo = (h2 - hif).astype(BF16)
    hp_ref[...] = _pack_halves(hif)
    rwhi = rwhi_ref[...]
    logits = _dot(hi, rwhi) + _dot(lo, rwhi) + _dot(hi, rwlo_ref[...]) + rb_ref[...]

    rows = x.shape[0]
    lane = lax.broadcasted_iota(I32, (rows, LANES), 1)
    lane_f = lane.astype(F32)
    work = jnp.where(lane < N_EXPERTS, logits, -jnp.inf)
    member = jnp.zeros((rows, LANES), F32)
    gates = jnp.zeros((rows, LANES), F32)
    ids = jnp.zeros((rows, LANES), F32)
    den = None
    top = None
    for k in range(TOP_K):
        m = jnp.max(work, axis=1, keepdims=True)
        idx = jnp.min(jnp.where(work == m, lane_f, float(LANES)), axis=1, keepdims=True)
        onehot = lane_f == idx
        work = jnp.where(onehot, -jnp.inf, work)
        member = member + onehot.astype(F32)
        if k == 0:
            top = m
        e = jnp.exp(m - top)
        den = e if den is None else den + e
        gates = gates + jnp.where(lane == k, e, 0.0)
        ids = ids + jnp.where(lane == k, idx, 0.0)
    gt_ref[...] = gates * (1.0 / den)
    ei_ref[...] = ids.astype(I32)
    for i in range(rows // TM):
        cnt_ref[SUBLANES * i:SUBLANES * (i + 1), :] = jnp.broadcast_to(
            jnp.sum(member[TM * i:TM * (i + 1), :], axis=0, keepdims=True), (SUBLANES, LANES))


def _rope(x, c, sa, sb):
    outs = []
    for h in range(N_GROUPS):
        xs = x[:, GROUP * h:GROUP * (h + 1)]
        outs.append(xs * c + pltpu.roll(xs, GROUP - 16, 1) * sa + pltpu.roll(xs, 16, 1) * sb)
    return jnp.concatenate(outs, axis=1)


PROJ_TILES = 2


def _even_proj_kernel(has_prev, *refs):
    x_ref = refs[0]
    refs = refs[1:]
    if has_prev:
        prev_refs = refs[:N_PREV]
        refs = refs[N_PREV:]
    (g_ref, shift_ref, scale_ref, w_ref, bd_ref, qg_ref, kg_ref, cos_ref, sa_ref, sb_ref) = refs[:10]
    outs = refs[10:]
    if has_prev:
        xo_ref = outs[0]
        outs = outs[1:]
    a_ref, q_ref, k_ref, v_ref, kf_ref, vf_ref = outs

    t = pl.program_id(0)
    x = x_ref[...]
    if has_prev:
        x = _apply_prev(x, prev_refs)
        xo_ref[...] = x
    h = (_rms(x, g_ref[...]) * (1.0 + scale_ref[...]) + shift_ref[...]).astype(BF16)

    a_ref[...] = _dot(h, w_ref[:, 0:HALF]).astype(BF16)
    zv = _dot(h, w_ref[:, 3 * HALF:4 * HALF])
    v_ref[...] = zv.astype(BF16)

    bd = bd_ref[...]

    def qk_norm(z, gain):
        shi, slo = _split_bf16(z * z)
        ssq = _dot(shi, bd) + _dot(slo, bd)
        return z * lax.rsqrt(ssq * (1.0 / QK_DIM) + EPS) * gain

    qn = qk_norm(_dot(h, w_ref[:, HALF:2 * HALF]), qg_ref[...]) * (QK_DIM ** -0.5 * LOG2E)
    kn = qk_norm(_dot(h, w_ref[:, 2 * HALF:3 * HALF]), kg_ref[...])

    is_ctx = t < CTX_TILES // PROJ_TILES

    @pl.when(is_ctx)
    def _():
        kf_ref[...] = kn
        vf_ref[...] = zv
        q_ref[...] = qn.astype(BF16)
        k_ref[...] = kn.astype(BF16)

    @pl.when(jnp.logical_not(is_ctx))
    def _():
        c, sa, sb = cos_ref[...], sa_ref[...], sb_ref[...]
        q_ref[...] = _rope(qn, c, sa, sb).astype(BF16)
        k_ref[...] = _rope(kn, c, sa, sb).astype(BF16)


def _even_proj(layer, x, prev, mods, norm_g, w_in, blockdiag, qg, kg, rope_tabs):
    has_prev = prev is not None
    rows = PROJ_TILES * TM
    ctx_blocks = CTX_TILES // PROJ_TILES
    mod_row = lambda t: _tile_mod_row(t * PROJ_TILES)
    tile = lambda w: pl.BlockSpec((rows, w), lambda t: (t, 0))
    rope_spec = pl.BlockSpec(
        (rows, GROUP), lambda t: (jnp.where(t < ctx_blocks, 0, (t - ctx_blocks) % (LAT_SEQ // rows)), 0))
    in_specs = [tile(D_MODEL)]
    args = [x]
    if has_prev:
        yg, gates = prev
        in_specs += _prev_specs(layer, lambda t: t, PROJ_TILES)
        args += [yg] * TOP_K + [gates, mods]
    in_specs += [
        _full_spec((1, D_MODEL)),
        _mod_spec(layer, 0, mod_row), _mod_spec(layer, 1, mod_row),
        _full_spec((D_MODEL, 4 * HALF)), _full_spec((HALF, HALF)),
        _full_spec((1, HALF)), _full_spec((1, HALF)),
        rope_spec, rope_spec, rope_spec,
    ]
    args += [norm_g, mods, mods, w_in, blockdiag, qg, kg, *rope_tabs]
    cache_tile = pl.BlockSpec((rows, HALF), lambda t: (jnp.minimum(t, ctx_blocks - 1), 0))
    out_specs = [tile(HALF)] * 4 + [cache_tile] * 2
    out_shape = [jax.ShapeDtypeStruct((N_TOK, HALF), BF16)] * 4 + \
                [jax.ShapeDtypeStruct((N_CTX, HALF), F32)] * 2
    if has_prev:
        out_specs = [tile(D_MODEL)] + out_specs
        out_shape = [jax.ShapeDtypeStruct((N_TOK, D_MODEL), F32)] + out_shape
    res = pl.pallas_call(
        functools.partial(_even_proj_kernel, has_prev),
        grid=(N_TILES // PROJ_TILES,),
        in_specs=in_specs, out_specs=out_specs, out_shape=out_shape,
        compiler_params=pltpu.CompilerParams(dimension_semantics=("arbitrary",),
                                             vmem_limit_bytes=VMEM_LIMIT),
        name="even_proj",
    )(*args)
    if has_prev:
        return res[0], res[1:]
    return x, res


def _even_mix_kernel(lambda_init, *refs):
    (x_ref, q_ref, ac_ref, kc_ref, vc_ref, al_ref, kl_ref, vl_ref, ck_ref, cv_ref,
     cnc_ref, snc_ref, cnl_ref, snl_ref) = refs[:14]
    shared = refs[14:]
    t = pl.program_id(0)

    @pl.when(t < CTX_TILES)
    def _():
        _even_mix_body(lambda_init, x_ref, q_ref, ac_ref, [(kc_ref, vc_ref, False)],
                       cnc_ref, snc_ref, *shared)

    @pl.when(t >= CTX_TILES)
    def _():
        _even_mix_body(lambda_init, x_ref, q_ref, al_ref, [(kl_ref, vl_ref, False), (ck_ref, cv_ref, True)],
                       cnl_ref, snl_ref, *shared)


def _even_mix_body(lambda_init, x_ref, q_ref, a_ref, kv_refs, cn_ref, sn_ref, *refs):
    (cs_ref, lq_ref, lk_ref, sg_ref, wo_ref,
     gate_ref, gffn_ref, shift_ref, scale_ref, rwhi_ref, rwlo_ref, rb_ref) = refs[:12]
    out_refs = refs[12:]

    a = a_ref[...]
    y1 = _dot(cn_ref[...], a).astype(BF16)
    y2 = _dot(sn_ref[...], a).astype(BF16)
    cs = cs_ref[...]
    pieces = []
    for g in range(N_GROUPS):
        sl = slice(GROUP * g, GROUP * (g + 1))
        pieces.append(_dot(jnp.concatenate([y1[:, sl], y2[:, sl]], axis=1), cs).astype(BF16))

    d = jnp.sum(lq_ref[...] * lk_ref[...], axis=1, keepdims=True)
    ed = jnp.exp(d)
    lam = ed[0:1, :] - ed[1:2, :] + lambda_init
    lane = lax.broadcasted_iota(I32, (TM, GROUP), 1)
    nt = (((1,), (1,)), ((), ()))
    for hd in range(N_GROUPS):
        sl = slice(GROUP * hd, GROUP * (hd + 1))
        qf = q_ref[:, sl].astype(F32)
        parts = []
        for k_ref, v_ref, is_f32 in kv_refs:
            kk, vv = k_ref[:, sl], v_ref[:, sl]
            parts.append((kk.astype(BF16), vv.astype(BF16)) if is_f32 else (kk, vv))

        def probs(qm):
            ss = [lax.dot_general(qm, kk, nt, preferred_element_type=F32) for kk, _ in parts]
            m = functools.reduce(jnp.maximum, [jnp.max(s, axis=1, keepdims=True) for s in ss])
            es = [jnp.exp2(s - m) for s in ss]
            den = functools.reduce(lambda u, w: u + w, [jnp.sum(e, axis=1, keepdims=True) for e in es])
            return es, 1.0 / den

        es0, inv0 = probs(jnp.where(lane < QK_DIM, qf, 0.0).astype(BF16))
        es1, inv1 = probs(jnp.where(lane >= QK_DIM, qf, 0.0).astype(BF16))
        o0 = functools.reduce(lambda u, w: u + w,
                              [_dot(e.astype(BF16), vv) for e, (_, vv) in zip(es0, parts)])
        o1 = functools.reduce(lambda u, w: u + w,
                              [_dot(e.astype(BF16), vv) for e, (_, vv) in zip(es1, parts)])
        o = o0 * inv0 - o1 * (lam * inv1)
        pieces.append((_rms(o, sg_ref[...]) * (1.0 - lambda_init)).astype(BF16))

    mix = _dot(jnp.concatenate(pieces, axis=1), wo_ref[...])
    _post_mixer(x_ref[...], mix, gate_ref, gffn_ref, shift_ref, scale_ref, rwhi_ref, rwlo_ref, rb_ref,
                *out_refs)


def _even_mix(layer, x, a, q, k, v, cache_k, cache_v, dft_ctx, dft_lat, cs128, lam_q, lam_k, subln_g,
              w_out, mods, gffn, rw_hi, rw_lo, rb):
    j = layer // 2
    lambda_init = 0.8 - 0.6 * math.exp(-0.3 * layer)
    lat = lambda t: jnp.maximum(t - CTX_TILES, 0)
    tile = lambda w: pl.BlockSpec((TM, w), lambda t: (t, 0))
    ctx_seq = pl.BlockSpec((CTX_SEQ, HALF), lambda t: (jnp.minimum(t, CTX_TILES - 1), 0))
    lat_seq = pl.BlockSpec((LAT_SEQ, HALF),
                           lambda t: (N_CTX // LAT_SEQ + lat(t) // LAT_TILES_PER_BATCH, 0))
    cache = pl.BlockSpec((None, None, PAST_LEN, HALF),
                         lambda t: (lat(t) // LAT_TILES_PER_BATCH, j, 0, 0))
    dft_lat_spec = pl.BlockSpec((TM, LAT_SEQ), lambda t: (lat(t) % LAT_TILES_PER_BATCH, 0))
    in_specs = [
        tile(D_MODEL), tile(HALF), ctx_seq, ctx_seq, ctx_seq, lat_seq, lat_seq, lat_seq, cache, cache,
        _full_spec((CTX_SEQ, CTX_SEQ)), _full_spec((CTX_SEQ, CTX_SEQ)), dft_lat_spec, dft_lat_spec,
        _full_spec((2 * GROUP, GROUP)),
        _full_spec((2, QK_DIM)), _full_spec((2, QK_DIM)), _full_spec((1, GROUP)),
        _full_spec((D_MODEL, D_MODEL)),
        _mod_spec(layer, 2, _tile_mod_row), _full_spec((1, D_MODEL)),
        _mod_spec(layer, 3, _tile_mod_row), _mod_spec(layer, 4, _tile_mod_row),
        _full_spec((D_MODEL, LANES)), _full_spec((D_MODEL, LANES)), _full_spec((1, LANES)),
    ]
    args = [x, q, a, k, v, a, k, v, cache_k, cache_v, dft_ctx[0], dft_ctx[1], dft_lat[0], dft_lat[1],
            cs128, lam_q, lam_k, subln_g, w_out, mods, gffn, mods, mods, rw_hi, rw_lo, rb]
    return pl.pallas_call(
        functools.partial(_even_mix_kernel, lambda_init),
        grid=(N_TILES,),
        in_specs=in_specs,
        out_specs=_route_out_specs(lambda t: t),
        out_shape=_ROUTE_OUT_SHAPES,
        compiler_params=pltpu.CompilerParams(dimension_semantics=("arbitrary",),
                                             vmem_limit_bytes=VMEM_LIMIT),
        name="even_mix",
    )(*args)


ODD_TILES = 2


def _odd_kernel(has_prev, *refs):
    x_ref = refs[0]
    refs = refs[1:]
    if has_prev:
        prev_refs = refs[:N_PREV]
        refs = refs[N_PREV:]
    (g_ref, shift_ref, scale_ref, w_ref, vg_ref, ws_ref, bs_ref, wo_ref,
     gate_ref, gffn_ref, shift2_ref, scale2_ref, rwhi_ref, rwlo_ref, rb_ref) = refs[:15]
    out_refs = refs[15:]

    x = x_ref[...]
    if has_prev:
        x = _apply_prev(x, prev_refs)
    h = (_rms(x, g_ref[...]) * (1.0 + scale_ref[...]) + shift_ref[...]).astype(BF16)

    def gelu(z):
        return 0.5 * z * (1.0 + lax.erf(z * (2.0 ** -0.5)))

    u = gelu(_dot(h, w_ref[:, 0:D_MODEL]))
    vn = _rms(gelu(_dot(h, w_ref[:, D_MODEL:2 * D_MODEL])), vg_ref[...]).astype(BF16)
    bs = bs_ref[...]
    rows = []
    for c in range(x.shape[0] // CHUNK):
        cols = []
        for g in range(C_GROUPS):
            cols.append(_dot(ws_ref[g], vn[CHUNK * c:CHUNK * (c + 1), GROUP * g:GROUP * (g + 1)]))
        rows.append(jnp.concatenate(cols, axis=1) + bs)
    sv = jnp.concatenate(rows, axis=0)
    mix = _dot((u * sv).astype(BF16), wo_ref[...])
    _post_mixer(x, mix, gate_ref, gffn_ref, shift2_ref, scale2_ref, rwhi_ref, rwlo_ref, rb_ref,
                *out_refs)


def _odd_layer(layer, x, prev, mods, norm_g, w_in, v_norm_g, w_s, b_s, w_out, gffn, rw_hi, rw_lo, rb):
    has_prev = prev is not None
    mod_row = lambda t: _tile_mod_row(t * ODD_TILES)
    in_specs = [pl.BlockSpec((ODD_TILES * TM, D_MODEL), lambda t: (t, 0))]
    args = [x]
    if has_prev:
        yg, gates = prev
        in_specs += _prev_specs(layer, lambda t: t, ODD_TILES)
        args += [yg] * TOP_K + [gates, mods]
    in_specs += [
        _full_spec((1, D_MODEL)),
        _mod_spec(layer, 0, mod_row), _mod_spec(layer, 1, mod_row),
        _full_spec((D_MODEL, 2 * D_MODEL)), _full_spec((1, D_MODEL)),
        _full_spec((C_GROUPS, CHUNK, CHUNK)), _full_spec((CHUNK, D_MODEL)),
        _full_spec((D_MODEL, D_MODEL)),
        _mod_spec(layer, 2, mod_row), _full_spec((1, D_MODEL)),
        _mod_spec(layer, 3, mod_row), _mod_spec(layer, 4, mod_row),
        _full_spec((D_MODEL, LANES)), _full_spec((D_MODEL, LANES)), _full_spec((1, LANES)),
    ]
    args += [norm_g, mods, mods, w_in, v_norm_g, w_s, b_s, w_out, mods, gffn, mods, mods, rw_hi, rw_lo, rb]
    return pl.pallas_call(
        functools.partial(_odd_kernel, has_prev),
        grid=(N_TILES // ODD_TILES,),
        in_specs=in_specs,
        out_specs=_route_out_specs(lambda t: t, ODD_TILES),
        out_shape=_ROUTE_OUT_SHAPES,
        compiler_params=pltpu.CompilerParams(dimension_semantics=("arbitrary",),
                                             vmem_limit_bytes=VMEM_LIMIT),
        name="odd_layer",
    )(*args)


def _slot_kernel(ei_ref, base_ref, tri_ref, o_ref):
    lane = lax.broadcasted_iota(I32, (TM, LANES), 1)
    tri = tri_ref[...]
    for i in range(SLOT_TILES):
        ei = ei_ref[TM * i:TM * (i + 1), :]
        onehots = [lane == ei[:, k:k + 1] for k in range(TOP_K)]
        member = functools.reduce(lambda u, w: u + w, [o.astype(F32) for o in onehots])
        before = _dot(tri, member.astype(BF16)) + base_ref[SUBLANES * i:SUBLANES * i + 1, :]
        slots = jnp.zeros((TM, LANES), F32)
        for k in range(TOP_K):
            s = jnp.sum(jnp.where(onehots[k], before, 0.0), axis=1, keepdims=True)
            slots = slots + jnp.where(lane == k, s, 0.0)
        o_ref[i] = slots.T[0:SUBLANES, :].astype(I32)


SLOT_TILES = 4


def _slots(eidx, tile_base, tri):
    return pl.pallas_call(
        _slot_kernel,
        grid=(N_TILES // SLOT_TILES,),
        in_specs=[pl.BlockSpec((SLOT_TILES * TM, LANES), lambda t: (t, 0)),
                  pl.BlockSpec((SLOT_TILES * SUBLANES, LANES), lambda t: (t, 0)),
                  _full_spec((TM, TM))],
        out_specs=pl.BlockSpec((SLOT_TILES, SUBLANES, TM), lambda t: (t, 0, 0)),
        out_shape=jax.ShapeDtypeStruct((N_TILES, SUBLANES, TM), I32),
        compiler_params=pltpu.CompilerParams(dimension_semantics=("arbitrary",)),
        name="moe_slots",
    )(eidx, tile_base, tri)


def _sc_mesh():
    return plsc.VectorSubcoreMesh(core_axis_name="c", subcore_axis_name="s")


def _sc_worker():
    return lax.axis_index("s") * SC_CORES + lax.axis_index("c")


def _dispatch(hp, idx):
    n_chunks = N_TOK // SC_WORKERS // SC_ROWS

    @functools.partial(
        pl.kernel, mesh=_sc_mesh(),
        out_type=jax.ShapeDtypeStruct((N_SLOTS, HALF), U32),
        scratch_types=[pltpu.VMEM((n_chunks, TOP_K, SC_ROWS), I32), pltpu.VMEM((SC_BUFS, SC_ROWS, HALF), U32),
                       pltpu.SemaphoreType.DMA((SC_BUFS,)), pltpu.SemaphoreType.DMA((SC_BUFS,))],
        name="moe_dispatch",
    )
    def k(x_hbm, idx_hbm, out_hbm, idx_v, rows_v, read_sem, scat_sem):
        wid = _sc_worker()
        base = wid * n_chunks
        pltpu.sync_copy(idx_hbm.at[wid], idx_v)

        def read(j, b):
            return pltpu.make_async_copy(x_hbm.at[pl.ds((base + j) * SC_ROWS, SC_ROWS)], rows_v.at[b],
                                         read_sem.at[b])

        def scatter(j, b, kk):
            return pltpu.make_async_copy(rows_v.at[b], out_hbm.at[idx_v.at[j, kk]], scat_sem.at[b])

        def drain(j, b):
            for kk in range(TOP_K):
                scatter(j, b, kk).wait()

        read(0, 0).start()

        @pl.loop(0, n_chunks, step=SC_BUFS)
        def _(j):
            for b in range(SC_BUFS):
                jj = j + b
                other = (b + 1) % SC_BUFS
                read(jj, b).wait()

                @pl.when(jj >= 1)
                def _():
                    drain(jj - 1, other)

                @pl.when(jj + 1 < n_chunks)
                def _():
                    read(jj + 1, other).start()

                for kk in range(TOP_K):
                    scatter(jj, b, kk).start()

        drain(n_chunks - 1, (n_chunks - 1) % SC_BUFS)

    return k(hp, idx)


def _combine_gather(ys, idx):
    n_chunks = N_ASSIGN // SC_WORKERS // SC_ROWS

    @functools.partial(
        pl.kernel, mesh=_sc_mesh(),
        out_type=jax.ShapeDtypeStruct((N_ASSIGN, HALF), U32),
        scratch_types=[pltpu.VMEM((n_chunks, SC_ROWS), I32), pltpu.VMEM((SC_BUFS, SC_ROWS, HALF), U32),
                       pltpu.SemaphoreType.DMA((SC_BUFS,)), pltpu.SemaphoreType.DMA((SC_BUFS,))],
        name="moe_combine",
    )
    def k(ys_hbm, idx_hbm, out_hbm, idx_v, rows_v, gather_sem, write_sem):
        wid = _sc_worker()
        base = wid * n_chunks
        pltpu.sync_copy(idx_hbm.at[wid], idx_v)

        def gather(j, b):
            return pltpu.make_async_copy(ys_hbm.at[idx_v.at[j]], rows_v.at[b], gather_sem.at[b])

        def write(j, b):
            return pltpu.make_async_copy(rows_v.at[b], out_hbm.at[pl.ds((base + j) * SC_ROWS, SC_ROWS)],
                                         write_sem.at[b])

        gather(0, 0).start()

        @pl.loop(0, n_chunks, step=SC_BUFS)
        def _(j):
            for b in range(SC_BUFS):
                jj = j + b
                other = (b + 1) % SC_BUFS
                gather(jj, b).wait()

                @pl.when(jj >= 1)
                def _():
                    write(jj - 1, other).wait()

                @pl.when(jj + 1 < n_chunks)
                def _():
                    gather(jj + 1, other).start()

                write(jj, b).start()

        write(n_chunks - 1, (n_chunks - 1) % SC_BUFS).wait()

    return k(ys, idx)


N_MATS = 3


def _moe_kernel(layer, first_ref, nblk_ref, xs_hbm, wg_hbm, wu_hbm, wd_hbm, bg_ref, bu_ref, bd_ref,
                ys_hbm, wf32, wbf, xbuf, obuf, wsem, xsem, osem):
    e = pl.program_id(0)
    slot = e % 2
    w_hbm = (wg_hbm, wu_hbm, wd_hbm)
    nb = nblk_ref[e]
    row0 = first_ref[e] * MOE_BLOCK
    n_chunks = nb // BLOCKS_PER_CHUNK
    has_tail = nb % BLOCKS_PER_CHUNK == 1
    tail_row = row0 + n_chunks * MOE_CHUNK
    tail_slot = n_chunks % 2

    def w_copy(ee, s, m):
        return pltpu.make_async_copy(w_hbm[m].at[layer, ee], wf32.at[s, m], wsem.at[s, m])

    def x_copy(row, rows, s):
        return pltpu.make_async_copy(xs_hbm.at[pl.ds(row, rows)], xbuf.at[s, pl.ds(0, rows)], xsem.at[s])

    def o_copy(row, rows, s):
        return pltpu.make_async_copy(obuf.at[s, pl.ds(0, rows)], ys_hbm.at[pl.ds(row, rows)], osem.at[s])

    def expert(xp):
        lo, hi = _unpack_halves(xp)
        xb = jnp.concatenate([lo.astype(BF16), hi.astype(BF16)], axis=1)
        gt = jnp.minimum(_dot(xb, wbf[0]) + bg_ref[...], SWIGLU_LIMIT)
        up = jnp.clip(_dot(xb, wbf[1]) + bu_ref[...], -SWIGLU_LIMIT, SWIGLU_LIMIT)
        glu = gt * jax.nn.sigmoid(SWIGLU_ALPHA * gt)
        hmid = ((up + 1.0) * glu).astype(BF16)
        out = _dot(hmid, wbf[2]) + bd_ref[...]
        return _pack_halves(out.astype(BF16).astype(F32))

    @pl.when(e == 0)
    def _():
        for m in range(N_MATS):
            w_copy(0, 0, m).start()

    @pl.when(n_chunks > 0)
    def _():
        x_copy(row0, MOE_CHUNK, 0).start()

    @pl.when((n_chunks == 0) & has_tail)
    def _():
        x_copy(row0, MOE_BLOCK, 0).start()

    for m in range(N_MATS):
        w_copy(e, slot, m).wait()

    @pl.when(e + 1 < N_EXPERTS)
    def _():
        for m in range(N_MATS):
            w_copy(e + 1, 1 - slot, m).start()

    for m in range(N_MATS):
        wbf[m] = wf32[slot, m].astype(BF16)

    def chunk(c, carry):
        s = c % 2
        row = row0 + c * MOE_CHUNK
        x_copy(row, MOE_CHUNK, s).wait()

        @pl.when(c + 1 < n_chunks)
        def _():
            x_copy(row + MOE_CHUNK, MOE_CHUNK, 1 - s).start()

        @pl.when((c + 1 == n_chunks) & has_tail)
        def _():
            x_copy(tail_row, MOE_BLOCK, 1 - s).start()

        obuf[s] = expert(xbuf[s])

        @pl.when(c >= 1)
        def _():
            o_copy(row - MOE_CHUNK, MOE_CHUNK, 1 - s).wait()

        o_copy(row, MOE_CHUNK, s).start()
        return carry

    lax.fori_loop(0, n_chunks, chunk, 0)

    @pl.when(has_tail)
    def _():
        x_copy(tail_row, MOE_BLOCK, tail_slot).wait()
        obuf[tail_slot, 0:MOE_BLOCK, :] = expert(xbuf[tail_slot, 0:MOE_BLOCK, :])

        @pl.when(n_chunks >= 1)
        def _():
            o_copy(tail_row - MOE_CHUNK, MOE_CHUNK, 1 - tail_slot).wait()

        o_copy(tail_row, MOE_BLOCK, tail_slot).start()
        o_copy(tail_row, MOE_BLOCK, tail_slot).wait()

    @pl.when(jnp.logical_not(has_tail) & (n_chunks >= 1))
    def _():
        o_copy(tail_row - MOE_CHUNK, MOE_CHUNK, 1 - tail_slot).wait()


def _moe_experts(layer, xs, first_block, n_blocks, w_gate, b_gate, w_up, b_up, w_down, b_down):
    hbm = pl.BlockSpec(memory_space=pl.ANY)
    bspec = pl.BlockSpec((None, None, 1, D_MODEL), lambda e, fb, nb: (layer, e, 0, 0))
    bias = lambda b: b.reshape(DEPTH, N_EXPERTS, 1, D_MODEL)
    return pl.pallas_call(
        functools.partial(_moe_kernel, layer),
        grid_spec=pltpu.PrefetchScalarGridSpec(
            num_scalar_prefetch=2,
            grid=(N_EXPERTS,),
            in_specs=[hbm, hbm, hbm, hbm, bspec, bspec, bspec],
            out_specs=hbm,
            scratch_shapes=[
                pltpu.VMEM((2, N_MATS, D_MODEL, D_MODEL), F32),
                pltpu.VMEM((N_MATS, D_MODEL, D_MODEL), BF16),
                pltpu.VMEM((2, MOE_CHUNK, HALF), U32),
                pltpu.VMEM((2, MOE_CHUNK, HALF), U32),
                pltpu.SemaphoreType.DMA((2, N_MATS)),
                pltpu.SemaphoreType.DMA((2,)),
                pltpu.SemaphoreType.DMA((2,)),
            ],
        ),
        out_shape=jax.ShapeDtypeStruct((N_SLOTS, HALF), U32),
        compiler_params=pltpu.CompilerParams(dimension_semantics=("arbitrary",),
                                             vmem_limit_bytes=VMEM_LIMIT),
        name="moe_experts",
    )(first_block, n_blocks, xs, w_gate, w_up, w_down, bias(b_gate), bias(b_up), bias(b_down))


def _moe(layer, hp, eidx, cnt, tri, w_gate, b_gate, w_up, b_up, w_down, b_down):
    tile_cnt = cnt[::SUBLANES, :N_EXPERTS].astype(I32)
    counts = jnp.sum(tile_cnt, axis=0)
    padded = (counts + MOE_BLOCK - 1) // MOE_BLOCK * MOE_BLOCK
    pad_ends = jnp.cumsum(padded)
    pad_starts = pad_ends - padded
    tile_base = pad_starts[None, :] + jnp.cumsum(tile_cnt, axis=0) - tile_cnt
    tile_base = jnp.pad(tile_base.astype(F32), ((0, 0), (0, LANES - N_EXPERTS)))
    tile_base = jnp.repeat(tile_base, SUBLANES, axis=0)
    first_block = (pad_starts // MOE_BLOCK).astype(I32)
    n_blocks = (padded // MOE_BLOCK).astype(I32)

    slots = _slots(eidx, tile_base, tri)[:, :TOP_K, :]
    per_tile = TM // SC_ROWS
    d_idx = slots.reshape(N_TILES, TOP_K, per_tile, SC_ROWS).transpose(0, 2, 1, 3)
    d_idx = d_idx.reshape(SC_WORKERS, N_TOK // SC_WORKERS // SC_ROWS, TOP_K, SC_ROWS)
    c_idx = slots.transpose(1, 0, 2).reshape(SC_WORKERS, N_ASSIGN // SC_WORKERS // SC_ROWS, SC_ROWS)

    xs = _dispatch(hp, d_idx)
    ys = _moe_experts(layer, xs, first_block, n_blocks, w_gate, b_gate, w_up, b_up, w_down, b_down)
    return _combine_gather(ys, c_idx).reshape(TOP_K, N_TOK, HALF)


FINAL_TILES = 2


def _final_kernel(x_ref, *refs):
    o_ref = refs[N_PREV]
    o_ref[...] = _apply_prev(x_ref[...], refs[:N_PREV])


def _final_residual(x, prev, mods, first_tile, n_tiles):
    rows = FINAL_TILES * TM
    block_fn = lambda t: first_tile // FINAL_TILES + t
    yg, gates = prev
    return pl.pallas_call(
        _final_kernel,
        grid=(n_tiles // FINAL_TILES,),
        in_specs=[pl.BlockSpec((rows, D_MODEL), lambda t: (block_fn(t), 0))]
        + _prev_specs(DEPTH, block_fn, FINAL_TILES),
        out_specs=pl.BlockSpec((rows, D_MODEL), lambda t: (t, 0)),
        out_shape=jax.ShapeDtypeStruct((n_tiles * TM, D_MODEL), F32),
        compiler_params=pltpu.CompilerParams(dimension_semantics=("arbitrary",)),
        name="final_residual",
    )(x, *([yg] * TOP_K), gates, mods)


def _dft_pair(n):
    idx = np.arange(n)
    ang = 2.0 * np.pi * ((idx[:, None] * idx[None, :]) % n) / n
    return np.cos(ang) / np.sqrt(n), np.sin(ang) / np.sqrt(n)


def _rope_tables():
    lane = np.arange(GROUP)
    within = lane % QK_DIM
    axis = within // 32
    e = within % 32
    inv16 = ROPE_BASE ** (-jnp.arange(16, dtype=F32) / 16)
    pos = np.arange(LAT_SEQ)
    coord = np.where(axis[None, :] == 0, (pos // GRID_W)[:, None], (pos % GRID_W)[:, None])
    ang = jnp.asarray(coord, F32) * inv16[e % 16][None, :]
    first = jnp.asarray((e // 16) == 0)[None, :]
    cos, sin = jnp.cos(ang), jnp.sin(ang)
    return cos, jnp.where(first, -sin, 0.0), jnp.where(first, 0.0, sin)


def kernel(x_prompt, x_sample, cache_k, cache_v, c, c_ctx, ada_w, ada_b, norm_mix_g, norm_ffn_g,
           w_in_even, w_out_even, q_norm_g, k_norm_g, lambda_q, lambda_k, subln_g,
           w_in_odd, v_norm_g, w_spatial, b_spatial, w_out_odd,
           router_w, router_b, w_gate, b_gate, w_up, b_up, w_down, b_down):
    n_even = w_in_even.shape[0]
    x = jnp.concatenate([x_prompt.reshape(N_CTX, D_MODEL), x_sample.reshape(N_LAT, D_MODEL)], axis=0)
    cond = jnp.zeros((COND_ROWS, D_MODEL), F32).at[0].set(c_ctx).at[1:1 + LAT_BATCH].set(c)
    mods = _ada_modulation(cond, ada_w, ada_b).reshape(DEPTH, N_MOD, COND_ROWS, 1, D_MODEL)

    dft_ctx = [jnp.asarray(m, F32).astype(BF16) for m in _dft_pair(CTX_SEQ)]
    dft_lat = [jnp.asarray(m, F32).astype(BF16) for m in _dft_pair(LAT_SEQ)]
    c128, s128 = _dft_pair(GROUP)
    cs128 = jnp.asarray(np.concatenate([c128, -s128], axis=0), F32).astype(BF16)
    grp = np.arange(HALF) // QK_DIM
    blockdiag = jnp.asarray(grp[:, None] == grp[None, :], F32).astype(BF16)
    tri = jnp.asarray(np.arange(TM)[:, None] > np.arange(TM)[None, :], F32).astype(BF16)
    rope_tabs = _rope_tables()
    cache_k2 = cache_k.reshape(LAT_BATCH, n_even, PAST_LEN, HALF)
    cache_v2 = cache_v.reshape(LAT_BATCH, n_even, PAST_LEN, HALF)

    rw = jnp.pad(router_w, ((0, 0), (0, 0), (0, LANES - N_EXPERTS)))
    rw_hi = rw.astype(BF16)
    rw_lo = (rw - rw_hi.astype(F32)).astype(BF16)
    rb = jnp.pad(router_b, ((0, 0), (0, LANES - N_EXPERTS)))[:, None, :]

    prev = None
    new_k, new_v = [], []
    for l in range(DEPTH):
        j = l // 2
        gmix = norm_mix_g[l][None, :]
        gffn = norm_ffn_g[l][None, :]
        if l % 2 == 0:
            x, (a, q, k, v, kf, vf) = _even_proj(
                l, x, prev, mods, gmix, w_in_even[j].astype(BF16), blockdiag,
                jnp.tile(q_norm_g[j], HALF // QK_DIM)[None, :], jnp.tile(k_norm_g[j], HALF // QK_DIM)[None, :],
                rope_tabs)
            new_k.append(kf.reshape(CTX_BATCH, CTX_SEQ, N_GROUPS, 2, QK_DIM))
            new_v.append(vf.reshape(CTX_BATCH, CTX_SEQ, N_GROUPS, GROUP))
            common = (cs128, lambda_q[j], lambda_k[j], subln_g[j][None, :], w_out_even[j].astype(BF16),
                      mods, gffn, rw_hi[l], rw_lo[l], rb[l])
            x, hp, gates, eidx, cnt = _even_mix(l, x, a, q, k, v, cache_k2, cache_v2, dft_ctx, dft_lat, *common)
        else:
            b_s = jnp.broadcast_to(jnp.transpose(b_spatial[j])[:, :, None],
                                   (CHUNK, C_GROUPS, GROUP)).reshape(CHUNK, D_MODEL)
            x, hp, gates, eidx, cnt = _odd_layer(
                l, x, prev, mods, gmix, w_in_odd[j].astype(BF16), v_norm_g[j][None, :],
                w_spatial[j].astype(BF16), b_s, w_out_odd[j].astype(BF16), gffn, rw_hi[l], rw_lo[l], rb[l])
        yg = _moe(l, hp, eidx, cnt, tri, w_gate, b_gate, w_up, b_up, w_down, b_down)
        prev = (yg, gates)
    y_ctx = _final_residual(x, prev, mods, 0, CTX_TILES)
    y_lat = _final_residual(x, prev, mods, CTX_TILES, N_TILES - CTX_TILES)
    return (y_ctx.reshape(CTX_BATCH, CTX_SEQ, D_MODEL),
            y_lat.reshape(LAT_BATCH, LAT_SEQ, D_MODEL),
            jnp.stack(new_k, axis=1),
            jnp.stack(new_v, axis=1))
```

```python
import functools
import math

import numpy as np
import jax
import jax.numpy as jnp
from jax import lax
from jax.experimental import pallas as pl
from jax.experimental.pallas import tpu as pltpu
from jax.experimental.pallas import tpu_sc as plsc

F32 = jnp.float32
BF16 = jnp.bfloat16
U32 = jnp.uint32
I32 = jnp.int32

D_MODEL = 1024
DEPTH = 4
N_MOD = 6
EPS = 1e-6
CTX_BATCH, CTX_SEQ = 32, 256
LAT_BATCH, LAT_SEQ = 8, 1024
PAST_LEN = 512
GRID_W = 64
N_CTX = CTX_BATCH * CTX_SEQ
N_LAT = LAT_BATCH * LAT_SEQ
N_TOK = N_CTX + N_LAT
TM = 256
N_TILES = N_TOK // TM
CTX_TILES = N_CTX // TM
LAT_TILES_PER_BATCH = LAT_SEQ // TM
COND_ROWS = 16
GROUP = 128
N_GROUPS = 4
QK_DIM = 64
HALF = 512
CHUNK = 128
C_GROUPS = 8
N_EXPERTS = 32
TOP_K = 4
LANES = 128
SUBLANES = 8
SWIGLU_LIMIT = 7.0
SWIGLU_ALPHA = 1.702
MOE_BLOCK = 256
BLOCKS_PER_CHUNK = 2
MOE_CHUNK = BLOCKS_PER_CHUNK * MOE_BLOCK
N_ASSIGN = N_TOK * TOP_K
MOE_BLOCKS = N_ASSIGN // MOE_BLOCK + N_EXPERTS
N_SLOTS = MOE_BLOCKS * MOE_BLOCK
ROPE_BASE = 10000.0
VMEM_LIMIT = 52 * 1024 * 1024
SC_CORES = 2
SC_SUBCORES = 16
SC_WORKERS = SC_CORES * SC_SUBCORES
SC_ROWS = 64
SC_BUFS = 2
HI_MASK = 0xFFFF0000
LOG2E = math.log2(math.e)


def _dot(a, b):
    return jnp.dot(a, b, preferred_element_type=F32)


def _rms(x, g):
    return x * lax.rsqrt(jnp.mean(x * x, axis=-1, keepdims=True) + EPS) * g


def _split_bf16(x):
    hi = x.astype(BF16)
    lo = (x - hi.astype(F32)).astype(BF16)
    return hi, lo


def _pack_halves(xf):
    b = lax.bitcast_convert_type(xf, U32)
    return (b[:, :HALF] >> 16) | (b[:, HALF:] & jnp.uint32(HI_MASK))


def _unpack_halves(w):
    lo = lax.bitcast_convert_type(w << 16, F32)
    hi = lax.bitcast_convert_type(w & jnp.uint32(HI_MASK), F32)
    return lo, hi


def _tile_mod_row(t):
    return jnp.where(t < CTX_TILES, 0, 1 + (t - CTX_TILES) // LAT_TILES_PER_BATCH)


def _mod_spec(layer, k, row_fn):
    return pl.BlockSpec((None, None, None, 1, D_MODEL),
                        lambda *g: (layer, k, row_fn(*g), 0, 0))


def _full_spec(shape):
    return pl.BlockSpec(shape, lambda *g: (0,) * len(shape))


def _ada_kernel(cond_ref, w_ref, b_ref, o_ref):
    c = cond_ref[...]
    s = (c * jax.nn.sigmoid(c)).astype(BF16)
    o_ref[...] = _dot(s, w_ref[...].astype(BF16)) + b_ref[...]


def _ada_modulation(cond, ada_w, ada_b):
    return pl.pallas_call(
        _ada_kernel,
        grid=(DEPTH, N_MOD),
        in_specs=[
            _full_spec((COND_ROWS, D_MODEL)),
            pl.BlockSpec((None, D_MODEL, D_MODEL), lambda l, n: (l, 0, n)),
            pl.BlockSpec((None, None, 1, D_MODEL), lambda l, n: (l, n, 0, 0)),
        ],
        out_specs=pl.BlockSpec((None, None, COND_ROWS, D_MODEL), lambda l, n: (l, n, 0, 0)),
        out_shape=jax.ShapeDtypeStruct((DEPTH, N_MOD, COND_ROWS, D_MODEL), F32),
        compiler_params=pltpu.CompilerParams(dimension_semantics=("arbitrary", "arbitrary"),
                                             vmem_limit_bytes=VMEM_LIMIT),
        name="ada_modulation",
    )(cond, ada_w, ada_b.reshape(DEPTH, N_MOD, 1, D_MODEL))


N_PREV = 6


def _group_blocks(group, tiles_per_block):
    ctx = CTX_TILES // tiles_per_block
    return (0, ctx) if group == 0 else (ctx, N_TILES // tiles_per_block - ctx)


def _prev_specs(layer, first_block, tiles_per_block):
    rows = tiles_per_block * TM
    ys = [pl.BlockSpec((None, rows, HALF), functools.partial(lambda k, t: (k, t, 0), k))
          for k in range(TOP_K)]
    return ys + [pl.BlockSpec((rows, LANES), lambda t: (first_block + t, 0)),
                 _mod_spec(layer - 1, 5, lambda t: _tile_mod_row((first_block + t) * tiles_per_block))]


def _apply_prev(x, prev_refs):
    y_refs, gt_ref, gate_ref = prev_refs[:TOP_K], prev_refs[TOP_K], prev_refs[TOP_K + 1]
    gt = gt_ref[...]
    acc_lo = acc_hi = None
    for k in range(TOP_K):
        lo, hi = _unpack_halves(y_refs[k][...])
        g = gt[:, k:k + 1]
        acc_lo = g * lo if acc_lo is None else acc_lo + g * lo
        acc_hi = g * hi if acc_hi is None else acc_hi + g * hi
    return x + gate_ref[...] * jnp.concatenate([acc_lo, acc_hi], axis=1)


N_ROUTE_OUT = 5


def _route_out_specs(tile_fn, tiles_per_block=1):
    tile = lambda w: pl.BlockSpec((tiles_per_block * TM, w), lambda *g: (tile_fn(*g), 0))
    return [tile(D_MODEL), tile(HALF), tile(LANES), tile(LANES),
            pl.BlockSpec((tiles_per_block * SUBLANES, LANES), lambda *g: (tile_fn(*g), 0))]


_ROUTE_OUT_SHAPES = [
    jax.ShapeDtypeStruct((N_TOK, D_MODEL), F32),
    jax.ShapeDtypeStruct((N_TOK, HALF), U32),
    jax.ShapeDtypeStruct((N_TOK, LANES), F32),
    jax.ShapeDtypeStruct((N_TOK, LANES), I32),
    jax.ShapeDtypeStruct((N_TILES * SUBLANES, LANES), F32),
]


def _post_mixer(x, mix, gate_ref, gffn_ref, shift_ref, scale_ref, rwhi_ref, rwlo_ref, rb_ref,
                xo_ref, hp_ref, gt_ref, ei_ref, cnt_ref):
    xn = x + gate_ref[...] * mix
    xo_ref[...] = xn
    h2 = _rms(xn, gffn_ref[...]) * (1.0 + scale_ref[...]) + shift_ref[...]
    hi = h2.astype(BF16)
    hif = hi.astype(F32)
    lo = (h2 - hif).astype(BF16)
    hp_ref[...] = _pack_halves(hif)
    rwhi = rwhi_ref[...]
    logits = _dot(hi, rwhi) + _dot(lo, rwhi) + _dot(hi, rwlo_ref[...]) + rb_ref[...]

    rows = x.shape[0]
    lane = lax.broadcasted_iota(I32, (rows, LANES), 1)
    lane_f = lane.astype(F32)
    work = jnp.where(lane < N_EXPERTS, logits, -jnp.inf)
    member = jnp.zeros((rows, LANES), F32)
    gates = jnp.zeros((rows, LANES), F32)
    ids = jnp.zeros((rows, LANES), F32)
    den = None
    top = None
    for k in range(TOP_K):
        m = jnp.max(work, axis=1, keepdims=True)
        idx = jnp.min(jnp.where(work == m, lane_f, float(LANES)), axis=1, keepdims=True)
        onehot = lane_f == idx
        work = jnp.where(onehot, -jnp.inf, work)
        member = member + onehot.astype(F32)
        if k == 0:
            top = m
        e = jnp.exp(m - top)
        den = e if den is None else den + e
        gates = gates + jnp.where(lane == k, e, 0.0)
        ids = ids + jnp.where(lane == k, idx, 0.0)
    gt_ref[...] = gates * (1.0 / den)
    ei_ref[...] = ids.astype(I32)
    for i in range(rows // TM):
        cnt_ref[SUBLANES * i:SUBLANES * (i + 1), :] = jnp.broadcast_to(
            jnp.sum(member[TM * i:TM * (i + 1), :], axis=0, keepdims=True), (SUBLANES, LANES))


def _rope(x, c, sa, sb):
    outs = []
    for h in range(N_GROUPS):
        xs = x[:, GROUP * h:GROUP * (h + 1)]
        outs.append(xs * c + pltpu.roll(xs, GROUP - 16, 1) * sa + pltpu.roll(xs, 16, 1) * sb)
    return jnp.concatenate(outs, axis=1)


PROJ_TILES = 2


N_PROJ_OUT = 5


def _even_proj_kernel(group, has_prev, *refs):
    x_ref = refs[0]
    refs = refs[1:]
    if has_prev:
        prev_refs = refs[:N_PREV]
        refs = refs[N_PREV:]
    (g_ref, shift_ref, scale_ref, w_ref, bd_ref, qg_ref, kg_ref, cos_ref, sa_ref, sb_ref) = refs[:10]
    outs = refs[10:]
    if group == 1:
        outs = outs[N_PROJ_OUT:]
    xo_ref, a_ref, q_ref, k_ref, v_ref = outs[:N_PROJ_OUT]

    x = x_ref[...]
    if has_prev:
        x = _apply_prev(x, prev_refs)
    xo_ref[...] = x
    h = (_rms(x, g_ref[...]) * (1.0 + scale_ref[...]) + shift_ref[...]).astype(BF16)

    a_ref[...] = _dot(h, w_ref[:, 0:HALF]).astype(BF16)
    zv = _dot(h, w_ref[:, 3 * HALF:4 * HALF])
    v_ref[...] = zv.astype(BF16)

    bd = bd_ref[...]

    def qk_norm(z, gain):
        shi, slo = _split_bf16(z * z)
        ssq = _dot(shi, bd) + _dot(slo, bd)
        return z * lax.rsqrt(ssq * (1.0 / QK_DIM) + EPS) * gain

    qn = qk_norm(_dot(h, w_ref[:, HALF:2 * HALF]), qg_ref[...]) * (QK_DIM ** -0.5 * LOG2E)
    kn = qk_norm(_dot(h, w_ref[:, 2 * HALF:3 * HALF]), kg_ref[...])

    if group == 0:
        kf_ref, vf_ref = outs[N_PROJ_OUT:]
        kf_ref[...] = kn
        vf_ref[...] = zv
        q_ref[...] = qn.astype(BF16)
        k_ref[...] = kn.astype(BF16)
    else:
        c, sa, sb = cos_ref[...], sa_ref[...], sb_ref[...]
        q_ref[...] = _rope(qn, c, sa, sb).astype(BF16)
        k_ref[...] = _rope(kn, c, sa, sb).astype(BF16)


def _even_proj(layer, group, x, x_is_group_local, prev, ctx_outs, mods, norm_g, w_in, blockdiag, qg, kg,
               rope_tabs):
    has_prev = prev is not None
    rows = PROJ_TILES * TM
    first, n_blocks = _group_blocks(group, PROJ_TILES)
    mod_row = lambda t: _tile_mod_row((first + t) * PROJ_TILES)
    glob = lambda w: pl.BlockSpec((rows, w), lambda t: (first + t, 0))
    local = lambda w: pl.BlockSpec((rows, w), lambda t: (t, 0))
    rope_spec = pl.BlockSpec((rows, GROUP), lambda t: (t % (LAT_SEQ // rows) if group == 1 else 0, 0))
    in_specs = [local(D_MODEL) if x_is_group_local else glob(D_MODEL)]
    args = [x]
    if has_prev:
        yg, gates = prev
        in_specs += _prev_specs(layer, first, PROJ_TILES)
        args += [yg] * TOP_K + [gates, mods]
    in_specs += [
        _full_spec((1, D_MODEL)),
        _mod_spec(layer, 0, mod_row), _mod_spec(layer, 1, mod_row),
        _full_spec((D_MODEL, 4 * HALF)), _full_spec((HALF, HALF)),
        _full_spec((1, HALF)), _full_spec((1, HALF)),
        rope_spec, rope_spec, rope_spec,
    ]
    args += [norm_g, mods, mods, w_in, blockdiag, qg, kg, *rope_tabs]
    out_specs = [glob(D_MODEL)] + [glob(HALF)] * 4
    out_shape = [jax.ShapeDtypeStruct((N_TOK, D_MODEL), F32)] + [jax.ShapeDtypeStruct((N_TOK, HALF), BF16)] * 4
    aliases = {}
    if group == 0:
        out_specs += [local(HALF)] * 2
        out_shape += [jax.ShapeDtypeStruct((N_CTX, HALF), F32)] * 2
    else:
        aliases = {len(args) + i: i for i in range(N_PROJ_OUT)}
        in_specs += [pl.BlockSpec(memory_space=pl.ANY)] * N_PROJ_OUT
        args += list(ctx_outs[:N_PROJ_OUT])
    return pl.pallas_call(
        functools.partial(_even_proj_kernel, group, has_prev),
        grid=(n_blocks,),
        in_specs=in_specs, out_specs=out_specs, out_shape=out_shape,
        input_output_aliases=aliases,
        compiler_params=pltpu.CompilerParams(dimension_semantics=("arbitrary",),
                                             vmem_limit_bytes=VMEM_LIMIT),
        name="even_proj_ctx" if group == 0 else "even_proj_lat",
    )(*args)


def _even_mix_kernel(lambda_init, *refs):
    (x_ref, q_ref, ac_ref, kc_ref, vc_ref, al_ref, kl_ref, vl_ref, ck_ref, cv_ref,
     cnc_ref, snc_ref, cnl_ref, snl_ref) = refs[:14]
    shared = refs[14:]
    t = pl.program_id(0)

    @pl.when(t < CTX_TILES)
    def _():
        _even_mix_body(lambda_init, x_ref, q_ref, ac_ref, [(kc_ref, vc_ref, False)],
                       cnc_ref, snc_ref, *shared)

    @pl.when(t >= CTX_TILES)
    def _():
        _even_mix_body(lambda_init, x_ref, q_ref, al_ref, [(kl_ref, vl_ref, False), (ck_ref, cv_ref, True)],
                       cnl_ref, snl_ref, *shared)


def _even_mix_body(lambda_init, x_ref, q_ref, a_ref, kv_refs, cn_ref, sn_ref, *refs):
    (cs_ref, lq_ref, lk_ref, sg_ref, wo_ref,
     gate_ref, gffn_ref, shift_ref, scale_ref, rwhi_ref, rwlo_ref, rb_ref) = refs[:12]
    out_refs = refs[12:]

    a = a_ref[...]
    y1 = _dot(cn_ref[...], a).astype(BF16)
    y2 = _dot(sn_ref[...], a).astype(BF16)
    cs = cs_ref[...]
    pieces = []
    for g in range(N_GROUPS):
        sl = slice(GROUP * g, GROUP * (g + 1))
        pieces.append(_dot(jnp.concatenate([y1[:, sl], y2[:, sl]], axis=1), cs).astype(BF16))

    d = jnp.sum(lq_ref[...] * lk_ref[...], axis=1, keepdims=True)
    ed = jnp.exp(d)
    lam = ed[0:1, :] - ed[1:2, :] + lambda_init
    lane = lax.broadcasted_iota(I32, (TM, GROUP), 1)
    nt = (((1,), (1,)), ((), ()))
    for hd in range(N_GROUPS):
        sl = slice(GROUP * hd, GROUP * (hd + 1))
        qf = q_ref[:, sl].astype(F32)
        parts = []
        for k_ref, v_ref, is_f32 in kv_refs:
            kk, vv = k_ref[:, sl], v_ref[:, sl]
            parts.append((kk.astype(BF16), vv.astype(BF16)) if is_f32 else (kk, vv))

        def probs(qm):
            ss = [lax.dot_general(qm, kk, nt, preferred_element_type=F32) for kk, _ in parts]
            m = functools.reduce(jnp.maximum, [jnp.max(s, axis=1, keepdims=True) for s in ss])
            es = [jnp.exp2(s - m) for s in ss]
            den = functools.reduce(lambda u, w: u + w, [jnp.sum(e, axis=1, keepdims=True) for e in es])
            return es, 1.0 / den

        es0, inv0 = probs(jnp.where(lane < QK_DIM, qf, 0.0).astype(BF16))
        es1, inv1 = probs(jnp.where(lane >= QK_DIM, qf, 0.0).astype(BF16))
        o0 = functools.reduce(lambda u, w: u + w,
                              [_dot(e.astype(BF16), vv) for e, (_, vv) in zip(es0, parts)])
        o1 = functools.reduce(lambda u, w: u + w,
                              [_dot(e.astype(BF16), vv) for e, (_, vv) in zip(es1, parts)])
        o = o0 * inv0 - o1 * (lam * inv1)
        pieces.append((_rms(o, sg_ref[...]) * (1.0 - lambda_init)).astype(BF16))

    mix = _dot(jnp.concatenate(pieces, axis=1), wo_ref[...])
    _post_mixer(x_ref[...], mix, gate_ref, gffn_ref, shift_ref, scale_ref, rwhi_ref, rwlo_ref, rb_ref,
                *out_refs)


def _even_mix(layer, x, a, q, k, v, cache_k, cache_v, dft_ctx, dft_lat, cs128, lam_q, lam_k, subln_g,
              w_out, mods, gffn, rw_hi, rw_lo, rb):
    j = layer // 2
    lambda_init = 0.8 - 0.6 * math.exp(-0.3 * layer)
    lat = lambda t: jnp.maximum(t - CTX_TILES, 0)
    tile = lambda w: pl.BlockSpec((TM, w), lambda t: (t, 0))
    ctx_seq = pl.BlockSpec((CTX_SEQ, HALF), lambda t: (jnp.minimum(t, CTX_TILES - 1), 0))
    lat_seq = pl.BlockSpec((LAT_SEQ, HALF),
                           lambda t: (N_CTX // LAT_SEQ + lat(t) // LAT_TILES_PER_BATCH, 0))
    cache = pl.BlockSpec((None, None, PAST_LEN, HALF),
                         lambda t: (lat(t) // LAT_TILES_PER_BATCH, j, 0, 0))
    dft_lat_spec = pl.BlockSpec((TM, LAT_SEQ), lambda t: (lat(t) % LAT_TILES_PER_BATCH, 0))
    in_specs = [
        tile(D_MODEL), tile(HALF), ctx_seq, ctx_seq, ctx_seq, lat_seq, lat_seq, lat_seq, cache, cache,
        _full_spec((CTX_SEQ, CTX_SEQ)), _full_spec((CTX_SEQ, CTX_SEQ)), dft_lat_spec, dft_lat_spec,
        _full_spec((2 * GROUP, GROUP)),
        _full_spec((2, QK_DIM)), _full_spec((2, QK_DIM)), _full_spec((1, GROUP)),
        _full_spec((D_MODEL, D_MODEL)),
        _mod_spec(layer, 2, _tile_mod_row), _full_spec((1, D_MODEL)),
        _mod_spec(layer, 3, _tile_mod_row), _mod_spec(layer, 4, _tile_mod_row),
        _full_spec((D_MODEL, LANES)), _full_spec((D_MODEL, LANES)), _full_spec((1, LANES)),
    ]
    args = [x, q, a, k, v, a, k, v, cache_k, cache_v, dft_ctx[0], dft_ctx[1], dft_lat[0], dft_lat[1],
            cs128, lam_q, lam_k, subln_g, w_out, mods, gffn, mods, mods, rw_hi, rw_lo, rb]
    return pl.pallas_call(
        functools.partial(_even_mix_kernel, lambda_init),
        grid=(N_TILES,),
        in_specs=in_specs,
        out_specs=_route_out_specs(lambda t: t),
        out_shape=_ROUTE_OUT_SHAPES,
        compiler_params=pltpu.CompilerParams(dimension_semantics=("arbitrary",),
                                             vmem_limit_bytes=VMEM_LIMIT),
        name="even_mix",
    )(*args)


ODD_TILES = 2


def _odd_kernel(group, has_prev, *refs):
    x_ref = refs[0]
    refs = refs[1:]
    if has_prev:
        prev_refs = refs[:N_PREV]
        refs = refs[N_PREV:]
    (g_ref, shift_ref, scale_ref, w_ref, vg_ref, ws_ref, bs_ref, wo_ref,
     gate_ref, gffn_ref, shift2_ref, scale2_ref, rwhi_ref, rwlo_ref, rb_ref) = refs[:15]
    out_refs = refs[15:]
    if group == 1:
        out_refs = out_refs[N_ROUTE_OUT:]

    x = x_ref[...]
    if has_prev:
        x = _apply_prev(x, prev_refs)
    h = (_rms(x, g_ref[...]) * (1.0 + scale_ref[...]) + shift_ref[...]).astype(BF16)

    def gelu(z):
        return 0.5 * z * (1.0 + lax.erf(z * (2.0 ** -0.5)))

    u = gelu(_dot(h, w_ref[:, 0:D_MODEL]))
    vn = _rms(gelu(_dot(h, w_ref[:, D_MODEL:2 * D_MODEL])), vg_ref[...]).astype(BF16)
    bs = bs_ref[...]
    rows = []
    for c in range(x.shape[0] // CHUNK):
        cols = []
        for g in range(C_GROUPS):
            cols.append(_dot(ws_ref[g], vn[CHUNK * c:CHUNK * (c + 1), GROUP * g:GROUP * (g + 1)]))
        rows.append(jnp.concatenate(cols, axis=1) + bs)
    sv = jnp.concatenate(rows, axis=0)
    mix = _dot((u * sv).astype(BF16), wo_ref[...])
    _post_mixer(x, mix, gate_ref, gffn_ref, shift2_ref, scale2_ref, rwhi_ref, rwlo_ref, rb_ref,
                *out_refs)


def _odd_layer(layer, group, x, prev, ctx_outs, mods, norm_g, w_in, v_norm_g, w_s, b_s, w_out, gffn,
               rw_hi, rw_lo, rb):
    has_prev = prev is not None
    first, n_blocks = _group_blocks(group, ODD_TILES)
    mod_row = lambda t: _tile_mod_row((first + t) * ODD_TILES)
    in_specs = [pl.BlockSpec((ODD_TILES * TM, D_MODEL), lambda t: (first + t, 0))]
    args = [x]
    if has_prev:
        yg, gates = prev
        in_specs += _prev_specs(layer, first, ODD_TILES)
        args += [yg] * TOP_K + [gates, mods]
    in_specs += [
        _full_spec((1, D_MODEL)),
        _mod_spec(layer, 0, mod_row), _mod_spec(layer, 1, mod_row),
        _full_spec((D_MODEL, 2 * D_MODEL)), _full_spec((1, D_MODEL)),
        _full_spec((C_GROUPS, CHUNK, CHUNK)), _full_spec((CHUNK, D_MODEL)),
        _full_spec((D_MODEL, D_MODEL)),
        _mod_spec(layer, 2, mod_row), _full_spec((1, D_MODEL)),
        _mod_spec(layer, 3, mod_row), _mod_spec(layer, 4, mod_row),
        _full_spec((D_MODEL, LANES)), _full_spec((D_MODEL, LANES)), _full_spec((1, LANES)),
    ]
    args += [norm_g, mods, mods, w_in, v_norm_g, w_s, b_s, w_out, mods, gffn, mods, mods, rw_hi, rw_lo, rb]
    aliases = {}
    if group == 1:
        aliases = {len(args) + i: i for i in range(N_ROUTE_OUT)}
        in_specs += [pl.BlockSpec(memory_space=pl.ANY)] * N_ROUTE_OUT
        args += list(ctx_outs)
    return pl.pallas_call(
        functools.partial(_odd_kernel, group, has_prev),
        grid=(n_blocks,),
        in_specs=in_specs,
        out_specs=_route_out_specs(lambda t: first + t, ODD_TILES),
        out_shape=_ROUTE_OUT_SHAPES,
        input_output_aliases=aliases,
        compiler_params=pltpu.CompilerParams(dimension_semantics=("arbitrary",),
                                             vmem_limit_bytes=VMEM_LIMIT),
        name="odd_layer_ctx" if group == 0 else "odd_layer_lat",
    )(*args)


def _slot_kernel(ei_ref, base_ref, tri_ref, o_ref):
    lane = lax.broadcasted_iota(I32, (TM, LANES), 1)
    tri = tri_ref[...]
    for i in range(SLOT_TILES):
        ei = ei_ref[TM * i:TM * (i + 1), :]
        onehots = [lane == ei[:, k:k + 1] for k in range(TOP_K)]
        member = functools.reduce(lambda u, w: u + w, [o.astype(F32) for o in onehots])
        before = _dot(tri, member.astype(BF16)) + base_ref[SUBLANES * i:SUBLANES * i + 1, :]
        slots = jnp.zeros((TM, LANES), F32)
        for k in range(TOP_K):
            s = jnp.sum(jnp.where(onehots[k], before, 0.0), axis=1, keepdims=True)
            slots = slots + jnp.where(lane == k, s, 0.0)
        o_ref[i] = slots.T[0:SUBLANES, :].astype(I32)


SLOT_TILES = 4


def _slots(eidx, tile_base, tri):
    return pl.pallas_call(
        _slot_kernel,
        grid=(N_TILES // SLOT_TILES,),
        in_specs=[pl.BlockSpec((SLOT_TILES * TM, LANES), lambda t: (t, 0)),
                  pl.BlockSpec((SLOT_TILES * SUBLANES, LANES), lambda t: (t, 0)),
                  _full_spec((TM, TM))],
        out_specs=pl.BlockSpec((SLOT_TILES, SUBLANES, TM), lambda t: (t, 0, 0)),
        out_shape=jax.ShapeDtypeStruct((N_TILES, SUBLANES, TM), I32),
        compiler_params=pltpu.CompilerParams(dimension_semantics=("arbitrary",)),
        name="moe_slots",
    )(eidx, tile_base, tri)


def _sc_mesh():
    return plsc.VectorSubcoreMesh(core_axis_name="c", subcore_axis_name="s")


def _sc_worker():
    return lax.axis_index("s") * SC_CORES + lax.axis_index("c")


def _dispatch(hp, idx):
    n_chunks = N_TOK // SC_WORKERS // SC_ROWS

    @functools.partial(
        pl.kernel, mesh=_sc_mesh(),
        out_type=jax.ShapeDtypeStruct((N_SLOTS, HALF), U32),
        scratch_types=[pltpu.VMEM((n_chunks, TOP_K, SC_ROWS), I32), pltpu.VMEM((SC_BUFS, SC_ROWS, HALF), U32),
                       pltpu.SemaphoreType.DMA((SC_BUFS,)), pltpu.SemaphoreType.DMA((SC_BUFS,))],
        name="moe_dispatch",
    )
    def k(x_hbm, idx_hbm, out_hbm, idx_v, rows_v, read_sem, scat_sem):
        wid = _sc_worker()
        base = wid * n_chunks
        pltpu.sync_copy(idx_hbm.at[wid], idx_v)

        def read(j, b):
            return pltpu.make_async_copy(x_hbm.at[pl.ds((base + j) * SC_ROWS, SC_ROWS)], rows_v.at[b],
                                         read_sem.at[b])

        def scatter(j, b, kk):
            return pltpu.make_async_copy(rows_v.at[b], out_hbm.at[idx_v.at[j, kk]], scat_sem.at[b])

        def drain(j, b):
            for kk in range(TOP_K):
                scatter(j, b, kk).wait()

        read(0, 0).start()

        @pl.loop(0, n_chunks, step=SC_BUFS)
        def _(j):
            for b in range(SC_BUFS):
                jj = j + b
                other = (b + 1) % SC_BUFS
                read(jj, b).wait()

                @pl.when(jj >= 1)
                def _():
                    drain(jj - 1, other)

                @pl.when(jj + 1 < n_chunks)
                def _():
                    read(jj + 1, other).start()

                for kk in range(TOP_K):
                    scatter(jj, b, kk).start()

        drain(n_chunks - 1, (n_chunks - 1) % SC_BUFS)

    return k(hp, idx)


def _combine_gather(ys, idx):
    n_chunks = idx.shape[1]

    @functools.partial(
        pl.kernel, mesh=_sc_mesh(),
        out_type=jax.ShapeDtypeStruct((SC_WORKERS * n_chunks * SC_ROWS, HALF), U32),
        scratch_types=[pltpu.VMEM((n_chunks, SC_ROWS), I32), pltpu.VMEM((SC_BUFS, SC_ROWS, HALF), U32),
                       pltpu.SemaphoreType.DMA((SC_BUFS,)), pltpu.SemaphoreType.DMA((SC_BUFS,))],
        name="moe_combine",
    )
    def k(ys_hbm, idx_hbm, out_hbm, idx_v, rows_v, gather_sem, write_sem):
        wid = _sc_worker()
        base = wid * n_chunks
        pltpu.sync_copy(idx_hbm.at[wid], idx_v)

        def gather(j, b):
            return pltpu.make_async_copy(ys_hbm.at[idx_v.at[j]], rows_v.at[b], gather_sem.at[b])

        def write(j, b):
            return pltpu.make_async_copy(rows_v.at[b], out_hbm.at[pl.ds((base + j) * SC_ROWS, SC_ROWS)],
                                         write_sem.at[b])

        gather(0, 0).start()

        @pl.loop(0, n_chunks, step=SC_BUFS)
        def _(j):
            for b in range(SC_BUFS):
                jj = j + b
                other = (b + 1) % SC_BUFS
                gather(jj, b).wait()

                @pl.when(jj >= 1)
                def _():
                    write(jj - 1, other).wait()

                @pl.when(jj + 1 < n_chunks)
                def _():
                    gather(jj + 1, other).start()

                write(jj, b).start()

        write(n_chunks - 1, (n_chunks - 1) % SC_BUFS).wait()

    return k(ys, idx)


N_MATS = 3


def _moe_kernel(layer, first_ref, nblk_ref, xs_hbm, wg_hbm, wu_hbm, wd_hbm, bg_ref, bu_ref, bd_ref,
                ys_hbm, wf32, wbf, xbuf, obuf, wsem, xsem, osem):
    e = pl.program_id(0)
    slot = e % 2
    w_hbm = (wg_hbm, wu_hbm, wd_hbm)
    nb = nblk_ref[e]
    row0 = first_ref[e] * MOE_BLOCK
    n_chunks = nb // BLOCKS_PER_CHUNK
    has_tail = nb % BLOCKS_PER_CHUNK == 1
    tail_row = row0 + n_chunks * MOE_CHUNK
    tail_slot = n_chunks % 2

    def w_copy(ee, s, m):
        return pltpu.make_async_copy(w_hbm[m].at[layer, ee], wf32.at[s, m], wsem.at[s, m])

    def x_copy(row, rows, s):
        return pltpu.make_async_copy(xs_hbm.at[pl.ds(row, rows)], xbuf.at[s, pl.ds(0, rows)], xsem.at[s])

    def o_copy(row, rows, s):
        return pltpu.make_async_copy(obuf.at[s, pl.ds(0, rows)], ys_hbm.at[pl.ds(row, rows)], osem.at[s])

    def expert(xp):
        lo, hi = _unpack_halves(xp)
        xb = jnp.concatenate([lo.astype(BF16), hi.astype(BF16)], axis=1)
        gt = jnp.minimum(_dot(xb, wbf[0]) + bg_ref[...], SWIGLU_LIMIT)
        up = jnp.clip(_dot(xb, wbf[1]) + bu_ref[...], -SWIGLU_LIMIT, SWIGLU_LIMIT)
        glu = gt * jax.nn.sigmoid(SWIGLU_ALPHA * gt)
        hmid = ((up + 1.0) * glu).astype(BF16)
        out = _dot(hmid, wbf[2]) + bd_ref[...]
        return _pack_halves(out.astype(BF16).astype(F32))

    @pl.when(e == 0)
    def _():
        for m in range(N_MATS):
            w_copy(0, 0, m).start()

    @pl.when(n_chunks > 0)
    def _():
        x_copy(row0, MOE_CHUNK, 0).start()

    @pl.when((n_chunks == 0) & has_tail)
    def _():
        x_copy(row0, MOE_BLOCK, 0).start()

    for m in range(N_MATS):
        w_copy(e, slot, m).wait()

    @pl.when(e + 1 < N_EXPERTS)
    def _():
        for m in range(N_MATS):
            w_copy(e + 1, 1 - slot, m).start()

    for m in range(N_MATS):
        wbf[m] = wf32[slot, m].astype(BF16)

    def chunk(c, carry):
        s = c % 2
        row = row0 + c * MOE_CHUNK
        x_copy(row, MOE_CHUNK, s).wait()

        @pl.when(c + 1 < n_chunks)
        def _():
            x_copy(row + MOE_CHUNK, MOE_CHUNK, 1 - s).start()

        @pl.when((c + 1 == n_chunks) & has_tail)
        def _():
            x_copy(tail_row, MOE_BLOCK, 1 - s).start()

        obuf[s] = expert(xbuf[s])

        @pl.when(c >= 1)
        def _():
            o_copy(row - MOE_CHUNK, MOE_CHUNK, 1 - s).wait()

        o_copy(row, MOE_CHUNK, s).start()
        return carry

    lax.fori_loop(0, n_chunks, chunk, 0)

    @pl.when(has_tail)
    def _():
        x_copy(tail_row, MOE_BLOCK, tail_slot).wait()
        obuf[tail_slot, 0:MOE_BLOCK, :] = expert(xbuf[tail_slot, 0:MOE_BLOCK, :])

        @pl.when(n_chunks >= 1)
        def _():
            o_copy(tail_row - MOE_CHUNK, MOE_CHUNK, 1 - tail_slot).wait()

        o_copy(tail_row, MOE_BLOCK, tail_slot).start()
        o_copy(tail_row, MOE_BLOCK, tail_slot).wait()

    @pl.when(jnp.logical_not(has_tail) & (n_chunks >= 1))
    def _():
        o_copy(tail_row - MOE_CHUNK, MOE_CHUNK, 1 - tail_slot).wait()


def _moe_experts(layer, xs, first_block, n_blocks, w_gate, b_gate, w_up, b_up, w_down, b_down):
    hbm = pl.BlockSpec(memory_space=pl.ANY)
    bspec = pl.BlockSpec((None, None, 1, D_MODEL), lambda e, fb, nb: (layer, e, 0, 0))
    bias = lambda b: b.reshape(DEPTH, N_EXPERTS, 1, D_MODEL)
    return pl.pallas_call(
        functools.partial(_moe_kernel, layer),
        grid_spec=pltpu.PrefetchScalarGridSpec(
            num_scalar_prefetch=2,
            grid=(N_EXPERTS,),
            in_specs=[hbm, hbm, hbm, hbm, bspec, bspec, bspec],
            out_specs=hbm,
            scratch_shapes=[
                pltpu.VMEM((2, N_MATS, D_MODEL, D_MODEL), F32),
                pltpu.VMEM((N_MATS, D_MODEL, D_MODEL), BF16),
                pltpu.VMEM((2, MOE_CHUNK, HALF), U32),
                pltpu.VMEM((2, MOE_CHUNK, HALF), U32),
                pltpu.SemaphoreType.DMA((2, N_MATS)),
                pltpu.SemaphoreType.DMA((2,)),
                pltpu.SemaphoreType.DMA((2,)),
            ],
        ),
        out_shape=jax.ShapeDtypeStruct((N_SLOTS, HALF), U32),
        compiler_params=pltpu.CompilerParams(dimension_semantics=("arbitrary",),
                                             vmem_limit_bytes=VMEM_LIMIT),
        name="moe_experts",
    )(first_block, n_blocks, xs, w_gate, w_up, w_down, bias(b_gate), bias(b_up), bias(b_down))


def _moe(layer, hp, eidx, cnt, tri, w_gate, b_gate, w_up, b_up, w_down, b_down):
    tile_cnt = cnt[::SUBLANES, :N_EXPERTS].astype(I32)
    counts = jnp.sum(tile_cnt, axis=0)
    padded = (counts + MOE_BLOCK - 1) // MOE_BLOCK * MOE_BLOCK
    pad_ends = jnp.cumsum(padded)
    pad_starts = pad_ends - padded
    tile_base = pad_starts[None, :] + jnp.cumsum(tile_cnt, axis=0) - tile_cnt
    tile_base = jnp.pad(tile_base.astype(F32), ((0, 0), (0, LANES - N_EXPERTS)))
    tile_base = jnp.repeat(tile_base, SUBLANES, axis=0)
    first_block = (pad_starts // MOE_BLOCK).astype(I32)
    n_blocks = (padded // MOE_BLOCK).astype(I32)

    slots = _slots(eidx, tile_base, tri)[:, :TOP_K, :]
    per_tile = TM // SC_ROWS
    d_idx = slots.reshape(N_TILES, TOP_K, per_tile, SC_ROWS).transpose(0, 2, 1, 3)
    d_idx = d_idx.reshape(SC_WORKERS, N_TOK // SC_WORKERS // SC_ROWS, TOP_K, SC_ROWS)

    xs = _dispatch(hp, d_idx)
    ys = _moe_experts(layer, xs, first_block, n_blocks, w_gate, b_gate, w_up, b_up, w_down, b_down)
    gathered = []
    for tiles in (slots[:CTX_TILES], slots[CTX_TILES:]):
        n_tok = tiles.shape[0] * TM
        c_idx = tiles.transpose(1, 0, 2).reshape(SC_WORKERS, n_tok * TOP_K // SC_WORKERS // SC_ROWS, SC_ROWS)
        gathered.append(_combine_gather(ys, c_idx).reshape(TOP_K, n_tok, HALF))
    return gathered


FINAL_TILES = 2


def _final_kernel(x_ref, *refs):
    o_ref = refs[N_PREV]
    o_ref[...] = _apply_prev(x_ref[...], refs[:N_PREV])


def _final_residual(group, x, prev, mods):
    rows = FINAL_TILES * TM
    first, n_blocks = _group_blocks(group, FINAL_TILES)
    yg, gates = prev
    return pl.pallas_call(
        _final_kernel,
        grid=(n_blocks,),
        in_specs=[pl.BlockSpec((rows, D_MODEL), lambda t: (first + t, 0))]
        + _prev_specs(DEPTH, first, FINAL_TILES),
        out_specs=pl.BlockSpec((rows, D_MODEL), lambda t: (t, 0)),
        out_shape=jax.ShapeDtypeStruct((n_blocks * rows, D_MODEL), F32),
        compiler_params=pltpu.CompilerParams(dimension_semantics=("arbitrary",)),
        name="final_residual",
    )(x, *([yg] * TOP_K), gates, mods)


def _dft_pair(n):
    idx = np.arange(n)
    ang = 2.0 * np.pi * ((idx[:, None] * idx[None, :]) % n) / n
    return np.cos(ang) / np.sqrt(n), np.sin(ang) / np.sqrt(n)


def _rope_tables():
    lane = np.arange(GROUP)
    within = lane % QK_DIM
    axis = within // 32
    e = within % 32
    inv16 = ROPE_BASE ** (-jnp.arange(16, dtype=F32) / 16)
    pos = np.arange(LAT_SEQ)
    coord = np.where(axis[None, :] == 0, (pos // GRID_W)[:, None], (pos % GRID_W)[:, None])
    ang = jnp.asarray(coord, F32) * inv16[e % 16][None, :]
    first = jnp.asarray((e // 16) == 0)[None, :]
    cos, sin = jnp.cos(ang), jnp.sin(ang)
    return cos, jnp.where(first, -sin, 0.0), jnp.where(first, 0.0, sin)


def kernel(x_prompt, x_sample, cache_k, cache_v, c, c_ctx, ada_w, ada_b, norm_mix_g, norm_ffn_g,
           w_in_even, w_out_even, q_norm_g, k_norm_g, lambda_q, lambda_k, subln_g,
           w_in_odd, v_norm_g, w_spatial, b_spatial, w_out_odd,
           router_w, router_b, w_gate, b_gate, w_up, b_up, w_down, b_down):
    n_even = w_in_even.shape[0]
    x_groups = (x_prompt.reshape(N_CTX, D_MODEL), x_sample.reshape(N_LAT, D_MODEL))
    x = None
    cond = jnp.zeros((COND_ROWS, D_MODEL), F32).at[0].set(c_ctx).at[1:1 + LAT_BATCH].set(c)
    mods = _ada_modulation(cond, ada_w, ada_b).reshape(DEPTH, N_MOD, COND_ROWS, 1, D_MODEL)

    dft_ctx = [jnp.asarray(m, F32).astype(BF16) for m in _dft_pair(CTX_SEQ)]
    dft_lat = [jnp.asarray(m, F32).astype(BF16) for m in _dft_pair(LAT_SEQ)]
    c128, s128 = _dft_pair(GROUP)
    cs128 = jnp.asarray(np.concatenate([c128, -s128], axis=0), F32).astype(BF16)
    grp = np.arange(HALF) // QK_DIM
    blockdiag = jnp.asarray(grp[:, None] == grp[None, :], F32).astype(BF16)
    tri = jnp.asarray(np.arange(TM)[:, None] > np.arange(TM)[None, :], F32).astype(BF16)
    rope_tabs = _rope_tables()
    cache_k2 = cache_k.reshape(LAT_BATCH, n_even, PAST_LEN, HALF)
    cache_v2 = cache_v.reshape(LAT_BATCH, n_even, PAST_LEN, HALF)

    rw = jnp.pad(router_w, ((0, 0), (0, 0), (0, LANES - N_EXPERTS)))
    rw_hi = rw.astype(BF16)
    rw_lo = (rw - rw_hi.astype(F32)).astype(BF16)
    rb = jnp.pad(router_b, ((0, 0), (0, LANES - N_EXPERTS)))[:, None, :]

    prev = None
    new_k, new_v = [], []
    for l in range(DEPTH):
        j = l // 2
        gmix = norm_mix_g[l][None, :]
        gffn = norm_ffn_g[l][None, :]
        if l % 2 == 0:
            proj_args = (mods, gmix, w_in_even[j].astype(BF16), blockdiag,
                         jnp.tile(q_norm_g[j], HALF // QK_DIM)[None, :],
                         jnp.tile(k_norm_g[j], HALF // QK_DIM)[None, :], rope_tabs)
            outs = None
            for g in range(2):
                x_in = x_groups[g] if l == 0 else x
                outs_g = _even_proj(l, g, x_in, l == 0, None if prev is None else (prev[0][g], prev[1]),
                                    outs, *proj_args)
                if g == 0:
                    kf, vf = outs_g[N_PROJ_OUT:]
                outs = outs_g[:N_PROJ_OUT]
            x, a, q, k, v = outs
            new_k.append(kf.reshape(CTX_BATCH, CTX_SEQ, N_GROUPS, 2, QK_DIM))
            new_v.append(vf.reshape(CTX_BATCH, CTX_SEQ, N_GROUPS, GROUP))
            common = (cs128, lambda_q[j], lambda_k[j], subln_g[j][None, :], w_out_even[j].astype(BF16),
                      mods, gffn, rw_hi[l], rw_lo[l], rb[l])
            x, hp, gates, eidx, cnt = _even_mix(l, x, a, q, k, v, cache_k2, cache_v2, dft_ctx, dft_lat, *common)
        else:
            b_s = jnp.broadcast_to(jnp.transpose(b_spatial[j])[:, :, None],
                                   (CHUNK, C_GROUPS, GROUP)).reshape(CHUNK, D_MODEL)
            odd_args = (mods, gmix, w_in_odd[j].astype(BF16), v_norm_g[j][None, :],
                        w_spatial[j].astype(BF16), b_s, w_out_odd[j].astype(BF16), gffn, rw_hi[l], rw_lo[l], rb[l])
            outs = None
            for g in range(2):
                outs = _odd_layer(l, g, x, None if prev is None else (prev[0][g], prev[1]), outs, *odd_args)
            x, hp, gates, eidx, cnt = outs
        yg = _moe(l, hp, eidx, cnt, tri, w_gate, b_gate, w_up, b_up, w_down, b_down)
        prev = (yg, gates)
    y_ctx = _final_residual(0, x, (prev[0][0], prev[1]), mods)
    y_lat = _final_residual(1, x, (prev[0][1], prev[1]), mods)
    return (y_ctx.reshape(CTX_BATCH, CTX_SEQ, D_MODEL),
            y_lat.reshape(LAT_BATCH, LAT_SEQ, D_MODEL),
            jnp.stack(new_k, axis=1),
            jnp.stack(new_v, axis=1))
```

```python
import functools
import math

import numpy as np
import jax
import jax.numpy as jnp
from jax import lax
from jax.experimental import pallas as pl
from jax.experimental.pallas import tpu as pltpu
from jax.experimental.pallas import tpu_sc as plsc

F32 = jnp.float32
BF16 = jnp.bfloat16
U32 = jnp.uint32
I32 = jnp.int32

D_MODEL = 1024
DEPTH = 4
N_EVEN = (DEPTH + 1) // 2
N_MOD = 6
EPS = 1e-6
CTX_BATCH, CTX_SEQ = 32, 256
LAT_BATCH, LAT_SEQ = 8, 1024
PAST_LEN = 512
GRID_W = 64
N_CTX = CTX_BATCH * CTX_SEQ
N_LAT = LAT_BATCH * LAT_SEQ
N_TOK = N_CTX + N_LAT
TM = 256
N_TILES = N_TOK // TM
CTX_TILES = N_CTX // TM
LAT_TILES_PER_BATCH = LAT_SEQ // TM
COND_ROWS = 16
GROUP = 128
N_GROUPS = 4
QK_DIM = 64
HALF = 512
CHUNK = 128
C_GROUPS = 8
N_EXPERTS = 32
TOP_K = 4
LANES = 128
SUBLANES = 8
SWIGLU_LIMIT = 7.0
SWIGLU_ALPHA = 1.702
MOE_BLOCK = 256
BLOCKS_PER_CHUNK = 2
MOE_CHUNK = BLOCKS_PER_CHUNK * MOE_BLOCK
N_ASSIGN = N_TOK * TOP_K
MOE_BLOCKS = N_ASSIGN // MOE_BLOCK + N_EXPERTS
N_SLOTS = MOE_BLOCKS * MOE_BLOCK
ROPE_BASE = 10000.0
VMEM_LIMIT = 52 * 1024 * 1024
SC_CORES = 2
SC_SUBCORES = 16
SC_WORKERS = SC_CORES * SC_SUBCORES
SC_ROWS = 64
SC_BUFS = 2
HI_MASK = 0xFFFF0000
LOG2E = math.log2(math.e)


def _dot(a, b):
    return jnp.dot(a, b, preferred_element_type=F32)


def _rms(x, g):
    return x * lax.rsqrt(jnp.mean(x * x, axis=-1, keepdims=True) + EPS) * g


def _split_bf16(x):
    hi = x.astype(BF16)
    lo = (x - hi.astype(F32)).astype(BF16)
    return hi, lo


def _pack_halves(xf):
    b = lax.bitcast_convert_type(xf, U32)
    return (b[:, :HALF] >> 16) | (b[:, HALF:] & jnp.uint32(HI_MASK))


def _unpack_halves(w):
    lo = lax.bitcast_convert_type(w << 16, F32)
    hi = lax.bitcast_convert_type(w & jnp.uint32(HI_MASK), F32)
    return lo, hi


def _tile_mod_row(t):
    return jnp.where(t < CTX_TILES, 0, 1 + (t - CTX_TILES) // LAT_TILES_PER_BATCH)


def _mod_spec(layer, k, row_fn):
    return pl.BlockSpec((None, None, None, 1, D_MODEL),
                        lambda *g: (layer, k, row_fn(*g), 0, 0))


def _full_spec(shape):
    return pl.BlockSpec(shape, lambda *g: (0,) * len(shape))


def _ada_kernel(cond_ref, w_ref, b_ref, o_ref):
    c = cond_ref[...]
    s = (c * jax.nn.sigmoid(c)).astype(BF16)
    o_ref[...] = _dot(s, w_ref[...].astype(BF16)) + b_ref[...]


def _ada_modulation(cond, ada_w, ada_b):
    return pl.pallas_call(
        _ada_kernel,
        grid=(DEPTH, N_MOD),
        in_specs=[
            _full_spec((COND_ROWS, D_MODEL)),
            pl.BlockSpec((None, D_MODEL, D_MODEL), lambda l, n: (l, 0, n)),
            pl.BlockSpec((None, None, 1, D_MODEL), lambda l, n: (l, n, 0, 0)),
        ],
        out_specs=pl.BlockSpec((None, None, COND_ROWS, D_MODEL), lambda l, n: (l, n, 0, 0)),
        out_shape=jax.ShapeDtypeStruct((DEPTH, N_MOD, COND_ROWS, D_MODEL), F32),
        compiler_params=pltpu.CompilerParams(dimension_semantics=("arbitrary", "arbitrary"),
                                             vmem_limit_bytes=VMEM_LIMIT),
        name="ada_modulation",
    )(cond, ada_w, ada_b.reshape(DEPTH, N_MOD, 1, D_MODEL))


N_PREV = 6


def _group_blocks(group, tiles_per_block):
    ctx = CTX_TILES // tiles_per_block
    return (0, ctx) if group == 0 else (ctx, N_TILES // tiles_per_block - ctx)


def _prev_specs(layer, first_block, tiles_per_block):
    rows = tiles_per_block * TM
    ys = [pl.BlockSpec((None, rows, HALF), functools.partial(lambda k, t: (k, t, 0), k))
          for k in range(TOP_K)]
    return ys + [pl.BlockSpec((rows, LANES), lambda t: (first_block + t, 0)),
                 _mod_spec(layer - 1, 5, lambda t: _tile_mod_row((first_block + t) * tiles_per_block))]


def _apply_prev(x, prev_refs):
    y_refs, gt_ref, gate_ref = prev_refs[:TOP_K], prev_refs[TOP_K], prev_refs[TOP_K + 1]
    gt = gt_ref[...]
    acc_lo = acc_hi = None
    for k in range(TOP_K):
        lo, hi = _unpack_halves(y_refs[k][...])
        g = gt[:, k:k + 1]
        acc_lo = g * lo if acc_lo is None else acc_lo + g * lo
        acc_hi = g * hi if acc_hi is None else acc_hi + g * hi
    return x + gate_ref[...] * jnp.concatenate([acc_lo, acc_hi], axis=1)


N_ROUTE_OUT = 5


def _route_out_specs(tile_fn, tiles_per_block=1):
    tile = lambda w: pl.BlockSpec((tiles_per_block * TM, w), lambda *g: (tile_fn(*g), 0))
    return [tile(D_MODEL), tile(HALF), tile(LANES), tile(LANES),
            pl.BlockSpec((tiles_per_block * SUBLANES, LANES), lambda *g: (tile_fn(*g), 0))]


_ROUTE_OUT_SHAPES = [
    jax.ShapeDtypeStruct((N_TOK, D_MODEL), F32),
    jax.ShapeDtypeStruct((N_TOK, HALF), U32),
    jax.ShapeDtypeStruct((N_TOK, LANES), F32),
    jax.ShapeDtypeStruct((N_TOK, LANES), I32),
    jax.ShapeDtypeStruct((N_TILES * SUBLANES, LANES), F32),
]


def _post_mixer(x, mix, gate_ref, gffn_ref, shift_ref, scale_ref, rwhi_ref, rwlo_ref, rb_ref,
                xo_ref, hp_ref, gt_ref, ei_ref, cnt_ref):
    xn = x + gate_ref[...] * mix
    xo_ref[...] = xn
    h2 = _rms(xn, gffn_ref[...]) * (1.0 + scale_ref[...]) + shift_ref[...]
    hi = h2.astype(BF16)
    hif = hi.astype(F32)
    lo = (h2 - hif).astype(BF16)
    hp_ref[...] = _pack_halves(hif)
    rwhi = rwhi_ref[...]
    logits = _dot(hi, rwhi) + _dot(lo, rwhi) + _dot(hi, rwlo_ref[...]) + rb_ref[...]

    rows = x.shape[0]
    lane = lax.broadcasted_iota(I32, (rows, LANES), 1)
    lane_f = lane.astype(F32)
    work = jnp.where(lane < N_EXPERTS, logits, -jnp.inf)
    member = jnp.zeros((rows, LANES), F32)
    gates = jnp.zeros((rows, LANES), F32)
    ids = jnp.zeros((rows, LANES), F32)
    den = None
    top = None
    for k in range(TOP_K):
        m = jnp.max(work, axis=1, keepdims=True)
        idx = jnp.min(jnp.where(work == m, lane_f, float(LANES)), axis=1, keepdims=True)
        onehot = lane_f == idx
        work = jnp.where(onehot, -jnp.inf, work)
        member = member + onehot.astype(F32)
        if k == 0:
            top = m
        e = jnp.exp(m - top)
        den = e if den is None else den + e
        gates = gates + jnp.where(lane == k, e, 0.0)
        ids = ids + jnp.where(lane == k, idx, 0.0)
    gt_ref[...] = gates * (1.0 / den)
    ei_ref[...] = ids.astype(I32)
    for i in range(rows // TM):
        cnt_ref[SUBLANES * i:SUBLANES * (i + 1), :] = jnp.broadcast_to(
            jnp.sum(member[TM * i:TM * (i + 1), :], axis=0, keepdims=True), (SUBLANES, LANES))


def _rope(x, c, sa, sb):
    outs = []
    for h in range(N_GROUPS):
        xs = x[:, GROUP * h:GROUP * (h + 1)]
        outs.append(xs * c + pltpu.roll(xs, GROUP - 16, 1) * sa + pltpu.roll(xs, 16, 1) * sb)
    return jnp.concatenate(outs, axis=1)


PROJ_TILES = 2


N_PROJ_OUT = 5


def _even_proj_kernel(group, has_prev, n_aliased, *refs):
    x_ref = refs[0]
    refs = refs[1:]
    if has_prev:
        prev_refs = refs[:N_PREV]
        refs = refs[N_PREV:]
    (g_ref, shift_ref, scale_ref, w_ref, bd_ref, qg_ref, kg_ref, cos_ref, sa_ref, sb_ref) = refs[:10]
    outs = refs[10 + n_aliased:]
    xo_ref, a_ref, q_ref, k_ref, v_ref = outs[:N_PROJ_OUT]

    x = x_ref[...]
    if has_prev:
        x = _apply_prev(x, prev_refs)
    xo_ref[...] = x
    h = (_rms(x, g_ref[...]) * (1.0 + scale_ref[...]) + shift_ref[...]).astype(BF16)

    a_ref[...] = _dot(h, w_ref[:, 0:HALF]).astype(BF16)
    zv = _dot(h, w_ref[:, 3 * HALF:4 * HALF])
    v_ref[...] = zv.astype(BF16)

    bd = bd_ref[...]

    def qk_norm(z, gain):
        shi, slo = _split_bf16(z * z)
        ssq = _dot(shi, bd) + _dot(slo, bd)
        return z * lax.rsqrt(ssq * (1.0 / QK_DIM) + EPS) * gain

    qn = qk_norm(_dot(h, w_ref[:, HALF:2 * HALF]), qg_ref[...]) * (QK_DIM ** -0.5 * LOG2E)
    kn = qk_norm(_dot(h, w_ref[:, 2 * HALF:3 * HALF]), kg_ref[...])

    if group == 0:
        kf_ref, vf_ref = outs[N_PROJ_OUT:]
        kf_ref[...] = kn.reshape(kf_ref.shape)
        vf_ref[...] = zv.reshape(vf_ref.shape)
        q_ref[...] = qn.astype(BF16)
        k_ref[...] = kn.astype(BF16)
    else:
        c, sa, sb = cos_ref[...], sa_ref[...], sb_ref[...]
        q_ref[...] = _rope(qn, c, sa, sb).astype(BF16)
        k_ref[...] = _rope(kn, c, sa, sb).astype(BF16)


def _even_proj(layer, group, x, x_is_group_local, prev, ctx_outs, caches, mods, norm_g, w_in, blockdiag,
               qg, kg, rope_tabs):
    has_prev = prev is not None
    rows = PROJ_TILES * TM
    first, n_blocks = _group_blocks(group, PROJ_TILES)
    mod_row = lambda t: _tile_mod_row((first + t) * PROJ_TILES)
    glob = lambda w: pl.BlockSpec((rows, w), lambda t: (first + t, 0))
    local = lambda w: pl.BlockSpec((rows, w), lambda t: (t, 0))
    rope_spec = pl.BlockSpec((rows, GROUP), lambda t: (t % (LAT_SEQ // rows) if group == 1 else 0, 0))
    in_specs = [local(D_MODEL) if x_is_group_local else glob(D_MODEL)]
    args = [x]
    if has_prev:
        yg, gates = prev
        in_specs += _prev_specs(layer, first, PROJ_TILES)
        args += [yg] * TOP_K + [gates, mods]
    in_specs += [
        _full_spec((1, D_MODEL)),
        _mod_spec(layer, 0, mod_row), _mod_spec(layer, 1, mod_row),
        _full_spec((D_MODEL, 4 * HALF)), _full_spec((HALF, HALF)),
        _full_spec((1, HALF)), _full_spec((1, HALF)),
        rope_spec, rope_spec, rope_spec,
    ]
    args += [norm_g, mods, mods, w_in, blockdiag, qg, kg, *rope_tabs]
    out_specs = [glob(D_MODEL)] + [glob(HALF)] * 4
    out_shape = [jax.ShapeDtypeStruct((N_TOK, D_MODEL), F32)] + [jax.ShapeDtypeStruct((N_TOK, HALF), BF16)] * 4
    aliased = []
    if group == 0:
        seqs = rows // CTX_SEQ
        cache_spec = pl.BlockSpec((seqs, None, CTX_SEQ, HALF), lambda t: (t, layer // 2, 0, 0))
        out_specs += [cache_spec] * 2
        out_shape += [jax.ShapeDtypeStruct((CTX_BATCH, N_EVEN, CTX_SEQ, HALF), F32)] * 2
        if caches is not None:
            aliased = [(caches[i], N_PROJ_OUT + i) for i in range(2)]
    else:
        aliased = [(ctx_outs[i], i) for i in range(N_PROJ_OUT)]
    aliases = {len(args) + i: out_idx for i, (_, out_idx) in enumerate(aliased)}
    in_specs += [pl.BlockSpec(memory_space=pl.ANY)] * len(aliased)
    args += [arr for arr, _ in aliased]
    return pl.pallas_call(
        functools.partial(_even_proj_kernel, group, has_prev, len(aliased)),
        grid=(n_blocks,),
        in_specs=in_specs, out_specs=out_specs, out_shape=out_shape,
        input_output_aliases=aliases,
        compiler_params=pltpu.CompilerParams(dimension_semantics=("arbitrary",),
                                             vmem_limit_bytes=VMEM_LIMIT),
        name="even_proj_ctx" if group == 0 else "even_proj_lat",
    )(*args)


def _even_mix_kernel(lambda_init, *refs):
    (x_ref, q_ref, ac_ref, kc_ref, vc_ref, al_ref, kl_ref, vl_ref, ck_ref, cv_ref,
     cnc_ref, snc_ref, cnl_ref, snl_ref) = refs[:14]
    shared = refs[14:]
    t = pl.program_id(0)

    @pl.when(t < CTX_TILES)
    def _():
        _even_mix_body(lambda_init, x_ref, q_ref, ac_ref, [(kc_ref, vc_ref, False)],
                       cnc_ref, snc_ref, *shared)

    @pl.when(t >= CTX_TILES)
    def _():
        _even_mix_body(lambda_init, x_ref, q_ref, al_ref, [(kl_ref, vl_ref, False), (ck_ref, cv_ref, True)],
                       cnl_ref, snl_ref, *shared)


def _even_mix_body(lambda_init, x_ref, q_ref, a_ref, kv_refs, cn_ref, sn_ref, *refs):
    (cs_ref, lq_ref, lk_ref, sg_ref, wo_ref,
     gate_ref, gffn_ref, shift_ref, scale_ref, rwhi_ref, rwlo_ref, rb_ref) = refs[:12]
    out_refs = refs[12:]

    a = a_ref[...]
    y1 = _dot(cn_ref[...], a).astype(BF16)
    y2 = _dot(sn_ref[...], a).astype(BF16)
    cs = cs_ref[...]
    pieces = []
    for g in range(N_GROUPS):
        sl = slice(GROUP * g, GROUP * (g + 1))
        pieces.append(_dot(jnp.concatenate([y1[:, sl], y2[:, sl]], axis=1), cs).astype(BF16))

    d = jnp.sum(lq_ref[...] * lk_ref[...], axis=1, keepdims=True)
    ed = jnp.exp(d)
    lam = ed[0:1, :] - ed[1:2, :] + lambda_init
    lane = lax.broadcasted_iota(I32, (TM, GROUP), 1)
    nt = (((1,), (1,)), ((), ()))
    for hd in range(N_GROUPS):
        sl = slice(GROUP * hd, GROUP * (hd + 1))
        qf = q_ref[:, sl].astype(F32)
        parts = []
        for k_ref, v_ref, is_f32 in kv_refs:
            kk, vv = k_ref[:, sl], v_ref[:, sl]
            parts.append((kk.astype(BF16), vv.astype(BF16)) if is_f32 else (kk, vv))

        def probs(qm):
            ss = [lax.dot_general(qm, kk, nt, preferred_element_type=F32) for kk, _ in parts]
            m = functools.reduce(jnp.maximum, [jnp.max(s, axis=1, keepdims=True) for s in ss])
            es = [jnp.exp2(s - m) for s in ss]
            den = functools.reduce(lambda u, w: u + w, [jnp.sum(e, axis=1, keepdims=True) for e in es])
            return es, 1.0 / den

        es0, inv0 = probs(jnp.where(lane < QK_DIM, qf, 0.0).astype(BF16))
        es1, inv1 = probs(jnp.where(lane >= QK_DIM, qf, 0.0).astype(BF16))
        o0 = functools.reduce(lambda u, w: u + w,
                              [_dot(e.astype(BF16), vv) for e, (_, vv) in zip(es0, parts)])
        o1 = functools.reduce(lambda u, w: u + w,
                              [_dot(e.astype(BF16), vv) for e, (_, vv) in zip(es1, parts)])
        o = o0 * inv0 - o1 * (lam * inv1)
        pieces.append((_rms(o, sg_ref[...]) * (1.0 - lambda_init)).astype(BF16))

    mix = _dot(jnp.concatenate(pieces, axis=1), wo_ref[...])
    _post_mixer(x_ref[...], mix, gate_ref, gffn_ref, shift_ref, scale_ref, rwhi_ref, rwlo_ref, rb_ref,
                *out_refs)


def _even_mix(layer, x, a, q, k, v, cache_k, cache_v, dft_ctx, dft_lat, cs128, lam_q, lam_k, subln_g,
              w_out, mods, gffn, rw_hi, rw_lo, rb):
    j = layer // 2
    lambda_init = 0.8 - 0.6 * math.exp(-0.3 * layer)
    lat = lambda t: jnp.maximum(t - CTX_TILES, 0)
    tile = lambda w: pl.BlockSpec((TM, w), lambda t: (t, 0))
    ctx_seq = pl.BlockSpec((CTX_SEQ, HALF), lambda t: (jnp.minimum(t, CTX_TILES - 1), 0))
    lat_seq = pl.BlockSpec((LAT_SEQ, HALF),
                           lambda t: (N_CTX // LAT_SEQ + lat(t) // LAT_TILES_PER_BATCH, 0))
    cache = pl.BlockSpec((None, None, PAST_LEN, HALF),
                         lambda t: (lat(t) // LAT_TILES_PER_BATCH, j, 0, 0))
    dft_lat_spec = pl.BlockSpec((TM, LAT_SEQ), lambda t: (lat(t) % LAT_TILES_PER_BATCH, 0))
    in_specs = [
        tile(D_MODEL), tile(HALF), ctx_seq, ctx_seq, ctx_seq, lat_seq, lat_seq, lat_seq, cache, cache,
        _full_spec((CTX_SEQ, CTX_SEQ)), _full_spec((CTX_SEQ, CTX_SEQ)), dft_lat_spec, dft_lat_spec,
        _full_spec((2 * GROUP, GROUP)),
        _full_spec((2, QK_DIM)), _full_spec((2, QK_DIM)), _full_spec((1, GROUP)),
        _full_spec((D_MODEL, D_MODEL)),
        _mod_spec(layer, 2, _tile_mod_row), _full_spec((1, D_MODEL)),
        _mod_spec(layer, 3, _tile_mod_row), _mod_spec(layer, 4, _tile_mod_row),
        _full_spec((D_MODEL, LANES)), _full_spec((D_MODEL, LANES)), _full_spec((1, LANES)),
    ]
    args = [x, q, a, k, v, a, k, v, cache_k, cache_v, dft_ctx[0], dft_ctx[1], dft_lat[0], dft_lat[1],
            cs128, lam_q, lam_k, subln_g, w_out, mods, gffn, mods, mods, rw_hi, rw_lo, rb]
    return pl.pallas_call(
        functools.partial(_even_mix_kernel, lambda_init),
        grid=(N_TILES,),
        in_specs=in_specs,
        out_specs=_route_out_specs(lambda t: t),
        out_shape=_ROUTE_OUT_SHAPES,
        compiler_params=pltpu.CompilerParams(dimension_semantics=("arbitrary",),
                                             vmem_limit_bytes=VMEM_LIMIT),
        name="even_mix",
    )(*args)


ODD_TILES = 2


def _odd_kernel(group, has_prev, *refs):
    x_ref = refs[0]
    refs = refs[1:]
    if has_prev:
        prev_refs = refs[:N_PREV]
        refs = refs[N_PREV:]
    (g_ref, shift_ref, scale_ref, w_ref, vg_ref, ws_ref, bs_ref, wo_ref,
     gate_ref, gffn_ref, shift2_ref, scale2_ref, rwhi_ref, rwlo_ref, rb_ref) = refs[:15]
    out_refs = refs[15:]
    if group == 1:
        out_refs = out_refs[N_ROUTE_OUT:]

    x = x_ref[...]
    if has_prev:
        x = _apply_prev(x, prev_refs)
    h = (_rms(x, g_ref[...]) * (1.0 + scale_ref[...]) + shift_ref[...]).astype(BF16)

    def gelu(z):
        return 0.5 * z * (1.0 + lax.erf(z * (2.0 ** -0.5)))

    u = gelu(_dot(h, w_ref[:, 0:D_MODEL]))
    vn = _rms(gelu(_dot(h, w_ref[:, D_MODEL:2 * D_MODEL])), vg_ref[...]).astype(BF16)
    bs = bs_ref[...]
    rows = []
    for c in range(x.shape[0] // CHUNK):
        cols = []
        for g in range(C_GROUPS):
            cols.append(_dot(ws_ref[g], vn[CHUNK * c:CHUNK * (c + 1), GROUP * g:GROUP * (g + 1)]))
        rows.append(jnp.concatenate(cols, axis=1) + bs)
    sv = jnp.concatenate(rows, axis=0)
    mix = _dot((u * sv).astype(BF16), wo_ref[...])
    _post_mixer(x, mix, gate_ref, gffn_ref, shift2_ref, scale2_ref, rwhi_ref, rwlo_ref, rb_ref,
                *out_refs)


def _odd_layer(layer, group, x, prev, ctx_outs, mods, norm_g, w_in, v_norm_g, w_s, b_s, w_out, gffn,
               rw_hi, rw_lo, rb):
    has_prev = prev is not None
    first, n_blocks = _group_blocks(group, ODD_TILES)
    mod_row = lambda t: _tile_mod_row((first + t) * ODD_TILES)
    in_specs = [pl.BlockSpec((ODD_TILES * TM, D_MODEL), lambda t: (first + t, 0))]
    args = [x]
    if has_prev:
        yg, gates = prev
        in_specs += _prev_specs(layer, first, ODD_TILES)
        args += [yg] * TOP_K + [gates, mods]
    in_specs += [
        _full_spec((1, D_MODEL)),
        _mod_spec(layer, 0, mod_row), _mod_spec(layer, 1, mod_row),
        _full_spec((D_MODEL, 2 * D_MODEL)), _full_spec((1, D_MODEL)),
        _full_spec((C_GROUPS, CHUNK, CHUNK)), _full_spec((CHUNK, D_MODEL)),
        _full_spec((D_MODEL, D_MODEL)),
        _mod_spec(layer, 2, mod_row), _full_spec((1, D_MODEL)),
        _mod_spec(layer, 3, mod_row), _mod_spec(layer, 4, mod_row),
        _full_spec((D_MODEL, LANES)), _full_spec((D_MODEL, LANES)), _full_spec((1, LANES)),
    ]
    args += [norm_g, mods, mods, w_in, v_norm_g, w_s, b_s, w_out, mods, gffn, mods, mods, rw_hi, rw_lo, rb]
    aliases = {}
    if group == 1:
        aliases = {len(args) + i: i for i in range(N_ROUTE_OUT)}
        in_specs += [pl.BlockSpec(memory_space=pl.ANY)] * N_ROUTE_OUT
        args += list(ctx_outs)
    return pl.pallas_call(
        functools.partial(_odd_kernel, group, has_prev),
        grid=(n_blocks,),
        in_specs=in_specs,
        out_specs=_route_out_specs(lambda t: first + t, ODD_TILES),
        out_shape=_ROUTE_OUT_SHAPES,
        input_output_aliases=aliases,
        compiler_params=pltpu.CompilerParams(dimension_semantics=("arbitrary",),
                                             vmem_limit_bytes=VMEM_LIMIT),
        name="odd_layer_ctx" if group == 0 else "odd_layer_lat",
    )(*args)


def _slot_kernel(ei_ref, base_ref, tri_ref, o_ref):
    lane = lax.broadcasted_iota(I32, (TM, LANES), 1)
    tri = tri_ref[...]
    for i in range(SLOT_TILES):
        ei = ei_ref[TM * i:TM * (i + 1), :]
        onehots = [lane == ei[:, k:k + 1] for k in range(TOP_K)]
        member = functools.reduce(lambda u, w: u + w, [o.astype(F32) for o in onehots])
        before = _dot(tri, member.astype(BF16)) + base_ref[SUBLANES * i:SUBLANES * i + 1, :]
        slots = jnp.zeros((TM, LANES), F32)
        for k in range(TOP_K):
            s = jnp.sum(jnp.where(onehots[k], before, 0.0), axis=1, keepdims=True)
            slots = slots + jnp.where(lane == k, s, 0.0)
        o_ref[i] = slots.T[0:SUBLANES, :].astype(I32)


SLOT_TILES = 4


def _slots(eidx, tile_base, tri):
    return pl.pallas_call(
        _slot_kernel,
        grid=(N_TILES // SLOT_TILES,),
        in_specs=[pl.BlockSpec((SLOT_TILES * TM, LANES), lambda t: (t, 0)),
                  pl.BlockSpec((SLOT_TILES * SUBLANES, LANES), lambda t: (t, 0)),
                  _full_spec((TM, TM))],
        out_specs=pl.BlockSpec((SLOT_TILES, SUBLANES, TM), lambda t: (t, 0, 0)),
        out_shape=jax.ShapeDtypeStruct((N_TILES, SUBLANES, TM), I32),
        compiler_params=pltpu.CompilerParams(dimension_semantics=("arbitrary",)),
        name="moe_slots",
    )(eidx, tile_base, tri)


def _sc_mesh():
    return plsc.VectorSubcoreMesh(core_axis_name="c", subcore_axis_name="s")


def _sc_worker():
    return lax.axis_index("s") * SC_CORES + lax.axis_index("c")


def _dispatch(hp, idx):
    n_chunks = N_TOK // SC_WORKERS // SC_ROWS

    @functools.partial(
        pl.kernel, mesh=_sc_mesh(),
        out_type=jax.ShapeDtypeStruct((N_SLOTS, HALF), U32),
        scratch_types=[pltpu.VMEM((n_chunks, TOP_K, SC_ROWS), I32), pltpu.VMEM((SC_BUFS, SC_ROWS, HALF), U32),
                       pltpu.SemaphoreType.DMA((SC_BUFS,)), pltpu.SemaphoreType.DMA((SC_BUFS,))],
        name="moe_dispatch",
    )
    def k(x_hbm, idx_hbm, out_hbm, idx_v, rows_v, read_sem, scat_sem):
        wid = _sc_worker()
        base = wid * n_chunks
        pltpu.sync_copy(idx_hbm.at[wid], idx_v)

        def read(j, b):
            return pltpu.make_async_copy(x_hbm.at[pl.ds((base + j) * SC_ROWS, SC_ROWS)], rows_v.at[b],
                                         read_sem.at[b])

        def scatter(j, b, kk):
            return pltpu.make_async_copy(rows_v.at[b], out_hbm.at[idx_v.at[j, kk]], scat_sem.at[b])

        def drain(j, b):
            for kk in range(TOP_K):
                scatter(j, b, kk).wait()

        read(0, 0).start()

        @pl.loop(0, n_chunks, step=SC_BUFS)
        def _(j):
            for b in range(SC_BUFS):
                jj = j + b
                other = (b + 1) % SC_BUFS
                read(jj, b).wait()

                @pl.when(jj >= 1)
                def _():
                    drain(jj - 1, other)

                @pl.when(jj + 1 < n_chunks)
                def _():
                    read(jj + 1, other).start()

                for kk in range(TOP_K):
                    scatter(jj, b, kk).start()

        drain(n_chunks - 1, (n_chunks - 1) % SC_BUFS)

    return k(hp, idx)


def _combine_gather(ys, idx):
    n_chunks = idx.shape[1]

    @functools.partial(
        pl.kernel, mesh=_sc_mesh(),
        out_type=jax.ShapeDtypeStruct((SC_WORKERS * n_chunks * SC_ROWS, HALF), U32),
        scratch_types=[pltpu.VMEM((n_chunks, SC_ROWS), I32), pltpu.VMEM((SC_BUFS, SC_ROWS, HALF), U32),
                       pltpu.SemaphoreType.DMA((SC_BUFS,)), pltpu.SemaphoreType.DMA((SC_BUFS,))],
        name="moe_combine",
    )
    def k(ys_hbm, idx_hbm, out_hbm, idx_v, rows_v, gather_sem, write_sem):
        wid = _sc_worker()
        base = wid * n_chunks
        pltpu.sync_copy(idx_hbm.at[wid], idx_v)

        def gather(j, b):
            return pltpu.make_async_copy(ys_hbm.at[idx_v.at[j]], rows_v.at[b], gather_sem.at[b])

        def write(j, b):
            return pltpu.make_async_copy(rows_v.at[b], out_hbm.at[pl.ds((base + j) * SC_ROWS, SC_ROWS)],
                                         write_sem.at[b])

        gather(0, 0).start()

        @pl.loop(0, n_chunks, step=SC_BUFS)
        def _(j):
            for b in range(SC_BUFS):
                jj = j + b
                other = (b + 1) % SC_BUFS
                gather(jj, b).wait()

                @pl.when(jj >= 1)
                def _():
                    write(jj - 1, other).wait()

                @pl.when(jj + 1 < n_chunks)
                def _():
                    gather(jj + 1, other).start()

                write(jj, b).start()

        write(n_chunks - 1, (n_chunks - 1) % SC_BUFS).wait()

    return k(ys, idx)


N_MATS = 3


def _moe_kernel(layer, first_ref, nblk_ref, xs_hbm, wg_hbm, wu_hbm, wd_hbm, bg_ref, bu_ref, bd_ref,
                ys_hbm, wf32, wbf, xbuf, obuf, wsem, xsem, osem):
    e = pl.program_id(0)
    slot = e % 2
    w_hbm = (wg_hbm, wu_hbm, wd_hbm)
    nb = nblk_ref[e]
    row0 = first_ref[e] * MOE_BLOCK
    n_chunks = nb // BLOCKS_PER_CHUNK
    has_tail = nb % BLOCKS_PER_CHUNK == 1
    tail_row = row0 + n_chunks * MOE_CHUNK
    tail_slot = n_chunks % 2

    def w_copy(ee, s, m):
        return pltpu.make_async_copy(w_hbm[m].at[layer, ee], wf32.at[s, m], wsem.at[s, m])

    def x_copy(row, rows, s):
        return pltpu.make_async_copy(xs_hbm.at[pl.ds(row, rows)], xbuf.at[s, pl.ds(0, rows)], xsem.at[s])

    def o_copy(row, rows, s):
        return pltpu.make_async_copy(obuf.at[s, pl.ds(0, rows)], ys_hbm.at[pl.ds(row, rows)], osem.at[s])

    def expert(xp):
        lo, hi = _unpack_halves(xp)
        xb = jnp.concatenate([lo.astype(BF16), hi.astype(BF16)], axis=1)
        gt = jnp.minimum(_dot(xb, wbf[0]) + bg_ref[...], SWIGLU_LIMIT)
        up = jnp.clip(_dot(xb, wbf[1]) + bu_ref[...], -SWIGLU_LIMIT, SWIGLU_LIMIT)
        glu = gt * jax.nn.sigmoid(SWIGLU_ALPHA * gt)
        hmid = ((up + 1.0) * glu).astype(BF16)
        out = _dot(hmid, wbf[2]) + bd_ref[...]
        return _pack_halves(out.astype(BF16).astype(F32))

    @pl.when(e == 0)
    def _():
        for m in range(N_MATS):
            w_copy(0, 0, m).start()

    @pl.when(n_chunks > 0)
    def _():
        x_copy(row0, MOE_CHUNK, 0).start()

    @pl.when((n_chunks == 0) & has_tail)
    def _():
        x_copy(row0, MOE_BLOCK, 0).start()

    for m in range(N_MATS):
        w_copy(e, slot, m).wait()

    @pl.when(e + 1 < N_EXPERTS)
    def _():
        for m in range(N_MATS):
            w_copy(e + 1, 1 - slot, m).start()

    for m in range(N_MATS):
        wbf[m] = wf32[slot, m].astype(BF16)

    def chunk(c, carry):
        s = c % 2
        row = row0 + c * MOE_CHUNK
        x_copy(row, MOE_CHUNK, s).wait()

        @pl.when(c + 1 < n_chunks)
        def _():
            x_copy(row + MOE_CHUNK, MOE_CHUNK, 1 - s).start()

        @pl.when((c + 1 == n_chunks) & has_tail)
        def _():
            x_copy(tail_row, MOE_BLOCK, 1 - s).start()

        obuf[s] = expert(xbuf[s])

        @pl.when(c >= 1)
        def _():
            o_copy(row - MOE_CHUNK, MOE_CHUNK, 1 - s).wait()

        o_copy(row, MOE_CHUNK, s).start()
        return carry

    lax.fori_loop(0, n_chunks, chunk, 0)

    @pl.when(has_tail)
    def _():
        x_copy(tail_row, MOE_BLOCK, tail_slot).wait()
        obuf[tail_slot, 0:MOE_BLOCK, :] = expert(xbuf[tail_slot, 0:MOE_BLOCK, :])

        @pl.when(n_chunks >= 1)
        def _():
            o_copy(tail_row - MOE_CHUNK, MOE_CHUNK, 1 - tail_slot).wait()

        o_copy(tail_row, MOE_BLOCK, tail_slot).start()
        o_copy(tail_row, MOE_BLOCK, tail_slot).wait()

    @pl.when(jnp.logical_not(has_tail) & (n_chunks >= 1))
    def _():
        o_copy(tail_row - MOE_CHUNK, MOE_CHUNK, 1 - tail_slot).wait()


def _moe_experts(layer, xs, first_block, n_blocks, w_gate, b_gate, w_up, b_up, w_down, b_down):
    hbm = pl.BlockSpec(memory_space=pl.ANY)
    bspec = pl.BlockSpec((None, None, 1, D_MODEL), lambda e, fb, nb: (layer, e, 0, 0))
    bias = lambda b: b.reshape(DEPTH, N_EXPERTS, 1, D_MODEL)
    return pl.pallas_call(
        functools.partial(_moe_kernel, layer),
        grid_spec=pltpu.PrefetchScalarGridSpec(
            num_scalar_prefetch=2,
            grid=(N_EXPERTS,),
            in_specs=[hbm, hbm, hbm, hbm, bspec, bspec, bspec],
            out_specs=hbm,
            scratch_shapes=[
                pltpu.VMEM((2, N_MATS, D_MODEL, D_MODEL), F32),
                pltpu.VMEM((N_MATS, D_MODEL, D_MODEL), BF16),
                pltpu.VMEM((2, MOE_CHUNK, HALF), U32),
                pltpu.VMEM((2, MOE_CHUNK, HALF), U32),
                pltpu.SemaphoreType.DMA((2, N_MATS)),
                pltpu.SemaphoreType.DMA((2,)),
                pltpu.SemaphoreType.DMA((2,)),
            ],
        ),
        out_shape=jax.ShapeDtypeStruct((N_SLOTS, HALF), U32),
        compiler_params=pltpu.CompilerParams(dimension_semantics=("arbitrary",),
                                             vmem_limit_bytes=VMEM_LIMIT),
        name="moe_experts",
    )(first_block, n_blocks, xs, w_gate, w_up, w_down, bias(b_gate), bias(b_up), bias(b_down))


def _moe(layer, hp, eidx, cnt, tri, w_gate, b_gate, w_up, b_up, w_down, b_down):
    tile_cnt = cnt[::SUBLANES, :N_EXPERTS].astype(I32)
    counts = jnp.sum(tile_cnt, axis=0)
    padded = (counts + MOE_BLOCK - 1) // MOE_BLOCK * MOE_BLOCK
    pad_ends = jnp.cumsum(padded)
    pad_starts = pad_ends - padded
    tile_base = pad_starts[None, :] + jnp.cumsum(tile_cnt, axis=0) - tile_cnt
    tile_base = jnp.pad(tile_base.astype(F32), ((0, 0), (0, LANES - N_EXPERTS)))
    tile_base = jnp.repeat(tile_base, SUBLANES, axis=0)
    first_block = (pad_starts // MOE_BLOCK).astype(I32)
    n_blocks = (padded // MOE_BLOCK).astype(I32)

    slots = _slots(eidx, tile_base, tri)[:, :TOP_K, :]
    per_tile = TM // SC_ROWS
    d_idx = slots.reshape(N_TILES, TOP_K, per_tile, SC_ROWS).transpose(0, 2, 1, 3)
    d_idx = d_idx.reshape(SC_WORKERS, N_TOK // SC_WORKERS // SC_ROWS, TOP_K, SC_ROWS)

    xs = _dispatch(hp, d_idx)
    ys = _moe_experts(layer, xs, first_block, n_blocks, w_gate, b_gate, w_up, b_up, w_down, b_down)
    gathered = []
    for tiles in (slots[:CTX_TILES], slots[CTX_TILES:]):
        n_tok = tiles.shape[0] * TM
        c_idx = tiles.transpose(1, 0, 2).reshape(SC_WORKERS, n_tok * TOP_K // SC_WORKERS // SC_ROWS, SC_ROWS)
        gathered.append(_combine_gather(ys, c_idx).reshape(TOP_K, n_tok, HALF))
    return gathered


FINAL_TILES = 2


def _final_kernel(x_ref, *refs):
    o_ref = refs[N_PREV]
    o_ref[...] = _apply_prev(x_ref[...], refs[:N_PREV])


def _final_residual(group, x, prev, mods):
    rows = FINAL_TILES * TM
    first, n_blocks = _group_blocks(group, FINAL_TILES)
    yg, gates = prev
    return pl.pallas_call(
        _final_kernel,
        grid=(n_blocks,),
        in_specs=[pl.BlockSpec((rows, D_MODEL), lambda t: (first + t, 0))]
        + _prev_specs(DEPTH, first, FINAL_TILES),
        out_specs=pl.BlockSpec((rows, D_MODEL), lambda t: (t, 0)),
        out_shape=jax.ShapeDtypeStruct((n_blocks * rows, D_MODEL), F32),
        compiler_params=pltpu.CompilerParams(dimension_semantics=("arbitrary",)),
        name="final_residual",
    )(x, *([yg] * TOP_K), gates, mods)


def _dft_pair(n):
    idx = np.arange(n)
    ang = 2.0 * np.pi * ((idx[:, None] * idx[None, :]) % n) / n
    return np.cos(ang) / np.sqrt(n), np.sin(ang) / np.sqrt(n)


def _rope_tables():
    lane = np.arange(GROUP)
    within = lane % QK_DIM
    axis = within // 32
    e = within % 32
    inv16 = ROPE_BASE ** (-jnp.arange(16, dtype=F32) / 16)
    pos = np.arange(LAT_SEQ)
    coord = np.where(axis[None, :] == 0, (pos // GRID_W)[:, None], (pos % GRID_W)[:, None])
    ang = jnp.asarray(coord, F32) * inv16[e % 16][None, :]
    first = jnp.asarray((e // 16) == 0)[None, :]
    cos, sin = jnp.cos(ang), jnp.sin(ang)
    return cos, jnp.where(first, -sin, 0.0), jnp.where(first, 0.0, sin)


def kernel(x_prompt, x_sample, cache_k, cache_v, c, c_ctx, ada_w, ada_b, norm_mix_g, norm_ffn_g,
           w_in_even, w_out_even, q_norm_g, k_norm_g, lambda_q, lambda_k, subln_g,
           w_in_odd, v_norm_g, w_spatial, b_spatial, w_out_odd,
           router_w, router_b, w_gate, b_gate, w_up, b_up, w_down, b_down):
    n_even = w_in_even.shape[0]
    x_groups = (x_prompt.reshape(N_CTX, D_MODEL), x_sample.reshape(N_LAT, D_MODEL))
    x = None
    cond = jnp.zeros((COND_ROWS, D_MODEL), F32).at[0].set(c_ctx).at[1:1 + LAT_BATCH].set(c)
    mods = _ada_modulation(cond, ada_w, ada_b).reshape(DEPTH, N_MOD, COND_ROWS, 1, D_MODEL)

    dft_ctx = [jnp.asarray(m, F32).astype(BF16) for m in _dft_pair(CTX_SEQ)]
    dft_lat = [jnp.asarray(m, F32).astype(BF16) for m in _dft_pair(LAT_SEQ)]
    c128, s128 = _dft_pair(GROUP)
    cs128 = jnp.asarray(np.concatenate([c128, -s128], axis=0), F32).astype(BF16)
    grp = np.arange(HALF) // QK_DIM
    blockdiag = jnp.asarray(grp[:, None] == grp[None, :], F32).astype(BF16)
    tri = jnp.asarray(np.arange(TM)[:, None] > np.arange(TM)[None, :], F32).astype(BF16)
    rope_tabs = _rope_tables()
    cache_k2 = cache_k.reshape(LAT_BATCH, n_even, PAST_LEN, HALF)
    cache_v2 = cache_v.reshape(LAT_BATCH, n_even, PAST_LEN, HALF)

    rw = jnp.pad(router_w, ((0, 0), (0, 0), (0, LANES - N_EXPERTS)))
    rw_hi = rw.astype(BF16)
    rw_lo = (rw - rw_hi.astype(F32)).astype(BF16)
    rb = jnp.pad(router_b, ((0, 0), (0, LANES - N_EXPERTS)))[:, None, :]

    prev = None
    caches = None
    for l in range(DEPTH):
        j = l // 2
        gmix = norm_mix_g[l][None, :]
        gffn = norm_ffn_g[l][None, :]
        if l % 2 == 0:
            proj_args = (mods, gmix, w_in_even[j].astype(BF16), blockdiag,
                         jnp.tile(q_norm_g[j], HALF // QK_DIM)[None, :],
                         jnp.tile(k_norm_g[j], HALF // QK_DIM)[None, :], rope_tabs)
            outs = None
            for g in range(2):
                x_in = x_groups[g] if l == 0 else x
                outs_g = _even_proj(l, g, x_in, l == 0, None if prev is None else (prev[0][g], prev[1]),
                                    outs, caches, *proj_args)
                if g == 0:
                    caches = outs_g[N_PROJ_OUT:]
                outs = outs_g[:N_PROJ_OUT]
            x, a, q, k, v = outs
            common = (cs128, lambda_q[j], lambda_k[j], subln_g[j][None, :], w_out_even[j].astype(BF16),
                      mods, gffn, rw_hi[l], rw_lo[l], rb[l])
            x, hp, gates, eidx, cnt = _even_mix(l, x, a, q, k, v, cache_k2, cache_v2, dft_ctx, dft_lat, *common)
        else:
            b_s = jnp.broadcast_to(jnp.transpose(b_spatial[j])[:, :, None],
                                   (CHUNK, C_GROUPS, GROUP)).reshape(CHUNK, D_MODEL)
            odd_args = (mods, gmix, w_in_odd[j].astype(BF16), v_norm_g[j][None, :],
                        w_spatial[j].astype(BF16), b_s, w_out_odd[j].astype(BF16), gffn, rw_hi[l], rw_lo[l], rb[l])
            outs = None
            for g in range(2):
                outs = _odd_layer(l, g, x, None if prev is None else (prev[0][g], prev[1]), outs, *odd_args)
            x, hp, gates, eidx, cnt = outs
        yg = _moe(l, hp, eidx, cnt, tri, w_gate, b_gate, w_up, b_up, w_down, b_down)
        prev = (yg, gates)
    y_ctx = _final_residual(0, x, (prev[0][0], prev[1]), mods)
    y_lat = _final_residual(1, x, (prev[0][1], prev[1]), mods)
    return (y_ctx.reshape(CTX_BATCH, CTX_SEQ, D_MODEL),
            y_lat.reshape(LAT_BATCH, LAT_SEQ, D_MODEL),
            caches[0].reshape(CTX_BATCH, N_EVEN, CTX_SEQ, N_GROUPS, 2, QK_DIM),
            caches[1].reshape(CTX_BATCH, N_EVEN, CTX_SEQ, N_GROUPS, GROUP))
```

```python
import functools
import math

import numpy as np
import jax
import jax.numpy as jnp
from jax import lax
from jax.experimental import pallas as pl
from jax.experimental.pallas import tpu as pltpu
from jax.experimental.pallas import tpu_sc as plsc

F32 = jnp.float32
BF16 = jnp.bfloat16
U32 = jnp.uint32
I32 = jnp.int32

D_MODEL = 1024
DEPTH = 4
N_EVEN = (DEPTH + 1) // 2
N_MOD = 6
EPS = 1e-6
CTX_BATCH, CTX_SEQ = 32, 256
LAT_BATCH, LAT_SEQ = 8, 1024
PAST_LEN = 512
GRID_W = 64
N_CTX = CTX_BATCH * CTX_SEQ
N_LAT = LAT_BATCH * LAT_SEQ
N_TOK = N_CTX + N_LAT
TM = 256
N_TILES = N_TOK // TM
CTX_TILES = N_CTX // TM
LAT_TILES_PER_BATCH = LAT_SEQ // TM
COND_ROWS = 16
GROUP = 128
N_GROUPS = 4
QK_DIM = 64
HALF = 512
CHUNK = 128
C_GROUPS = 8
N_EXPERTS = 32
TOP_K = 4
LANES = 128
SUBLANES = 8
SWIGLU_LIMIT = 7.0
SWIGLU_ALPHA = 1.702
MOE_BLOCK = 256
BLOCKS_PER_CHUNK = 2
MOE_CHUNK = BLOCKS_PER_CHUNK * MOE_BLOCK
N_ASSIGN = N_TOK * TOP_K
MOE_BLOCKS = N_ASSIGN // MOE_BLOCK + N_EXPERTS
N_SLOTS = MOE_BLOCKS * MOE_BLOCK
ROPE_BASE = 10000.0
VMEM_LIMIT = 52 * 1024 * 1024
SC_CORES = 2
SC_SUBCORES = 16
SC_WORKERS = SC_CORES * SC_SUBCORES
SC_ROWS = 64
SC_BUFS = 2
HI_MASK = 0xFFFF0000
LOG2E = math.log2(math.e)


def _dot(a, b):
    return jnp.dot(a, b, preferred_element_type=F32)


def _rms(x, g):
    return x * lax.rsqrt(jnp.mean(x * x, axis=-1, keepdims=True) + EPS) * g


def _split_bf16(x):
    hi = x.astype(BF16)
    lo = (x - hi.astype(F32)).astype(BF16)
    return hi, lo


def _pack_halves(xf):
    b = lax.bitcast_convert_type(xf, U32)
    return (b[:, :HALF] >> 16) | (b[:, HALF:] & jnp.uint32(HI_MASK))


def _unpack_halves(w):
    lo = lax.bitcast_convert_type(w << 16, F32)
    hi = lax.bitcast_convert_type(w & jnp.uint32(HI_MASK), F32)
    return lo, hi


def _tile_mod_row(t):
    return jnp.where(t < CTX_TILES, 0, 1 + (t - CTX_TILES) // LAT_TILES_PER_BATCH)


def _mod_spec(layer, k, row_fn):
    return pl.BlockSpec((None, None, None, 1, D_MODEL),
                        lambda *g: (layer, k, row_fn(*g), 0, 0))


def _full_spec(shape):
    return pl.BlockSpec(shape, lambda *g: (0,) * len(shape))


def _ada_kernel(cond_ref, w_ref, b_ref, o_ref):
    c = cond_ref[...]
    s = (c * jax.nn.sigmoid(c)).astype(BF16)
    o_ref[...] = _dot(s, w_ref[...].astype(BF16)) + b_ref[...]


def _ada_modulation(cond, ada_w, ada_b):
    return pl.pallas_call(
        _ada_kernel,
        grid=(DEPTH, N_MOD),
        in_specs=[
            _full_spec((COND_ROWS, D_MODEL)),
            pl.BlockSpec((None, D_MODEL, D_MODEL), lambda l, n: (l, 0, n)),
            pl.BlockSpec((None, None, 1, D_MODEL), lambda l, n: (l, n, 0, 0)),
        ],
        out_specs=pl.BlockSpec((None, None, COND_ROWS, D_MODEL), lambda l, n: (l, n, 0, 0)),
        out_shape=jax.ShapeDtypeStruct((DEPTH, N_MOD, COND_ROWS, D_MODEL), F32),
        compiler_params=pltpu.CompilerParams(dimension_semantics=("arbitrary", "arbitrary"),
                                             vmem_limit_bytes=VMEM_LIMIT),
        name="ada_modulation",
    )(cond, ada_w, ada_b.reshape(DEPTH, N_MOD, 1, D_MODEL))


N_PREV = 6


def _group_blocks(group, tiles_per_block):
    ctx = CTX_TILES // tiles_per_block
    return (0, ctx) if group == 0 else (ctx, N_TILES // tiles_per_block - ctx)


def _prev_specs(layer, first_block, tiles_per_block):
    rows = tiles_per_block * TM
    ys = [pl.BlockSpec((None, rows, HALF), functools.partial(lambda k, t: (k, t, 0), k))
          for k in range(TOP_K)]
    return ys + [pl.BlockSpec((rows, LANES), lambda t: (first_block + t, 0)),
                 _mod_spec(layer - 1, 5, lambda t: _tile_mod_row((first_block + t) * tiles_per_block))]


def _apply_prev(x, prev_refs):
    y_refs, gt_ref, gate_ref = prev_refs[:TOP_K], prev_refs[TOP_K], prev_refs[TOP_K + 1]
    gt = gt_ref[...]
    acc_lo = acc_hi = None
    for k in range(TOP_K):
        lo, hi = _unpack_halves(y_refs[k][...])
        g = gt[:, k:k + 1]
        acc_lo = g * lo if acc_lo is None else acc_lo + g * lo
        acc_hi = g * hi if acc_hi is None else acc_hi + g * hi
    return x + gate_ref[...] * jnp.concatenate([acc_lo, acc_hi], axis=1)


N_ROUTE_OUT = 5


def _route_out_specs(tile_fn, tiles_per_block=1):
    tile = lambda w: pl.BlockSpec((tiles_per_block * TM, w), lambda *g: (tile_fn(*g), 0))
    return [tile(D_MODEL), tile(HALF), tile(LANES),
            pl.BlockSpec((tiles_per_block * SUBLANES, TM), lambda *g: (tile_fn(*g), 0)),
            pl.BlockSpec((tiles_per_block * N_EXPERTS, LANES), lambda *g: (tile_fn(*g), 0))]


_ROUTE_OUT_SHAPES = [
    jax.ShapeDtypeStruct((N_TOK, D_MODEL), F32),
    jax.ShapeDtypeStruct((N_TOK, HALF), U32),
    jax.ShapeDtypeStruct((N_TOK, LANES), F32),
    jax.ShapeDtypeStruct((N_TILES * SUBLANES, TM), I32),
    jax.ShapeDtypeStruct((N_TILES * N_EXPERTS, LANES), F32),
]


def _post_mixer(x, mix, gate_ref, gffn_ref, shift_ref, scale_ref, rwhi_ref, rwlo_ref, rb_ref,
                xo_ref, hp_ref, gt_ref, ei_ref, cnt_ref):
    xn = x + gate_ref[...] * mix
    xo_ref[...] = xn
    h2 = _rms(xn, gffn_ref[...]) * (1.0 + scale_ref[...]) + shift_ref[...]
    hi = h2.astype(BF16)
    hif = hi.astype(F32)
    lo = (h2 - hif).astype(BF16)
    hp_ref[...] = _pack_halves(hif)
    rwhi = rwhi_ref[...]
    logits = _dot(hi, rwhi) + _dot(lo, rwhi) + _dot(hi, rwlo_ref[...]) + rb_ref[...]

    rows = x.shape[0]
    work = logits.T[0:N_EXPERTS, :]
    expert = lax.broadcasted_iota(I32, (N_EXPERTS, rows), 0).astype(F32)
    krow = lax.broadcasted_iota(I32, (SUBLANES, rows), 0)
    member = jnp.zeros((N_EXPERTS, rows), F32)
    gates = jnp.zeros((SUBLANES, rows), F32)
    ids = jnp.zeros((SUBLANES, rows), F32)
    den = None
    top = None
    for k in range(TOP_K):
        m = jnp.max(work, axis=0, keepdims=True)
        idx = jnp.min(jnp.where(work == m, expert, float(N_EXPERTS)), axis=0, keepdims=True)
        onehot = expert == idx
        work = jnp.where(onehot, -jnp.inf, work)
        member = member + onehot.astype(F32)
        if k == 0:
            top = m
        e = jnp.exp(m - top)
        den = e if den is None else den + e
        gates = gates + jnp.where(krow == k, e, 0.0)
        ids = ids + jnp.where(krow == k, idx, 0.0)
    gates = gates * (1.0 / den)
    gt_ref[...] = jnp.concatenate([gates, jnp.zeros((LANES - SUBLANES, rows), F32)], axis=0).T
    ids = ids.astype(I32)
    for i in range(rows // TM):
        ei_ref[SUBLANES * i:SUBLANES * (i + 1), :] = ids[:, TM * i:TM * (i + 1)]
        cnt_ref[N_EXPERTS * i:N_EXPERTS * (i + 1), :] = jnp.broadcast_to(
            jnp.sum(member[:, TM * i:TM * (i + 1)], axis=1, keepdims=True), (N_EXPERTS, LANES))


def _rope(x, c, sa, sb):
    outs = []
    for h in range(N_GROUPS):
        xs = x[:, GROUP * h:GROUP * (h + 1)]
        outs.append(xs * c + pltpu.roll(xs, GROUP - 16, 1) * sa + pltpu.roll(xs, 16, 1) * sb)
    return jnp.concatenate(outs, axis=1)


PROJ_TILES = 2


N_PROJ_OUT = 5


def _even_proj_kernel(group, has_prev, n_aliased, *refs):
    x_ref = refs[0]
    refs = refs[1:]
    if has_prev:
        prev_refs = refs[:N_PREV]
        refs = refs[N_PREV:]
    (g_ref, shift_ref, scale_ref, w_ref, bd_ref, qg_ref, kg_ref, cos_ref, sa_ref, sb_ref) = refs[:10]
    outs = refs[10 + n_aliased:]
    xo_ref, a_ref, q_ref, k_ref, v_ref = outs[:N_PROJ_OUT]

    x = x_ref[...]
    if has_prev:
        x = _apply_prev(x, prev_refs)
    xo_ref[...] = x
    h = (_rms(x, g_ref[...]) * (1.0 + scale_ref[...]) + shift_ref[...]).astype(BF16)

    a_ref[...] = _dot(h, w_ref[:, 0:HALF]).astype(BF16)
    zv = _dot(h, w_ref[:, 3 * HALF:4 * HALF])
    v_ref[...] = zv.astype(BF16)

    bd = bd_ref[...]

    def qk_norm(z, gain):
        shi, slo = _split_bf16(z * z)
        ssq = _dot(shi, bd) + _dot(slo, bd)
        return z * lax.rsqrt(ssq * (1.0 / QK_DIM) + EPS) * gain

    qn = qk_norm(_dot(h, w_ref[:, HALF:2 * HALF]), qg_ref[...]) * (QK_DIM ** -0.5 * LOG2E)
    kn = qk_norm(_dot(h, w_ref[:, 2 * HALF:3 * HALF]), kg_ref[...])

    if group == 0:
        kf_ref, vf_ref = outs[N_PROJ_OUT:]
        kf_ref[...] = kn.reshape(kf_ref.shape)
        vf_ref[...] = zv.reshape(vf_ref.shape)
        q_ref[...] = qn.astype(BF16)
        k_ref[...] = kn.astype(BF16)
    else:
        c, sa, sb = cos_ref[...], sa_ref[...], sb_ref[...]
        q_ref[...] = _rope(qn, c, sa, sb).astype(BF16)
        k_ref[...] = _rope(kn, c, sa, sb).astype(BF16)


def _even_proj(layer, group, x, x_is_group_local, prev, ctx_outs, caches, mods, norm_g, w_in, blockdiag,
               qg, kg, rope_tabs):
    has_prev = prev is not None
    rows = PROJ_TILES * TM
    first, n_blocks = _group_blocks(group, PROJ_TILES)
    mod_row = lambda t: _tile_mod_row((first + t) * PROJ_TILES)
    glob = lambda w: pl.BlockSpec((rows, w), lambda t: (first + t, 0))
    local = lambda w: pl.BlockSpec((rows, w), lambda t: (t, 0))
    rope_spec = pl.BlockSpec((rows, GROUP), lambda t: (t % (LAT_SEQ // rows) if group == 1 else 0, 0))
    in_specs = [local(D_MODEL) if x_is_group_local else glob(D_MODEL)]
    args = [x]
    if has_prev:
        yg, gates = prev
        in_specs += _prev_specs(layer, first, PROJ_TILES)
        args += [yg] * TOP_K + [gates, mods]
    in_specs += [
        _full_spec((1, D_MODEL)),
        _mod_spec(layer, 0, mod_row), _mod_spec(layer, 1, mod_row),
        _full_spec((D_MODEL, 4 * HALF)), _full_spec((HALF, HALF)),
        _full_spec((1, HALF)), _full_spec((1, HALF)),
        rope_spec, rope_spec, rope_spec,
    ]
    args += [norm_g, mods, mods, w_in, blockdiag, qg, kg, *rope_tabs]
    out_specs = [glob(D_MODEL)] + [glob(HALF)] * 4
    out_shape = [jax.ShapeDtypeStruct((N_TOK, D_MODEL), F32)] + [jax.ShapeDtypeStruct((N_TOK, HALF), BF16)] * 4
    aliased = []
    if group == 0:
        seqs = rows // CTX_SEQ
        cache_spec = pl.BlockSpec((seqs, None, CTX_SEQ, HALF), lambda t: (t, layer // 2, 0, 0))
        out_specs += [cache_spec] * 2
        out_shape += [jax.ShapeDtypeStruct((CTX_BATCH, N_EVEN, CTX_SEQ, HALF), F32)] * 2
        if caches is not None:
            aliased = [(caches[i], N_PROJ_OUT + i) for i in range(2)]
    else:
        aliased = [(ctx_outs[i], i) for i in range(N_PROJ_OUT)]
    aliases = {len(args) + i: out_idx for i, (_, out_idx) in enumerate(aliased)}
    in_specs += [pl.BlockSpec(memory_space=pl.ANY)] * len(aliased)
    args += [arr for arr, _ in aliased]
    return pl.pallas_call(
        functools.partial(_even_proj_kernel, group, has_prev, len(aliased)),
        grid=(n_blocks,),
        in_specs=in_specs, out_specs=out_specs, out_shape=out_shape,
        input_output_aliases=aliases,
        compiler_params=pltpu.CompilerParams(dimension_semantics=("arbitrary",),
                                             vmem_limit_bytes=VMEM_LIMIT),
        name="even_proj_ctx" if group == 0 else "even_proj_lat",
    )(*args)


def _even_mix_kernel(lambda_init, *refs):
    (x_ref, q_ref, ac_ref, kc_ref, vc_ref, al_ref, kl_ref, vl_ref, ck_ref, cv_ref,
     cnc_ref, snc_ref, cnl_ref, snl_ref) = refs[:14]
    shared = refs[14:]
    t = pl.program_id(0)

    @pl.when(t < CTX_TILES)
    def _():
        _even_mix_body(lambda_init, x_ref, q_ref, ac_ref, [(kc_ref, vc_ref, False)],
                       cnc_ref, snc_ref, *shared)

    @pl.when(t >= CTX_TILES)
    def _():
        _even_mix_body(lambda_init, x_ref, q_ref, al_ref, [(kl_ref, vl_ref, False), (ck_ref, cv_ref, True)],
                       cnl_ref, snl_ref, *shared)


def _even_mix_body(lambda_init, x_ref, q_ref, a_ref, kv_refs, cn_ref, sn_ref, *refs):
    (cs_ref, lq_ref, lk_ref, sg_ref, wo_ref,
     gate_ref, gffn_ref, shift_ref, scale_ref, rwhi_ref, rwlo_ref, rb_ref) = refs[:12]
    out_refs = refs[12:]

    a = a_ref[...]
    y1 = _dot(cn_ref[...], a).astype(BF16)
    y2 = _dot(sn_ref[...], a).astype(BF16)
    cs = cs_ref[...]
    pieces = []
    for g in range(N_GROUPS):
        sl = slice(GROUP * g, GROUP * (g + 1))
        pieces.append(_dot(jnp.concatenate([y1[:, sl], y2[:, sl]], axis=1), cs).astype(BF16))

    d = jnp.sum(lq_ref[...] * lk_ref[...], axis=1, keepdims=True)
    ed = jnp.exp(d)
    lam = ed[0:1, :] - ed[1:2, :] + lambda_init
    lane = lax.broadcasted_iota(I32, (TM, GROUP), 1)
    nt = (((1,), (1,)), ((), ()))
    for hd in range(N_GROUPS):
        sl = slice(GROUP * hd, GROUP * (hd + 1))
        qf = q_ref[:, sl].astype(F32)
        parts = []
        for k_ref, v_ref, is_f32 in kv_refs:
            kk, vv = k_ref[:, sl], v_ref[:, sl]
            parts.append((kk.astype(BF16), vv.astype(BF16)) if is_f32 else (kk, vv))

        def probs(qm):
            ss = [lax.dot_general(qm, kk, nt, preferred_element_type=F32) for kk, _ in parts]
            m = functools.reduce(jnp.maximum, [jnp.max(s, axis=1, keepdims=True) for s in ss])
            es = [jnp.exp2(s - m) for s in ss]
            den = functools.reduce(lambda u, w: u + w, [jnp.sum(e, axis=1, keepdims=True) for e in es])
            return es, 1.0 / den

        es0, inv0 = probs(jnp.where(lane < QK_DIM, qf, 0.0).astype(BF16))
        es1, inv1 = probs(jnp.where(lane >= QK_DIM, qf, 0.0).astype(BF16))
        o0 = functools.reduce(lambda u, w: u + w,
                              [_dot(e.astype(BF16), vv) for e, (_, vv) in zip(es0, parts)])
        o1 = functools.reduce(lambda u, w: u + w,
                              [_dot(e.astype(BF16), vv) for e, (_, vv) in zip(es1, parts)])
        o = o0 * inv0 - o1 * (lam * inv1)
        pieces.append((_rms(o, sg_ref[...]) * (1.0 - lambda_init)).astype(BF16))

    mix = _dot(jnp.concatenate(pieces, axis=1), wo_ref[...])
    _post_mixer(x_ref[...], mix, gate_ref, gffn_ref, shift_ref, scale_ref, rwhi_ref, rwlo_ref, rb_ref,
                *out_refs)


def _even_mix(layer, x, a, q, k, v, cache_k, cache_v, dft_ctx, dft_lat, cs128, lam_q, lam_k, subln_g,
              w_out, mods, gffn, rw_hi, rw_lo, rb):
    j = layer // 2
    lambda_init = 0.8 - 0.6 * math.exp(-0.3 * layer)
    lat = lambda t: jnp.maximum(t - CTX_TILES, 0)
    tile = lambda w: pl.BlockSpec((TM, w), lambda t: (t, 0))
    ctx_seq = pl.BlockSpec((CTX_SEQ, HALF), lambda t: (jnp.minimum(t, CTX_TILES - 1), 0))
    lat_seq = pl.BlockSpec((LAT_SEQ, HALF),
                           lambda t: (N_CTX // LAT_SEQ + lat(t) // LAT_TILES_PER_BATCH, 0))
    cache = pl.BlockSpec((None, None, PAST_LEN, HALF),
                         lambda t: (lat(t) // LAT_TILES_PER_BATCH, j, 0, 0))
    dft_lat_spec = pl.BlockSpec((TM, LAT_SEQ), lambda t: (lat(t) % LAT_TILES_PER_BATCH, 0))
    in_specs = [
        tile(D_MODEL), tile(HALF), ctx_seq, ctx_seq, ctx_seq, lat_seq, lat_seq, lat_seq, cache, cache,
        _full_spec((CTX_SEQ, CTX_SEQ)), _full_spec((CTX_SEQ, CTX_SEQ)), dft_lat_spec, dft_lat_spec,
        _full_spec((2 * GROUP, GROUP)),
        _full_spec((2, QK_DIM)), _full_spec((2, QK_DIM)), _full_spec((1, GROUP)),
        _full_spec((D_MODEL, D_MODEL)),
        _mod_spec(layer, 2, _tile_mod_row), _full_spec((1, D_MODEL)),
        _mod_spec(layer, 3, _tile_mod_row), _mod_spec(layer, 4, _tile_mod_row),
        _full_spec((D_MODEL, LANES)), _full_spec((D_MODEL, LANES)), _full_spec((1, LANES)),
    ]
    args = [x, q, a, k, v, a, k, v, cache_k, cache_v, dft_ctx[0], dft_ctx[1], dft_lat[0], dft_lat[1],
            cs128, lam_q, lam_k, subln_g, w_out, mods, gffn, mods, mods, rw_hi, rw_lo, rb]
    return pl.pallas_call(
        functools.partial(_even_mix_kernel, lambda_init),
        grid=(N_TILES,),
        in_specs=in_specs,
        out_specs=_route_out_specs(lambda t: t),
        out_shape=_ROUTE_OUT_SHAPES,
        compiler_params=pltpu.CompilerParams(dimension_semantics=("arbitrary",),
                                             vmem_limit_bytes=VMEM_LIMIT),
        name="even_mix",
    )(*args)


ODD_TILES = 2


def _odd_kernel(group, has_prev, *refs):
    x_ref = refs[0]
    refs = refs[1:]
    if has_prev:
        prev_refs = refs[:N_PREV]
        refs = refs[N_PREV:]
    (g_ref, shift_ref, scale_ref, w_ref, vg_ref, ws_ref, bs_ref, wo_ref,
     gate_ref, gffn_ref, shift2_ref, scale2_ref, rwhi_ref, rwlo_ref, rb_ref) = refs[:15]
    out_refs = refs[15:]
    if group == 1:
        out_refs = out_refs[N_ROUTE_OUT:]

    x = x_ref[...]
    if has_prev:
        x = _apply_prev(x, prev_refs)
    h = (_rms(x, g_ref[...]) * (1.0 + scale_ref[...]) + shift_ref[...]).astype(BF16)

    def gelu(z):
        return 0.5 * z * (1.0 + lax.erf(z * (2.0 ** -0.5)))

    u = gelu(_dot(h, w_ref[:, 0:D_MODEL]))
    vn = _rms(gelu(_dot(h, w_ref[:, D_MODEL:2 * D_MODEL])), vg_ref[...]).astype(BF16)
    bs = bs_ref[...]
    rows = []
    for c in range(x.shape[0] // CHUNK):
        cols = []
        for g in range(C_GROUPS):
            cols.append(_dot(ws_ref[g], vn[CHUNK * c:CHUNK * (c + 1), GROUP * g:GROUP * (g + 1)]))
        rows.append(jnp.concatenate(cols, axis=1) + bs)
    sv = jnp.concatenate(rows, axis=0)
    mix = _dot((u * sv).astype(BF16), wo_ref[...])
    _post_mixer(x, mix, gate_ref, gffn_ref, shift2_ref, scale2_ref, rwhi_ref, rwlo_ref, rb_ref,
                *out_refs)


def _odd_layer(layer, group, x, prev, ctx_outs, mods, norm_g, w_in, v_norm_g, w_s, b_s, w_out, gffn,
               rw_hi, rw_lo, rb):
    has_prev = prev is not None
    first, n_blocks = _group_blocks(group, ODD_TILES)
    mod_row = lambda t: _tile_mod_row((first + t) * ODD_TILES)
    in_specs = [pl.BlockSpec((ODD_TILES * TM, D_MODEL), lambda t: (first + t, 0))]
    args = [x]
    if has_prev:
        yg, gates = prev
        in_specs += _prev_specs(layer, first, ODD_TILES)
        args += [yg] * TOP_K + [gates, mods]
    in_specs += [
        _full_spec((1, D_MODEL)),
        _mod_spec(layer, 0, mod_row), _mod_spec(layer, 1, mod_row),
        _full_spec((D_MODEL, 2 * D_MODEL)), _full_spec((1, D_MODEL)),
        _full_spec((C_GROUPS, CHUNK, CHUNK)), _full_spec((CHUNK, D_MODEL)),
        _full_spec((D_MODEL, D_MODEL)),
        _mod_spec(layer, 2, mod_row), _full_spec((1, D_MODEL)),
        _mod_spec(layer, 3, mod_row), _mod_spec(layer, 4, mod_row),
        _full_spec((D_MODEL, LANES)), _full_spec((D_MODEL, LANES)), _full_spec((1, LANES)),
    ]
    args += [norm_g, mods, mods, w_in, v_norm_g, w_s, b_s, w_out, mods, gffn, mods, mods, rw_hi, rw_lo, rb]
    aliases = {}
    if group == 1:
        aliases = {len(args) + i: i for i in range(N_ROUTE_OUT)}
        in_specs += [pl.BlockSpec(memory_space=pl.ANY)] * N_ROUTE_OUT
        args += list(ctx_outs)
    return pl.pallas_call(
        functools.partial(_odd_kernel, group, has_prev),
        grid=(n_blocks,),
        in_specs=in_specs,
        out_specs=_route_out_specs(lambda t: first + t, ODD_TILES),
        out_shape=_ROUTE_OUT_SHAPES,
        input_output_aliases=aliases,
        compiler_params=pltpu.CompilerParams(dimension_semantics=("arbitrary",),
                                             vmem_limit_bytes=VMEM_LIMIT),
        name="odd_layer_ctx" if group == 0 else "odd_layer_lat",
    )(*args)


def _slot_kernel(ei_ref, base_ref, tri_ref, o_ref):
    expert = lax.broadcasted_iota(I32, (N_EXPERTS, TM), 0)
    krow = lax.broadcasted_iota(I32, (SUBLANES, TM), 0)
    tri = tri_ref[...]
    for i in range(SLOT_TILES):
        ei = ei_ref[SUBLANES * i:SUBLANES * (i + 1), :]
        onehots = [expert == ei[k:k + 1, :] for k in range(TOP_K)]
        member = functools.reduce(lambda u, w: u + w, [o.astype(F32) for o in onehots])
        before = _dot(member.astype(BF16), tri) + base_ref[N_EXPERTS * i:N_EXPERTS * (i + 1), :]
        slots = jnp.zeros((SUBLANES, TM), F32)
        for k in range(TOP_K):
            s = jnp.sum(jnp.where(onehots[k], before, 0.0), axis=0, keepdims=True)
            slots = slots + jnp.where(krow == k, s, 0.0)
        o_ref[i] = slots.astype(I32)


SLOT_TILES = 4


def _slots(eidx, tile_base, tri):
    return pl.pallas_call(
        _slot_kernel,
        grid=(N_TILES // SLOT_TILES,),
        in_specs=[pl.BlockSpec((SLOT_TILES * SUBLANES, TM), lambda t: (t, 0)),
                  pl.BlockSpec((SLOT_TILES * N_EXPERTS, TM), lambda t: (t, 0)),
                  _full_spec((TM, TM))],
        out_specs=pl.BlockSpec((SLOT_TILES, SUBLANES, TM), lambda t: (t, 0, 0)),
        out_shape=jax.ShapeDtypeStruct((N_TILES, SUBLANES, TM), I32),
        compiler_params=pltpu.CompilerParams(dimension_semantics=("arbitrary",)),
        name="moe_slots",
    )(eidx, tile_base, tri)


def _sc_mesh():
    return plsc.VectorSubcoreMesh(core_axis_name="c", subcore_axis_name="s")


def _sc_worker():
    return lax.axis_index("s") * SC_CORES + lax.axis_index("c")


def _dispatch(hp, idx):
    n_chunks = N_TOK // SC_WORKERS // SC_ROWS

    @functools.partial(
        pl.kernel, mesh=_sc_mesh(),
        out_type=jax.ShapeDtypeStruct((N_SLOTS, HALF), U32),
        scratch_types=[pltpu.VMEM((n_chunks, TOP_K, SC_ROWS), I32), pltpu.VMEM((SC_BUFS, SC_ROWS, HALF), U32),
                       pltpu.SemaphoreType.DMA((SC_BUFS,)), pltpu.SemaphoreType.DMA((SC_BUFS,))],
        name="moe_dispatch",
    )
    def k(x_hbm, idx_hbm, out_hbm, idx_v, rows_v, read_sem, scat_sem):
        wid = _sc_worker()
        base = wid * n_chunks
        pltpu.sync_copy(idx_hbm.at[wid], idx_v)

        def read(j, b):
            return pltpu.make_async_copy(x_hbm.at[pl.ds((base + j) * SC_ROWS, SC_ROWS)], rows_v.at[b],
                                         read_sem.at[b])

        def scatter(j, b, kk):
            return pltpu.make_async_copy(rows_v.at[b], out_hbm.at[idx_v.at[j, kk]], scat_sem.at[b])

        def drain(j, b):
            for kk in range(TOP_K):
                scatter(j, b, kk).wait()

        read(0, 0).start()

        @pl.loop(0, n_chunks, step=SC_BUFS)
        def _(j):
            for b in range(SC_BUFS):
                jj = j + b
                other = (b + 1) % SC_BUFS
                read(jj, b).wait()

                @pl.when(jj >= 1)
                def _():
                    drain(jj - 1, other)

                @pl.when(jj + 1 < n_chunks)
                def _():
                    read(jj + 1, other).start()

                for kk in range(TOP_K):
                    scatter(jj, b, kk).start()

        drain(n_chunks - 1, (n_chunks - 1) % SC_BUFS)

    return k(hp, idx)


def _combine_gather(ys, idx):
    n_chunks = idx.shape[1]

    @functools.partial(
        pl.kernel, mesh=_sc_mesh(),
        out_type=jax.ShapeDtypeStruct((SC_WORKERS * n_chunks * SC_ROWS, HALF), U32),
        scratch_types=[pltpu.VMEM((n_chunks, SC_ROWS), I32), pltpu.VMEM((SC_BUFS, SC_ROWS, HALF), U32),
                       pltpu.SemaphoreType.DMA((SC_BUFS,)), pltpu.SemaphoreType.DMA((SC_BUFS,))],
        name="moe_combine",
    )
    def k(ys_hbm, idx_hbm, out_hbm, idx_v, rows_v, gather_sem, write_sem):
        wid = _sc_worker()
        base = wid * n_chunks
        pltpu.sync_copy(idx_hbm.at[wid], idx_v)

        def gather(j, b):
            return pltpu.make_async_copy(ys_hbm.at[idx_v.at[j]], rows_v.at[b], gather_sem.at[b])

        def write(j, b):
            return pltpu.make_async_copy(rows_v.at[b], out_hbm.at[pl.ds((base + j) * SC_ROWS, SC_ROWS)],
                                         write_sem.at[b])

        gather(0, 0).start()

        @pl.loop(0, n_chunks, step=SC_BUFS)
        def _(j):
            for b in range(SC_BUFS):
                jj = j + b
                other = (b + 1) % SC_BUFS
                gather(jj, b).wait()

                @pl.when(jj >= 1)
                def _():
                    write(jj - 1, other).wait()

                @pl.when(jj + 1 < n_chunks)
                def _():
                    gather(jj + 1, other).start()

                write(jj, b).start()

        write(n_chunks - 1, (n_chunks - 1) % SC_BUFS).wait()

    return k(ys, idx)


N_MATS = 3


def _moe_kernel(layer, first_ref, nblk_ref, xs_hbm, wg_hbm, wu_hbm, wd_hbm, bg_ref, bu_ref, bd_ref,
                ys_hbm, wf32, wbf, xbuf, obuf, wsem, xsem, osem):
    e = pl.program_id(0)
    slot = e % 2
    w_hbm = (wg_hbm, wu_hbm, wd_hbm)
    nb = nblk_ref[e]
    row0 = first_ref[e] * MOE_BLOCK
    n_chunks = nb // BLOCKS_PER_CHUNK
    has_tail = nb % BLOCKS_PER_CHUNK == 1
    tail_row = row0 + n_chunks * MOE_CHUNK
    tail_slot = n_chunks % 2

    def w_copy(ee, s, m):
        return pltpu.make_async_copy(w_hbm[m].at[layer, ee], wf32.at[s, m], wsem.at[s, m])

    def x_copy(row, rows, s):
        return pltpu.make_async_copy(xs_hbm.at[pl.ds(row, rows)], xbuf.at[s, pl.ds(0, rows)], xsem.at[s])

    def o_copy(row, rows, s):
        return pltpu.make_async_copy(obuf.at[s, pl.ds(0, rows)], ys_hbm.at[pl.ds(row, rows)], osem.at[s])

    def expert(xp):
        lo, hi = _unpack_halves(xp)
        xb = jnp.concatenate([lo.astype(BF16), hi.astype(BF16)], axis=1)
        gt = jnp.minimum(_dot(xb, wbf[0]) + bg_ref[...], SWIGLU_LIMIT)
        up = jnp.clip(_dot(xb, wbf[1]) + bu_ref[...], -SWIGLU_LIMIT, SWIGLU_LIMIT)
        glu = gt * jax.nn.sigmoid(SWIGLU_ALPHA * gt)
        hmid = ((up + 1.0) * glu).astype(BF16)
        out = _dot(hmid, wbf[2]) + bd_ref[...]
        return _pack_halves(out.astype(BF16).astype(F32))

    @pl.when(e == 0)
    def _():
        for m in range(N_MATS):
            w_copy(0, 0, m).start()

    @pl.when(n_chunks > 0)
    def _():
        x_copy(row0, MOE_CHUNK, 0).start()

    @pl.when((n_chunks == 0) & has_tail)
    def _():
        x_copy(row0, MOE_BLOCK, 0).start()

    for m in range(N_MATS):
        w_copy(e, slot, m).wait()

    @pl.when(e + 1 < N_EXPERTS)
    def _():
        for m in range(N_MATS):
            w_copy(e + 1, 1 - slot, m).start()

    for m in range(N_MATS):
        wbf[m] = wf32[slot, m].astype(BF16)

    def chunk(c, carry):
        s = c % 2
        row = row0 + c * MOE_CHUNK
        x_copy(row, MOE_CHUNK, s).wait()

        @pl.when(c + 1 < n_chunks)
        def _():
            x_copy(row + MOE_CHUNK, MOE_CHUNK, 1 - s).start()

        @pl.when((c + 1 == n_chunks) & has_tail)
        def _():
            x_copy(tail_row, MOE_BLOCK, 1 - s).start()

        obuf[s] = expert(xbuf[s])

        @pl.when(c >= 1)
        def _():
            o_copy(row - MOE_CHUNK, MOE_CHUNK, 1 - s).wait()

        o_copy(row, MOE_CHUNK, s).start()
        return carry

    lax.fori_loop(0, n_chunks, chunk, 0)

    @pl.when(has_tail)
    def _():
        x_copy(tail_row, MOE_BLOCK, tail_slot).wait()
        obuf[tail_slot, 0:MOE_BLOCK, :] = expert(xbuf[tail_slot, 0:MOE_BLOCK, :])

        @pl.when(n_chunks >= 1)
        def _():
            o_copy(tail_row - MOE_CHUNK, MOE_CHUNK, 1 - tail_slot).wait()

        o_copy(tail_row, MOE_BLOCK, tail_slot).start()
        o_copy(tail_row, MOE_BLOCK, tail_slot).wait()

    @pl.when(jnp.logical_not(has_tail) & (n_chunks >= 1))
    def _():
        o_copy(tail_row - MOE_CHUNK, MOE_CHUNK, 1 - tail_slot).wait()


def _moe_experts(layer, xs, first_block, n_blocks, w_gate, b_gate, w_up, b_up, w_down, b_down):
    hbm = pl.BlockSpec(memory_space=pl.ANY)
    bspec = pl.BlockSpec((None, None, 1, D_MODEL), lambda e, fb, nb: (layer, e, 0, 0))
    bias = lambda b: b.reshape(DEPTH, N_EXPERTS, 1, D_MODEL)
    return pl.pallas_call(
        functools.partial(_moe_kernel, layer),
        grid_spec=pltpu.PrefetchScalarGridSpec(
            num_scalar_prefetch=2,
            grid=(N_EXPERTS,),
            in_specs=[hbm, hbm, hbm, hbm, bspec, bspec, bspec],
            out_specs=hbm,
            scratch_shapes=[
                pltpu.VMEM((2, N_MATS, D_MODEL, D_MODEL), F32),
                pltpu.VMEM((N_MATS, D_MODEL, D_MODEL), BF16),
                pltpu.VMEM((2, MOE_CHUNK, HALF), U32),
                pltpu.VMEM((2, MOE_CHUNK, HALF), U32),
                pltpu.SemaphoreType.DMA((2, N_MATS)),
                pltpu.SemaphoreType.DMA((2,)),
                pltpu.SemaphoreType.DMA((2,)),
            ],
        ),
        out_shape=jax.ShapeDtypeStruct((N_SLOTS, HALF), U32),
        compiler_params=pltpu.CompilerParams(dimension_semantics=("arbitrary",),
                                             vmem_limit_bytes=VMEM_LIMIT),
        name="moe_experts",
    )(first_block, n_blocks, xs, w_gate, w_up, w_down, bias(b_gate), bias(b_up), bias(b_down))


def _moe(layer, hp, eidx, cnt, tri, w_gate, b_gate, w_up, b_up, w_down, b_down):
    tile_cnt = cnt[:, 0].reshape(N_TILES, N_EXPERTS).astype(I32)
    counts = jnp.sum(tile_cnt, axis=0)
    padded = (counts + MOE_BLOCK - 1) // MOE_BLOCK * MOE_BLOCK
    pad_ends = jnp.cumsum(padded)
    pad_starts = pad_ends - padded
    tile_base = pad_starts[None, :] + jnp.cumsum(tile_cnt, axis=0) - tile_cnt
    tile_base = jnp.broadcast_to(tile_base.astype(F32).reshape(N_TILES * N_EXPERTS, 1),
                                 (N_TILES * N_EXPERTS, TM))
    first_block = (pad_starts // MOE_BLOCK).astype(I32)
    n_blocks = (padded // MOE_BLOCK).astype(I32)

    slots = _slots(eidx, tile_base, tri)[:, :TOP_K, :]
    per_tile = TM // SC_ROWS
    d_idx = slots.reshape(N_TILES, TOP_K, per_tile, SC_ROWS).transpose(0, 2, 1, 3)
    d_idx = d_idx.reshape(SC_WORKERS, N_TOK // SC_WORKERS // SC_ROWS, TOP_K, SC_ROWS)

    xs = _dispatch(hp, d_idx)
    ys = _moe_experts(layer, xs, first_block, n_blocks, w_gate, b_gate, w_up, b_up, w_down, b_down)
    gathered = []
    for tiles in (slots[:CTX_TILES], slots[CTX_TILES:]):
        n_tok = tiles.shape[0] * TM
        c_idx = tiles.transpose(1, 0, 2).reshape(SC_WORKERS, n_tok * TOP_K // SC_WORKERS // SC_ROWS, SC_ROWS)
        gathered.append(_combine_gather(ys, c_idx).reshape(TOP_K, n_tok, HALF))
    return gathered


FINAL_TILES = 2


def _final_kernel(x_ref, *refs):
    o_ref = refs[N_PREV]
    o_ref[...] = _apply_prev(x_ref[...], refs[:N_PREV])


def _final_residual(group, x, prev, mods):
    rows = FINAL_TILES * TM
    first, n_blocks = _group_blocks(group, FINAL_TILES)
    yg, gates = prev
    return pl.pallas_call(
        _final_kernel,
        grid=(n_blocks,),
        in_specs=[pl.BlockSpec((rows, D_MODEL), lambda t: (first + t, 0))]
        + _prev_specs(DEPTH, first, FINAL_TILES),
        out_specs=pl.BlockSpec((rows, D_MODEL), lambda t: (t, 0)),
        out_shape=jax.ShapeDtypeStruct((n_blocks * rows, D_MODEL), F32),
        compiler_params=pltpu.CompilerParams(dimension_semantics=("arbitrary",)),
        name="final_residual",
    )(x, *([yg] * TOP_K), gates, mods)


def _dft_pair(n):
    idx = np.arange(n)
    ang = 2.0 * np.pi * ((idx[:, None] * idx[None, :]) % n) / n
    return np.cos(ang) / np.sqrt(n), np.sin(ang) / np.sqrt(n)


def _rope_tables():
    lane = np.arange(GROUP)
    within = lane % QK_DIM
    axis = within // 32
    e = within % 32
    inv16 = ROPE_BASE ** (-jnp.arange(16, dtype=F32) / 16)
    pos = np.arange(LAT_SEQ)
    coord = np.where(axis[None, :] == 0, (pos // GRID_W)[:, None], (pos % GRID_W)[:, None])
    ang = jnp.asarray(coord, F32) * inv16[e % 16][None, :]
    first = jnp.asarray((e // 16) == 0)[None, :]
    cos, sin = jnp.cos(ang), jnp.sin(ang)
    return cos, jnp.where(first, -sin, 0.0), jnp.where(first, 0.0, sin)


def kernel(x_prompt, x_sample, cache_k, cache_v, c, c_ctx, ada_w, ada_b, norm_mix_g, norm_ffn_g,
           w_in_even, w_out_even, q_norm_g, k_norm_g, lambda_q, lambda_k, subln_g,
           w_in_odd, v_norm_g, w_spatial, b_spatial, w_out_odd,
           router_w, router_b, w_gate, b_gate, w_up, b_up, w_down, b_down):
    n_even = w_in_even.shape[0]
    x_groups = (x_prompt.reshape(N_CTX, D_MODEL), x_sample.reshape(N_LAT, D_MODEL))
    x = None
    cond = jnp.zeros((COND_ROWS, D_MODEL), F32).at[0].set(c_ctx).at[1:1 + LAT_BATCH].set(c)
    mods = _ada_modulation(cond, ada_w, ada_b).reshape(DEPTH, N_MOD, COND_ROWS, 1, D_MODEL)

    dft_ctx = [jnp.asarray(m, F32).astype(BF16) for m in _dft_pair(CTX_SEQ)]
    dft_lat = [jnp.asarray(m, F32).astype(BF16) for m in _dft_pair(LAT_SEQ)]
    c128, s128 = _dft_pair(GROUP)
    cs128 = jnp.asarray(np.concatenate([c128, -s128], axis=0), F32).astype(BF16)
    grp = np.arange(HALF) // QK_DIM
    blockdiag = jnp.asarray(grp[:, None] == grp[None, :], F32).astype(BF16)
    tri = jnp.asarray(np.arange(TM)[:, None] < np.arange(TM)[None, :], F32).astype(BF16)
    rope_tabs = _rope_tables()
    cache_k2 = cache_k.reshape(LAT_BATCH, n_even, PAST_LEN, HALF)
    cache_v2 = cache_v.reshape(LAT_BATCH, n_even, PAST_LEN, HALF)

    rw = jnp.pad(router_w, ((0, 0), (0, 0), (0, LANES - N_EXPERTS)))
    rw_hi = rw.astype(BF16)
    rw_lo = (rw - rw_hi.astype(F32)).astype(BF16)
    rb = jnp.pad(router_b, ((0, 0), (0, LANES - N_EXPERTS)))[:, None, :]

    prev = None
    caches = None
    for l in range(DEPTH):
        j = l // 2
        gmix = norm_mix_g[l][None, :]
        gffn = norm_ffn_g[l][None, :]
        if l % 2 == 0:
            proj_args = (mods, gmix, w_in_even[j].astype(BF16), blockdiag,
                         jnp.tile(q_norm_g[j], HALF // QK_DIM)[None, :],
                         jnp.tile(k_norm_g[j], HALF // QK_DIM)[None, :], rope_tabs)
            outs = None
            for g in range(2):
                x_in = x_groups[g] if l == 0 else x
                outs_g = _even_proj(l, g, x_in, l == 0, None if prev is None else (prev[0][g], prev[1]),
                                    outs, caches, *proj_args)
                if g == 0:
                    caches = outs_g[N_PROJ_OUT:]
                outs = outs_g[:N_PROJ_OUT]
            x, a, q, k, v = outs
            common = (cs128, lambda_q[j], lambda_k[j], subln_g[j][None, :], w_out_even[j].astype(BF16),
                      mods, gffn, rw_hi[l], rw_lo[l], rb[l])
            x, hp, gates, eidx, cnt = _even_mix(l, x, a, q, k, v, cache_k2, cache_v2, dft_ctx, dft_lat, *common)
        else:
            b_s = jnp.broadcast_to(jnp.transpose(b_spatial[j])[:, :, None],
                                   (CHUNK, C_GROUPS, GROUP)).reshape(CHUNK, D_MODEL)
            odd_args = (mods, gmix, w_in_odd[j].astype(BF16), v_norm_g[j][None, :],
                        w_spatial[j].astype(BF16), b_s, w_out_odd[j].astype(BF16), gffn, rw_hi[l], rw_lo[l], rb[l])
            outs = None
            for g in range(2):
                outs = _odd_layer(l, g, x, None if prev is None else (prev[0][g], prev[1]), outs, *odd_args)
            x, hp, gates, eidx, cnt = outs
        yg = _moe(l, hp, eidx, cnt, tri, w_gate, b_gate, w_up, b_up, w_down, b_down)
        prev = (yg, gates)
    y_ctx = _final_residual(0, x, (prev[0][0], prev[1]), mods)
    y_lat = _final_residual(1, x, (prev[0][1], prev[1]), mods)
    return (y_ctx.reshape(CTX_BATCH, CTX_SEQ, D_MODEL),
            y_lat.reshape(LAT_BATCH, LAT_SEQ, D_MODEL),
            caches[0].reshape(CTX_BATCH, N_EVEN, CTX_SEQ, N_GROUPS, 2, QK_DIM),
            caches[1].reshape(CTX_BATCH, N_EVEN, CTX_SEQ, N_GROUPS, GROUP))
```

```python
import functools
import math

import numpy as np
import jax
import jax.numpy as jnp
from jax import lax
from jax.experimental import pallas as pl
from jax.experimental.pallas import tpu as pltpu
from jax.experimental.pallas import tpu_sc as plsc

F32 = jnp.float32
BF16 = jnp.bfloat16
U32 = jnp.uint32
I32 = jnp.int32

D_MODEL = 1024
DEPTH = 4
N_EVEN = (DEPTH + 1) // 2
N_MOD = 6
EPS = 1e-6
CTX_BATCH, CTX_SEQ = 32, 256
LAT_BATCH, LAT_SEQ = 8, 1024
PAST_LEN = 512
GRID_W = 64
N_CTX = CTX_BATCH * CTX_SEQ
N_LAT = LAT_BATCH * LAT_SEQ
N_TOK = N_CTX + N_LAT
TM = 256
N_TILES = N_TOK // TM
CTX_TILES = N_CTX // TM
LAT_TILES_PER_BATCH = LAT_SEQ // TM
COND_ROWS = 16
GROUP = 128
N_GROUPS = 4
QK_DIM = 64
HALF = 512
CHUNK = 128
C_GROUPS = 8
N_EXPERTS = 32
TOP_K = 4
LANES = 128
SUBLANES = 8
SWIGLU_LIMIT = 7.0
SWIGLU_ALPHA = 1.702
MOE_BLOCK = 256
BLOCKS_PER_CHUNK = 2
MOE_CHUNK = BLOCKS_PER_CHUNK * MOE_BLOCK
N_ASSIGN = N_TOK * TOP_K
MOE_BLOCKS = N_ASSIGN // MOE_BLOCK + N_EXPERTS
N_SLOTS = MOE_BLOCKS * MOE_BLOCK
ROPE_BASE = 10000.0
VMEM_LIMIT = 52 * 1024 * 1024
SC_CORES = 2
SC_SUBCORES = 16
SC_WORKERS = SC_CORES * SC_SUBCORES
SC_ROWS = 64
SC_BUFS = 2
HI_MASK = 0xFFFF0000
LOG2E = math.log2(math.e)


def _dot(a, b):
    return jnp.dot(a, b, preferred_element_type=F32)


def _rms(x, g):
    return x * lax.rsqrt(jnp.mean(x * x, axis=-1, keepdims=True) + EPS) * g


def _split_bf16(x):
    hi = x.astype(BF16)
    lo = (x - hi.astype(F32)).astype(BF16)
    return hi, lo


def _pack_halves(xf):
    b = lax.bitcast_convert_type(xf, U32)
    return (b[:, :HALF] >> 16) | (b[:, HALF:] & jnp.uint32(HI_MASK))


def _unpack_halves(w):
    lo = lax.bitcast_convert_type(w << 16, F32)
    hi = lax.bitcast_convert_type(w & jnp.uint32(HI_MASK), F32)
    return lo, hi


def _tile_mod_row(t):
    return jnp.where(t < CTX_TILES, 0, 1 + (t - CTX_TILES) // LAT_TILES_PER_BATCH)


def _mod_spec(layer, k, row_fn):
    return pl.BlockSpec((None, None, None, 1, D_MODEL),
                        lambda *g: (layer, k, row_fn(*g), 0, 0))


def _full_spec(shape):
    return pl.BlockSpec(shape, lambda *g: (0,) * len(shape))


def _ada_kernel(cond_ref, w_ref, b_ref, o_ref):
    c = cond_ref[...]
    s = (c * jax.nn.sigmoid(c)).astype(BF16)
    o_ref[...] = _dot(s, w_ref[...].astype(BF16)) + b_ref[...]


def _ada_modulation(cond, ada_w, ada_b):
    return pl.pallas_call(
        _ada_kernel,
        grid=(DEPTH, N_MOD),
        in_specs=[
            _full_spec((COND_ROWS, D_MODEL)),
            pl.BlockSpec((None, D_MODEL, D_MODEL), lambda l, n: (l, 0, n)),
            pl.BlockSpec((None, None, 1, D_MODEL), lambda l, n: (l, n, 0, 0)),
        ],
        out_specs=pl.BlockSpec((None, None, COND_ROWS, D_MODEL), lambda l, n: (l, n, 0, 0)),
        out_shape=jax.ShapeDtypeStruct((DEPTH, N_MOD, COND_ROWS, D_MODEL), F32),
        compiler_params=pltpu.CompilerParams(dimension_semantics=("arbitrary", "arbitrary"),
                                             vmem_limit_bytes=VMEM_LIMIT),
        name="ada_modulation",
    )(cond, ada_w, ada_b.reshape(DEPTH, N_MOD, 1, D_MODEL))


N_PREV = 6


def _group_blocks(group, tiles_per_block):
    ctx = CTX_TILES // tiles_per_block
    return (0, ctx) if group == 0 else (ctx, N_TILES // tiles_per_block - ctx)


def _prev_specs(layer, first_block, tiles_per_block):
    rows = tiles_per_block * TM
    ys = [pl.BlockSpec((None, rows, HALF), functools.partial(lambda k, t: (k, t, 0), k))
          for k in range(TOP_K)]
    return ys + [pl.BlockSpec((rows, LANES), lambda t: (first_block + t, 0)),
                 _mod_spec(layer - 1, 5, lambda t: _tile_mod_row((first_block + t) * tiles_per_block))]


def _apply_prev(x, prev_refs):
    y_refs, gt_ref, gate_ref = prev_refs[:TOP_K], prev_refs[TOP_K], prev_refs[TOP_K + 1]
    gt = gt_ref[...]
    acc_lo = acc_hi = None
    for k in range(TOP_K):
        lo, hi = _unpack_halves(y_refs[k][...])
        g = gt[:, k:k + 1]
        acc_lo = g * lo if acc_lo is None else acc_lo + g * lo
        acc_hi = g * hi if acc_hi is None else acc_hi + g * hi
    return x + gate_ref[...] * jnp.concatenate([acc_lo, acc_hi], axis=1)


N_ROUTE_OUT = 5


def _route_out_specs(tile_fn, tiles_per_block=1):
    tile = lambda w: pl.BlockSpec((tiles_per_block * TM, w), lambda *g: (tile_fn(*g), 0))
    return [tile(D_MODEL), tile(HALF), tile(LANES),
            pl.BlockSpec((tiles_per_block * SUBLANES, TM), lambda *g: (tile_fn(*g), 0)),
            pl.BlockSpec((tiles_per_block * N_EXPERTS, LANES), lambda *g: (tile_fn(*g), 0))]


_ROUTE_OUT_SHAPES = [
    jax.ShapeDtypeStruct((N_TOK, D_MODEL), F32),
    jax.ShapeDtypeStruct((N_TOK, HALF), U32),
    jax.ShapeDtypeStruct((N_TOK, LANES), F32),
    jax.ShapeDtypeStruct((N_TILES * SUBLANES, TM), I32),
    jax.ShapeDtypeStruct((N_TILES * N_EXPERTS, LANES), F32),
]


def _post_mixer(x, mix, gate_ref, gffn_ref, shift_ref, scale_ref, rwhi_ref, rwlo_ref, rb_ref,
                xo_ref, hp_ref, gt_ref, ei_ref, cnt_ref):
    xn = x + gate_ref[...] * mix
    xo_ref[...] = xn
    h2 = _rms(xn, gffn_ref[...]) * (1.0 + scale_ref[...]) + shift_ref[...]
    hi = h2.astype(BF16)
    hif = hi.astype(F32)
    lo = (h2 - hif).astype(BF16)
    hp_ref[...] = _pack_halves(hif)
    rwhi = rwhi_ref[...]
    logits = _dot(hi, rwhi) + _dot(lo, rwhi) + _dot(hi, rwlo_ref[...]) + rb_ref[...]

    rows = x.shape[0]
    work = logits.T[0:N_EXPERTS, :]
    expert = lax.broadcasted_iota(I32, (N_EXPERTS, rows), 0).astype(F32)
    krow = lax.broadcasted_iota(I32, (SUBLANES, rows), 0)
    member = jnp.zeros((N_EXPERTS, rows), F32)
    gates = jnp.zeros((SUBLANES, rows), F32)
    ids = jnp.zeros((SUBLANES, rows), F32)
    den = None
    top = None
    for k in range(TOP_K):
        m = jnp.max(work, axis=0, keepdims=True)
        idx = jnp.min(jnp.where(work == m, expert, float(N_EXPERTS)), axis=0, keepdims=True)
        onehot = expert == idx
        work = jnp.where(onehot, -jnp.inf, work)
        member = member + onehot.astype(F32)
        if k == 0:
            top = m
        e = jnp.exp(m - top)
        den = e if den is None else den + e
        gates = gates + jnp.where(krow == k, e, 0.0)
        ids = ids + jnp.where(krow == k, idx, 0.0)
    gates = gates * (1.0 / den)
    gt_ref[...] = jnp.concatenate([gates, jnp.zeros((LANES - SUBLANES, rows), F32)], axis=0).T
    ids = ids.astype(I32)
    for i in range(rows // TM):
        ei_ref[SUBLANES * i:SUBLANES * (i + 1), :] = ids[:, TM * i:TM * (i + 1)]
        cnt_ref[N_EXPERTS * i:N_EXPERTS * (i + 1), :] = jnp.broadcast_to(
            jnp.sum(member[:, TM * i:TM * (i + 1)], axis=1, keepdims=True), (N_EXPERTS, LANES))


def _rope(x, c, sa, sb):
    outs = []
    for h in range(N_GROUPS):
        xs = x[:, GROUP * h:GROUP * (h + 1)]
        outs.append(xs * c + pltpu.roll(xs, GROUP - 16, 1) * sa + pltpu.roll(xs, 16, 1) * sb)
    return jnp.concatenate(outs, axis=1)


PROJ_TILES = 2


N_PROJ_OUT = 5


def _even_proj_kernel(group, has_prev, n_aliased, *refs):
    x_ref = refs[0]
    refs = refs[1:]
    if has_prev:
        prev_refs = refs[:N_PREV]
        refs = refs[N_PREV:]
    (g_ref, shift_ref, scale_ref, w_ref, bd_ref, qg_ref, kg_ref, cos_ref, sa_ref, sb_ref) = refs[:10]
    outs = refs[10 + n_aliased:]
    xo_ref, a_ref, q_ref, k_ref, v_ref = outs[:N_PROJ_OUT]

    x = x_ref[...]
    if has_prev:
        x = _apply_prev(x, prev_refs)
    xo_ref[...] = x
    h = (_rms(x, g_ref[...]) * (1.0 + scale_ref[...]) + shift_ref[...]).astype(BF16)

    a_ref[...] = _dot(h, w_ref[:, 0:HALF]).astype(BF16)
    zv = _dot(h, w_ref[:, 3 * HALF:4 * HALF])
    v_ref[...] = zv.astype(BF16)

    bd = bd_ref[...]

    def qk_norm(z, gain):
        shi, slo = _split_bf16(z * z)
        ssq = _dot(shi, bd) + _dot(slo, bd)
        return z * lax.rsqrt(ssq * (1.0 / QK_DIM) + EPS) * gain

    qn = qk_norm(_dot(h, w_ref[:, HALF:2 * HALF]), qg_ref[...]) * (QK_DIM ** -0.5 * LOG2E)
    kn = qk_norm(_dot(h, w_ref[:, 2 * HALF:3 * HALF]), kg_ref[...])

    if group == 0:
        kf_ref, vf_ref = outs[N_PROJ_OUT:]
        kf_ref[...] = kn.reshape(kf_ref.shape)
        vf_ref[...] = zv.reshape(vf_ref.shape)
        q_ref[...] = qn.astype(BF16)
        k_ref[...] = kn.astype(BF16)
    else:
        c, sa, sb = cos_ref[...], sa_ref[...], sb_ref[...]
        q_ref[...] = _rope(qn, c, sa, sb).astype(BF16)
        k_ref[...] = _rope(kn, c, sa, sb).astype(BF16)


def _even_proj(layer, group, x, x_is_group_local, prev, ctx_outs, caches, mods, norm_g, w_in, blockdiag,
               qg, kg, rope_tabs):
    has_prev = prev is not None
    rows = PROJ_TILES * TM
    first, n_blocks = _group_blocks(group, PROJ_TILES)
    mod_row = lambda t: _tile_mod_row((first + t) * PROJ_TILES)
    glob = lambda w: pl.BlockSpec((rows, w), lambda t: (first + t, 0))
    local = lambda w: pl.BlockSpec((rows, w), lambda t: (t, 0))
    rope_spec = pl.BlockSpec((rows, GROUP), lambda t: (t % (LAT_SEQ // rows) if group == 1 else 0, 0))
    in_specs = [local(D_MODEL) if x_is_group_local else glob(D_MODEL)]
    args = [x]
    if has_prev:
        yg, gates = prev
        in_specs += _prev_specs(layer, first, PROJ_TILES)
        args += [yg] * TOP_K + [gates, mods]
    in_specs += [
        _full_spec((1, D_MODEL)),
        _mod_spec(layer, 0, mod_row), _mod_spec(layer, 1, mod_row),
        _full_spec((D_MODEL, 4 * HALF)), _full_spec((HALF, HALF)),
        _full_spec((1, HALF)), _full_spec((1, HALF)),
        rope_spec, rope_spec, rope_spec,
    ]
    args += [norm_g, mods, mods, w_in, blockdiag, qg, kg, *rope_tabs]
    out_specs = [glob(D_MODEL)] + [glob(HALF)] * 4
    out_shape = [jax.ShapeDtypeStruct((N_TOK, D_MODEL), F32)] + [jax.ShapeDtypeStruct((N_TOK, HALF), BF16)] * 4
    aliased = []
    if group == 0:
        seqs = rows // CTX_SEQ
        cache_spec = pl.BlockSpec((seqs, None, CTX_SEQ, HALF), lambda t: (t, layer // 2, 0, 0))
        out_specs += [cache_spec] * 2
        out_shape += [jax.ShapeDtypeStruct((CTX_BATCH, N_EVEN, CTX_SEQ, HALF), F32)] * 2
        if caches is not None:
            aliased = [(caches[i], N_PROJ_OUT + i) for i in range(2)]
    else:
        aliased = [(ctx_outs[i], i) for i in range(N_PROJ_OUT)]
    aliases = {len(args) + i: out_idx for i, (_, out_idx) in enumerate(aliased)}
    in_specs += [pl.BlockSpec(memory_space=pl.ANY)] * len(aliased)
    args += [arr for arr, _ in aliased]
    return pl.pallas_call(
        functools.partial(_even_proj_kernel, group, has_prev, len(aliased)),
        grid=(n_blocks,),
        in_specs=in_specs, out_specs=out_specs, out_shape=out_shape,
        input_output_aliases=aliases,
        compiler_params=pltpu.CompilerParams(dimension_semantics=("arbitrary",),
                                             vmem_limit_bytes=VMEM_LIMIT),
        name="even_proj_ctx" if group == 0 else "even_proj_lat",
    )(*args)


MIX_TILES = 2
MIX_ROWS = MIX_TILES * TM


def _even_mix_kernel(lambda_init, *refs):
    (x_ref, q_ref, ac_ref, kc_ref, vc_ref, al_ref, kl_ref, vl_ref, ck_ref, cv_ref,
     cnc_ref, snc_ref, cnl_ref, snl_ref) = refs[:14]
    shared = refs[14:]
    t = pl.program_id(0)
    everything = slice(None)

    @pl.when(t < CTX_TILES // MIX_TILES)
    def _():
        seqs = []
        for s in range(MIX_ROWS // CTX_SEQ):
            rows = slice(CTX_SEQ * s, CTX_SEQ * (s + 1))
            seqs.append((rows, ac_ref, rows, [(kc_ref, vc_ref, rows, False)], cnc_ref, snc_ref))
        _even_mix_body(lambda_init, x_ref, q_ref, seqs, *shared)

    @pl.when(t >= CTX_TILES // MIX_TILES)
    def _():
        seqs = [(everything, al_ref, everything,
                 [(kl_ref, vl_ref, everything, False), (ck_ref, cv_ref, everything, True)], cnl_ref, snl_ref)]
        _even_mix_body(lambda_init, x_ref, q_ref, seqs, *shared)


def _even_mix_body(lambda_init, x_ref, q_ref, seqs, *refs):
    (cs_ref, lq_ref, lk_ref, sg_ref, wo_ref,
     gate_ref, gffn_ref, shift_ref, scale_ref, rwhi_ref, rwlo_ref, rb_ref) = refs[:12]
    out_refs = refs[12:]
    d = jnp.sum(lq_ref[...] * lk_ref[...], axis=1, keepdims=True)
    ed = jnp.exp(d)
    lam = ed[0:1, :] - ed[1:2, :] + lambda_init
    mixed = [_mix_sequence(lambda_init, lam, q_ref, cs_ref, sg_ref, *seq) for seq in seqs]
    mix = _dot(mixed[0] if len(mixed) == 1 else jnp.concatenate(mixed, axis=0), wo_ref[...])
    _post_mixer(x_ref[...], mix, gate_ref, gffn_ref, shift_ref, scale_ref, rwhi_ref, rwlo_ref, rb_ref,
                *out_refs)


def _mix_sequence(lambda_init, lam, q_ref, cs_ref, sg_ref, q_rows, a_ref, a_rows, kv_refs, cn_ref, sn_ref):
    a = a_ref[a_rows, :]
    y1 = _dot(cn_ref[...], a).astype(BF16)
    y2 = _dot(sn_ref[...], a).astype(BF16)
    cs = cs_ref[...]
    pieces = []
    for g in range(N_GROUPS):
        sl = slice(GROUP * g, GROUP * (g + 1))
        pieces.append(_dot(jnp.concatenate([y1[:, sl], y2[:, sl]], axis=1), cs).astype(BF16))

    lane = lax.broadcasted_iota(I32, (y1.shape[0], GROUP), 1)
    nt = (((1,), (1,)), ((), ()))
    for hd in range(N_GROUPS):
        sl = slice(GROUP * hd, GROUP * (hd + 1))
        qf = q_ref[q_rows, sl].astype(F32)
        parts = []
        for k_ref, v_ref, kv_rows, is_f32 in kv_refs:
            kk, vv = k_ref[kv_rows, sl], v_ref[kv_rows, sl]
            parts.append((kk.astype(BF16), vv.astype(BF16)) if is_f32 else (kk, vv))

        def probs(qm):
            ss = [lax.dot_general(qm, kk, nt, preferred_element_type=F32) for kk, _ in parts]
            m = functools.reduce(jnp.maximum, [jnp.max(s, axis=1, keepdims=True) for s in ss])
            es = [jnp.exp2(s - m) for s in ss]
            den = functools.reduce(lambda u, w: u + w, [jnp.sum(e, axis=1, keepdims=True) for e in es])
            return es, 1.0 / den

        es0, inv0 = probs(jnp.where(lane < QK_DIM, qf, 0.0).astype(BF16))
        es1, inv1 = probs(jnp.where(lane >= QK_DIM, qf, 0.0).astype(BF16))
        o0 = functools.reduce(lambda u, w: u + w,
                              [_dot(e.astype(BF16), vv) for e, (_, vv) in zip(es0, parts)])
        o1 = functools.reduce(lambda u, w: u + w,
                              [_dot(e.astype(BF16), vv) for e, (_, vv) in zip(es1, parts)])
        o = o0 * inv0 - o1 * (lam * inv1)
        pieces.append((_rms(o, sg_ref[...]) * (1.0 - lambda_init)).astype(BF16))
    return jnp.concatenate(pieces, axis=1)


def _even_mix(layer, x, a, q, k, v, cache_k, cache_v, dft_ctx, dft_lat, cs128, lam_q, lam_k, subln_g,
              w_out, mods, gffn, rw_hi, rw_lo, rb):
    j = layer // 2
    lambda_init = 0.8 - 0.6 * math.exp(-0.3 * layer)
    ctx_blocks = CTX_TILES // MIX_TILES
    per_seq = LAT_SEQ // MIX_ROWS
    lat = lambda t: jnp.maximum(t - ctx_blocks, 0)
    mod_row = lambda t: _tile_mod_row(t * MIX_TILES)
    tile = lambda w: pl.BlockSpec((MIX_ROWS, w), lambda t: (t, 0))
    ctx_seq = pl.BlockSpec((MIX_ROWS, HALF), lambda t: (jnp.minimum(t, ctx_blocks - 1), 0))
    lat_seq = pl.BlockSpec((LAT_SEQ, HALF), lambda t: (N_CTX // LAT_SEQ + lat(t) // per_seq, 0))
    cache = pl.BlockSpec((None, None, PAST_LEN, HALF), lambda t: (lat(t) // per_seq, j, 0, 0))
    dft_lat_spec = pl.BlockSpec((MIX_ROWS, LAT_SEQ), lambda t: (lat(t) % per_seq, 0))
    in_specs = [
        tile(D_MODEL), tile(HALF), ctx_seq, ctx_seq, ctx_seq, lat_seq, lat_seq, lat_seq, cache, cache,
        _full_spec((CTX_SEQ, CTX_SEQ)), _full_spec((CTX_SEQ, CTX_SEQ)), dft_lat_spec, dft_lat_spec,
        _full_spec((2 * GROUP, GROUP)),
        _full_spec((2, QK_DIM)), _full_spec((2, QK_DIM)), _full_spec((1, GROUP)),
        _full_spec((D_MODEL, D_MODEL)),
        _mod_spec(layer, 2, mod_row), _full_spec((1, D_MODEL)),
        _mod_spec(layer, 3, mod_row), _mod_spec(layer, 4, mod_row),
        _full_spec((D_MODEL, LANES)), _full_spec((D_MODEL, LANES)), _full_spec((1, LANES)),
    ]
    args = [x, q, a, k, v, a, k, v, cache_k, cache_v, dft_ctx[0], dft_ctx[1], dft_lat[0], dft_lat[1],
            cs128, lam_q, lam_k, subln_g, w_out, mods, gffn, mods, mods, rw_hi, rw_lo, rb]
    return pl.pallas_call(
        functools.partial(_even_mix_kernel, lambda_init),
        grid=(N_TILES // MIX_TILES,),
        in_specs=in_specs,
        out_specs=_route_out_specs(lambda t: t, MIX_TILES),
        out_shape=_ROUTE_OUT_SHAPES,
        compiler_params=pltpu.CompilerParams(dimension_semantics=("arbitrary",),
                                             vmem_limit_bytes=VMEM_LIMIT),
        name="even_mix",
    )(*args)


ODD_TILES = 2


def _odd_kernel(group, has_prev, *refs):
    x_ref = refs[0]
    refs = refs[1:]
    if has_prev:
        prev_refs = refs[:N_PREV]
        refs = refs[N_PREV:]
    (g_ref, shift_ref, scale_ref, w_ref, vg_ref, ws_ref, bs_ref, wo_ref,
     gate_ref, gffn_ref, shift2_ref, scale2_ref, rwhi_ref, rwlo_ref, rb_ref) = refs[:15]
    out_refs = refs[15:]
    if group == 1:
        out_refs = out_refs[N_ROUTE_OUT:]

    x = x_ref[...]
    if has_prev:
        x = _apply_prev(x, prev_refs)
    h = (_rms(x, g_ref[...]) * (1.0 + scale_ref[...]) + shift_ref[...]).astype(BF16)

    def gelu(z):
        return 0.5 * z * (1.0 + lax.erf(z * (2.0 ** -0.5)))

    u = gelu(_dot(h, w_ref[:, 0:D_MODEL]))
    vn = _rms(gelu(_dot(h, w_ref[:, D_MODEL:2 * D_MODEL])), vg_ref[...]).astype(BF16)
    bs = bs_ref[...]
    rows = []
    for c in range(x.shape[0] // CHUNK):
        cols = []
        for g in range(C_GROUPS):
            cols.append(_dot(ws_ref[g], vn[CHUNK * c:CHUNK * (c + 1), GROUP * g:GROUP * (g + 1)]))
        rows.append(jnp.concatenate(cols, axis=1) + bs)
    sv = jnp.concatenate(rows, axis=0)
    mix = _dot((u * sv).astype(BF16), wo_ref[...])
    _post_mixer(x, mix, gate_ref, gffn_ref, shift2_ref, scale2_ref, rwhi_ref, rwlo_ref, rb_ref,
                *out_refs)


def _odd_layer(layer, group, x, prev, ctx_outs, mods, norm_g, w_in, v_norm_g, w_s, b_s, w_out, gffn,
               rw_hi, rw_lo, rb):
    has_prev = prev is not None
    first, n_blocks = _group_blocks(group, ODD_TILES)
    mod_row = lambda t: _tile_mod_row((first + t) * ODD_TILES)
    in_specs = [pl.BlockSpec((ODD_TILES * TM, D_MODEL), lambda t: (first + t, 0))]
    args = [x]
    if has_prev:
        yg, gates = prev
        in_specs += _prev_specs(layer, first, ODD_TILES)
        args += [yg] * TOP_K + [gates, mods]
    in_specs += [
        _full_spec((1, D_MODEL)),
        _mod_spec(layer, 0, mod_row), _mod_spec(layer, 1, mod_row),
        _full_spec((D_MODEL, 2 * D_MODEL)), _full_spec((1, D_MODEL)),
        _full_spec((C_GROUPS, CHUNK, CHUNK)), _full_spec((CHUNK, D_MODEL)),
        _full_spec((D_MODEL, D_MODEL)),
        _mod_spec(layer, 2, mod_row), _full_spec((1, D_MODEL)),
        _mod_spec(layer, 3, mod_row), _mod_spec(layer, 4, mod_row),
        _full_spec((D_MODEL, LANES)), _full_spec((D_MODEL, LANES)), _full_spec((1, LANES)),
    ]
    args += [norm_g, mods, mods, w_in, v_norm_g, w_s, b_s, w_out, mods, gffn, mods, mods, rw_hi, rw_lo, rb]
    aliases = {}
    if group == 1:
        aliases = {len(args) + i: i for i in range(N_ROUTE_OUT)}
        in_specs += [pl.BlockSpec(memory_space=pl.ANY)] * N_ROUTE_OUT
        args += list(ctx_outs)
    return pl.pallas_call(
        functools.partial(_odd_kernel, group, has_prev),
        grid=(n_blocks,),
        in_specs=in_specs,
        out_specs=_route_out_specs(lambda t: first + t, ODD_TILES),
        out_shape=_ROUTE_OUT_SHAPES,
        input_output_aliases=aliases,
        compiler_params=pltpu.CompilerParams(dimension_semantics=("arbitrary",),
                                             vmem_limit_bytes=VMEM_LIMIT),
        name="odd_layer_ctx" if group == 0 else "odd_layer_lat",
    )(*args)


def _slot_kernel(ei_ref, base_ref, tri_ref, o_ref):
    expert = lax.broadcasted_iota(I32, (N_EXPERTS, TM), 0)
    krow = lax.broadcasted_iota(I32, (SUBLANES, TM), 0)
    tri = tri_ref[...]
    for i in range(SLOT_TILES):
        ei = ei_ref[SUBLANES * i:SUBLANES * (i + 1), :]
        onehots = [expert == ei[k:k + 1, :] for k in range(TOP_K)]
        member = functools.reduce(lambda u, w: u + w, [o.astype(F32) for o in onehots])
        before = _dot(member.astype(BF16), tri) + base_ref[N_EXPERTS * i:N_EXPERTS * (i + 1), :]
        slots = jnp.zeros((SUBLANES, TM), F32)
        for k in range(TOP_K):
            s = jnp.sum(jnp.where(onehots[k], before, 0.0), axis=0, keepdims=True)
            slots = slots + jnp.where(krow == k, s, 0.0)
        o_ref[i] = slots.astype(I32)


SLOT_TILES = 4


def _slots(eidx, tile_base, tri):
    return pl.pallas_call(
        _slot_kernel,
        grid=(N_TILES // SLOT_TILES,),
        in_specs=[pl.BlockSpec((SLOT_TILES * SUBLANES, TM), lambda t: (t, 0)),
                  pl.BlockSpec((SLOT_TILES * N_EXPERTS, TM), lambda t: (t, 0)),
                  _full_spec((TM, TM))],
        out_specs=pl.BlockSpec((SLOT_TILES, SUBLANES, TM), lambda t: (t, 0, 0)),
        out_shape=jax.ShapeDtypeStruct((N_TILES, SUBLANES, TM), I32),
        compiler_params=pltpu.CompilerParams(dimension_semantics=("arbitrary",)),
        name="moe_slots",
    )(eidx, tile_base, tri)


def _sc_mesh():
    return plsc.VectorSubcoreMesh(core_axis_name="c", subcore_axis_name="s")


def _sc_worker():
    return lax.axis_index("s") * SC_CORES + lax.axis_index("c")


def _dispatch(hp, idx):
    n_chunks = N_TOK // SC_WORKERS // SC_ROWS

    @functools.partial(
        pl.kernel, mesh=_sc_mesh(),
        out_type=jax.ShapeDtypeStruct((N_SLOTS, HALF), U32),
        scratch_types=[pltpu.VMEM((n_chunks, TOP_K, SC_ROWS), I32), pltpu.VMEM((SC_BUFS, SC_ROWS, HALF), U32),
                       pltpu.SemaphoreType.DMA((SC_BUFS,)), pltpu.SemaphoreType.DMA((SC_BUFS,))],
        name="moe_dispatch",
    )
    def k(x_hbm, idx_hbm, out_hbm, idx_v, rows_v, read_sem, scat_sem):
        wid = _sc_worker()
        base = wid * n_chunks
        pltpu.sync_copy(idx_hbm.at[wid], idx_v)

        def read(j, b):
            return pltpu.make_async_copy(x_hbm.at[pl.ds((base + j) * SC_ROWS, SC_ROWS)], rows_v.at[b],
                                         read_sem.at[b])

        def scatter(j, b, kk):
            return pltpu.make_async_copy(rows_v.at[b], out_hbm.at[idx_v.at[j, kk]], scat_sem.at[b])

        def drain(j, b):
            for kk in range(TOP_K):
                scatter(j, b, kk).wait()

        read(0, 0).start()

        @pl.loop(0, n_chunks, step=SC_BUFS)
        def _(j):
            for b in range(SC_BUFS):
                jj = j + b
                other = (b + 1) % SC_BUFS
                read(jj, b).wait()

                @pl.when(jj >= 1)
                def _():
                    drain(jj - 1, other)

                @pl.when(jj + 1 < n_chunks)
                def _():
                    read(jj + 1, other).start()

                for kk in range(TOP_K):
                    scatter(jj, b, kk).start()

        drain(n_chunks - 1, (n_chunks - 1) % SC_BUFS)

    return k(hp, idx)


def _combine_gather(ys, idx):
    n_chunks = idx.shape[1]

    @functools.partial(
        pl.kernel, mesh=_sc_mesh(),
        out_type=jax.ShapeDtypeStruct((SC_WORKERS * n_chunks * SC_ROWS, HALF), U32),
        scratch_types=[pltpu.VMEM((n_chunks, SC_ROWS), I32), pltpu.VMEM((SC_BUFS, SC_ROWS, HALF), U32),
                       pltpu.SemaphoreType.DMA((SC_BUFS,)), pltpu.SemaphoreType.DMA((SC_BUFS,))],
        name="moe_combine",
    )
    def k(ys_hbm, idx_hbm, out_hbm, idx_v, rows_v, gather_sem, write_sem):
        wid = _sc_worker()
        base = wid * n_chunks
        pltpu.sync_copy(idx_hbm.at[wid], idx_v)

        def gather(j, b):
            return pltpu.make_async_copy(ys_hbm.at[idx_v.at[j]], rows_v.at[b], gather_sem.at[b])

        def write(j, b):
            return pltpu.make_async_copy(rows_v.at[b], out_hbm.at[pl.ds((base + j) * SC_ROWS, SC_ROWS)],
                                         write_sem.at[b])

        gather(0, 0).start()

        @pl.loop(0, n_chunks, step=SC_BUFS)
        def _(j):
            for b in range(SC_BUFS):
                jj = j + b
                other = (b + 1) % SC_BUFS
                gather(jj, b).wait()

                @pl.when(jj >= 1)
                def _():
                    write(jj - 1, other).wait()

                @pl.when(jj + 1 < n_chunks)
                def _():
                    gather(jj + 1, other).start()

                write(jj, b).start()

        write(n_chunks - 1, (n_chunks - 1) % SC_BUFS).wait()

    return k(ys, idx)


N_MATS = 3


def _moe_kernel(layer, first_ref, nblk_ref, xs_hbm, wg_hbm, wu_hbm, wd_hbm, bg_ref, bu_ref, bd_ref,
                ys_hbm, wf32, wbf, xbuf, obuf, wsem, xsem, osem):
    e = pl.program_id(0)
    slot = e % 2
    w_hbm = (wg_hbm, wu_hbm, wd_hbm)
    nb = nblk_ref[e]
    row0 = first_ref[e] * MOE_BLOCK
    n_chunks = nb // BLOCKS_PER_CHUNK
    has_tail = nb % BLOCKS_PER_CHUNK == 1
    tail_row = row0 + n_chunks * MOE_CHUNK
    tail_slot = n_chunks % 2

    def w_copy(ee, s, m):
        return pltpu.make_async_copy(w_hbm[m].at[layer, ee], wf32.at[s, m], wsem.at[s, m])

    def x_copy(row, rows, s):
        return pltpu.make_async_copy(xs_hbm.at[pl.ds(row, rows)], xbuf.at[s, pl.ds(0, rows)], xsem.at[s])

    def o_copy(row, rows, s):
        return pltpu.make_async_copy(obuf.at[s, pl.ds(0, rows)], ys_hbm.at[pl.ds(row, rows)], osem.at[s])

    def expert(xp):
        lo, hi = _unpack_halves(xp)
        xb = jnp.concatenate([lo.astype(BF16), hi.astype(BF16)], axis=1)
        gt = jnp.minimum(_dot(xb, wbf[0]) + bg_ref[...], SWIGLU_LIMIT)
        up = jnp.clip(_dot(xb, wbf[1]) + bu_ref[...], -SWIGLU_LIMIT, SWIGLU_LIMIT)
        glu = gt * jax.nn.sigmoid(SWIGLU_ALPHA * gt)
        hmid = ((up + 1.0) * glu).astype(BF16)
        out = _dot(hmid, wbf[2]) + bd_ref[...]
        return _pack_halves(out.astype(BF16).astype(F32))

    @pl.when(e == 0)
    def _():
        for m in range(N_MATS):
            w_copy(0, 0, m).start()

    @pl.when(n_chunks > 0)
    def _():
        x_copy(row0, MOE_CHUNK, 0).start()

    @pl.when((n_chunks == 0) & has_tail)
    def _():
        x_copy(row0, MOE_BLOCK, 0).start()

    for m in range(N_MATS):
        w_copy(e, slot, m).wait()

    @pl.when(e + 1 < N_EXPERTS)
    def _():
        for m in range(N_MATS):
            w_copy(e + 1, 1 - slot, m).start()

    for m in range(N_MATS):
        wbf[m] = wf32[slot, m].astype(BF16)

    def chunk(c, carry):
        s = c % 2
        row = row0 + c * MOE_CHUNK
        x_copy(row, MOE_CHUNK, s).wait()

        @pl.when(c + 1 < n_chunks)
        def _():
            x_copy(row + MOE_CHUNK, MOE_CHUNK, 1 - s).start()

        @pl.when((c + 1 == n_chunks) & has_tail)
        def _():
            x_copy(tail_row, MOE_BLOCK, 1 - s).start()

        obuf[s] = expert(xbuf[s])

        @pl.when(c >= 1)
        def _():
            o_copy(row - MOE_CHUNK, MOE_CHUNK, 1 - s).wait()

        o_copy(row, MOE_CHUNK, s).start()
        return carry

    lax.fori_loop(0, n_chunks, chunk, 0)

    @pl.when(has_tail)
    def _():
        x_copy(tail_row, MOE_BLOCK, tail_slot).wait()
        obuf[tail_slot, 0:MOE_BLOCK, :] = expert(xbuf[tail_slot, 0:MOE_BLOCK, :])

        @pl.when(n_chunks >= 1)
        def _():
            o_copy(tail_row - MOE_CHUNK, MOE_CHUNK, 1 - tail_slot).wait()

        o_copy(tail_row, MOE_BLOCK, tail_slot).start()
        o_copy(tail_row, MOE_BLOCK, tail_slot).wait()

    @pl.when(jnp.logical_not(has_tail) & (n_chunks >= 1))
    def _():
        o_copy(tail_row - MOE_CHUNK, MOE_CHUNK, 1 - tail_slot).wait()


def _moe_experts(layer, xs, first_block, n_blocks, w_gate, b_gate, w_up, b_up, w_down, b_down):
    hbm = pl.BlockSpec(memory_space=pl.ANY)
    bspec = pl.BlockSpec((None, None, 1, D_MODEL), lambda e, fb, nb: (layer, e, 0, 0))
    bias = lambda b: b.reshape(DEPTH, N_EXPERTS, 1, D_MODEL)
    return pl.pallas_call(
        functools.partial(_moe_kernel, layer),
        grid_spec=pltpu.PrefetchScalarGridSpec(
            num_scalar_prefetch=2,
            grid=(N_EXPERTS,),
            in_specs=[hbm, hbm, hbm, hbm, bspec, bspec, bspec],
            out_specs=hbm,
            scratch_shapes=[
                pltpu.VMEM((2, N_MATS, D_MODEL, D_MODEL), F32),
                pltpu.VMEM((N_MATS, D_MODEL, D_MODEL), BF16),
                pltpu.VMEM((2, MOE_CHUNK, HALF), U32),
                pltpu.VMEM((2, MOE_CHUNK, HALF), U32),
                pltpu.SemaphoreType.DMA((2, N_MATS)),
                pltpu.SemaphoreType.DMA((2,)),
                pltpu.SemaphoreType.DMA((2,)),
            ],
        ),
        out_shape=jax.ShapeDtypeStruct((N_SLOTS, HALF), U32),
        compiler_params=pltpu.CompilerParams(dimension_semantics=("arbitrary",),
                                             vmem_limit_bytes=VMEM_LIMIT),
        name="moe_experts",
    )(first_block, n_blocks, xs, w_gate, w_up, w_down, bias(b_gate), bias(b_up), bias(b_down))


def _moe(layer, hp, eidx, cnt, tri, w_gate, b_gate, w_up, b_up, w_down, b_down):
    tile_cnt = cnt[:, 0].reshape(N_TILES, N_EXPERTS).astype(I32)
    counts = jnp.sum(tile_cnt, axis=0)
    padded = (counts + MOE_BLOCK - 1) // MOE_BLOCK * MOE_BLOCK
    pad_ends = jnp.cumsum(padded)
    pad_starts = pad_ends - padded
    tile_base = pad_starts[None, :] + jnp.cumsum(tile_cnt, axis=0) - tile_cnt
    tile_base = jnp.broadcast_to(tile_base.astype(F32).reshape(N_TILES * N_EXPERTS, 1),
                                 (N_TILES * N_EXPERTS, TM))
    first_block = (pad_starts // MOE_BLOCK).astype(I32)
    n_blocks = (padded // MOE_BLOCK).astype(I32)

    slots = _slots(eidx, tile_base, tri)[:, :TOP_K, :]
    per_tile = TM // SC_ROWS
    d_idx = slots.reshape(N_TILES, TOP_K, per_tile, SC_ROWS).transpose(0, 2, 1, 3)
    d_idx = d_idx.reshape(SC_WORKERS, N_TOK // SC_WORKERS // SC_ROWS, TOP_K, SC_ROWS)

    xs = _dispatch(hp, d_idx)
    ys = _moe_experts(layer, xs, first_block, n_blocks, w_gate, b_gate, w_up, b_up, w_down, b_down)
    gathered = []
    for tiles in (slots[:CTX_TILES], slots[CTX_TILES:]):
        n_tok = tiles.shape[0] * TM
        c_idx = tiles.transpose(1, 0, 2).reshape(SC_WORKERS, n_tok * TOP_K // SC_WORKERS // SC_ROWS, SC_ROWS)
        gathered.append(_combine_gather(ys, c_idx).reshape(TOP_K, n_tok, HALF))
    return gathered


FINAL_TILES = 2


def _final_kernel(x_ref, *refs):
    o_ref = refs[N_PREV]
    o_ref[...] = _apply_prev(x_ref[...], refs[:N_PREV])


def _final_residual(group, x, prev, mods):
    rows = FINAL_TILES * TM
    first, n_blocks = _group_blocks(group, FINAL_TILES)
    yg, gates = prev
    return pl.pallas_call(
        _final_kernel,
        grid=(n_blocks,),
        in_specs=[pl.BlockSpec((rows, D_MODEL), lambda t: (first + t, 0))]
        + _prev_specs(DEPTH, first, FINAL_TILES),
        out_specs=pl.BlockSpec((rows, D_MODEL), lambda t: (t, 0)),
        out_shape=jax.ShapeDtypeStruct((n_blocks * rows, D_MODEL), F32),
        compiler_params=pltpu.CompilerParams(dimension_semantics=("arbitrary",)),
        name="final_residual",
    )(x, *([yg] * TOP_K), gates, mods)


def _dft_pair(n):
    idx = np.arange(n)
    ang = 2.0 * np.pi * ((idx[:, None] * idx[None, :]) % n) / n
    return np.cos(ang) / np.sqrt(n), np.sin(ang) / np.sqrt(n)


def _rope_tables():
    lane = np.arange(GROUP)
    within = lane % QK_DIM
    axis = within // 32
    e = within % 32
    inv16 = ROPE_BASE ** (-jnp.arange(16, dtype=F32) / 16)
    pos = np.arange(LAT_SEQ)
    coord = np.where(axis[None, :] == 0, (pos // GRID_W)[:, None], (pos % GRID_W)[:, None])
    ang = jnp.asarray(coord, F32) * inv16[e % 16][None, :]
    first = jnp.asarray((e // 16) == 0)[None, :]
    cos, sin = jnp.cos(ang), jnp.sin(ang)
    return cos, jnp.where(first, -sin, 0.0), jnp.where(first, 0.0, sin)


def kernel(x_prompt, x_sample, cache_k, cache_v, c, c_ctx, ada_w, ada_b, norm_mix_g, norm_ffn_g,
           w_in_even, w_out_even, q_norm_g, k_norm_g, lambda_q, lambda_k, subln_g,
           w_in_odd, v_norm_g, w_spatial, b_spatial, w_out_odd,
           router_w, router_b, w_gate, b_gate, w_up, b_up, w_down, b_down):
    n_even = w_in_even.shape[0]
    x_groups = (x_prompt.reshape(N_CTX, D_MODEL), x_sample.reshape(N_LAT, D_MODEL))
    x = None
    cond = jnp.zeros((COND_ROWS, D_MODEL), F32).at[0].set(c_ctx).at[1:1 + LAT_BATCH].set(c)
    mods = _ada_modulation(cond, ada_w, ada_b).reshape(DEPTH, N_MOD, COND_ROWS, 1, D_MODEL)

    dft_ctx = [jnp.asarray(m, F32).astype(BF16) for m in _dft_pair(CTX_SEQ)]
    dft_lat = [jnp.asarray(m, F32).astype(BF16) for m in _dft_pair(LAT_SEQ)]
    c128, s128 = _dft_pair(GROUP)
    cs128 = jnp.asarray(np.concatenate([c128, -s128], axis=0), F32).astype(BF16)
    grp = np.arange(HALF) // QK_DIM
    blockdiag = jnp.asarray(grp[:, None] == grp[None, :], F32).astype(BF16)
    tri = jnp.asarray(np.arange(TM)[:, None] < np.arange(TM)[None, :], F32).astype(BF16)
    rope_tabs = _rope_tables()
    cache_k2 = cache_k.reshape(LAT_BATCH, n_even, PAST_LEN, HALF)
    cache_v2 = cache_v.reshape(LAT_BATCH, n_even, PAST_LEN, HALF)

    rw = jnp.pad(router_w, ((0, 0), (0, 0), (0, LANES - N_EXPERTS)))
    rw_hi = rw.astype(BF16)
    rw_lo = (rw - rw_hi.astype(F32)).astype(BF16)
    rb = jnp.pad(router_b, ((0, 0), (0, LANES - N_EXPERTS)))[:, None, :]

    prev = None
    caches = None
    for l in range(DEPTH):
        j = l // 2
        gmix = norm_mix_g[l][None, :]
        gffn = norm_ffn_g[l][None, :]
        if l % 2 == 0:
            proj_args = (mods, gmix, w_in_even[j].astype(BF16), blockdiag,
                         jnp.tile(q_norm_g[j], HALF // QK_DIM)[None, :],
                         jnp.tile(k_norm_g[j], HALF // QK_DIM)[None, :], rope_tabs)
            outs = None
            for g in range(2):
                x_in = x_groups[g] if l == 0 else x
                outs_g = _even_proj(l, g, x_in, l == 0, None if prev is None else (prev[0][g], prev[1]),
                                    outs, caches, *proj_args)
                if g == 0:
                    caches = outs_g[N_PROJ_OUT:]
                outs = outs_g[:N_PROJ_OUT]
            x, a, q, k, v = outs
            common = (cs128, lambda_q[j], lambda_k[j], subln_g[j][None, :], w_out_even[j].astype(BF16),
                      mods, gffn, rw_hi[l], rw_lo[l], rb[l])
            x, hp, gates, eidx, cnt = _even_mix(l, x, a, q, k, v, cache_k2, cache_v2, dft_ctx, dft_lat, *common)
        else:
            b_s = jnp.broadcast_to(jnp.transpose(b_spatial[j])[:, :, None],
                                   (CHUNK, C_GROUPS, GROUP)).reshape(CHUNK, D_MODEL)
            odd_args = (mods, gmix, w_in_odd[j].astype(BF16), v_norm_g[j][None, :],
                        w_spatial[j].astype(BF16), b_s, w_out_odd[j].astype(BF16), gffn, rw_hi[l], rw_lo[l], rb[l])
            outs = None
            for g in range(2):
                outs = _odd_layer(l, g, x, None if prev is None else (prev[0][g], prev[1]), outs, *odd_args)
            x, hp, gates, eidx, cnt = outs
        yg = _moe(l, hp, eidx, cnt, tri, w_gate, b_gate, w_up, b_up, w_down, b_down)
        prev = (yg, gates)
    y_ctx = _final_residual(0, x, (prev[0][0], prev[1]), mods)
    y_lat = _final_residual(1, x, (prev[0][1], prev[1]), mods)
    return (y_ctx.reshape(CTX_BATCH, CTX_SEQ, D_MODEL),
            y_lat.reshape(LAT_BATCH, LAT_SEQ, D_MODEL),
            caches[0].reshape(CTX_BATCH, N_EVEN, CTX_SEQ, N_GROUPS, 2, QK_DIM),
            caches[1].reshape(CTX_BATCH, N_EVEN, CTX_SEQ, N_GROUPS, GROUP))
```

```python
import functools
import math

import numpy as np
import jax
import jax.numpy as jnp
from jax import lax
from jax.experimental import pallas as pl
from jax.experimental.pallas import tpu as pltpu
from jax.experimental.pallas import tpu_sc as plsc

F32 = jnp.float32
BF16 = jnp.bfloat16
U32 = jnp.uint32
I32 = jnp.int32

D_MODEL = 1024
DEPTH = 4
N_EVEN = (DEPTH + 1) // 2
N_MOD = 6
EPS = 1e-6
CTX_BATCH, CTX_SEQ = 32, 256
LAT_BATCH, LAT_SEQ = 8, 1024
PAST_LEN = 512
GRID_W = 64
N_CTX = CTX_BATCH * CTX_SEQ
N_LAT = LAT_BATCH * LAT_SEQ
N_TOK = N_CTX + N_LAT
TM = 256
N_TILES = N_TOK // TM
CTX_TILES = N_CTX // TM
LAT_TILES_PER_BATCH = LAT_SEQ // TM
COND_ROWS = 16
GROUP = 128
N_GROUPS = 4
QK_DIM = 64
HALF = 512
CHUNK = 128
C_GROUPS = 8
N_EXPERTS = 32
TOP_K = 4
LANES = 128
SUBLANES = 8
SWIGLU_LIMIT = 7.0
SWIGLU_ALPHA = 1.702
MOE_BLOCK = 128
BLOCKS_PER_CHUNK = 4
MOE_CHUNK = BLOCKS_PER_CHUNK * MOE_BLOCK
TAIL_A = 2 * MOE_BLOCK
TAIL_B = MOE_BLOCK
N_ASSIGN = N_TOK * TOP_K
MOE_BLOCKS = N_ASSIGN // MOE_BLOCK + N_EXPERTS
N_SLOTS = MOE_BLOCKS * MOE_BLOCK
ROPE_BASE = 10000.0
VMEM_LIMIT = 52 * 1024 * 1024
SC_CORES = 2
SC_SUBCORES = 16
SC_WORKERS = SC_CORES * SC_SUBCORES
SC_ROWS = 64
SC_BUFS = 2
HI_MASK = 0xFFFF0000
LOG2E = math.log2(math.e)


def _dot(a, b):
    return jnp.dot(a, b, preferred_element_type=F32)


def _rms(x, g):
    return x * lax.rsqrt(jnp.mean(x * x, axis=-1, keepdims=True) + EPS) * g


def _split_bf16(x):
    hi = x.astype(BF16)
    lo = (x - hi.astype(F32)).astype(BF16)
    return hi, lo


def _pack_halves(xf):
    b = lax.bitcast_convert_type(xf, U32)
    return (b[:, :HALF] >> 16) | (b[:, HALF:] & jnp.uint32(HI_MASK))


def _unpack_halves(w):
    lo = lax.bitcast_convert_type(w << 16, F32)
    hi = lax.bitcast_convert_type(w & jnp.uint32(HI_MASK), F32)
    return lo, hi


def _tile_mod_row(t):
    return jnp.where(t < CTX_TILES, 0, 1 + (t - CTX_TILES) // LAT_TILES_PER_BATCH)


def _mod_spec(layer, k, row_fn):
    return pl.BlockSpec((None, None, None, 1, D_MODEL),
                        lambda *g: (layer, k, row_fn(*g), 0, 0))


def _full_spec(shape):
    return pl.BlockSpec(shape, lambda *g: (0,) * len(shape))


def _ada_kernel(cond_ref, w_ref, b_ref, o_ref):
    c = cond_ref[...]
    s = (c * jax.nn.sigmoid(c)).astype(BF16)
    o_ref[...] = _dot(s, w_ref[...].astype(BF16)) + b_ref[...]


def _ada_modulation(cond, ada_w, ada_b):
    return pl.pallas_call(
        _ada_kernel,
        grid=(DEPTH, N_MOD),
        in_specs=[
            _full_spec((COND_ROWS, D_MODEL)),
            pl.BlockSpec((None, D_MODEL, D_MODEL), lambda l, n: (l, 0, n)),
            pl.BlockSpec((None, None, 1, D_MODEL), lambda l, n: (l, n, 0, 0)),
        ],
        out_specs=pl.BlockSpec((None, None, COND_ROWS, D_MODEL), lambda l, n: (l, n, 0, 0)),
        out_shape=jax.ShapeDtypeStruct((DEPTH, N_MOD, COND_ROWS, D_MODEL), F32),
        compiler_params=pltpu.CompilerParams(dimension_semantics=("arbitrary", "arbitrary"),
                                             vmem_limit_bytes=VMEM_LIMIT),
        name="ada_modulation",
    )(cond, ada_w, ada_b.reshape(DEPTH, N_MOD, 1, D_MODEL))


N_PREV = 6


def _group_blocks(group, tiles_per_block):
    ctx = CTX_TILES // tiles_per_block
    return (0, ctx) if group == 0 else (ctx, N_TILES // tiles_per_block - ctx)


def _prev_specs(layer, first_block, tiles_per_block):
    rows = tiles_per_block * TM
    ys = [pl.BlockSpec((None, rows, HALF), functools.partial(lambda k, t: (k, t, 0), k))
          for k in range(TOP_K)]
    return ys + [pl.BlockSpec((rows, LANES), lambda t: (first_block + t, 0)),
                 _mod_spec(layer - 1, 5, lambda t: _tile_mod_row((first_block + t) * tiles_per_block))]


def _apply_prev(x, prev_refs):
    y_refs, gt_ref, gate_ref = prev_refs[:TOP_K], prev_refs[TOP_K], prev_refs[TOP_K + 1]
    gt = gt_ref[...]
    acc_lo = acc_hi = None
    for k in range(TOP_K):
        lo, hi = _unpack_halves(y_refs[k][...])
        g = gt[:, k:k + 1]
        acc_lo = g * lo if acc_lo is None else acc_lo + g * lo
        acc_hi = g * hi if acc_hi is None else acc_hi + g * hi
    return x + gate_ref[...] * jnp.concatenate([acc_lo, acc_hi], axis=1)


N_ROUTE_OUT = 5


def _route_out_specs(tile_fn, tiles_per_block=1):
    tile = lambda w: pl.BlockSpec((tiles_per_block * TM, w), lambda *g: (tile_fn(*g), 0))
    return [tile(D_MODEL), tile(HALF), tile(LANES),
            pl.BlockSpec((tiles_per_block * SUBLANES, TM), lambda *g: (tile_fn(*g), 0)),
            pl.BlockSpec((tiles_per_block * N_EXPERTS, LANES), lambda *g: (tile_fn(*g), 0))]


_ROUTE_OUT_SHAPES = [
    jax.ShapeDtypeStruct((N_TOK, D_MODEL), F32),
    jax.ShapeDtypeStruct((N_TOK, HALF), U32),
    jax.ShapeDtypeStruct((N_TOK, LANES), F32),
    jax.ShapeDtypeStruct((N_TILES * SUBLANES, TM), I32),
    jax.ShapeDtypeStruct((N_TILES * N_EXPERTS, LANES), F32),
]


def _post_mixer(x, mix, gate_ref, gffn_ref, shift_ref, scale_ref, rwhi_ref, rwlo_ref, rb_ref,
                xo_ref, hp_ref, gt_ref, ei_ref, cnt_ref):
    xn = x + gate_ref[...] * mix
    xo_ref[...] = xn
    h2 = _rms(xn, gffn_ref[...]) * (1.0 + scale_ref[...]) + shift_ref[...]
    hi = h2.astype(BF16)
    hif = hi.astype(F32)
    lo = (h2 - hif).astype(BF16)
    hp_ref[...] = _pack_halves(hif)
    rwhi = rwhi_ref[...]
    logits = _dot(hi, rwhi) + _dot(lo, rwhi) + _dot(hi, rwlo_ref[...]) + rb_ref[...]

    rows = x.shape[0]
    work = logits.T[0:N_EXPERTS, :]
    expert = lax.broadcasted_iota(I32, (N_EXPERTS, rows), 0).astype(F32)
    krow = lax.broadcasted_iota(I32, (SUBLANES, rows), 0)
    member = jnp.zeros((N_EXPERTS, rows), F32)
    gates = jnp.zeros((SUBLANES, rows), F32)
    ids = jnp.zeros((SUBLANES, rows), F32)
    den = None
    top = None
    for k in range(TOP_K):
        m = jnp.max(work, axis=0, keepdims=True)
        idx = jnp.min(jnp.where(work == m, expert, float(N_EXPERTS)), axis=0, keepdims=True)
        onehot = expert == idx
        work = jnp.where(onehot, -jnp.inf, work)
        member = member + onehot.astype(F32)
        if k == 0:
            top = m
        e = jnp.exp(m - top)
        den = e if den is None else den + e
        gates = gates + jnp.where(krow == k, e, 0.0)
        ids = ids + jnp.where(krow == k, idx, 0.0)
    gates = gates * (1.0 / den)
    gt_ref[...] = jnp.concatenate([gates, jnp.zeros((LANES - SUBLANES, rows), F32)], axis=0).T
    ids = ids.astype(I32)
    for i in range(rows // TM):
        ei_ref[SUBLANES * i:SUBLANES * (i + 1), :] = ids[:, TM * i:TM * (i + 1)]
        cnt_ref[N_EXPERTS * i:N_EXPERTS * (i + 1), :] = jnp.broadcast_to(
            jnp.sum(member[:, TM * i:TM * (i + 1)], axis=1, keepdims=True), (N_EXPERTS, LANES))


def _rope(x, c, sa, sb):
    outs = []
    for h in range(N_GROUPS):
        xs = x[:, GROUP * h:GROUP * (h + 1)]
        outs.append(xs * c + pltpu.roll(xs, GROUP - 16, 1) * sa + pltpu.roll(xs, 16, 1) * sb)
    return jnp.concatenate(outs, axis=1)


PROJ_TILES = 2


N_PROJ_OUT = 5


def _even_proj_kernel(group, has_prev, n_aliased, *refs):
    x_ref = refs[0]
    refs = refs[1:]
    if has_prev:
        prev_refs = refs[:N_PREV]
        refs = refs[N_PREV:]
    (g_ref, shift_ref, scale_ref, w_ref, bd_ref, qg_ref, kg_ref, cos_ref, sa_ref, sb_ref) = refs[:10]
    outs = refs[10 + n_aliased:]
    xo_ref, a_ref, q_ref, k_ref, v_ref = outs[:N_PROJ_OUT]

    x = x_ref[...]
    if has_prev:
        x = _apply_prev(x, prev_refs)
    xo_ref[...] = x
    h = (_rms(x, g_ref[...]) * (1.0 + scale_ref[...]) + shift_ref[...]).astype(BF16)

    a_ref[...] = _dot(h, w_ref[:, 0:HALF]).astype(BF16)
    zv = _dot(h, w_ref[:, 3 * HALF:4 * HALF])
    v_ref[...] = zv.astype(BF16)

    bd = bd_ref[...]

    def qk_norm(z, gain):
        shi, slo = _split_bf16(z * z)
        ssq = _dot(shi, bd) + _dot(slo, bd)
        return z * lax.rsqrt(ssq * (1.0 / QK_DIM) + EPS) * gain

    qn = qk_norm(_dot(h, w_ref[:, HALF:2 * HALF]), qg_ref[...]) * (QK_DIM ** -0.5 * LOG2E)
    kn = qk_norm(_dot(h, w_ref[:, 2 * HALF:3 * HALF]), kg_ref[...])

    if group == 0:
        kf_ref, vf_ref = outs[N_PROJ_OUT:]
        kf_ref[...] = kn.reshape(kf_ref.shape)
        vf_ref[...] = zv.reshape(vf_ref.shape)
        q_ref[...] = qn.astype(BF16)
        k_ref[...] = kn.astype(BF16)
    else:
        c, sa, sb = cos_ref[...], sa_ref[...], sb_ref[...]
        q_ref[...] = _rope(qn, c, sa, sb).astype(BF16)
        k_ref[...] = _rope(kn, c, sa, sb).astype(BF16)


def _even_proj(layer, group, x, x_is_group_local, prev, ctx_outs, caches, mods, norm_g, w_in, blockdiag,
               qg, kg, rope_tabs):
    has_prev = prev is not None
    rows = PROJ_TILES * TM
    first, n_blocks = _group_blocks(group, PROJ_TILES)
    mod_row = lambda t: _tile_mod_row((first + t) * PROJ_TILES)
    glob = lambda w: pl.BlockSpec((rows, w), lambda t: (first + t, 0))
    local = lambda w: pl.BlockSpec((rows, w), lambda t: (t, 0))
    rope_spec = pl.BlockSpec((rows, GROUP), lambda t: (t % (LAT_SEQ // rows) if group == 1 else 0, 0))
    in_specs = [local(D_MODEL) if x_is_group_local else glob(D_MODEL)]
    args = [x]
    if has_prev:
        yg, gates = prev
        in_specs += _prev_specs(layer, first, PROJ_TILES)
        args += [yg] * TOP_K + [gates, mods]
    in_specs += [
        _full_spec((1, D_MODEL)),
        _mod_spec(layer, 0, mod_row), _mod_spec(layer, 1, mod_row),
        _full_spec((D_MODEL, 4 * HALF)), _full_spec((HALF, HALF)),
        _full_spec((1, HALF)), _full_spec((1, HALF)),
        rope_spec, rope_spec, rope_spec,
    ]
    args += [norm_g, mods, mods, w_in, blockdiag, qg, kg, *rope_tabs]
    out_specs = [glob(D_MODEL)] + [glob(HALF)] * 4
    out_shape = [jax.ShapeDtypeStruct((N_TOK, D_MODEL), F32)] + [jax.ShapeDtypeStruct((N_TOK, HALF), BF16)] * 4
    aliased = []
    if group == 0:
        seqs = rows // CTX_SEQ
        cache_spec = pl.BlockSpec((seqs, None, CTX_SEQ, HALF), lambda t: (t, layer // 2, 0, 0))
        out_specs += [cache_spec] * 2
        out_shape += [jax.ShapeDtypeStruct((CTX_BATCH, N_EVEN, CTX_SEQ, HALF), F32)] * 2
        if caches is not None:
            aliased = [(caches[i], N_PROJ_OUT + i) for i in range(2)]
    else:
        aliased = [(ctx_outs[i], i) for i in range(N_PROJ_OUT)]
    aliases = {len(args) + i: out_idx for i, (_, out_idx) in enumerate(aliased)}
    in_specs += [pl.BlockSpec(memory_space=pl.ANY)] * len(aliased)
    args += [arr for arr, _ in aliased]
    return pl.pallas_call(
        functools.partial(_even_proj_kernel, group, has_prev, len(aliased)),
        grid=(n_blocks,),
        in_specs=in_specs, out_specs=out_specs, out_shape=out_shape,
        input_output_aliases=aliases,
        compiler_params=pltpu.CompilerParams(dimension_semantics=("arbitrary",),
                                             vmem_limit_bytes=VMEM_LIMIT),
        name="even_proj_ctx" if group == 0 else "even_proj_lat",
    )(*args)


MIX_TILES = 2
MIX_ROWS = MIX_TILES * TM


def _even_mix_kernel(lambda_init, *refs):
    (x_ref, q_ref, ac_ref, kc_ref, vc_ref, al_ref, kl_ref, vl_ref, ck_ref, cv_ref,
     cnc_ref, snc_ref, cnl_ref, snl_ref) = refs[:14]
    shared = refs[14:]
    t = pl.program_id(0)
    everything = slice(None)

    @pl.when(t < CTX_TILES // MIX_TILES)
    def _():
        seqs = []
        for s in range(MIX_ROWS // CTX_SEQ):
            rows = slice(CTX_SEQ * s, CTX_SEQ * (s + 1))
            seqs.append((rows, ac_ref, rows, [(kc_ref, vc_ref, rows, False)], cnc_ref, snc_ref))
        _even_mix_body(lambda_init, x_ref, q_ref, seqs, *shared)

    @pl.when(t >= CTX_TILES // MIX_TILES)
    def _():
        seqs = [(everything, al_ref, everything,
                 [(kl_ref, vl_ref, everything, False), (ck_ref, cv_ref, everything, True)], cnl_ref, snl_ref)]
        _even_mix_body(lambda_init, x_ref, q_ref, seqs, *shared)


def _even_mix_body(lambda_init, x_ref, q_ref, seqs, *refs):
    (cs_ref, lq_ref, lk_ref, sg_ref, wo_ref,
     gate_ref, gffn_ref, shift_ref, scale_ref, rwhi_ref, rwlo_ref, rb_ref) = refs[:12]
    out_refs = refs[12:]
    d = jnp.sum(lq_ref[...] * lk_ref[...], axis=1, keepdims=True)
    ed = jnp.exp(d)
    lam = ed[0:1, :] - ed[1:2, :] + lambda_init
    mixed = [_mix_sequence(lambda_init, lam, q_ref, cs_ref, sg_ref, *seq) for seq in seqs]
    mix = _dot(mixed[0] if len(mixed) == 1 else jnp.concatenate(mixed, axis=0), wo_ref[...])
    _post_mixer(x_ref[...], mix, gate_ref, gffn_ref, shift_ref, scale_ref, rwhi_ref, rwlo_ref, rb_ref,
                *out_refs)


def _mix_sequence(lambda_init, lam, q_ref, cs_ref, sg_ref, q_rows, a_ref, a_rows, kv_refs, cn_ref, sn_ref):
    a = a_ref[a_rows, :]
    y1 = _dot(cn_ref[...], a).astype(BF16)
    y2 = _dot(sn_ref[...], a).astype(BF16)
    cs = cs_ref[...]
    pieces = []
    for g in range(N_GROUPS):
        sl = slice(GROUP * g, GROUP * (g + 1))
        pieces.append(_dot(jnp.concatenate([y1[:, sl], y2[:, sl]], axis=1), cs).astype(BF16))

    lane = lax.broadcasted_iota(I32, (y1.shape[0], GROUP), 1)
    nt = (((1,), (1,)), ((), ()))
    for hd in range(N_GROUPS):
        sl = slice(GROUP * hd, GROUP * (hd + 1))
        qf = q_ref[q_rows, sl].astype(F32)
        parts = []
        for k_ref, v_ref, kv_rows, is_f32 in kv_refs:
            kk, vv = k_ref[kv_rows, sl], v_ref[kv_rows, sl]
            parts.append((kk.astype(BF16), vv.astype(BF16)) if is_f32 else (kk, vv))

        def probs(qm):
            ss = [lax.dot_general(qm, kk, nt, preferred_element_type=F32) for kk, _ in parts]
            m = functools.reduce(jnp.maximum, [jnp.max(s, axis=1, keepdims=True) for s in ss])
            es = [jnp.exp2(s - m) for s in ss]
            den = functools.reduce(lambda u, w: u + w, [jnp.sum(e, axis=1, keepdims=True) for e in es])
            return es, 1.0 / den

        es0, inv0 = probs(jnp.where(lane < QK_DIM, qf, 0.0).astype(BF16))
        es1, inv1 = probs(jnp.where(lane >= QK_DIM, qf, 0.0).astype(BF16))
        linv1 = lam * inv1
        o = functools.reduce(lambda u, w: u + w,
                             [_dot((e0 * inv0 - e1 * linv1).astype(BF16), vv)
                              for e0, e1, (_, vv) in zip(es0, es1, parts)])
        pieces.append((_rms(o, sg_ref[...]) * (1.0 - lambda_init)).astype(BF16))
    return jnp.concatenate(pieces, axis=1)


def _even_mix(layer, x, a, q, k, v, cache_k, cache_v, dft_ctx, dft_lat, cs128, lam_q, lam_k, subln_g,
              w_out, mods, gffn, rw_hi, rw_lo, rb):
    j = layer // 2
    lambda_init = 0.8 - 0.6 * math.exp(-0.3 * layer)
    ctx_blocks = CTX_TILES // MIX_TILES
    per_seq = LAT_SEQ // MIX_ROWS
    lat = lambda t: jnp.maximum(t - ctx_blocks, 0)
    mod_row = lambda t: _tile_mod_row(t * MIX_TILES)
    tile = lambda w: pl.BlockSpec((MIX_ROWS, w), lambda t: (t, 0))
    ctx_seq = pl.BlockSpec((MIX_ROWS, HALF), lambda t: (jnp.minimum(t, ctx_blocks - 1), 0))
    lat_seq = pl.BlockSpec((LAT_SEQ, HALF), lambda t: (N_CTX // LAT_SEQ + lat(t) // per_seq, 0))
    cache = pl.BlockSpec((None, None, PAST_LEN, HALF), lambda t: (lat(t) // per_seq, j, 0, 0))
    dft_lat_spec = pl.BlockSpec((MIX_ROWS, LAT_SEQ), lambda t: (lat(t) % per_seq, 0))
    in_specs = [
        tile(D_MODEL), tile(HALF), ctx_seq, ctx_seq, ctx_seq, lat_seq, lat_seq, lat_seq, cache, cache,
        _full_spec((CTX_SEQ, CTX_SEQ)), _full_spec((CTX_SEQ, CTX_SEQ)), dft_lat_spec, dft_lat_spec,
        _full_spec((2 * GROUP, GROUP)),
        _full_spec((2, QK_DIM)), _full_spec((2, QK_DIM)), _full_spec((1, GROUP)),
        _full_spec((D_MODEL, D_MODEL)),
        _mod_spec(layer, 2, mod_row), _full_spec((1, D_MODEL)),
        _mod_spec(layer, 3, mod_row), _mod_spec(layer, 4, mod_row),
        _full_spec((D_MODEL, LANES)), _full_spec((D_MODEL, LANES)), _full_spec((1, LANES)),
    ]
    args = [x, q, a, k, v, a, k, v, cache_k, cache_v, dft_ctx[0], dft_ctx[1], dft_lat[0], dft_lat[1],
            cs128, lam_q, lam_k, subln_g, w_out, mods, gffn, mods, mods, rw_hi, rw_lo, rb]
    return pl.pallas_call(
        functools.partial(_even_mix_kernel, lambda_init),
        grid=(N_TILES // MIX_TILES,),
        in_specs=in_specs,
        out_specs=_route_out_specs(lambda t: t, MIX_TILES),
        out_shape=_ROUTE_OUT_SHAPES,
        compiler_params=pltpu.CompilerParams(dimension_semantics=("arbitrary",),
                                             vmem_limit_bytes=VMEM_LIMIT),
        name="even_mix",
    )(*args)


ODD_TILES = 2


def _odd_kernel(group, has_prev, *refs):
    x_ref = refs[0]
    refs = refs[1:]
    if has_prev:
        prev_refs = refs[:N_PREV]
        refs = refs[N_PREV:]
    (g_ref, shift_ref, scale_ref, w_ref, vg_ref, ws_ref, bs_ref, wo_ref,
     gate_ref, gffn_ref, shift2_ref, scale2_ref, rwhi_ref, rwlo_ref, rb_ref) = refs[:15]
    out_refs = refs[15:]
    if group == 1:
        out_refs = out_refs[N_ROUTE_OUT:]

    x = x_ref[...]
    if has_prev:
        x = _apply_prev(x, prev_refs)
    h = (_rms(x, g_ref[...]) * (1.0 + scale_ref[...]) + shift_ref[...]).astype(BF16)

    def gelu(z):
        return 0.5 * z * (1.0 + lax.erf(z * (2.0 ** -0.5)))

    u = gelu(_dot(h, w_ref[:, 0:D_MODEL]))
    vn = _rms(gelu(_dot(h, w_ref[:, D_MODEL:2 * D_MODEL])), vg_ref[...]).astype(BF16)
    bs = bs_ref[...]
    rows = []
    for c in range(x.shape[0] // CHUNK):
        cols = []
        for g in range(C_GROUPS):
            cols.append(_dot(ws_ref[g], vn[CHUNK * c:CHUNK * (c + 1), GROUP * g:GROUP * (g + 1)]))
        rows.append(jnp.concatenate(cols, axis=1) + bs)
    sv = jnp.concatenate(rows, axis=0)
    mix = _dot((u * sv).astype(BF16), wo_ref[...])
    _post_mixer(x, mix, gate_ref, gffn_ref, shift2_ref, scale2_ref, rwhi_ref, rwlo_ref, rb_ref,
                *out_refs)


def _odd_layer(layer, group, x, prev, ctx_outs, mods, norm_g, w_in, v_norm_g, w_s, b_s, w_out, gffn,
               rw_hi, rw_lo, rb):
    has_prev = prev is not None
    first, n_blocks = _group_blocks(group, ODD_TILES)
    mod_row = lambda t: _tile_mod_row((first + t) * ODD_TILES)
    in_specs = [pl.BlockSpec((ODD_TILES * TM, D_MODEL), lambda t: (first + t, 0))]
    args = [x]
    if has_prev:
        yg, gates = prev
        in_specs += _prev_specs(layer, first, ODD_TILES)
        args += [yg] * TOP_K + [gates, mods]
    in_specs += [
        _full_spec((1, D_MODEL)),
        _mod_spec(layer, 0, mod_row), _mod_spec(layer, 1, mod_row),
        _full_spec((D_MODEL, 2 * D_MODEL)), _full_spec((1, D_MODEL)),
        _full_spec((C_GROUPS, CHUNK, CHUNK)), _full_spec((CHUNK, D_MODEL)),
        _full_spec((D_MODEL, D_MODEL)),
        _mod_spec(layer, 2, mod_row), _full_spec((1, D_MODEL)),
        _mod_spec(layer, 3, mod_row), _mod_spec(layer, 4, mod_row),
        _full_spec((D_MODEL, LANES)), _full_spec((D_MODEL, LANES)), _full_spec((1, LANES)),
    ]
    args += [norm_g, mods, mods, w_in, v_norm_g, w_s, b_s, w_out, mods, gffn, mods, mods, rw_hi, rw_lo, rb]
    aliases = {}
    if group == 1:
        aliases = {len(args) + i: i for i in range(N_ROUTE_OUT)}
        in_specs += [pl.BlockSpec(memory_space=pl.ANY)] * N_ROUTE_OUT
        args += list(ctx_outs)
    return pl.pallas_call(
        functools.partial(_odd_kernel, group, has_prev),
        grid=(n_blocks,),
        in_specs=in_specs,
        out_specs=_route_out_specs(lambda t: first + t, ODD_TILES),
        out_shape=_ROUTE_OUT_SHAPES,
        input_output_aliases=aliases,
        compiler_params=pltpu.CompilerParams(dimension_semantics=("arbitrary",),
                                             vmem_limit_bytes=VMEM_LIMIT),
        name="odd_layer_ctx" if group == 0 else "odd_layer_lat",
    )(*args)


def _slot_kernel(ei_ref, base_ref, tri_ref, o_ref):
    expert = lax.broadcasted_iota(I32, (N_EXPERTS, TM), 0)
    krow = lax.broadcasted_iota(I32, (SUBLANES, TM), 0)
    tri = tri_ref[...]
    for i in range(SLOT_TILES):
        ei = ei_ref[SUBLANES * i:SUBLANES * (i + 1), :]
        onehots = [expert == ei[k:k + 1, :] for k in range(TOP_K)]
        member = functools.reduce(lambda u, w: u + w, [o.astype(F32) for o in onehots])
        before = _dot(member.astype(BF16), tri) + base_ref[N_EXPERTS * i:N_EXPERTS * (i + 1), :]
        slots = jnp.zeros((SUBLANES, TM), F32)
        for k in range(TOP_K):
            s = jnp.sum(jnp.where(onehots[k], before, 0.0), axis=0, keepdims=True)
            slots = slots + jnp.where(krow == k, s, 0.0)
        o_ref[i] = slots.astype(I32)


SLOT_TILES = 4


def _slots(eidx, tile_base, tri):
    return pl.pallas_call(
        _slot_kernel,
        grid=(N_TILES // SLOT_TILES,),
        in_specs=[pl.BlockSpec((SLOT_TILES * SUBLANES, TM), lambda t: (t, 0)),
                  pl.BlockSpec((SLOT_TILES * N_EXPERTS, TM), lambda t: (t, 0)),
                  _full_spec((TM, TM))],
        out_specs=pl.BlockSpec((SLOT_TILES, SUBLANES, TM), lambda t: (t, 0, 0)),
        out_shape=jax.ShapeDtypeStruct((N_TILES, SUBLANES, TM), I32),
        compiler_params=pltpu.CompilerParams(dimension_semantics=("arbitrary",)),
        name="moe_slots",
    )(eidx, tile_base, tri)


def _sc_mesh():
    return plsc.VectorSubcoreMesh(core_axis_name="c", subcore_axis_name="s")


def _sc_worker():
    return lax.axis_index("s") * SC_CORES + lax.axis_index("c")


def _dispatch(hp, idx):
    n_chunks = N_TOK // SC_WORKERS // SC_ROWS

    @functools.partial(
        pl.kernel, mesh=_sc_mesh(),
        out_type=jax.ShapeDtypeStruct((N_SLOTS, HALF), U32),
        scratch_types=[pltpu.VMEM((n_chunks, TOP_K, SC_ROWS), I32), pltpu.VMEM((SC_BUFS, SC_ROWS, HALF), U32),
                       pltpu.SemaphoreType.DMA((SC_BUFS,)), pltpu.SemaphoreType.DMA((SC_BUFS,))],
        name="moe_dispatch",
    )
    def k(x_hbm, idx_hbm, out_hbm, idx_v, rows_v, read_sem, scat_sem):
        wid = _sc_worker()
        base = wid * n_chunks
        pltpu.sync_copy(idx_hbm.at[wid], idx_v)

        def read(j, b):
            return pltpu.make_async_copy(x_hbm.at[pl.ds((base + j) * SC_ROWS, SC_ROWS)], rows_v.at[b],
                                         read_sem.at[b])

        def scatter(j, b, kk):
            return pltpu.make_async_copy(rows_v.at[b], out_hbm.at[idx_v.at[j, kk]], scat_sem.at[b])

        def drain(j, b):
            for kk in range(TOP_K):
                scatter(j, b, kk).wait()

        read(0, 0).start()

        @pl.loop(0, n_chunks, step=SC_BUFS)
        def _(j):
            for b in range(SC_BUFS):
                jj = j + b
                other = (b + 1) % SC_BUFS
                read(jj, b).wait()

                @pl.when(jj >= 1)
                def _():
                    drain(jj - 1, other)

                @pl.when(jj + 1 < n_chunks)
                def _():
                    read(jj + 1, other).start()

                for kk in range(TOP_K):
                    scatter(jj, b, kk).start()

        drain(n_chunks - 1, (n_chunks - 1) % SC_BUFS)

    return k(hp, idx)


def _combine_gather(ys, idx):
    n_chunks = idx.shape[1]

    @functools.partial(
        pl.kernel, mesh=_sc_mesh(),
        out_type=jax.ShapeDtypeStruct((SC_WORKERS * n_chunks * SC_ROWS, HALF), U32),
        scratch_types=[pltpu.VMEM((n_chunks, SC_ROWS), I32), pltpu.VMEM((SC_BUFS, SC_ROWS, HALF), U32),
                       pltpu.SemaphoreType.DMA((SC_BUFS,)), pltpu.SemaphoreType.DMA((SC_BUFS,))],
        name="moe_combine",
    )
    def k(ys_hbm, idx_hbm, out_hbm, idx_v, rows_v, gather_sem, write_sem):
        wid = _sc_worker()
        base = wid * n_chunks
        pltpu.sync_copy(idx_hbm.at[wid], idx_v)

        def gather(j, b):
            return pltpu.make_async_copy(ys_hbm.at[idx_v.at[j]], rows_v.at[b], gather_sem.at[b])

        def write(j, b):
            return pltpu.make_async_copy(rows_v.at[b], out_hbm.at[pl.ds((base + j) * SC_ROWS, SC_ROWS)],
                                         write_sem.at[b])

        gather(0, 0).start()

        @pl.loop(0, n_chunks, step=SC_BUFS)
        def _(j):
            for b in range(SC_BUFS):
                jj = j + b
                other = (b + 1) % SC_BUFS
                gather(jj, b).wait()

                @pl.when(jj >= 1)
                def _():
                    write(jj - 1, other).wait()

                @pl.when(jj + 1 < n_chunks)
                def _():
                    gather(jj + 1, other).start()

                write(jj, b).start()

        write(n_chunks - 1, (n_chunks - 1) % SC_BUFS).wait()

    return k(ys, idx)


N_MATS = 3


def _moe_kernel(layer, first_ref, nblk_ref, xs_hbm, wg_hbm, wu_hbm, wd_hbm, bg_ref, bu_ref, bd_ref,
                ys_hbm, wf32, wbf, xbuf, obuf, wsem, xsem, osem):
    e = pl.program_id(0)
    slot = e % 2
    w_hbm = (wg_hbm, wu_hbm, wd_hbm)
    nb = nblk_ref[e]
    row0 = first_ref[e] * MOE_BLOCK
    n_chunks = nb // BLOCKS_PER_CHUNK
    has_a = nb % BLOCKS_PER_CHUNK >= 2
    has_b = nb % 2 == 1
    row_a = row0 + n_chunks * MOE_CHUNK
    row_b = row_a + jnp.where(has_a, TAIL_A, 0)
    slot_a = n_chunks % 2
    slot_b = (n_chunks + has_a.astype(I32)) % 2
    last_chunk = (row_a - MOE_CHUNK, MOE_CHUNK, 1 - slot_a)

    def w_copy(ee, s, m):
        return pltpu.make_async_copy(w_hbm[m].at[layer, ee], wf32.at[s, m], wsem.at[s, m])

    def x_copy(row, rows, s):
        return pltpu.make_async_copy(xs_hbm.at[pl.ds(row, rows)], xbuf.at[s, pl.ds(0, rows)], xsem.at[s])

    def o_copy(row, rows, s):
        return pltpu.make_async_copy(obuf.at[s, pl.ds(0, rows)], ys_hbm.at[pl.ds(row, rows)], osem.at[s])

    def expert(xp):
        lo, hi = _unpack_halves(xp)
        xb = jnp.concatenate([lo.astype(BF16), hi.astype(BF16)], axis=1)
        gt = jnp.minimum(_dot(xb, wbf[0]) + bg_ref[...], SWIGLU_LIMIT)
        up = jnp.clip(_dot(xb, wbf[1]) + bu_ref[...], -SWIGLU_LIMIT, SWIGLU_LIMIT)
        glu = gt * jax.nn.sigmoid(SWIGLU_ALPHA * gt)
        hmid = ((up + 1.0) * glu).astype(BF16)
        out = _dot(hmid, wbf[2]) + bd_ref[...]
        return _pack_halves(out.astype(BF16).astype(F32))

    @pl.when(e == 0)
    def _():
        for m in range(N_MATS):
            w_copy(0, 0, m).start()

    @pl.when(n_chunks > 0)
    def _():
        x_copy(row0, MOE_CHUNK, 0).start()

    @pl.when((n_chunks == 0) & has_a)
    def _():
        x_copy(row0, TAIL_A, 0).start()

    @pl.when((n_chunks == 0) & jnp.logical_not(has_a) & has_b)
    def _():
        x_copy(row0, TAIL_B, 0).start()

    for m in range(N_MATS):
        w_copy(e, slot, m).wait()

    @pl.when(e + 1 < N_EXPERTS)
    def _():
        for m in range(N_MATS):
            w_copy(e + 1, 1 - slot, m).start()

    for m in range(N_MATS):
        wbf[m] = wf32[slot, m].astype(BF16)

    def chunk(c, carry):
        s = c % 2
        row = row0 + c * MOE_CHUNK
        x_copy(row, MOE_CHUNK, s).wait()

        @pl.when(c + 1 < n_chunks)
        def _():
            x_copy(row + MOE_CHUNK, MOE_CHUNK, 1 - s).start()

        @pl.when((c + 1 == n_chunks) & has_a)
        def _():
            x_copy(row_a, TAIL_A, 1 - s).start()

        @pl.when((c + 1 == n_chunks) & jnp.logical_not(has_a) & has_b)
        def _():
            x_copy(row_b, TAIL_B, 1 - s).start()

        obuf[s] = expert(xbuf[s])

        @pl.when(c >= 1)
        def _():
            o_copy(row - MOE_CHUNK, MOE_CHUNK, 1 - s).wait()

        o_copy(row, MOE_CHUNK, s).start()
        return carry

    lax.fori_loop(0, n_chunks, chunk, 0)

    @pl.when(has_a)
    def _():
        x_copy(row_a, TAIL_A, slot_a).wait()

        @pl.when(has_b)
        def _():
            x_copy(row_b, TAIL_B, 1 - slot_a).start()

        obuf[slot_a, 0:TAIL_A, :] = expert(xbuf[slot_a, 0:TAIL_A, :])

        @pl.when(n_chunks >= 1)
        def _():
            o_copy(*last_chunk).wait()

        o_copy(row_a, TAIL_A, slot_a).start()

    @pl.when(has_b)
    def _():
        x_copy(row_b, TAIL_B, slot_b).wait()
        obuf[slot_b, 0:TAIL_B, :] = expert(xbuf[slot_b, 0:TAIL_B, :])

        @pl.when(has_a)
        def _():
            o_copy(row_a, TAIL_A, slot_a).wait()

        @pl.when(jnp.logical_not(has_a) & (n_chunks >= 1))
        def _():
            o_copy(*last_chunk).wait()

        o_copy(row_b, TAIL_B, slot_b).start()
        o_copy(row_b, TAIL_B, slot_b).wait()

    @pl.when(jnp.logical_not(has_b) & has_a)
    def _():
        o_copy(row_a, TAIL_A, slot_a).wait()

    @pl.when(jnp.logical_not(has_b) & jnp.logical_not(has_a) & (n_chunks >= 1))
    def _():
        o_copy(*last_chunk).wait()


def _moe_experts(layer, xs, first_block, n_blocks, w_gate, b_gate, w_up, b_up, w_down, b_down):
    hbm = pl.BlockSpec(memory_space=pl.ANY)
    bspec = pl.BlockSpec((None, None, 1, D_MODEL), lambda e, fb, nb: (layer, e, 0, 0))
    bias = lambda b: b.reshape(DEPTH, N_EXPERTS, 1, D_MODEL)
    return pl.pallas_call(
        functools.partial(_moe_kernel, layer),
        grid_spec=pltpu.PrefetchScalarGridSpec(
            num_scalar_prefetch=2,
            grid=(N_EXPERTS,),
            in_specs=[hbm, hbm, hbm, hbm, bspec, bspec, bspec],
            out_specs=hbm,
            scratch_shapes=[
                pltpu.VMEM((2, N_MATS, D_MODEL, D_MODEL), F32),
                pltpu.VMEM((N_MATS, D_MODEL, D_MODEL), BF16),
                pltpu.VMEM((2, MOE_CHUNK, HALF), U32),
                pltpu.VMEM((2, MOE_CHUNK, HALF), U32),
                pltpu.SemaphoreType.DMA((2, N_MATS)),
                pltpu.SemaphoreType.DMA((2,)),
                pltpu.SemaphoreType.DMA((2,)),
            ],
        ),
        out_shape=jax.ShapeDtypeStruct((N_SLOTS, HALF), U32),
        compiler_params=pltpu.CompilerParams(dimension_semantics=("arbitrary",),
                                             vmem_limit_bytes=VMEM_LIMIT),
        name="moe_experts",
    )(first_block, n_blocks, xs, w_gate, w_up, w_down, bias(b_gate), bias(b_up), bias(b_down))


def _moe(layer, hp, eidx, cnt, tri, w_gate, b_gate, w_up, b_up, w_down, b_down):
    tile_cnt = cnt[:, 0].reshape(N_TILES, N_EXPERTS).astype(I32)
    counts = jnp.sum(tile_cnt, axis=0)
    padded = (counts + MOE_BLOCK - 1) // MOE_BLOCK * MOE_BLOCK
    pad_ends = jnp.cumsum(padded)
    pad_starts = pad_ends - padded
    tile_base = pad_starts[None, :] + jnp.cumsum(tile_cnt, axis=0) - tile_cnt
    tile_base = jnp.broadcast_to(tile_base.astype(F32).reshape(N_TILES * N_EXPERTS, 1),
                                 (N_TILES * N_EXPERTS, TM))
    first_block = (pad_starts // MOE_BLOCK).astype(I32)
    n_blocks = (padded // MOE_BLOCK).astype(I32)

    slots = _slots(eidx, tile_base, tri)[:, :TOP_K, :]
    per_tile = TM // SC_ROWS
    d_idx = slots.reshape(N_TILES, TOP_K, per_tile, SC_ROWS).transpose(0, 2, 1, 3)
    d_idx = d_idx.reshape(SC_WORKERS, N_TOK // SC_WORKERS // SC_ROWS, TOP_K, SC_ROWS)

    xs = _dispatch(hp, d_idx)
    ys = _moe_experts(layer, xs, first_block, n_blocks, w_gate, b_gate, w_up, b_up, w_down, b_down)
    gathered = []
    for tiles in (slots[:CTX_TILES], slots[CTX_TILES:]):
        n_tok = tiles.shape[0] * TM
        c_idx = tiles.transpose(1, 0, 2).reshape(SC_WORKERS, n_tok * TOP_K // SC_WORKERS // SC_ROWS, SC_ROWS)
        gathered.append(_combine_gather(ys, c_idx).reshape(TOP_K, n_tok, HALF))
    return gathered


FINAL_TILES = 2


def _final_kernel(x_ref, *refs):
    o_ref = refs[N_PREV]
    o_ref[...] = _apply_prev(x_ref[...], refs[:N_PREV])


def _final_residual(group, x, prev, mods):
    rows = FINAL_TILES * TM
    first, n_blocks = _group_blocks(group, FINAL_TILES)
    yg, gates = prev
    return pl.pallas_call(
        _final_kernel,
        grid=(n_blocks,),
        in_specs=[pl.BlockSpec((rows, D_MODEL), lambda t: (first + t, 0))]
        + _prev_specs(DEPTH, first, FINAL_TILES),
        out_specs=pl.BlockSpec((rows, D_MODEL), lambda t: (t, 0)),
        out_shape=jax.ShapeDtypeStruct((n_blocks * rows, D_MODEL), F32),
        compiler_params=pltpu.CompilerParams(dimension_semantics=("arbitrary",)),
        name="final_residual",
    )(x, *([yg] * TOP_K), gates, mods)


def _dft_pair(n):
    idx = np.arange(n)
    ang = 2.0 * np.pi * ((idx[:, None] * idx[None, :]) % n) / n
    return np.cos(ang) / np.sqrt(n), np.sin(ang) / np.sqrt(n)


def _rope_tables():
    lane = np.arange(GROUP)
    within = lane % QK_DIM
    axis = within // 32
    e = within % 32
    inv16 = ROPE_BASE ** (-jnp.arange(16, dtype=F32) / 16)
    pos = np.arange(LAT_SEQ)
    coord = np.where(axis[None, :] == 0, (pos // GRID_W)[:, None], (pos % GRID_W)[:, None])
    ang = jnp.asarray(coord, F32) * inv16[e % 16][None, :]
    first = jnp.asarray((e // 16) == 0)[None, :]
    cos, sin = jnp.cos(ang), jnp.sin(ang)
    return cos, jnp.where(first, -sin, 0.0), jnp.where(first, 0.0, sin)


def kernel(x_prompt, x_sample, cache_k, cache_v, c, c_ctx, ada_w, ada_b, norm_mix_g, norm_ffn_g,
           w_in_even, w_out_even, q_norm_g, k_norm_g, lambda_q, lambda_k, subln_g,
           w_in_odd, v_norm_g, w_spatial, b_spatial, w_out_odd,
           router_w, router_b, w_gate, b_gate, w_up, b_up, w_down, b_down):
    n_even = w_in_even.shape[0]
    x_groups = (x_prompt.reshape(N_CTX, D_MODEL), x_sample.reshape(N_LAT, D_MODEL))
    x = None
    cond = jnp.zeros((COND_ROWS, D_MODEL), F32).at[0].set(c_ctx).at[1:1 + LAT_BATCH].set(c)
    mods = _ada_modulation(cond, ada_w, ada_b).reshape(DEPTH, N_MOD, COND_ROWS, 1, D_MODEL)

    dft_ctx = [jnp.asarray(m, F32).astype(BF16) for m in _dft_pair(CTX_SEQ)]
    dft_lat = [jnp.asarray(m, F32).astype(BF16) for m in _dft_pair(LAT_SEQ)]
    c128, s128 = _dft_pair(GROUP)
    cs128 = jnp.asarray(np.concatenate([c128, -s128], axis=0), F32).astype(BF16)
    grp = np.arange(HALF) // QK_DIM
    blockdiag = jnp.asarray(grp[:, None] == grp[None, :], F32).astype(BF16)
    tri = jnp.asarray(np.arange(TM)[:, None] < np.arange(TM)[None, :], F32).astype(BF16)
    rope_tabs = _rope_tables()
    cache_k2 = cache_k.reshape(LAT_BATCH, n_even, PAST_LEN, HALF)
    cache_v2 = cache_v.reshape(LAT_BATCH, n_even, PAST_LEN, HALF)

    rw = jnp.pad(router_w, ((0, 0), (0, 0), (0, LANES - N_EXPERTS)))
    rw_hi = rw.astype(BF16)
    rw_lo = (rw - rw_hi.astype(F32)).astype(BF16)
    rb = jnp.pad(router_b, ((0, 0), (0, LANES - N_EXPERTS)))[:, None, :]

    prev = None
    caches = None
    for l in range(DEPTH):
        j = l // 2
        gmix = norm_mix_g[l][None, :]
        gffn = norm_ffn_g[l][None, :]
        if l % 2 == 0:
            proj_args = (mods, gmix, w_in_even[j].astype(BF16), blockdiag,
                         jnp.tile(q_norm_g[j], HALF // QK_DIM)[None, :],
                         jnp.tile(k_norm_g[j], HALF // QK_DIM)[None, :], rope_tabs)
            outs = None
            for g in range(2):
                x_in = x_groups[g] if l == 0 else x
                outs_g = _even_proj(l, g, x_in, l == 0, None if prev is None else (prev[0][g], prev[1]),
                                    outs, caches, *proj_args)
                if g == 0:
                    caches = outs_g[N_PROJ_OUT:]
                outs = outs_g[:N_PROJ_OUT]
            x, a, q, k, v = outs
            common = (cs128, lambda_q[j], lambda_k[j], subln_g[j][None, :], w_out_even[j].astype(BF16),
                      mods, gffn, rw_hi[l], rw_lo[l], rb[l])
            x, hp, gates, eidx, cnt = _even_mix(l, x, a, q, k, v, cache_k2, cache_v2, dft_ctx, dft_lat, *common)
        else:
            b_s = jnp.broadcast_to(jnp.transpose(b_spatial[j])[:, :, None],
                                   (CHUNK, C_GROUPS, GROUP)).reshape(CHUNK, D_MODEL)
            odd_args = (mods, gmix, w_in_odd[j].astype(BF16), v_norm_g[j][None, :],
                        w_spatial[j].astype(BF16), b_s, w_out_odd[j].astype(BF16), gffn, rw_hi[l], rw_lo[l], rb[l])
            outs = None
            for g in range(2):
                outs = _odd_layer(l, g, x, None if prev is None else (prev[0][g], prev[1]), outs, *odd_args)
            x, hp, gates, eidx, cnt = outs
        yg = _moe(l, hp, eidx, cnt, tri, w_gate, b_gate, w_up, b_up, w_down, b_down)
        prev = (yg, gates)
    y_ctx = _final_residual(0, x, (prev[0][0], prev[1]), mods)
    y_lat = _final_residual(1, x, (prev[0][1], prev[1]), mods)
    return (y_ctx.reshape(CTX_BATCH, CTX_SEQ, D_MODEL),
            y_lat.reshape(LAT_BATCH, LAT_SEQ, D_MODEL),
            caches[0].reshape(CTX_BATCH, N_EVEN, CTX_SEQ, N_GROUPS, 2, QK_DIM),
            caches[1].reshape(CTX_BATCH, N_EVEN, CTX_SEQ, N_GROUPS, GROUP))
```

```python
import functools
import math

import numpy as np
import jax
import jax.numpy as jnp
from jax import lax
from jax.experimental import pallas as pl
from jax.experimental.pallas import tpu as pltpu
from jax.experimental.pallas import tpu_sc as plsc

F32 = jnp.float32
BF16 = jnp.bfloat16
U32 = jnp.uint32
I32 = jnp.int32

D_MODEL = 1024
DEPTH = 4
N_EVEN = (DEPTH + 1) // 2
N_MOD = 6
EPS = 1e-6
CTX_BATCH, CTX_SEQ = 32, 256
LAT_BATCH, LAT_SEQ = 8, 1024
PAST_LEN = 512
GRID_W = 64
N_CTX = CTX_BATCH * CTX_SEQ
N_LAT = LAT_BATCH * LAT_SEQ
N_TOK = N_CTX + N_LAT
TM = 256
N_TILES = N_TOK // TM
CTX_TILES = N_CTX // TM
LAT_TILES_PER_BATCH = LAT_SEQ // TM
COND_ROWS = 16
GROUP = 128
N_GROUPS = 4
QK_DIM = 64
HALF = 512
CHUNK = 128
C_GROUPS = 8
N_EXPERTS = 32
TOP_K = 4
LANES = 128
SUBLANES = 8
SWIGLU_LIMIT = 7.0
SWIGLU_ALPHA = 1.702
MOE_BLOCK = 128
BLOCKS_PER_CHUNK = 4
MOE_CHUNK = BLOCKS_PER_CHUNK * MOE_BLOCK
TAIL_A = 2 * MOE_BLOCK
TAIL_B = MOE_BLOCK
N_ASSIGN = N_TOK * TOP_K
MOE_BLOCKS = N_ASSIGN // MOE_BLOCK + N_EXPERTS
N_SLOTS = MOE_BLOCKS * MOE_BLOCK
ROPE_BASE = 10000.0
VMEM_LIMIT = 52 * 1024 * 1024
SC_CORES = 2
SC_SUBCORES = 16
SC_WORKERS = SC_CORES * SC_SUBCORES
SC_ROWS = 64
SC_BUFS = 2
HI_MASK = 0xFFFF0000
LOG2E = math.log2(math.e)


def _dot(a, b):
    return jnp.dot(a, b, preferred_element_type=F32)


def _rms(x, g):
    return x * lax.rsqrt(jnp.mean(x * x, axis=-1, keepdims=True) + EPS) * g


def _split_bf16(x):
    hi = x.astype(BF16)
    lo = (x - hi.astype(F32)).astype(BF16)
    return hi, lo


def _pack_halves(xf):
    b = lax.bitcast_convert_type(xf, U32)
    return (b[:, :HALF] >> 16) | (b[:, HALF:] & jnp.uint32(HI_MASK))


def _unpack_halves(w):
    lo = lax.bitcast_convert_type(w << 16, F32)
    hi = lax.bitcast_convert_type(w & jnp.uint32(HI_MASK), F32)
    return lo, hi


def _tile_mod_row(t):
    return jnp.where(t < CTX_TILES, 0, 1 + (t - CTX_TILES) // LAT_TILES_PER_BATCH)


def _mod_spec(layer, k, row_fn):
    return pl.BlockSpec((None, None, None, 1, D_MODEL),
                        lambda *g: (layer, k, row_fn(*g), 0, 0))


def _full_spec(shape):
    return pl.BlockSpec(shape, lambda *g: (0,) * len(shape))


def _ada_kernel(cond_ref, w_ref, b_ref, o_ref):
    c = cond_ref[...]
    s = (c * jax.nn.sigmoid(c)).astype(BF16)
    o_ref[...] = _dot(s, w_ref[...].astype(BF16)) + b_ref[...]


def _ada_modulation(cond, ada_w, ada_b):
    return pl.pallas_call(
        _ada_kernel,
        grid=(DEPTH, N_MOD),
        in_specs=[
            _full_spec((COND_ROWS, D_MODEL)),
            pl.BlockSpec((None, D_MODEL, D_MODEL), lambda l, n: (l, 0, n)),
            pl.BlockSpec((None, None, 1, D_MODEL), lambda l, n: (l, n, 0, 0)),
        ],
        out_specs=pl.BlockSpec((None, None, COND_ROWS, D_MODEL), lambda l, n: (l, n, 0, 0)),
        out_shape=jax.ShapeDtypeStruct((DEPTH, N_MOD, COND_ROWS, D_MODEL), F32),
        compiler_params=pltpu.CompilerParams(dimension_semantics=("arbitrary", "arbitrary"),
                                             vmem_limit_bytes=VMEM_LIMIT),
        name="ada_modulation",
    )(cond, ada_w, ada_b.reshape(DEPTH, N_MOD, 1, D_MODEL))


N_PREV = 6


def _group_blocks(group, tiles_per_block):
    ctx = CTX_TILES // tiles_per_block
    return (0, ctx) if group == 0 else (ctx, N_TILES // tiles_per_block - ctx)


def _prev_specs(layer, first_block, tiles_per_block):
    rows = tiles_per_block * TM
    ys = [pl.BlockSpec((None, rows, HALF), functools.partial(lambda k, t: (k, t, 0), k))
          for k in range(TOP_K)]
    return ys + [pl.BlockSpec((rows, LANES), lambda t: (first_block + t, 0)),
                 _mod_spec(layer - 1, 5, lambda t: _tile_mod_row((first_block + t) * tiles_per_block))]


def _apply_prev(x, prev_refs):
    y_refs, gt_ref, gate_ref = prev_refs[:TOP_K], prev_refs[TOP_K], prev_refs[TOP_K + 1]
    gt = gt_ref[...]
    acc_lo = acc_hi = None
    for k in range(TOP_K):
        lo, hi = _unpack_halves(y_refs[k][...])
        g = gt[:, k:k + 1]
        acc_lo = g * lo if acc_lo is None else acc_lo + g * lo
        acc_hi = g * hi if acc_hi is None else acc_hi + g * hi
    return x + gate_ref[...] * jnp.concatenate([acc_lo, acc_hi], axis=1)


N_ROUTE_OUT = 5


def _route_out_specs(tile_fn, tiles_per_block=1):
    tile = lambda w: pl.BlockSpec((tiles_per_block * TM, w), lambda *g: (tile_fn(*g), 0))
    return [tile(D_MODEL), tile(HALF), tile(LANES),
            pl.BlockSpec((tiles_per_block * SUBLANES, TM), lambda *g: (tile_fn(*g), 0)),
            pl.BlockSpec((tiles_per_block * N_EXPERTS, LANES), lambda *g: (tile_fn(*g), 0))]


_ROUTE_OUT_SHAPES = [
    jax.ShapeDtypeStruct((N_TOK, D_MODEL), F32),
    jax.ShapeDtypeStruct((N_TOK, HALF), U32),
    jax.ShapeDtypeStruct((N_TOK, LANES), F32),
    jax.ShapeDtypeStruct((N_TILES * SUBLANES, TM), I32),
    jax.ShapeDtypeStruct((N_TILES * N_EXPERTS, LANES), F32),
]


def _post_mixer(x, mix, gate_ref, gffn_ref, shift_ref, scale_ref, rwhi_ref, rwlo_ref, rb_ref,
                xo_ref, hp_ref, gt_ref, ei_ref, cnt_ref):
    xn = x + gate_ref[...] * mix
    xo_ref[...] = xn
    h2 = _rms(xn, gffn_ref[...]) * (1.0 + scale_ref[...]) + shift_ref[...]
    hi = h2.astype(BF16)
    hif = hi.astype(F32)
    lo = (h2 - hif).astype(BF16)
    hp_ref[...] = _pack_halves(hif)
    rwhi = rwhi_ref[...]
    logits = _dot(hi, rwhi) + _dot(lo, rwhi) + _dot(hi, rwlo_ref[...]) + rb_ref[...]

    rows = x.shape[0]
    work = logits.T[0:N_EXPERTS, :]
    expert = lax.broadcasted_iota(I32, (N_EXPERTS, rows), 0).astype(F32)
    krow = lax.broadcasted_iota(I32, (SUBLANES, rows), 0)
    member = jnp.zeros((N_EXPERTS, rows), F32)
    gates = jnp.zeros((SUBLANES, rows), F32)
    ids = jnp.zeros((SUBLANES, rows), F32)
    den = None
    top = None
    for k in range(TOP_K):
        m = jnp.max(work, axis=0, keepdims=True)
        idx = jnp.min(jnp.where(work == m, expert, float(N_EXPERTS)), axis=0, keepdims=True)
        onehot = expert == idx
        work = jnp.where(onehot, -jnp.inf, work)
        member = member + onehot.astype(F32)
        if k == 0:
            top = m
        e = jnp.exp(m - top)
        den = e if den is None else den + e
        gates = gates + jnp.where(krow == k, e, 0.0)
        ids = ids + jnp.where(krow == k, idx, 0.0)
    gates = gates * (1.0 / den)
    gt_ref[...] = jnp.concatenate([gates, jnp.zeros((LANES - SUBLANES, rows), F32)], axis=0).T
    ids = ids.astype(I32)
    for i in range(rows // TM):
        ei_ref[SUBLANES * i:SUBLANES * (i + 1), :] = ids[:, TM * i:TM * (i + 1)]
        cnt_ref[N_EXPERTS * i:N_EXPERTS * (i + 1), :] = jnp.broadcast_to(
            jnp.sum(member[:, TM * i:TM * (i + 1)], axis=1, keepdims=True), (N_EXPERTS, LANES))


def _rope(x, c, sa, sb):
    outs = []
    for h in range(N_GROUPS):
        xs = x[:, GROUP * h:GROUP * (h + 1)]
        outs.append(xs * c + pltpu.roll(xs, GROUP - 16, 1) * sa + pltpu.roll(xs, 16, 1) * sb)
    return jnp.concatenate(outs, axis=1)


PROJ_TILES = 2


N_PROJ_OUT = 5


def _even_proj_kernel(group, has_prev, n_aliased, *refs):
    x_ref = refs[0]
    refs = refs[1:]
    if has_prev:
        prev_refs = refs[:N_PREV]
        refs = refs[N_PREV:]
    (g_ref, shift_ref, scale_ref, w_ref, bd_ref, qg_ref, kg_ref, cos_ref, sa_ref, sb_ref) = refs[:10]
    outs = refs[10 + n_aliased:]
    xo_ref, a_ref, q_ref, k_ref, v_ref = outs[:N_PROJ_OUT]

    x = x_ref[...]
    if has_prev:
        x = _apply_prev(x, prev_refs)
    xo_ref[...] = x
    h = (_rms(x, g_ref[...]) * (1.0 + scale_ref[...]) + shift_ref[...]).astype(BF16)

    a_ref[...] = _dot(h, w_ref[:, 0:HALF]).astype(BF16)
    zv = _dot(h, w_ref[:, 3 * HALF:4 * HALF])
    v_ref[...] = zv.astype(BF16)

    bd = bd_ref[...]

    def qk_norm(z, gain):
        shi, slo = _split_bf16(z * z)
        ssq = _dot(shi, bd) + _dot(slo, bd)
        return z * lax.rsqrt(ssq * (1.0 / QK_DIM) + EPS) * gain

    qn = qk_norm(_dot(h, w_ref[:, HALF:2 * HALF]), qg_ref[...]) * (QK_DIM ** -0.5 * LOG2E)
    kn = qk_norm(_dot(h, w_ref[:, 2 * HALF:3 * HALF]), kg_ref[...])

    if group == 0:
        kf_ref, vf_ref = outs[N_PROJ_OUT:]
        kf_ref[...] = kn.reshape(kf_ref.shape)
        vf_ref[...] = zv.reshape(vf_ref.shape)
        q_ref[...] = qn.astype(BF16)
        k_ref[...] = kn.astype(BF16)
    else:
        c, sa, sb = cos_ref[...], sa_ref[...], sb_ref[...]
        q_ref[...] = _rope(qn, c, sa, sb).astype(BF16)
        k_ref[...] = _rope(kn, c, sa, sb).astype(BF16)


def _even_proj(layer, group, x, x_is_group_local, prev, ctx_outs, caches, mods, norm_g, w_in, blockdiag,
               qg, kg, rope_tabs):
    has_prev = prev is not None
    rows = PROJ_TILES * TM
    first, n_blocks = _group_blocks(group, PROJ_TILES)
    mod_row = lambda t: _tile_mod_row((first + t) * PROJ_TILES)
    glob = lambda w: pl.BlockSpec((rows, w), lambda t: (first + t, 0))
    local = lambda w: pl.BlockSpec((rows, w), lambda t: (t, 0))
    rope_spec = pl.BlockSpec((rows, GROUP), lambda t: (t % (LAT_SEQ // rows) if group == 1 else 0, 0))
    in_specs = [local(D_MODEL) if x_is_group_local else glob(D_MODEL)]
    args = [x]
    if has_prev:
        yg, gates = prev
        in_specs += _prev_specs(layer, first, PROJ_TILES)
        args += [yg] * TOP_K + [gates, mods]
    in_specs += [
        _full_spec((1, D_MODEL)),
        _mod_spec(layer, 0, mod_row), _mod_spec(layer, 1, mod_row),
        _full_spec((D_MODEL, 4 * HALF)), _full_spec((HALF, HALF)),
        _full_spec((1, HALF)), _full_spec((1, HALF)),
        rope_spec, rope_spec, rope_spec,
    ]
    args += [norm_g, mods, mods, w_in, blockdiag, qg, kg, *rope_tabs]
    out_specs = [glob(D_MODEL)] + [glob(HALF)] * 4
    out_shape = [jax.ShapeDtypeStruct((N_TOK, D_MODEL), F32)] + [jax.ShapeDtypeStruct((N_TOK, HALF), BF16)] * 4
    aliased = []
    if group == 0:
        seqs = rows // CTX_SEQ
        cache_spec = pl.BlockSpec((seqs, None, CTX_SEQ, HALF), lambda t: (t, layer // 2, 0, 0))
        out_specs += [cache_spec] * 2
        out_shape += [jax.ShapeDtypeStruct((CTX_BATCH, N_EVEN, CTX_SEQ, HALF), F32)] * 2
        if caches is not None:
            aliased = [(caches[i], N_PROJ_OUT + i) for i in range(2)]
    else:
        aliased = [(ctx_outs[i], i) for i in range(N_PROJ_OUT)]
    aliases = {len(args) + i: out_idx for i, (_, out_idx) in enumerate(aliased)}
    in_specs += [pl.BlockSpec(memory_space=pl.ANY)] * len(aliased)
    args += [arr for arr, _ in aliased]
    return pl.pallas_call(
        functools.partial(_even_proj_kernel, group, has_prev, len(aliased)),
        grid=(n_blocks,),
        in_specs=in_specs, out_specs=out_specs, out_shape=out_shape,
        input_output_aliases=aliases,
        compiler_params=pltpu.CompilerParams(dimension_semantics=("arbitrary",),
                                             vmem_limit_bytes=VMEM_LIMIT),
        name="even_proj_ctx" if group == 0 else "even_proj_lat",
    )(*args)


MIX_TILES = 2
MIX_ROWS = MIX_TILES * TM


def _even_mix_kernel(lambda_init, *refs):
    (x_ref, q_ref, ac_ref, kc_ref, vc_ref, al_ref, kl_ref, vl_ref, ck_ref, cv_ref,
     cnc_ref, snc_ref, cnl_ref, snl_ref) = refs[:14]
    shared = refs[14:]
    t = pl.program_id(0)
    everything = slice(None)

    @pl.when(t < CTX_TILES // MIX_TILES)
    def _():
        seqs = []
        for s in range(MIX_ROWS // CTX_SEQ):
            rows = slice(CTX_SEQ * s, CTX_SEQ * (s + 1))
            seqs.append((rows, ac_ref, rows, [(kc_ref, vc_ref, rows, False)], cnc_ref, snc_ref))
        _even_mix_body(lambda_init, x_ref, q_ref, seqs, *shared)

    @pl.when(t >= CTX_TILES // MIX_TILES)
    def _():
        seqs = [(everything, al_ref, everything,
                 [(kl_ref, vl_ref, everything, False), (ck_ref, cv_ref, everything, True)], cnl_ref, snl_ref)]
        _even_mix_body(lambda_init, x_ref, q_ref, seqs, *shared)


def _even_mix_body(lambda_init, x_ref, q_ref, seqs, *refs):
    (cs_ref, lq_ref, lk_ref, sg_ref, wo_ref,
     gate_ref, gffn_ref, shift_ref, scale_ref, rwhi_ref, rwlo_ref, rb_ref) = refs[:12]
    out_refs = refs[12:]
    d = jnp.sum(lq_ref[...] * lk_ref[...], axis=1, keepdims=True)
    ed = jnp.exp(d)
    lam = ed[0:1, :] - ed[1:2, :] + lambda_init
    mixed = [_mix_sequence(lambda_init, lam, q_ref, cs_ref, sg_ref, *seq) for seq in seqs]
    mix = _dot(mixed[0] if len(mixed) == 1 else jnp.concatenate(mixed, axis=0), wo_ref[...])
    _post_mixer(x_ref[...], mix, gate_ref, gffn_ref, shift_ref, scale_ref, rwhi_ref, rwlo_ref, rb_ref,
                *out_refs)


def _mix_sequence(lambda_init, lam, q_ref, cs_ref, sg_ref, q_rows, a_ref, a_rows, kv_refs, cn_ref, sn_ref):
    a = a_ref[a_rows, :]
    y1 = _dot(cn_ref[...], a).astype(BF16)
    y2 = _dot(sn_ref[...], a).astype(BF16)
    cs = cs_ref[...]
    pieces = []
    for g in range(N_GROUPS):
        sl = slice(GROUP * g, GROUP * (g + 1))
        pieces.append(_dot(jnp.concatenate([y1[:, sl], y2[:, sl]], axis=1), cs).astype(BF16))

    lane = lax.broadcasted_iota(I32, (y1.shape[0], GROUP), 1)
    nt = (((1,), (1,)), ((), ()))
    for hd in range(N_GROUPS):
        sl = slice(GROUP * hd, GROUP * (hd + 1))
        qf = q_ref[q_rows, sl].astype(F32)
        parts = []
        for k_ref, v_ref, kv_rows, is_f32 in kv_refs:
            kk, vv = k_ref[kv_rows, sl], v_ref[kv_rows, sl]
            parts.append((kk.astype(BF16), vv.astype(BF16)) if is_f32 else (kk, vv))

        def probs(qm):
            ss = [lax.dot_general(qm, kk, nt, preferred_element_type=F32) for kk, _ in parts]
            m = functools.reduce(jnp.maximum, [jnp.max(s, axis=1, keepdims=True) for s in ss])
            es = [jnp.exp2(s - m) for s in ss]
            den = functools.reduce(lambda u, w: u + w, [jnp.sum(e, axis=1, keepdims=True) for e in es])
            return es, 1.0 / den

        es0, inv0 = probs(jnp.where(lane < QK_DIM, qf, 0.0).astype(BF16))
        es1, inv1 = probs(jnp.where(lane >= QK_DIM, qf, 0.0).astype(BF16))
        o0 = functools.reduce(lambda u, w: u + w,
                              [_dot(e.astype(BF16), vv) for e, (_, vv) in zip(es0, parts)])
        o1 = functools.reduce(lambda u, w: u + w,
                              [_dot(e.astype(BF16), vv) for e, (_, vv) in zip(es1, parts)])
        o = o0 * inv0 - o1 * (lam * inv1)
        pieces.append((_rms(o, sg_ref[...]) * (1.0 - lambda_init)).astype(BF16))
    return jnp.concatenate(pieces, axis=1)


def _even_mix(layer, x, a, q, k, v, cache_k, cache_v, dft_ctx, dft_lat, cs128, lam_q, lam_k, subln_g,
              w_out, mods, gffn, rw_hi, rw_lo, rb):
    j = layer // 2
    lambda_init = 0.8 - 0.6 * math.exp(-0.3 * layer)
    ctx_blocks = CTX_TILES // MIX_TILES
    per_seq = LAT_SEQ // MIX_ROWS
    lat = lambda t: jnp.maximum(t - ctx_blocks, 0)
    mod_row = lambda t: _tile_mod_row(t * MIX_TILES)
    tile = lambda w: pl.BlockSpec((MIX_ROWS, w), lambda t: (t, 0))
    ctx_seq = pl.BlockSpec((MIX_ROWS, HALF), lambda t: (jnp.minimum(t, ctx_blocks - 1), 0))
    lat_seq = pl.BlockSpec((LAT_SEQ, HALF), lambda t: (N_CTX // LAT_SEQ + lat(t) // per_seq, 0))
    cache = pl.BlockSpec((None, None, PAST_LEN, HALF), lambda t: (lat(t) // per_seq, j, 0, 0))
    dft_lat_spec = pl.BlockSpec((MIX_ROWS, LAT_SEQ), lambda t: (lat(t) % per_seq, 0))
    in_specs = [
        tile(D_MODEL), tile(HALF), ctx_seq, ctx_seq, ctx_seq, lat_seq, lat_seq, lat_seq, cache, cache,
        _full_spec((CTX_SEQ, CTX_SEQ)), _full_spec((CTX_SEQ, CTX_SEQ)), dft_lat_spec, dft_lat_spec,
        _full_spec((2 * GROUP, GROUP)),
        _full_spec((2, QK_DIM)), _full_spec((2, QK_DIM)), _full_spec((1, GROUP)),
        _full_spec((D_MODEL, D_MODEL)),
        _mod_spec(layer, 2, mod_row), _full_spec((1, D_MODEL)),
        _mod_spec(layer, 3, mod_row), _mod_spec(layer, 4, mod_row),
        _full_spec((D_MODEL, LANES)), _full_spec((D_MODEL, LANES)), _full_spec((1, LANES)),
    ]
    args = [x, q, a, k, v, a, k, v, cache_k, cache_v, dft_ctx[0], dft_ctx[1], dft_lat[0], dft_lat[1],
            cs128, lam_q, lam_k, subln_g, w_out, mods, gffn, mods, mods, rw_hi, rw_lo, rb]
    return pl.pallas_call(
        functools.partial(_even_mix_kernel, lambda_init),
        grid=(N_TILES // MIX_TILES,),
        in_specs=in_specs,
        out_specs=_route_out_specs(lambda t: t, MIX_TILES),
        out_shape=_ROUTE_OUT_SHAPES,
        compiler_params=pltpu.CompilerParams(dimension_semantics=("arbitrary",),
                                             vmem_limit_bytes=VMEM_LIMIT),
        name="even_mix",
    )(*args)


ODD_TILES = 2


def _odd_kernel(group, has_prev, *refs):
    x_ref = refs[0]
    refs = refs[1:]
    if has_prev:
        prev_refs = refs[:N_PREV]
        refs = refs[N_PREV:]
    (g_ref, shift_ref, scale_ref, w_ref, vg_ref, ws_ref, bs_ref, wo_ref,
     gate_ref, gffn_ref, shift2_ref, scale2_ref, rwhi_ref, rwlo_ref, rb_ref) = refs[:15]
    out_refs = refs[15:]
    if group == 1:
        out_refs = out_refs[N_ROUTE_OUT:]

    x = x_ref[...]
    if has_prev:
        x = _apply_prev(x, prev_refs)
    h = (_rms(x, g_ref[...]) * (1.0 + scale_ref[...]) + shift_ref[...]).astype(BF16)

    def gelu(z):
        return 0.5 * z * (1.0 + lax.erf(z * (2.0 ** -0.5)))

    u = gelu(_dot(h, w_ref[:, 0:D_MODEL]))
    vn = _rms(gelu(_dot(h, w_ref[:, D_MODEL:2 * D_MODEL])), vg_ref[...]).astype(BF16)
    bs = bs_ref[...]
    rows = []
    for c in range(x.shape[0] // CHUNK):
        cols = []
        for g in range(C_GROUPS):
            cols.append(_dot(ws_ref[g], vn[CHUNK * c:CHUNK * (c + 1), GROUP * g:GROUP * (g + 1)]))
        rows.append(jnp.concatenate(cols, axis=1) + bs)
    sv = jnp.concatenate(rows, axis=0)
    mix = _dot((u * sv).astype(BF16), wo_ref[...])
    _post_mixer(x, mix, gate_ref, gffn_ref, shift2_ref, scale2_ref, rwhi_ref, rwlo_ref, rb_ref,
                *out_refs)


def _odd_layer(layer, group, x, prev, ctx_outs, mods, norm_g, w_in, v_norm_g, w_s, b_s, w_out, gffn,
               rw_hi, rw_lo, rb):
    has_prev = prev is not None
    first, n_blocks = _group_blocks(group, ODD_TILES)
    mod_row = lambda t: _tile_mod_row((first + t) * ODD_TILES)
    in_specs = [pl.BlockSpec((ODD_TILES * TM, D_MODEL), lambda t: (first + t, 0))]
    args = [x]
    if has_prev:
        yg, gates = prev
        in_specs += _prev_specs(layer, first, ODD_TILES)
        args += [yg] * TOP_K + [gates, mods]
    in_specs += [
        _full_spec((1, D_MODEL)),
        _mod_spec(layer, 0, mod_row), _mod_spec(layer, 1, mod_row),
        _full_spec((D_MODEL, 2 * D_MODEL)), _full_spec((1, D_MODEL)),
        _full_spec((C_GROUPS, CHUNK, CHUNK)), _full_spec((CHUNK, D_MODEL)),
        _full_spec((D_MODEL, D_MODEL)),
        _mod_spec(layer, 2, mod_row), _full_spec((1, D_MODEL)),
        _mod_spec(layer, 3, mod_row), _mod_spec(layer, 4, mod_row),
        _full_spec((D_MODEL, LANES)), _full_spec((D_MODEL, LANES)), _full_spec((1, LANES)),
    ]
    args += [norm_g, mods, mods, w_in, v_norm_g, w_s, b_s, w_out, mods, gffn, mods, mods, rw_hi, rw_lo, rb]
    aliases = {}
    if group == 1:
        aliases = {len(args) + i: i for i in range(N_ROUTE_OUT)}
        in_specs += [pl.BlockSpec(memory_space=pl.ANY)] * N_ROUTE_OUT
        args += list(ctx_outs)
    return pl.pallas_call(
        functools.partial(_odd_kernel, group, has_prev),
        grid=(n_blocks,),
        in_specs=in_specs,
        out_specs=_route_out_specs(lambda t: first + t, ODD_TILES),
        out_shape=_ROUTE_OUT_SHAPES,
        input_output_aliases=aliases,
        compiler_params=pltpu.CompilerParams(dimension_semantics=("arbitrary",),
                                             vmem_limit_bytes=VMEM_LIMIT),
        name="odd_layer_ctx" if group == 0 else "odd_layer_lat",
    )(*args)


def _slot_kernel(ei_ref, base_ref, tri_ref, o_ref):
    expert = lax.broadcasted_iota(I32, (N_EXPERTS, TM), 0)
    krow = lax.broadcasted_iota(I32, (SUBLANES, TM), 0)
    tri = tri_ref[...]
    for i in range(SLOT_TILES):
        ei = ei_ref[SUBLANES * i:SUBLANES * (i + 1), :]
        onehots = [expert == ei[k:k + 1, :] for k in range(TOP_K)]
        member = functools.reduce(lambda u, w: u + w, [o.astype(F32) for o in onehots])
        before = _dot(member.astype(BF16), tri) + base_ref[N_EXPERTS * i:N_EXPERTS * (i + 1), :]
        slots = jnp.zeros((SUBLANES, TM), F32)
        for k in range(TOP_K):
            s = jnp.sum(jnp.where(onehots[k], before, 0.0), axis=0, keepdims=True)
            slots = slots + jnp.where(krow == k, s, 0.0)
        o_ref[i] = slots.astype(I32)


SLOT_TILES = 4


def _slots(eidx, tile_base, tri):
    return pl.pallas_call(
        _slot_kernel,
        grid=(N_TILES // SLOT_TILES,),
        in_specs=[pl.BlockSpec((SLOT_TILES * SUBLANES, TM), lambda t: (t, 0)),
                  pl.BlockSpec((SLOT_TILES * N_EXPERTS, TM), lambda t: (t, 0)),
                  _full_spec((TM, TM))],
        out_specs=pl.BlockSpec((SLOT_TILES, SUBLANES, TM), lambda t: (t, 0, 0)),
        out_shape=jax.ShapeDtypeStruct((N_TILES, SUBLANES, TM), I32),
        compiler_params=pltpu.CompilerParams(dimension_semantics=("arbitrary",)),
        name="moe_slots",
    )(eidx, tile_base, tri)


def _sc_mesh():
    return plsc.VectorSubcoreMesh(core_axis_name="c", subcore_axis_name="s")


def _sc_worker():
    return lax.axis_index("s") * SC_CORES + lax.axis_index("c")


def _dispatch(hp, idx):
    n_chunks = N_TOK // SC_WORKERS // SC_ROWS

    @functools.partial(
        pl.kernel, mesh=_sc_mesh(),
        out_type=jax.ShapeDtypeStruct((N_SLOTS, HALF), U32),
        scratch_types=[pltpu.VMEM((n_chunks, TOP_K, SC_ROWS), I32), pltpu.VMEM((SC_BUFS, SC_ROWS, HALF), U32),
                       pltpu.SemaphoreType.DMA((SC_BUFS,)), pltpu.SemaphoreType.DMA((SC_BUFS,))],
        name="moe_dispatch",
    )
    def k(x_hbm, idx_hbm, out_hbm, idx_v, rows_v, read_sem, scat_sem):
        wid = _sc_worker()
        base = wid * n_chunks
        pltpu.sync_copy(idx_hbm.at[wid], idx_v)

        def read(j, b):
            return pltpu.make_async_copy(x_hbm.at[pl.ds((base + j) * SC_ROWS, SC_ROWS)], rows_v.at[b],
                                         read_sem.at[b])

        def scatter(j, b, kk):
            return pltpu.make_async_copy(rows_v.at[b], out_hbm.at[idx_v.at[j, kk]], scat_sem.at[b])

        def drain(j, b):
            for kk in range(TOP_K):
                scatter(j, b, kk).wait()

        read(0, 0).start()

        @pl.loop(0, n_chunks, step=SC_BUFS)
        def _(j):
            for b in range(SC_BUFS):
                jj = j + b
                other = (b + 1) % SC_BUFS
                read(jj, b).wait()

                @pl.when(jj >= 1)
                def _():
                    drain(jj - 1, other)

                @pl.when(jj + 1 < n_chunks)
                def _():
                    read(jj + 1, other).start()

                for kk in range(TOP_K):
                    scatter(jj, b, kk).start()

        drain(n_chunks - 1, (n_chunks - 1) % SC_BUFS)

    return k(hp, idx)


def _combine_gather(ys, idx):
    n_chunks = idx.shape[1]

    @functools.partial(
        pl.kernel, mesh=_sc_mesh(),
        out_type=jax.ShapeDtypeStruct((SC_WORKERS * n_chunks * SC_ROWS, HALF), U32),
        scratch_types=[pltpu.VMEM((n_chunks, SC_ROWS), I32), pltpu.VMEM((SC_BUFS, SC_ROWS, HALF), U32),
                       pltpu.SemaphoreType.DMA((SC_BUFS,)), pltpu.SemaphoreType.DMA((SC_BUFS,))],
        name="moe_combine",
    )
    def k(ys_hbm, idx_hbm, out_hbm, idx_v, rows_v, gather_sem, write_sem):
        wid = _sc_worker()
        base = wid * n_chunks
        pltpu.sync_copy(idx_hbm.at[wid], idx_v)

        def gather(j, b):
            return pltpu.make_async_copy(ys_hbm.at[idx_v.at[j]], rows_v.at[b], gather_sem.at[b])

        def write(j, b):
            return pltpu.make_async_copy(rows_v.at[b], out_hbm.at[pl.ds((base + j) * SC_ROWS, SC_ROWS)],
                                         write_sem.at[b])

        gather(0, 0).start()

        @pl.loop(0, n_chunks, step=SC_BUFS)
        def _(j):
            for b in range(SC_BUFS):
                jj = j + b
                other = (b + 1) % SC_BUFS
                gather(jj, b).wait()

                @pl.when(jj >= 1)
                def _():
                    write(jj - 1, other).wait()

                @pl.when(jj + 1 < n_chunks)
                def _():
                    gather(jj + 1, other).start()

                write(jj, b).start()

        write(n_chunks - 1, (n_chunks - 1) % SC_BUFS).wait()

    return k(ys, idx)


N_MATS = 3


def _moe_kernel(layer, first_ref, nblk_ref, xs_hbm, wg_hbm, wu_hbm, wd_hbm, bg_ref, bu_ref, bd_ref,
                ys_hbm, wf32, wbf, xbuf, obuf, wsem, xsem, osem):
    e = pl.program_id(0)
    slot = e % 2
    w_hbm = (wg_hbm, wu_hbm, wd_hbm)
    nb = nblk_ref[e]
    row0 = first_ref[e] * MOE_BLOCK
    n_chunks = nb // BLOCKS_PER_CHUNK
    has_a = nb % BLOCKS_PER_CHUNK >= 2
    has_b = nb % 2 == 1
    row_a = row0 + n_chunks * MOE_CHUNK
    row_b = row_a + jnp.where(has_a, TAIL_A, 0)
    slot_a = n_chunks % 2
    slot_b = (n_chunks + has_a.astype(I32)) % 2
    last_chunk = (row_a - MOE_CHUNK, MOE_CHUNK, 1 - slot_a)

    def w_copy(ee, s, m):
        return pltpu.make_async_copy(w_hbm[m].at[layer, ee], wf32.at[s, m], wsem.at[s, m])

    def x_copy(row, rows, s):
        return pltpu.make_async_copy(xs_hbm.at[pl.ds(row, rows)], xbuf.at[s, pl.ds(0, rows)], xsem.at[s])

    def o_copy(row, rows, s):
        return pltpu.make_async_copy(obuf.at[s, pl.ds(0, rows)], ys_hbm.at[pl.ds(row, rows)], osem.at[s])

    def expert(xp):
        lo, hi = _unpack_halves(xp)
        xb = jnp.concatenate([lo.astype(BF16), hi.astype(BF16)], axis=1)
        gt = jnp.minimum(_dot(xb, wbf[0]) + bg_ref[...], SWIGLU_LIMIT)
        up = jnp.clip(_dot(xb, wbf[1]) + bu_ref[...], -SWIGLU_LIMIT, SWIGLU_LIMIT)
        glu = gt * jax.nn.sigmoid(SWIGLU_ALPHA * gt)
        hmid = ((up + 1.0) * glu).astype(BF16)
        out = _dot(hmid, wbf[2]) + bd_ref[...]
        return _pack_halves(out.astype(BF16).astype(F32))

    @pl.when(e == 0)
    def _():
        for m in range(N_MATS):
            w_copy(0, 0, m).start()

    @pl.when(n_chunks > 0)
    def _():
        x_copy(row0, MOE_CHUNK, 0).start()

    @pl.when((n_chunks == 0) & has_a)
    def _():
        x_copy(row0, TAIL_A, 0).start()

    @pl.when((n_chunks == 0) & jnp.logical_not(has_a) & has_b)
    def _():
        x_copy(row0, TAIL_B, 0).start()

    for m in range(N_MATS):
        w_copy(e, slot, m).wait()

    @pl.when(e + 1 < N_EXPERTS)
    def _():
        for m in range(N_MATS):
            w_copy(e + 1, 1 - slot, m).start()

    for m in range(N_MATS):
        wbf[m] = wf32[slot, m].astype(BF16)

    def chunk(c, carry):
        s = c % 2
        row = row0 + c * MOE_CHUNK
        x_copy(row, MOE_CHUNK, s).wait()

        @pl.when(c + 1 < n_chunks)
        def _():
            x_copy(row + MOE_CHUNK, MOE_CHUNK, 1 - s).start()

        @pl.when((c + 1 == n_chunks) & has_a)
        def _():
            x_copy(row_a, TAIL_A, 1 - s).start()

        @pl.when((c + 1 == n_chunks) & jnp.logical_not(has_a) & has_b)
        def _():
            x_copy(row_b, TAIL_B, 1 - s).start()

        obuf[s] = expert(xbuf[s])

        @pl.when(c >= 1)
        def _():
            o_copy(row - MOE_CHUNK, MOE_CHUNK, 1 - s).wait()

        o_copy(row, MOE_CHUNK, s).start()
        return carry

    lax.fori_loop(0, n_chunks, chunk, 0)

    @pl.when(has_a)
    def _():
        x_copy(row_a, TAIL_A, slot_a).wait()

        @pl.when(has_b)
        def _():
            x_copy(row_b, TAIL_B, 1 - slot_a).start()

        obuf[slot_a, 0:TAIL_A, :] = expert(xbuf[slot_a, 0:TAIL_A, :])

        @pl.when(n_chunks >= 1)
        def _():
            o_copy(*last_chunk).wait()

        o_copy(row_a, TAIL_A, slot_a).start()

    @pl.when(has_b)
    def _():
        x_copy(row_b, TAIL_B, slot_b).wait()
        obuf[slot_b, 0:TAIL_B, :] = expert(xbuf[slot_b, 0:TAIL_B, :])

        @pl.when(has_a)
        def _():
            o_copy(row_a, TAIL_A, slot_a).wait()

        @pl.when(jnp.logical_not(has_a) & (n_chunks >= 1))
        def _():
            o_copy(*last_chunk).wait()

        o_copy(row_b, TAIL_B, slot_b).start()
        o_copy(row_b, TAIL_B, slot_b).wait()

    @pl.when(jnp.logical_not(has_b) & has_a)
    def _():
        o_copy(row_a, TAIL_A, slot_a).wait()

    @pl.when(jnp.logical_not(has_b) & jnp.logical_not(has_a) & (n_chunks >= 1))
    def _():
        o_copy(*last_chunk).wait()


def _moe_experts(layer, xs, first_block, n_blocks, w_gate, b_gate, w_up, b_up, w_down, b_down):
    hbm = pl.BlockSpec(memory_space=pl.ANY)
    bspec = pl.BlockSpec((None, None, 1, D_MODEL), lambda e, fb, nb: (layer, e, 0, 0))
    bias = lambda b: b.reshape(DEPTH, N_EXPERTS, 1, D_MODEL)
    return pl.pallas_call(
        functools.partial(_moe_kernel, layer),
        grid_spec=pltpu.PrefetchScalarGridSpec(
            num_scalar_prefetch=2,
            grid=(N_EXPERTS,),
            in_specs=[hbm, hbm, hbm, hbm, bspec, bspec, bspec],
            out_specs=hbm,
            scratch_shapes=[
                pltpu.VMEM((2, N_MATS, D_MODEL, D_MODEL), F32),
                pltpu.VMEM((N_MATS, D_MODEL, D_MODEL), BF16),
                pltpu.VMEM((2, MOE_CHUNK, HALF), U32),
                pltpu.VMEM((2, MOE_CHUNK, HALF), U32),
                pltpu.SemaphoreType.DMA((2, N_MATS)),
                pltpu.SemaphoreType.DMA((2,)),
                pltpu.SemaphoreType.DMA((2,)),
            ],
        ),
        out_shape=jax.ShapeDtypeStruct((N_SLOTS, HALF), U32),
        compiler_params=pltpu.CompilerParams(dimension_semantics=("arbitrary",),
                                             vmem_limit_bytes=VMEM_LIMIT),
        name="moe_experts",
    )(first_block, n_blocks, xs, w_gate, w_up, w_down, bias(b_gate), bias(b_up), bias(b_down))


def _moe(layer, hp, eidx, cnt, tri, w_gate, b_gate, w_up, b_up, w_down, b_down):
    tile_cnt = cnt[:, 0].reshape(N_TILES, N_EXPERTS).astype(I32)
    counts = jnp.sum(tile_cnt, axis=0)
    padded = (counts + MOE_BLOCK - 1) // MOE_BLOCK * MOE_BLOCK
    pad_ends = jnp.cumsum(padded)
    pad_starts = pad_ends - padded
    tile_base = pad_starts[None, :] + jnp.cumsum(tile_cnt, axis=0) - tile_cnt
    tile_base = jnp.broadcast_to(tile_base.astype(F32).reshape(N_TILES * N_EXPERTS, 1),
                                 (N_TILES * N_EXPERTS, TM))
    first_block = (pad_starts // MOE_BLOCK).astype(I32)
    n_blocks = (padded // MOE_BLOCK).astype(I32)

    slots = _slots(eidx, tile_base, tri)[:, :TOP_K, :]
    per_tile = TM // SC_ROWS
    d_idx = slots.reshape(N_TILES, TOP_K, per_tile, SC_ROWS).transpose(0, 2, 1, 3)
    d_idx = d_idx.reshape(SC_WORKERS, N_TOK // SC_WORKERS // SC_ROWS, TOP_K, SC_ROWS)

    xs = _dispatch(hp, d_idx)
    ys = _moe_experts(layer, xs, first_block, n_blocks, w_gate, b_gate, w_up, b_up, w_down, b_down)
    gathered = []
    for tiles in (slots[:CTX_TILES], slots[CTX_TILES:]):
        n_tok = tiles.shape[0] * TM
        c_idx = tiles.transpose(1, 0, 2).reshape(SC_WORKERS, n_tok * TOP_K // SC_WORKERS // SC_ROWS, SC_ROWS)
        gathered.append(_combine_gather(ys, c_idx).reshape(TOP_K, n_tok, HALF))
    return gathered


FINAL_TILES = 2


def _final_kernel(x_ref, *refs):
    o_ref = refs[N_PREV]
    o_ref[...] = _apply_prev(x_ref[...], refs[:N_PREV])


def _final_residual(group, x, prev, mods):
    rows = FINAL_TILES * TM
    first, n_blocks = _group_blocks(group, FINAL_TILES)
    yg, gates = prev
    return pl.pallas_call(
        _final_kernel,
        grid=(n_blocks,),
        in_specs=[pl.BlockSpec((rows, D_MODEL), lambda t: (first + t, 0))]
        + _prev_specs(DEPTH, first, FINAL_TILES),
        out_specs=pl.BlockSpec((rows, D_MODEL), lambda t: (t, 0)),
        out_shape=jax.ShapeDtypeStruct((n_blocks * rows, D_MODEL), F32),
        compiler_params=pltpu.CompilerParams(dimension_semantics=("arbitrary",)),
        name="final_residual",
    )(x, *([yg] * TOP_K), gates, mods)


def _dft_pair(n):
    idx = np.arange(n)
    ang = 2.0 * np.pi * ((idx[:, None] * idx[None, :]) % n) / n
    return np.cos(ang) / np.sqrt(n), np.sin(ang) / np.sqrt(n)


def _rope_tables():
    lane = np.arange(GROUP)
    within = lane % QK_DIM
    axis = within // 32
    e = within % 32
    inv16 = ROPE_BASE ** (-jnp.arange(16, dtype=F32) / 16)
    pos = np.arange(LAT_SEQ)
    coord = np.where(axis[None, :] == 0, (pos // GRID_W)[:, None], (pos % GRID_W)[:, None])
    ang = jnp.asarray(coord, F32) * inv16[e % 16][None, :]
    first = jnp.asarray((e // 16) == 0)[None, :]
    cos, sin = jnp.cos(ang), jnp.sin(ang)
    return cos, jnp.where(first, -sin, 0.0), jnp.where(first, 0.0, sin)


def kernel(x_prompt, x_sample, cache_k, cache_v, c, c_ctx, ada_w, ada_b, norm_mix_g, norm_ffn_g,
           w_in_even, w_out_even, q_norm_g, k_norm_g, lambda_q, lambda_k, subln_g,
           w_in_odd, v_norm_g, w_spatial, b_spatial, w_out_odd,
           router_w, router_b, w_gate, b_gate, w_up, b_up, w_down, b_down):
    n_even = w_in_even.shape[0]
    x_groups = (x_prompt.reshape(N_CTX, D_MODEL), x_sample.reshape(N_LAT, D_MODEL))
    x = None
    cond = jnp.zeros((COND_ROWS, D_MODEL), F32).at[0].set(c_ctx).at[1:1 + LAT_BATCH].set(c)
    mods = _ada_modulation(cond, ada_w, ada_b).reshape(DEPTH, N_MOD, COND_ROWS, 1, D_MODEL)

    dft_ctx = [jnp.asarray(m, F32).astype(BF16) for m in _dft_pair(CTX_SEQ)]
    dft_lat = [jnp.asarray(m, F32).astype(BF16) for m in _dft_pair(LAT_SEQ)]
    c128, s128 = _dft_pair(GROUP)
    cs128 = jnp.asarray(np.concatenate([c128, -s128], axis=0), F32).astype(BF16)
    grp = np.arange(HALF) // QK_DIM
    blockdiag = jnp.asarray(grp[:, None] == grp[None, :], F32).astype(BF16)
    tri = jnp.asarray(np.arange(TM)[:, None] < np.arange(TM)[None, :], F32).astype(BF16)
    rope_tabs = _rope_tables()
    cache_k2 = cache_k.reshape(LAT_BATCH, n_even, PAST_LEN, HALF)
    cache_v2 = cache_v.reshape(LAT_BATCH, n_even, PAST_LEN, HALF)

    rw = jnp.pad(router_w, ((0, 0), (0, 0), (0, LANES - N_EXPERTS)))
    rw_hi = rw.astype(BF16)
    rw_lo = (rw - rw_hi.astype(F32)).astype(BF16)
    rb = jnp.pad(router_b, ((0, 0), (0, LANES - N_EXPERTS)))[:, None, :]

    prev = None
    caches = None
    for l in range(DEPTH):
        j = l // 2
        gmix = norm_mix_g[l][None, :]
        gffn = norm_ffn_g[l][None, :]
        if l % 2 == 0:
            proj_args = (mods, gmix, w_in_even[j].astype(BF16), blockdiag,
                         jnp.tile(q_norm_g[j], HALF // QK_DIM)[None, :],
                         jnp.tile(k_norm_g[j], HALF // QK_DIM)[None, :], rope_tabs)
            outs = None
            for g in range(2):
                x_in = x_groups[g] if l == 0 else x
                outs_g = _even_proj(l, g, x_in, l == 0, None if prev is None else (prev[0][g], prev[1]),
                                    outs, caches, *proj_args)
                if g == 0:
                    caches = outs_g[N_PROJ_OUT:]
                outs = outs_g[:N_PROJ_OUT]
            x, a, q, k, v = outs
            common = (cs128, lambda_q[j], lambda_k[j], subln_g[j][None, :], w_out_even[j].astype(BF16),
                      mods, gffn, rw_hi[l], rw_lo[l], rb[l])
            x, hp, gates, eidx, cnt = _even_mix(l, x, a, q, k, v, cache_k2, cache_v2, dft_ctx, dft_lat, *common)
        else:
            b_s = jnp.broadcast_to(jnp.transpose(b_spatial[j])[:, :, None],
                                   (CHUNK, C_GROUPS, GROUP)).reshape(CHUNK, D_MODEL)
            odd_args = (mods, gmix, w_in_odd[j].astype(BF16), v_norm_g[j][None, :],
                        w_spatial[j].astype(BF16), b_s, w_out_odd[j].astype(BF16), gffn, rw_hi[l], rw_lo[l], rb[l])
            outs = None
            for g in range(2):
                outs = _odd_layer(l, g, x, None if prev is None else (prev[0][g], prev[1]), outs, *odd_args)
            x, hp, gates, eidx, cnt = outs
        yg = _moe(l, hp, eidx, cnt, tri, w_gate, b_gate, w_up, b_up, w_down, b_down)
        prev = (yg, gates)
    y_ctx = _final_residual(0, x, (prev[0][0], prev[1]), mods)
    y_lat = _final_residual(1, x, (prev[0][1], prev[1]), mods)
    return (y_ctx.reshape(CTX_BATCH, CTX_SEQ, D_MODEL),
            y_lat.reshape(LAT_BATCH, LAT_SEQ, D_MODEL),
            caches[0].reshape(CTX_BATCH, N_EVEN, CTX_SEQ, N_GROUPS, 2, QK_DIM),
            caches[1].reshape(CTX_BATCH, N_EVEN, CTX_SEQ, N_GROUPS, GROUP))
```

```python
import functools
import math

import numpy as np
import jax
import jax.numpy as jnp
from jax import lax
from jax.experimental import pallas as pl
from jax.experimental.pallas import tpu as pltpu
from jax.experimental.pallas import tpu_sc as plsc

F32 = jnp.float32
BF16 = jnp.bfloat16
U32 = jnp.uint32
I32 = jnp.int32

D_MODEL = 1024
DEPTH = 4
N_EVEN = (DEPTH + 1) // 2
N_MOD = 6
EPS = 1e-6
CTX_BATCH, CTX_SEQ = 32, 256
LAT_BATCH, LAT_SEQ = 8, 1024
PAST_LEN = 512
GRID_W = 64
N_CTX = CTX_BATCH * CTX_SEQ
N_LAT = LAT_BATCH * LAT_SEQ
N_TOK = N_CTX + N_LAT
TM = 256
N_TILES = N_TOK // TM
CTX_TILES = N_CTX // TM
LAT_TILES_PER_BATCH = LAT_SEQ // TM
COND_ROWS = 16
GROUP = 128
N_GROUPS = 4
QK_DIM = 64
HALF = 512
CHUNK = 128
C_GROUPS = 8
N_EXPERTS = 32
TOP_K = 4
LANES = 128
SUBLANES = 8
SWIGLU_LIMIT = 7.0
SWIGLU_ALPHA = 1.702
MOE_BLOCK = 128
BLOCKS_PER_CHUNK = 4
MOE_CHUNK = BLOCKS_PER_CHUNK * MOE_BLOCK
TAIL_A = 2 * MOE_BLOCK
TAIL_B = MOE_BLOCK
N_ASSIGN = N_TOK * TOP_K
MOE_BLOCKS = N_ASSIGN // MOE_BLOCK + N_EXPERTS
N_SLOTS = MOE_BLOCKS * MOE_BLOCK
ROPE_BASE = 10000.0
VMEM_LIMIT = 52 * 1024 * 1024
SC_CORES = 2
SC_SUBCORES = 16
SC_WORKERS = SC_CORES * SC_SUBCORES
SC_ROWS = 64
SC_BUFS = 2
HI_MASK = 0xFFFF0000
LOG2E = math.log2(math.e)


def _dot(a, b):
    return jnp.dot(a, b, preferred_element_type=F32)


def _rms(x, g):
    return x * lax.rsqrt(jnp.mean(x * x, axis=-1, keepdims=True) + EPS) * g


def _pack_halves(xf):
    b = lax.bitcast_convert_type(xf, U32)
    return (b[:, :HALF] >> 16) | (b[:, HALF:] & jnp.uint32(HI_MASK))


def _unpack_halves(w):
    lo = lax.bitcast_convert_type(w << 16, F32)
    hi = lax.bitcast_convert_type(w & jnp.uint32(HI_MASK), F32)
    return lo, hi


def _tile_mod_row(t):
    return jnp.where(t < CTX_TILES, 0, 1 + (t - CTX_TILES) // LAT_TILES_PER_BATCH)


def _mod_spec(layer, k, row_fn):
    return pl.BlockSpec((None, None, None, 1, D_MODEL),
                        lambda *g: (layer, k, row_fn(*g), 0, 0))


def _full_spec(shape):
    return pl.BlockSpec(shape, lambda *g: (0,) * len(shape))


def _ada_kernel(cond_ref, w_ref, b_ref, o_ref):
    c = cond_ref[...]
    s = (c * jax.nn.sigmoid(c)).astype(BF16)
    o_ref[...] = _dot(s, w_ref[...].astype(BF16)) + b_ref[...]


def _ada_modulation(cond, ada_w, ada_b):
    return pl.pallas_call(
        _ada_kernel,
        grid=(DEPTH, N_MOD),
        in_specs=[
            _full_spec((COND_ROWS, D_MODEL)),
            pl.BlockSpec((None, D_MODEL, D_MODEL), lambda l, n: (l, 0, n)),
            pl.BlockSpec((None, None, 1, D_MODEL), lambda l, n: (l, n, 0, 0)),
        ],
        out_specs=pl.BlockSpec((None, None, COND_ROWS, D_MODEL), lambda l, n: (l, n, 0, 0)),
        out_shape=jax.ShapeDtypeStruct((DEPTH, N_MOD, COND_ROWS, D_MODEL), F32),
        compiler_params=pltpu.CompilerParams(dimension_semantics=("arbitrary", "arbitrary"),
                                             vmem_limit_bytes=VMEM_LIMIT),
        name="ada_modulation",
    )(cond, ada_w, ada_b.reshape(DEPTH, N_MOD, 1, D_MODEL))


N_PREV = 6


def _group_blocks(group, tiles_per_block):
    ctx = CTX_TILES // tiles_per_block
    return (0, ctx) if group == 0 else (ctx, N_TILES // tiles_per_block - ctx)


def _prev_specs(layer, first_block, tiles_per_block):
    rows = tiles_per_block * TM
    ys = [pl.BlockSpec((None, rows, HALF), functools.partial(lambda k, t: (k, t, 0), k))
          for k in range(TOP_K)]
    return ys + [pl.BlockSpec((rows, LANES), lambda t: (first_block + t, 0)),
                 _mod_spec(layer - 1, 5, lambda t: _tile_mod_row((first_block + t) * tiles_per_block))]


def _apply_prev(x, prev_refs):
    y_refs, gt_ref, gate_ref = prev_refs[:TOP_K], prev_refs[TOP_K], prev_refs[TOP_K + 1]
    gt = gt_ref[...]
    acc_lo = acc_hi = None
    for k in range(TOP_K):
        lo, hi = _unpack_halves(y_refs[k][...])
        g = gt[:, k:k + 1]
        acc_lo = g * lo if acc_lo is None else acc_lo + g * lo
        acc_hi = g * hi if acc_hi is None else acc_hi + g * hi
    return x + gate_ref[...] * jnp.concatenate([acc_lo, acc_hi], axis=1)


N_ROUTE_OUT = 5


def _route_out_specs(tile_fn, tiles_per_block=1):
    tile = lambda w: pl.BlockSpec((tiles_per_block * TM, w), lambda *g: (tile_fn(*g), 0))
    return [tile(D_MODEL), tile(HALF), tile(LANES),
            pl.BlockSpec((tiles_per_block * SUBLANES, TM), lambda *g: (tile_fn(*g), 0)),
            pl.BlockSpec((tiles_per_block * N_EXPERTS, LANES), lambda *g: (tile_fn(*g), 0))]


_ROUTE_OUT_SHAPES = [
    jax.ShapeDtypeStruct((N_TOK, D_MODEL), F32),
    jax.ShapeDtypeStruct((N_TOK, HALF), U32),
    jax.ShapeDtypeStruct((N_TOK, LANES), F32),
    jax.ShapeDtypeStruct((N_TILES * SUBLANES, TM), I32),
    jax.ShapeDtypeStruct((N_TILES * N_EXPERTS, LANES), F32),
]


def _post_mixer(x, mix, gate_ref, gffn_ref, shift_ref, scale_ref, rwhi_ref, rwlo_ref, rb_ref,
                xo_ref, hp_ref, gt_ref, ei_ref, cnt_ref):
    xn = x + gate_ref[...] * mix
    xo_ref[...] = xn
    h2 = _rms(xn, gffn_ref[...]) * (1.0 + scale_ref[...]) + shift_ref[...]
    hi = h2.astype(BF16)
    hif = hi.astype(F32)
    lo = (h2 - hif).astype(BF16)
    hp_ref[...] = _pack_halves(hif)
    rwhi = rwhi_ref[...]
    both = _dot(hi, jnp.concatenate([rwhi, rwlo_ref[...]], axis=1))
    logits = both[:, :LANES] + both[:, LANES:] + _dot(lo, rwhi) + rb_ref[...]

    rows = x.shape[0]
    work = logits.T[0:N_EXPERTS, :]
    expert = lax.broadcasted_iota(I32, (N_EXPERTS, rows), 0).astype(F32)
    krow = lax.broadcasted_iota(I32, (SUBLANES, rows), 0)
    member = jnp.zeros((N_EXPERTS, rows), F32)
    gates = jnp.zeros((SUBLANES, rows), F32)
    ids = jnp.zeros((SUBLANES, rows), F32)
    den = None
    top = None
    for k in range(TOP_K):
        m = jnp.max(work, axis=0, keepdims=True)
        idx = jnp.min(jnp.where(work == m, expert, float(N_EXPERTS)), axis=0, keepdims=True)
        onehot = expert == idx
        work = jnp.where(onehot, -jnp.inf, work)
        member = member + onehot.astype(F32)
        if k == 0:
            top = m
        e = jnp.exp(m - top)
        den = e if den is None else den + e
        gates = gates + jnp.where(krow == k, e, 0.0)
        ids = ids + jnp.where(krow == k, idx, 0.0)
    gates = gates * (1.0 / den)
    gt_ref[...] = jnp.concatenate([gates, jnp.zeros((LANES - SUBLANES, rows), F32)], axis=0).T
    ids = ids.astype(I32)
    for i in range(rows // TM):
        ei_ref[SUBLANES * i:SUBLANES * (i + 1), :] = ids[:, TM * i:TM * (i + 1)]
        cnt_ref[N_EXPERTS * i:N_EXPERTS * (i + 1), :] = jnp.broadcast_to(
            jnp.sum(member[:, TM * i:TM * (i + 1)], axis=1, keepdims=True), (N_EXPERTS, LANES))


def _rope(x, c, sa, sb):
    outs = []
    for h in range(N_GROUPS):
        xs = x[:, GROUP * h:GROUP * (h + 1)]
        outs.append(xs * c + pltpu.roll(xs, GROUP - 16, 1) * sa + pltpu.roll(xs, 16, 1) * sb)
    return jnp.concatenate(outs, axis=1)


PROJ_TILES = 2


N_PROJ_OUT = 5


def _even_proj_kernel(group, has_prev, n_aliased, *refs):
    x_ref = refs[0]
    refs = refs[1:]
    if has_prev:
        prev_refs = refs[:N_PREV]
        refs = refs[N_PREV:]
    (g_ref, shift_ref, scale_ref, w_ref, bd_ref, qg_ref, kg_ref, cos_ref, sa_ref, sb_ref) = refs[:10]
    outs = refs[10 + n_aliased:]
    xo_ref, a_ref, q_ref, k_ref, v_ref = outs[:N_PROJ_OUT]

    x = x_ref[...]
    if has_prev:
        x = _apply_prev(x, prev_refs)
    xo_ref[...] = x
    h = (_rms(x, g_ref[...]) * (1.0 + scale_ref[...]) + shift_ref[...]).astype(BF16)

    a_ref[...] = _dot(h, w_ref[:, 0:HALF]).astype(BF16)
    zv = _dot(h, w_ref[:, 3 * HALF:4 * HALF])
    v_ref[...] = zv.astype(BF16)

    bd = bd_ref[...]

    def qk_norm(z, gain):
        ssq = _dot((z * z).astype(BF16), bd)
        return z * lax.rsqrt(ssq * (1.0 / QK_DIM) + EPS) * gain

    qn = qk_norm(_dot(h, w_ref[:, HALF:2 * HALF]), qg_ref[...]) * (QK_DIM ** -0.5 * LOG2E)
    kn = qk_norm(_dot(h, w_ref[:, 2 * HALF:3 * HALF]), kg_ref[...])

    if group == 0:
        kf_ref, vf_ref = outs[N_PROJ_OUT:]
        kf_ref[...] = kn.reshape(kf_ref.shape)
        vf_ref[...] = zv.reshape(vf_ref.shape)
        q_ref[...] = qn.astype(BF16)
        k_ref[...] = kn.astype(BF16)
    else:
        c, sa, sb = cos_ref[...], sa_ref[...], sb_ref[...]
        q_ref[...] = _rope(qn, c, sa, sb).astype(BF16)
        k_ref[...] = _rope(kn, c, sa, sb).astype(BF16)


def _even_proj(layer, group, x, x_is_group_local, prev, ctx_outs, caches, mods, norm_g, w_in, blockdiag,
               qg, kg, rope_tabs):
    has_prev = prev is not None
    rows = PROJ_TILES * TM
    first, n_blocks = _group_blocks(group, PROJ_TILES)
    mod_row = lambda t: _tile_mod_row((first + t) * PROJ_TILES)
    glob = lambda w: pl.BlockSpec((rows, w), lambda t: (first + t, 0))
    local = lambda w: pl.BlockSpec((rows, w), lambda t: (t, 0))
    rope_spec = pl.BlockSpec((rows, GROUP), lambda t: (t % (LAT_SEQ // rows) if group == 1 else 0, 0))
    in_specs = [local(D_MODEL) if x_is_group_local else glob(D_MODEL)]
    args = [x]
    if has_prev:
        yg, gates = prev
        in_specs += _prev_specs(layer, first, PROJ_TILES)
        args += [yg] * TOP_K + [gates, mods]
    in_specs += [
        _full_spec((1, D_MODEL)),
        _mod_spec(layer, 0, mod_row), _mod_spec(layer, 1, mod_row),
        _full_spec((D_MODEL, 4 * HALF)), _full_spec((HALF, HALF)),
        _full_spec((1, HALF)), _full_spec((1, HALF)),
        rope_spec, rope_spec, rope_spec,
    ]
    args += [norm_g, mods, mods, w_in, blockdiag, qg, kg, *rope_tabs]
    out_specs = [glob(D_MODEL)] + [glob(HALF)] * 4
    out_shape = [jax.ShapeDtypeStruct((N_TOK, D_MODEL), F32)] + [jax.ShapeDtypeStruct((N_TOK, HALF), BF16)] * 4
    aliased = []
    if group == 0:
        seqs = rows // CTX_SEQ
        cache_spec = pl.BlockSpec((seqs, None, CTX_SEQ, HALF), lambda t: (t, layer // 2, 0, 0))
        out_specs += [cache_spec] * 2
        out_shape += [jax.ShapeDtypeStruct((CTX_BATCH, N_EVEN, CTX_SEQ, HALF), F32)] * 2
        if caches is not None:
            aliased = [(caches[i], N_PROJ_OUT + i) for i in range(2)]
    else:
        aliased = [(ctx_outs[i], i) for i in range(N_PROJ_OUT)]
    aliases = {len(args) + i: out_idx for i, (_, out_idx) in enumerate(aliased)}
    in_specs += [pl.BlockSpec(memory_space=pl.ANY)] * len(aliased)
    args += [arr for arr, _ in aliased]
    return pl.pallas_call(
        functools.partial(_even_proj_kernel, group, has_prev, len(aliased)),
        grid=(n_blocks,),
        in_specs=in_specs, out_specs=out_specs, out_shape=out_shape,
        input_output_aliases=aliases,
        compiler_params=pltpu.CompilerParams(dimension_semantics=("arbitrary",),
                                             vmem_limit_bytes=VMEM_LIMIT),
        name="even_proj_ctx" if group == 0 else "even_proj_lat",
    )(*args)


MIX_TILES = 2
MIX_ROWS = MIX_TILES * TM


def _even_mix_kernel(lambda_init, *refs):
    (x_ref, q_ref, ac_ref, kc_ref, vc_ref, al_ref, kl_ref, vl_ref, ck_ref, cv_ref,
     cnc_ref, snc_ref, cnl_ref, snl_ref) = refs[:14]
    shared = refs[14:]
    t = pl.program_id(0)
    everything = slice(None)

    @pl.when(t < CTX_TILES // MIX_TILES)
    def _():
        seqs = []
        for s in range(MIX_ROWS // CTX_SEQ):
            rows = slice(CTX_SEQ * s, CTX_SEQ * (s + 1))
            seqs.append((rows, ac_ref, rows, [(kc_ref, vc_ref, rows, False)], cnc_ref, snc_ref))
        _even_mix_body(lambda_init, x_ref, q_ref, seqs, *shared)

    @pl.when(t >= CTX_TILES // MIX_TILES)
    def _():
        seqs = [(everything, al_ref, everything,
                 [(kl_ref, vl_ref, everything, False), (ck_ref, cv_ref, everything, True)], cnl_ref, snl_ref)]
        _even_mix_body(lambda_init, x_ref, q_ref, seqs, *shared)


def _even_mix_body(lambda_init, x_ref, q_ref, seqs, *refs):
    (cs_ref, lq_ref, lk_ref, sg_ref, wo_ref,
     gate_ref, gffn_ref, shift_ref, scale_ref, rwhi_ref, rwlo_ref, rb_ref) = refs[:12]
    out_refs = refs[12:]
    d = jnp.sum(lq_ref[...] * lk_ref[...], axis=1, keepdims=True)
    ed = jnp.exp(d)
    lam = ed[0:1, :] - ed[1:2, :] + lambda_init
    mixed = [_mix_sequence(lambda_init, lam, q_ref, cs_ref, sg_ref, *seq) for seq in seqs]
    mix = _dot(mixed[0] if len(mixed) == 1 else jnp.concatenate(mixed, axis=0), wo_ref[...])
    _post_mixer(x_ref[...], mix, gate_ref, gffn_ref, shift_ref, scale_ref, rwhi_ref, rwlo_ref, rb_ref,
                *out_refs)


def _mix_sequence(lambda_init, lam, q_ref, cs_ref, sg_ref, q_rows, a_ref, a_rows, kv_refs, cn_ref, sn_ref):
    a = a_ref[a_rows, :]
    y1 = _dot(cn_ref[...], a).astype(BF16)
    y2 = _dot(sn_ref[...], a).astype(BF16)
    cs = cs_ref[...]
    pieces = []
    for g in range(N_GROUPS):
        sl = slice(GROUP * g, GROUP * (g + 1))
        pieces.append(_dot(jnp.concatenate([y1[:, sl], y2[:, sl]], axis=1), cs).astype(BF16))

    lane = lax.broadcasted_iota(I32, (y1.shape[0], GROUP), 1)
    nt = (((1,), (1,)), ((), ()))
    for hd in range(N_GROUPS):
        sl = slice(GROUP * hd, GROUP * (hd + 1))
        qf = q_ref[q_rows, sl].astype(F32)
        parts = []
        for k_ref, v_ref, kv_rows, is_f32 in kv_refs:
            kk, vv = k_ref[kv_rows, sl], v_ref[kv_rows, sl]
            parts.append((kk.astype(BF16), vv.astype(BF16)) if is_f32 else (kk, vv))

        def probs(qm):
            ss = [lax.dot_general(qm, kk, nt, preferred_element_type=F32) for kk, _ in parts]
            m = functools.reduce(jnp.maximum, [jnp.max(s, axis=1, keepdims=True) for s in ss])
            es = [jnp.exp2(s - m) for s in ss]
            den = functools.reduce(lambda u, w: u + w, [jnp.sum(e, axis=1, keepdims=True) for e in es])
            return es, 1.0 / den

        es0, inv0 = probs(jnp.where(lane < QK_DIM, qf, 0.0).astype(BF16))
        es1, inv1 = probs(jnp.where(lane >= QK_DIM, qf, 0.0).astype(BF16))
        o0 = functools.reduce(lambda u, w: u + w,
                              [_dot(e.astype(BF16), vv) for e, (_, vv) in zip(es0, parts)])
        o1 = functools.reduce(lambda u, w: u + w,
                              [_dot(e.astype(BF16), vv) for e, (_, vv) in zip(es1, parts)])
        o = o0 * inv0 - o1 * (lam * inv1)
        pieces.append((_rms(o, sg_ref[...]) * (1.0 - lambda_init)).astype(BF16))
    return jnp.concatenate(pieces, axis=1)


def _even_mix(layer, x, a, q, k, v, cache_k, cache_v, dft_ctx, dft_lat, cs128, lam_q, lam_k, subln_g,
              w_out, mods, gffn, rw_hi, rw_lo, rb):
    j = layer // 2
    lambda_init = 0.8 - 0.6 * math.exp(-0.3 * layer)
    ctx_blocks = CTX_TILES // MIX_TILES
    per_seq = LAT_SEQ // MIX_ROWS
    lat = lambda t: jnp.maximum(t - ctx_blocks, 0)
    mod_row = lambda t: _tile_mod_row(t * MIX_TILES)
    tile = lambda w: pl.BlockSpec((MIX_ROWS, w), lambda t: (t, 0))
    ctx_seq = pl.BlockSpec((MIX_ROWS, HALF), lambda t: (jnp.minimum(t, ctx_blocks - 1), 0))
    lat_seq = pl.BlockSpec((LAT_SEQ, HALF), lambda t: (N_CTX // LAT_SEQ + lat(t) // per_seq, 0))
    cache = pl.BlockSpec((None, None, PAST_LEN, HALF), lambda t: (lat(t) // per_seq, j, 0, 0))
    dft_lat_spec = pl.BlockSpec((MIX_ROWS, LAT_SEQ), lambda t: (lat(t) % per_seq, 0))
    in_specs = [
        tile(D_MODEL), tile(HALF), ctx_seq, ctx_seq, ctx_seq, lat_seq, lat_seq, lat_seq, cache, cache,
        _full_spec((CTX_SEQ, CTX_SEQ)), _full_spec((CTX_SEQ, CTX_SEQ)), dft_lat_spec, dft_lat_spec,
        _full_spec((2 * GROUP, GROUP)),
        _full_spec((2, QK_DIM)), _full_spec((2, QK_DIM)), _full_spec((1, GROUP)),
        _full_spec((D_MODEL, D_MODEL)),
        _mod_spec(layer, 2, mod_row), _full_spec((1, D_MODEL)),
        _mod_spec(layer, 3, mod_row), _mod_spec(layer, 4, mod_row),
        _full_spec((D_MODEL, LANES)), _full_spec((D_MODEL, LANES)), _full_spec((1, LANES)),
    ]
    args = [x, q, a, k, v, a, k, v, cache_k, cache_v, dft_ctx[0], dft_ctx[1], dft_lat[0], dft_lat[1],
            cs128, lam_q, lam_k, subln_g, w_out, mods, gffn, mods, mods, rw_hi, rw_lo, rb]
    return pl.pallas_call(
        functools.partial(_even_mix_kernel, lambda_init),
        grid=(N_TILES // MIX_TILES,),
        in_specs=in_specs,
        out_specs=_route_out_specs(lambda t: t, MIX_TILES),
        out_shape=_ROUTE_OUT_SHAPES,
        compiler_params=pltpu.CompilerParams(dimension_semantics=("arbitrary",),
                                             vmem_limit_bytes=VMEM_LIMIT),
        name="even_mix",
    )(*args)


ODD_TILES = 2


def _odd_kernel(group, has_prev, *refs):
    x_ref = refs[0]
    refs = refs[1:]
    if has_prev:
        prev_refs = refs[:N_PREV]
        refs = refs[N_PREV:]
    (g_ref, shift_ref, scale_ref, w_ref, vg_ref, ws_ref, bs_ref, wo_ref,
     gate_ref, gffn_ref, shift2_ref, scale2_ref, rwhi_ref, rwlo_ref, rb_ref) = refs[:15]
    out_refs = refs[15:]
    if group == 1:
        out_refs = out_refs[N_ROUTE_OUT:]

    x = x_ref[...]
    if has_prev:
        x = _apply_prev(x, prev_refs)
    h = (_rms(x, g_ref[...]) * (1.0 + scale_ref[...]) + shift_ref[...]).astype(BF16)

    def gelu(z):
        return 0.5 * z * (1.0 + lax.erf(z * (2.0 ** -0.5)))

    u = gelu(_dot(h, w_ref[:, 0:D_MODEL]))
    vn = _rms(gelu(_dot(h, w_ref[:, D_MODEL:2 * D_MODEL])), vg_ref[...]).astype(BF16)
    bs = bs_ref[...]
    rows = []
    for c in range(x.shape[0] // CHUNK):
        cols = []
        for g in range(C_GROUPS):
            cols.append(_dot(ws_ref[g], vn[CHUNK * c:CHUNK * (c + 1), GROUP * g:GROUP * (g + 1)]))
        rows.append(jnp.concatenate(cols, axis=1) + bs)
    sv = jnp.concatenate(rows, axis=0)
    mix = _dot((u * sv).astype(BF16), wo_ref[...])
    _post_mixer(x, mix, gate_ref, gffn_ref, shift2_ref, scale2_ref, rwhi_ref, rwlo_ref, rb_ref,
                *out_refs)


def _odd_layer(layer, group, x, prev, ctx_outs, mods, norm_g, w_in, v_norm_g, w_s, b_s, w_out, gffn,
               rw_hi, rw_lo, rb):
    has_prev = prev is not None
    first, n_blocks = _group_blocks(group, ODD_TILES)
    mod_row = lambda t: _tile_mod_row((first + t) * ODD_TILES)
    in_specs = [pl.BlockSpec((ODD_TILES * TM, D_MODEL), lambda t: (first + t, 0))]
    args = [x]
    if has_prev:
        yg, gates = prev
        in_specs += _prev_specs(layer, first, ODD_TILES)
        args += [yg] * TOP_K + [gates, mods]
    in_specs += [
        _full_spec((1, D_MODEL)),
        _mod_spec(layer, 0, mod_row), _mod_spec(layer, 1, mod_row),
        _full_spec((D_MODEL, 2 * D_MODEL)), _full_spec((1, D_MODEL)),
        _full_spec((C_GROUPS, CHUNK, CHUNK)), _full_spec((CHUNK, D_MODEL)),
        _full_spec((D_MODEL, D_MODEL)),
        _mod_spec(layer, 2, mod_row), _full_spec((1, D_MODEL)),
        _mod_spec(layer, 3, mod_row), _mod_spec(layer, 4, mod_row),
        _full_spec((D_MODEL, LANES)), _full_spec((D_MODEL, LANES)), _full_spec((1, LANES)),
    ]
    args += [norm_g, mods, mods, w_in, v_norm_g, w_s, b_s, w_out, mods, gffn, mods, mods, rw_hi, rw_lo, rb]
    aliases = {}
    if group == 1:
        aliases = {len(args) + i: i for i in range(N_ROUTE_OUT)}
        in_specs += [pl.BlockSpec(memory_space=pl.ANY)] * N_ROUTE_OUT
        args += list(ctx_outs)
    return pl.pallas_call(
        functools.partial(_odd_kernel, group, has_prev),
        grid=(n_blocks,),
        in_specs=in_specs,
        out_specs=_route_out_specs(lambda t: first + t, ODD_TILES),
        out_shape=_ROUTE_OUT_SHAPES,
        input_output_aliases=aliases,
        compiler_params=pltpu.CompilerParams(dimension_semantics=("arbitrary",),
                                             vmem_limit_bytes=VMEM_LIMIT),
        name="odd_layer_ctx" if group == 0 else "odd_layer_lat",
    )(*args)


def _slot_kernel(ei_ref, base_ref, tri_ref, o_ref):
    expert = lax.broadcasted_iota(I32, (N_EXPERTS, TM), 0)
    krow = lax.broadcasted_iota(I32, (SUBLANES, TM), 0)
    tri = tri_ref[...]
    for i in range(SLOT_TILES):
        ei = ei_ref[SUBLANES * i:SUBLANES * (i + 1), :]
        onehots = [expert == ei[k:k + 1, :] for k in range(TOP_K)]
        member = functools.reduce(lambda u, w: u + w, [o.astype(F32) for o in onehots])
        before = _dot(member.astype(BF16), tri) + base_ref[N_EXPERTS * i:N_EXPERTS * (i + 1), :]
        slots = jnp.zeros((SUBLANES, TM), F32)
        for k in range(TOP_K):
            s = jnp.sum(jnp.where(onehots[k], before, 0.0), axis=0, keepdims=True)
            slots = slots + jnp.where(krow == k, s, 0.0)
        o_ref[i] = slots.astype(I32)


SLOT_TILES = 4


def _slots(eidx, tile_base, tri):
    return pl.pallas_call(
        _slot_kernel,
        grid=(N_TILES // SLOT_TILES,),
        in_specs=[pl.BlockSpec((SLOT_TILES * SUBLANES, TM), lambda t: (t, 0)),
                  pl.BlockSpec((SLOT_TILES * N_EXPERTS, TM), lambda t: (t, 0)),
                  _full_spec((TM, TM))],
        out_specs=pl.BlockSpec((SLOT_TILES, SUBLANES, TM), lambda t: (t, 0, 0)),
        out_shape=jax.ShapeDtypeStruct((N_TILES, SUBLANES, TM), I32),
        compiler_params=pltpu.CompilerParams(dimension_semantics=("arbitrary",)),
        name="moe_slots",
    )(eidx, tile_base, tri)


def _sc_mesh():
    return plsc.VectorSubcoreMesh(core_axis_name="c", subcore_axis_name="s")


def _sc_worker():
    return lax.axis_index("s") * SC_CORES + lax.axis_index("c")


def _dispatch(hp, idx):
    n_chunks = N_TOK // SC_WORKERS // SC_ROWS

    @functools.partial(
        pl.kernel, mesh=_sc_mesh(),
        out_type=jax.ShapeDtypeStruct((N_SLOTS, HALF), U32),
        scratch_types=[pltpu.VMEM((n_chunks, TOP_K, SC_ROWS), I32), pltpu.VMEM((SC_BUFS, SC_ROWS, HALF), U32),
                       pltpu.SemaphoreType.DMA((SC_BUFS,)), pltpu.SemaphoreType.DMA((SC_BUFS,))],
        name="moe_dispatch",
    )
    def k(x_hbm, idx_hbm, out_hbm, idx_v, rows_v, read_sem, scat_sem):
        wid = _sc_worker()
        base = wid * n_chunks
        pltpu.sync_copy(idx_hbm.at[wid], idx_v)

        def read(j, b):
            return pltpu.make_async_copy(x_hbm.at[pl.ds((base + j) * SC_ROWS, SC_ROWS)], rows_v.at[b],
                                         read_sem.at[b])

        def scatter(j, b, kk):
            return pltpu.make_async_copy(rows_v.at[b], out_hbm.at[idx_v.at[j, kk]], scat_sem.at[b])

        def drain(j, b):
            for kk in range(TOP_K):
                scatter(j, b, kk).wait()

        read(0, 0).start()

        @pl.loop(0, n_chunks, step=SC_BUFS)
        def _(j):
            for b in range(SC_BUFS):
                jj = j + b
                other = (b + 1) % SC_BUFS
                read(jj, b).wait()

                @pl.when(jj >= 1)
                def _():
                    drain(jj - 1, other)

                @pl.when(jj + 1 < n_chunks)
                def _():
                    read(jj + 1, other).start()

                for kk in range(TOP_K):
                    scatter(jj, b, kk).start()

        drain(n_chunks - 1, (n_chunks - 1) % SC_BUFS)

    return k(hp, idx)


def _combine_gather(ys, idx):
    n_chunks = idx.shape[1]

    @functools.partial(
        pl.kernel, mesh=_sc_mesh(),
        out_type=jax.ShapeDtypeStruct((SC_WORKERS * n_chunks * SC_ROWS, HALF), U32),
        scratch_types=[pltpu.VMEM((n_chunks, SC_ROWS), I32), pltpu.VMEM((SC_BUFS, SC_ROWS, HALF), U32),
                       pltpu.SemaphoreType.DMA((SC_BUFS,)), pltpu.SemaphoreType.DMA((SC_BUFS,))],
        name="moe_combine",
    )
    def k(ys_hbm, idx_hbm, out_hbm, idx_v, rows_v, gather_sem, write_sem):
        wid = _sc_worker()
        base = wid * n_chunks
        pltpu.sync_copy(idx_hbm.at[wid], idx_v)

        def gather(j, b):
            return pltpu.make_async_copy(ys_hbm.at[idx_v.at[j]], rows_v.at[b], gather_sem.at[b])

        def write(j, b):
            return pltpu.make_async_copy(rows_v.at[b], out_hbm.at[pl.ds((base + j) * SC_ROWS, SC_ROWS)],
                                         write_sem.at[b])

        gather(0, 0).start()

        @pl.loop(0, n_chunks, step=SC_BUFS)
        def _(j):
            for b in range(SC_BUFS):
                jj = j + b
                other = (b + 1) % SC_BUFS
                gather(jj, b).wait()

                @pl.when(jj >= 1)
                def _():
                    write(jj - 1, other).wait()

                @pl.when(jj + 1 < n_chunks)
                def _():
                    gather(jj + 1, other).start()

                write(jj, b).start()

        write(n_chunks - 1, (n_chunks - 1) % SC_BUFS).wait()

    return k(ys, idx)


N_MATS = 3


def _moe_kernel(layer, first_ref, nblk_ref, xs_hbm, wg_hbm, wu_hbm, wd_hbm, bg_ref, bu_ref, bd_ref,
                ys_hbm, wf32, wbf, xbuf, obuf, wsem, xsem, osem):
    e = pl.program_id(0)
    slot = e % 2
    w_hbm = (wg_hbm, wu_hbm, wd_hbm)
    nb = nblk_ref[e]
    row0 = first_ref[e] * MOE_BLOCK
    n_chunks = nb // BLOCKS_PER_CHUNK
    has_a = nb % BLOCKS_PER_CHUNK >= 2
    has_b = nb % 2 == 1
    row_a = row0 + n_chunks * MOE_CHUNK
    row_b = row_a + jnp.where(has_a, TAIL_A, 0)
    slot_a = n_chunks % 2
    slot_b = (n_chunks + has_a.astype(I32)) % 2
    last_chunk = (row_a - MOE_CHUNK, MOE_CHUNK, 1 - slot_a)

    def w_copy(ee, s, m):
        return pltpu.make_async_copy(w_hbm[m].at[layer, ee], wf32.at[s, m], wsem.at[s, m])

    def x_copy(row, rows, s):
        return pltpu.make_async_copy(xs_hbm.at[pl.ds(row, rows)], xbuf.at[s, pl.ds(0, rows)], xsem.at[s])

    def o_copy(row, rows, s):
        return pltpu.make_async_copy(obuf.at[s, pl.ds(0, rows)], ys_hbm.at[pl.ds(row, rows)], osem.at[s])

    def expert(xp):
        lo, hi = _unpack_halves(xp)
        xb = jnp.concatenate([lo.astype(BF16), hi.astype(BF16)], axis=1)
        gt = jnp.minimum(_dot(xb, wbf[0]) + bg_ref[...], SWIGLU_LIMIT)
        up = jnp.clip(_dot(xb, wbf[1]) + bu_ref[...], -SWIGLU_LIMIT, SWIGLU_LIMIT)
        glu = gt * jax.nn.sigmoid(SWIGLU_ALPHA * gt)
        hmid = ((up + 1.0) * glu).astype(BF16)
        out = _dot(hmid, wbf[2]) + bd_ref[...]
        return _pack_halves(out.astype(BF16).astype(F32))

    @pl.when(e == 0)
    def _():
        for m in range(N_MATS):
            w_copy(0, 0, m).start()

    @pl.when(n_chunks > 0)
    def _():
        x_copy(row0, MOE_CHUNK, 0).start()

    @pl.when((n_chunks == 0) & has_a)
    def _():
        x_copy(row0, TAIL_A, 0).start()

    @pl.when((n_chunks == 0) & jnp.logical_not(has_a) & has_b)
    def _():
        x_copy(row0, TAIL_B, 0).start()

    for m in range(N_MATS):
        w_copy(e, slot, m).wait()

    @pl.when(e + 1 < N_EXPERTS)
    def _():
        for m in range(N_MATS):
            w_copy(e + 1, 1 - slot, m).start()

    for m in range(N_MATS):
        wbf[m] = wf32[slot, m].astype(BF16)

    def chunk(c, carry):
        s = c % 2
        row = row0 + c * MOE_CHUNK
        x_copy(row, MOE_CHUNK, s).wait()

        @pl.when(c + 1 < n_chunks)
        def _():
            x_copy(row + MOE_CHUNK, MOE_CHUNK, 1 - s).start()

        @pl.when((c + 1 == n_chunks) & has_a)
        def _():
            x_copy(row_a, TAIL_A, 1 - s).start()

        @pl.when((c + 1 == n_chunks) & jnp.logical_not(has_a) & has_b)
        def _():
            x_copy(row_b, TAIL_B, 1 - s).start()

        obuf[s] = expert(xbuf[s])

        @pl.when(c >= 1)
        def _():
            o_copy(row - MOE_CHUNK, MOE_CHUNK, 1 - s).wait()

        o_copy(row, MOE_CHUNK, s).start()
        return carry

    lax.fori_loop(0, n_chunks, chunk, 0)

    @pl.when(has_a)
    def _():
        x_copy(row_a, TAIL_A, slot_a).wait()

        @pl.when(has_b)
        def _():
            x_copy(row_b, TAIL_B, 1 - slot_a).start()

        obuf[slot_a, 0:TAIL_A, :] = expert(xbuf[slot_a, 0:TAIL_A, :])

        @pl.when(n_chunks >= 1)
        def _():
            o_copy(*last_chunk).wait()

        o_copy(row_a, TAIL_A, slot_a).start()

    @pl.when(has_b)
    def _():
        x_copy(row_b, TAIL_B, slot_b).wait()
        obuf[slot_b, 0:TAIL_B, :] = expert(xbuf[slot_b, 0:TAIL_B, :])

        @pl.when(has_a)
        def _():
            o_copy(row_a, TAIL_A, slot_a).wait()

        @pl.when(jnp.logical_not(has_a) & (n_chunks >= 1))
        def _():
            o_copy(*last_chunk).wait()

        o_copy(row_b, TAIL_B, slot_b).start()
        o_copy(row_b, TAIL_B, slot_b).wait()

    @pl.when(jnp.logical_not(has_b) & has_a)
    def _():
        o_copy(row_a, TAIL_A, slot_a).wait()

    @pl.when(jnp.logical_not(has_b) & jnp.logical_not(has_a) & (n_chunks >= 1))
    def _():
        o_copy(*last_chunk).wait()


def _moe_experts(layer, xs, first_block, n_blocks, w_gate, b_gate, w_up, b_up, w_down, b_down):
    hbm = pl.BlockSpec(memory_space=pl.ANY)
    bspec = pl.BlockSpec((None, None, 1, D_MODEL), lambda e, fb, nb: (layer, e, 0, 0))
    bias = lambda b: b.reshape(DEPTH, N_EXPERTS, 1, D_MODEL)
    return pl.pallas_call(
        functools.partial(_moe_kernel, layer),
        grid_spec=pltpu.PrefetchScalarGridSpec(
            num_scalar_prefetch=2,
            grid=(N_EXPERTS,),
            in_specs=[hbm, hbm, hbm, hbm, bspec, bspec, bspec],
            out_specs=hbm,
            scratch_shapes=[
                pltpu.VMEM((2, N_MATS, D_MODEL, D_MODEL), F32),
                pltpu.VMEM((N_MATS, D_MODEL, D_MODEL), BF16),
                pltpu.VMEM((2, MOE_CHUNK, HALF), U32),
                pltpu.VMEM((2, MOE_CHUNK, HALF), U32),
                pltpu.SemaphoreType.DMA((2, N_MATS)),
                pltpu.SemaphoreType.DMA((2,)),
                pltpu.SemaphoreType.DMA((2,)),
            ],
        ),
        out_shape=jax.ShapeDtypeStruct((N_SLOTS, HALF), U32),
        compiler_params=pltpu.CompilerParams(dimension_semantics=("arbitrary",),
                                             vmem_limit_bytes=VMEM_LIMIT),
        name="moe_experts",
    )(first_block, n_blocks, xs, w_gate, w_up, w_down, bias(b_gate), bias(b_up), bias(b_down))


def _moe(layer, hp, eidx, cnt, tri, w_gate, b_gate, w_up, b_up, w_down, b_down):
    tile_cnt = cnt[:, 0].reshape(N_TILES, N_EXPERTS).astype(I32)
    counts = jnp.sum(tile_cnt, axis=0)
    padded = (counts + MOE_BLOCK - 1) // MOE_BLOCK * MOE_BLOCK
    pad_ends = jnp.cumsum(padded)
    pad_starts = pad_ends - padded
    tile_base = pad_starts[None, :] + jnp.cumsum(tile_cnt, axis=0) - tile_cnt
    tile_base = jnp.broadcast_to(tile_base.astype(F32).reshape(N_TILES * N_EXPERTS, 1),
                                 (N_TILES * N_EXPERTS, TM))
    first_block = (pad_starts // MOE_BLOCK).astype(I32)
    n_blocks = (padded // MOE_BLOCK).astype(I32)

    slots = _slots(eidx, tile_base, tri)[:, :TOP_K, :]
    per_tile = TM // SC_ROWS
    d_idx = slots.reshape(N_TILES, TOP_K, per_tile, SC_ROWS).transpose(0, 2, 1, 3)
    d_idx = d_idx.reshape(SC_WORKERS, N_TOK // SC_WORKERS // SC_ROWS, TOP_K, SC_ROWS)

    xs = _dispatch(hp, d_idx)
    ys = _moe_experts(layer, xs, first_block, n_blocks, w_gate, b_gate, w_up, b_up, w_down, b_down)
    gathered = []
    for tiles in (slots[:CTX_TILES], slots[CTX_TILES:]):
        n_tok = tiles.shape[0] * TM
        c_idx = tiles.transpose(1, 0, 2).reshape(SC_WORKERS, n_tok * TOP_K // SC_WORKERS // SC_ROWS, SC_ROWS)
        gathered.append(_combine_gather(ys, c_idx).reshape(TOP_K, n_tok, HALF))
    return gathered


FINAL_TILES = 2


def _final_kernel(x_ref, *refs):
    o_ref = refs[N_PREV]
    o_ref[...] = _apply_prev(x_ref[...], refs[:N_PREV])


def _final_residual(group, x, prev, mods):
    rows = FINAL_TILES * TM
    first, n_blocks = _group_blocks(group, FINAL_TILES)
    yg, gates = prev
    return pl.pallas_call(
        _final_kernel,
        grid=(n_blocks,),
        in_specs=[pl.BlockSpec((rows, D_MODEL), lambda t: (first + t, 0))]
        + _prev_specs(DEPTH, first, FINAL_TILES),
        out_specs=pl.BlockSpec((rows, D_MODEL), lambda t: (t, 0)),
        out_shape=jax.ShapeDtypeStruct((n_blocks * rows, D_MODEL), F32),
        compiler_params=pltpu.CompilerParams(dimension_semantics=("arbitrary",)),
        name="final_residual",
    )(x, *([yg] * TOP_K), gates, mods)


def _dft_pair(n):
    idx = np.arange(n)
    ang = 2.0 * np.pi * ((idx[:, None] * idx[None, :]) % n) / n
    return np.cos(ang) / np.sqrt(n), np.sin(ang) / np.sqrt(n)


def _rope_tables():
    lane = np.arange(GROUP)
    within = lane % QK_DIM
    axis = within // 32
    e = within % 32
    inv16 = ROPE_BASE ** (-jnp.arange(16, dtype=F32) / 16)
    pos = np.arange(LAT_SEQ)
    coord = np.where(axis[None, :] == 0, (pos // GRID_W)[:, None], (pos % GRID_W)[:, None])
    ang = jnp.asarray(coord, F32) * inv16[e % 16][None, :]
    first = jnp.asarray((e // 16) == 0)[None, :]
    cos, sin = jnp.cos(ang), jnp.sin(ang)
    return cos, jnp.where(first, -sin, 0.0), jnp.where(first, 0.0, sin)


def kernel(x_prompt, x_sample, cache_k, cache_v, c, c_ctx, ada_w, ada_b, norm_mix_g, norm_ffn_g,
           w_in_even, w_out_even, q_norm_g, k_norm_g, lambda_q, lambda_k, subln_g,
           w_in_odd, v_norm_g, w_spatial, b_spatial, w_out_odd,
           router_w, router_b, w_gate, b_gate, w_up, b_up, w_down, b_down):
    n_even = w_in_even.shape[0]
    x_groups = (x_prompt.reshape(N_CTX, D_MODEL), x_sample.reshape(N_LAT, D_MODEL))
    x = None
    cond = jnp.zeros((COND_ROWS, D_MODEL), F32).at[0].set(c_ctx).at[1:1 + LAT_BATCH].set(c)
    mods = _ada_modulation(cond, ada_w, ada_b).reshape(DEPTH, N_MOD, COND_ROWS, 1, D_MODEL)

    dft_ctx = [jnp.asarray(m, F32).astype(BF16) for m in _dft_pair(CTX_SEQ)]
    dft_lat = [jnp.asarray(m, F32).astype(BF16) for m in _dft_pair(LAT_SEQ)]
    c128, s128 = _dft_pair(GROUP)
    cs128 = jnp.asarray(np.concatenate([c128, -s128], axis=0), F32).astype(BF16)
    grp = np.arange(HALF) // QK_DIM
    blockdiag = jnp.asarray(grp[:, None] == grp[None, :], F32).astype(BF16)
    tri = jnp.asarray(np.arange(TM)[:, None] < np.arange(TM)[None, :], F32).astype(BF16)
    rope_tabs = _rope_tables()
    cache_k2 = cache_k.reshape(LAT_BATCH, n_even, PAST_LEN, HALF)
    cache_v2 = cache_v.reshape(LAT_BATCH, n_even, PAST_LEN, HALF)

    rw = jnp.pad(router_w, ((0, 0), (0, 0), (0, LANES - N_EXPERTS)))
    rw_hi = rw.astype(BF16)
    rw_lo = (rw - rw_hi.astype(F32)).astype(BF16)
    rb = jnp.pad(router_b, ((0, 0), (0, LANES - N_EXPERTS)))[:, None, :]

    prev = None
    caches = None
    for l in range(DEPTH):
        j = l // 2
        gmix = norm_mix_g[l][None, :]
        gffn = norm_ffn_g[l][None, :]
        if l % 2 == 0:
            proj_args = (mods, gmix, w_in_even[j].astype(BF16), blockdiag,
                         jnp.tile(q_norm_g[j], HALF // QK_DIM)[None, :],
                         jnp.tile(k_norm_g[j], HALF // QK_DIM)[None, :], rope_tabs)
            outs = None
            for g in range(2):
                x_in = x_groups[g] if l == 0 else x
                outs_g = _even_proj(l, g, x_in, l == 0, None if prev is None else (prev[0][g], prev[1]),
                                    outs, caches, *proj_args)
                if g == 0:
                    caches = outs_g[N_PROJ_OUT:]
                outs = outs_g[:N_PROJ_OUT]
            x, a, q, k, v = outs
            common = (cs128, lambda_q[j], lambda_k[j], subln_g[j][None, :], w_out_even[j].astype(BF16),
                      mods, gffn, rw_hi[l], rw_lo[l], rb[l])
            x, hp, gates, eidx, cnt = _even_mix(l, x, a, q, k, v, cache_k2, cache_v2, dft_ctx, dft_lat, *common)
        else:
            b_s = jnp.broadcast_to(jnp.transpose(b_spatial[j])[:, :, None],
                                   (CHUNK, C_GROUPS, GROUP)).reshape(CHUNK, D_MODEL)
            odd_args = (mods, gmix, w_in_odd[j].astype(BF16), v_norm_g[j][None, :],
                        w_spatial[j].astype(BF16), b_s, w_out_odd[j].astype(BF16), gffn, rw_hi[l], rw_lo[l], rb[l])
            outs = None
            for g in range(2):
                outs = _odd_layer(l, g, x, None if prev is None else (prev[0][g], prev[1]), outs, *odd_args)
            x, hp, gates, eidx, cnt = outs
        yg = _moe(l, hp, eidx, cnt, tri, w_gate, b_gate, w_up, b_up, w_down, b_down)
        prev = (yg, gates)
    y_ctx = _final_residual(0, x, (prev[0][0], prev[1]), mods)
    y_lat = _final_residual(1, x, (prev[0][1], prev[1]), mods)
    return (y_ctx.reshape(CTX_BATCH, CTX_SEQ, D_MODEL),
            y_lat.reshape(LAT_BATCH, LAT_SEQ, D_MODEL),
            caches[0].reshape(CTX_BATCH, N_EVEN, CTX_SEQ, N_GROUPS, 2, QK_DIM),
            caches[1].reshape(CTX_BATCH, N_EVEN, CTX_SEQ, N_GROUPS, GROUP))
```

```python
import functools
import math

import numpy as np
import jax
import jax.numpy as jnp
from jax import lax
from jax.experimental import pallas as pl
from jax.experimental.pallas import tpu as pltpu
from jax.experimental.pallas import tpu_sc as plsc

F32 = jnp.float32
BF16 = jnp.bfloat16
U32 = jnp.uint32
I32 = jnp.int32

D_MODEL = 1024
DEPTH = 4
N_EVEN = (DEPTH + 1) // 2
N_MOD = 6
EPS = 1e-6
CTX_BATCH, CTX_SEQ = 32, 256
LAT_BATCH, LAT_SEQ = 8, 1024
PAST_LEN = 512
GRID_W = 64
N_CTX = CTX_BATCH * CTX_SEQ
N_LAT = LAT_BATCH * LAT_SEQ
N_TOK = N_CTX + N_LAT
TM = 256
N_TILES = N_TOK // TM
CTX_TILES = N_CTX // TM
LAT_TILES_PER_BATCH = LAT_SEQ // TM
COND_ROWS = 16
GROUP = 128
N_GROUPS = 4
QK_DIM = 64
HALF = 512
CHUNK = 128
C_GROUPS = 8
N_EXPERTS = 32
TOP_K = 4
LANES = 128
SUBLANES = 8
SWIGLU_LIMIT = 7.0
SWIGLU_ALPHA = 1.702
MOE_BLOCK = 128
BLOCKS_PER_CHUNK = 4
MOE_CHUNK = BLOCKS_PER_CHUNK * MOE_BLOCK
TAIL_A = 2 * MOE_BLOCK
TAIL_B = MOE_BLOCK
N_ASSIGN = N_TOK * TOP_K
MOE_BLOCKS = N_ASSIGN // MOE_BLOCK + N_EXPERTS
N_SLOTS = MOE_BLOCKS * MOE_BLOCK
ROPE_BASE = 10000.0
VMEM_LIMIT = 52 * 1024 * 1024
SC_CORES = 2
SC_SUBCORES = 16
SC_WORKERS = SC_CORES * SC_SUBCORES
SC_ROWS = 64
SC_BUFS = 2
HI_MASK = 0xFFFF0000
LOG2E = math.log2(math.e)


def _dot(a, b):
    return jnp.dot(a, b, preferred_element_type=F32)


def _rms(x, g):
    return x * lax.rsqrt(jnp.mean(x * x, axis=-1, keepdims=True) + EPS) * g


def _pack_halves(xf):
    b = lax.bitcast_convert_type(xf, U32)
    return (b[:, :HALF] >> 16) | (b[:, HALF:] & jnp.uint32(HI_MASK))


def _unpack_halves(w):
    lo = lax.bitcast_convert_type(w << 16, F32)
    hi = lax.bitcast_convert_type(w & jnp.uint32(HI_MASK), F32)
    return lo, hi


def _tile_mod_row(t):
    return jnp.where(t < CTX_TILES, 0, 1 + (t - CTX_TILES) // LAT_TILES_PER_BATCH)


def _mod_spec(layer, k, row_fn):
    return pl.BlockSpec((None, None, None, 1, D_MODEL),
                        lambda *g: (layer, k, row_fn(*g), 0, 0))


def _full_spec(shape):
    return pl.BlockSpec(shape, lambda *g: (0,) * len(shape))


def _ada_kernel(cond_ref, w_ref, b_ref, o_ref):
    c = cond_ref[...]
    s = (c * jax.nn.sigmoid(c)).astype(BF16)
    o_ref[...] = _dot(s, w_ref[...].astype(BF16)) + b_ref[...]


def _ada_modulation(cond, ada_w, ada_b):
    return pl.pallas_call(
        _ada_kernel,
        grid=(DEPTH, N_MOD),
        in_specs=[
            _full_spec((COND_ROWS, D_MODEL)),
            pl.BlockSpec((None, D_MODEL, D_MODEL), lambda l, n: (l, 0, n)),
            pl.BlockSpec((None, None, 1, D_MODEL), lambda l, n: (l, n, 0, 0)),
        ],
        out_specs=pl.BlockSpec((None, None, COND_ROWS, D_MODEL), lambda l, n: (l, n, 0, 0)),
        out_shape=jax.ShapeDtypeStruct((DEPTH, N_MOD, COND_ROWS, D_MODEL), F32),
        compiler_params=pltpu.CompilerParams(dimension_semantics=("arbitrary", "arbitrary"),
                                             vmem_limit_bytes=VMEM_LIMIT),
        name="ada_modulation",
    )(cond, ada_w, ada_b.reshape(DEPTH, N_MOD, 1, D_MODEL))


N_PREV = 6


def _group_blocks(group, tiles_per_block):
    ctx = CTX_TILES // tiles_per_block
    return (0, ctx) if group == 0 else (ctx, N_TILES // tiles_per_block - ctx)


def _prev_specs(layer, first_block, tiles_per_block):
    rows = tiles_per_block * TM
    ys = [pl.BlockSpec((None, rows, HALF), functools.partial(lambda k, t: (k, t, 0), k))
          for k in range(TOP_K)]
    return ys + [pl.BlockSpec((rows, LANES), lambda t: (first_block + t, 0)),
                 _mod_spec(layer - 1, 5, lambda t: _tile_mod_row((first_block + t) * tiles_per_block))]


def _apply_prev(x, prev_refs):
    y_refs, gt_ref, gate_ref = prev_refs[:TOP_K], prev_refs[TOP_K], prev_refs[TOP_K + 1]
    gt = gt_ref[...]
    acc_lo = acc_hi = None
    for k in range(TOP_K):
        lo, hi = _unpack_halves(y_refs[k][...])
        g = gt[:, k:k + 1]
        acc_lo = g * lo if acc_lo is None else acc_lo + g * lo
        acc_hi = g * hi if acc_hi is None else acc_hi + g * hi
    return x + gate_ref[...] * jnp.concatenate([acc_lo, acc_hi], axis=1)


N_ROUTE_OUT = 5


def _route_out_specs(tile_fn, tiles_per_block=1):
    tile = lambda w: pl.BlockSpec((tiles_per_block * TM, w), lambda *g: (tile_fn(*g), 0))
    return [tile(D_MODEL), tile(HALF), tile(LANES),
            pl.BlockSpec((tiles_per_block * SUBLANES, TM), lambda *g: (tile_fn(*g), 0)),
            pl.BlockSpec((tiles_per_block * N_EXPERTS, LANES), lambda *g: (tile_fn(*g), 0))]


_ROUTE_OUT_SHAPES = [
    jax.ShapeDtypeStruct((N_TOK, D_MODEL), F32),
    jax.ShapeDtypeStruct((N_TOK, HALF), U32),
    jax.ShapeDtypeStruct((N_TOK, LANES), F32),
    jax.ShapeDtypeStruct((N_TILES * SUBLANES, TM), I32),
    jax.ShapeDtypeStruct((N_TILES * N_EXPERTS, LANES), F32),
]


def _post_mixer(x, mix, gate_ref, gffn_ref, shift_ref, scale_ref, rwhi_ref, rwlo_ref, rb_ref,
                xo_ref, hp_ref, gt_ref, ei_ref, cnt_ref):
    xn = x + gate_ref[...] * mix
    xo_ref[...] = xn
    h2 = _rms(xn, gffn_ref[...]) * (1.0 + scale_ref[...]) + shift_ref[...]
    hb = h2.astype(BF16)
    hp_ref[...] = _pack_halves(hb.astype(F32))
    both = _dot(hb, jnp.concatenate([rwhi_ref[...], rwlo_ref[...]], axis=1))
    logits = both[:, :LANES] + both[:, LANES:] + rb_ref[...]

    rows = x.shape[0]
    work = logits.T[0:N_EXPERTS, :]
    expert = lax.broadcasted_iota(I32, (N_EXPERTS, rows), 0).astype(F32)
    krow = lax.broadcasted_iota(I32, (SUBLANES, rows), 0)
    member = jnp.zeros((N_EXPERTS, rows), F32)
    gates = jnp.zeros((SUBLANES, rows), F32)
    ids = jnp.zeros((SUBLANES, rows), F32)
    den = None
    top = None
    for k in range(TOP_K):
        m = jnp.max(work, axis=0, keepdims=True)
        idx = jnp.min(jnp.where(work == m, expert, float(N_EXPERTS)), axis=0, keepdims=True)
        onehot = expert == idx
        work = jnp.where(onehot, -jnp.inf, work)
        member = member + onehot.astype(F32)
        if k == 0:
            top = m
        e = jnp.exp(m - top)
        den = e if den is None else den + e
        gates = gates + jnp.where(krow == k, e, 0.0)
        ids = ids + jnp.where(krow == k, idx, 0.0)
    gates = gates * (1.0 / den)
    gt_ref[...] = jnp.concatenate([gates, jnp.zeros((LANES - SUBLANES, rows), F32)], axis=0).T
    ids = ids.astype(I32)
    for i in range(rows // TM):
        ei_ref[SUBLANES * i:SUBLANES * (i + 1), :] = ids[:, TM * i:TM * (i + 1)]
        cnt_ref[N_EXPERTS * i:N_EXPERTS * (i + 1), :] = jnp.broadcast_to(
            jnp.sum(member[:, TM * i:TM * (i + 1)], axis=1, keepdims=True), (N_EXPERTS, LANES))


def _rope(x, c, sa, sb):
    outs = []
    for h in range(N_GROUPS):
        xs = x[:, GROUP * h:GROUP * (h + 1)]
        outs.append(xs * c + pltpu.roll(xs, GROUP - 16, 1) * sa + pltpu.roll(xs, 16, 1) * sb)
    return jnp.concatenate(outs, axis=1)


PROJ_TILES = 2


N_PROJ_OUT = 5


def _even_proj_kernel(group, has_prev, n_aliased, *refs):
    x_ref = refs[0]
    refs = refs[1:]
    if has_prev:
        prev_refs = refs[:N_PREV]
        refs = refs[N_PREV:]
    (g_ref, shift_ref, scale_ref, w_ref, bd_ref, qg_ref, kg_ref, cos_ref, sa_ref, sb_ref) = refs[:10]
    outs = refs[10 + n_aliased:]
    xo_ref, a_ref, q_ref, k_ref, v_ref = outs[:N_PROJ_OUT]

    x = x_ref[...]
    if has_prev:
        x = _apply_prev(x, prev_refs)
    xo_ref[...] = x
    h = (_rms(x, g_ref[...]) * (1.0 + scale_ref[...]) + shift_ref[...]).astype(BF16)

    a_ref[...] = _dot(h, w_ref[:, 0:HALF]).astype(BF16)
    zv = _dot(h, w_ref[:, 3 * HALF:4 * HALF])
    v_ref[...] = zv.astype(BF16)

    bd = bd_ref[...]

    def qk_norm(z, gain):
        ssq = _dot((z * z).astype(BF16), bd)
        return z * lax.rsqrt(ssq * (1.0 / QK_DIM) + EPS) * gain

    qn = qk_norm(_dot(h, w_ref[:, HALF:2 * HALF]), qg_ref[...]) * (QK_DIM ** -0.5 * LOG2E)
    kn = qk_norm(_dot(h, w_ref[:, 2 * HALF:3 * HALF]), kg_ref[...])

    if group == 0:
        kf_ref, vf_ref = outs[N_PROJ_OUT:]
        kf_ref[...] = kn.reshape(kf_ref.shape)
        vf_ref[...] = zv.reshape(vf_ref.shape)
        q_ref[...] = qn.astype(BF16)
        k_ref[...] = kn.astype(BF16)
    else:
        c, sa, sb = cos_ref[...], sa_ref[...], sb_ref[...]
        q_ref[...] = _rope(qn, c, sa, sb).astype(BF16)
        k_ref[...] = _rope(kn, c, sa, sb).astype(BF16)


def _even_proj(layer, group, x, x_is_group_local, prev, ctx_outs, caches, mods, norm_g, w_in, blockdiag,
               qg, kg, rope_tabs):
    has_prev = prev is not None
    rows = PROJ_TILES * TM
    first, n_blocks = _group_blocks(group, PROJ_TILES)
    mod_row = lambda t: _tile_mod_row((first + t) * PROJ_TILES)
    glob = lambda w: pl.BlockSpec((rows, w), lambda t: (first + t, 0))
    local = lambda w: pl.BlockSpec((rows, w), lambda t: (t, 0))
    rope_spec = pl.BlockSpec((rows, GROUP), lambda t: (t % (LAT_SEQ // rows) if group == 1 else 0, 0))
    in_specs = [local(D_MODEL) if x_is_group_local else glob(D_MODEL)]
    args = [x]
    if has_prev:
        yg, gates = prev
        in_specs += _prev_specs(layer, first, PROJ_TILES)
        args += [yg] * TOP_K + [gates, mods]
    in_specs += [
        _full_spec((1, D_MODEL)),
        _mod_spec(layer, 0, mod_row), _mod_spec(layer, 1, mod_row),
        _full_spec((D_MODEL, 4 * HALF)), _full_spec((HALF, HALF)),
        _full_spec((1, HALF)), _full_spec((1, HALF)),
        rope_spec, rope_spec, rope_spec,
    ]
    args += [norm_g, mods, mods, w_in, blockdiag, qg, kg, *rope_tabs]
    out_specs = [glob(D_MODEL)] + [glob(HALF)] * 4
    out_shape = [jax.ShapeDtypeStruct((N_TOK, D_MODEL), F32)] + [jax.ShapeDtypeStruct((N_TOK, HALF), BF16)] * 4
    aliased = []
    if group == 0:
        seqs = rows // CTX_SEQ
        cache_spec = pl.BlockSpec((seqs, None, CTX_SEQ, HALF), lambda t: (t, layer // 2, 0, 0))
        out_specs += [cache_spec] * 2
        out_shape += [jax.ShapeDtypeStruct((CTX_BATCH, N_EVEN, CTX_SEQ, HALF), F32)] * 2
        if caches is not None:
            aliased = [(caches[i], N_PROJ_OUT + i) for i in range(2)]
    else:
        aliased = [(ctx_outs[i], i) for i in range(N_PROJ_OUT)]
    aliases = {len(args) + i: out_idx for i, (_, out_idx) in enumerate(aliased)}
    in_specs += [pl.BlockSpec(memory_space=pl.ANY)] * len(aliased)
    args += [arr for arr, _ in aliased]
    return pl.pallas_call(
        functools.partial(_even_proj_kernel, group, has_prev, len(aliased)),
        grid=(n_blocks,),
        in_specs=in_specs, out_specs=out_specs, out_shape=out_shape,
        input_output_aliases=aliases,
        compiler_params=pltpu.CompilerParams(dimension_semantics=("arbitrary",),
                                             vmem_limit_bytes=VMEM_LIMIT),
        name="even_proj_ctx" if group == 0 else "even_proj_lat",
    )(*args)


MIX_TILES = 2
MIX_ROWS = MIX_TILES * TM


def _even_mix_kernel(lambda_init, *refs):
    (x_ref, q_ref, ac_ref, kc_ref, vc_ref, al_ref, kl_ref, vl_ref, ck_ref, cv_ref,
     cnc_ref, snc_ref, cnl_ref, snl_ref) = refs[:14]
    shared = refs[14:]
    t = pl.program_id(0)
    everything = slice(None)

    @pl.when(t < CTX_TILES // MIX_TILES)
    def _():
        seqs = []
        for s in range(MIX_ROWS // CTX_SEQ):
            rows = slice(CTX_SEQ * s, CTX_SEQ * (s + 1))
            seqs.append((rows, ac_ref, rows, [(kc_ref, vc_ref, rows, False)], cnc_ref, snc_ref))
        _even_mix_body(lambda_init, x_ref, q_ref, seqs, *shared)

    @pl.when(t >= CTX_TILES // MIX_TILES)
    def _():
        seqs = [(everything, al_ref, everything,
                 [(kl_ref, vl_ref, everything, False), (ck_ref, cv_ref, everything, True)], cnl_ref, snl_ref)]
        _even_mix_body(lambda_init, x_ref, q_ref, seqs, *shared)


def _even_mix_body(lambda_init, x_ref, q_ref, seqs, *refs):
    (cs_ref, lq_ref, lk_ref, sg_ref, wo_ref,
     gate_ref, gffn_ref, shift_ref, scale_ref, rwhi_ref, rwlo_ref, rb_ref) = refs[:12]
    out_refs = refs[12:]
    d = jnp.sum(lq_ref[...] * lk_ref[...], axis=1, keepdims=True)
    ed = jnp.exp(d)
    lam = ed[0:1, :] - ed[1:2, :] + lambda_init
    mixed = [_mix_sequence(lambda_init, lam, q_ref, cs_ref, sg_ref, *seq) for seq in seqs]
    mix = _dot(mixed[0] if len(mixed) == 1 else jnp.concatenate(mixed, axis=0), wo_ref[...])
    _post_mixer(x_ref[...], mix, gate_ref, gffn_ref, shift_ref, scale_ref, rwhi_ref, rwlo_ref, rb_ref,
                *out_refs)


def _mix_sequence(lambda_init, lam, q_ref, cs_ref, sg_ref, q_rows, a_ref, a_rows, kv_refs, cn_ref, sn_ref):
    a = a_ref[a_rows, :]
    y1 = _dot(cn_ref[...], a).astype(BF16)
    y2 = _dot(sn_ref[...], a).astype(BF16)
    cs = cs_ref[...]
    pieces = []
    for g in range(N_GROUPS):
        sl = slice(GROUP * g, GROUP * (g + 1))
        pieces.append(_dot(jnp.concatenate([y1[:, sl], y2[:, sl]], axis=1), cs).astype(BF16))

    lane = lax.broadcasted_iota(I32, (y1.shape[0], GROUP), 1)
    nt = (((1,), (1,)), ((), ()))
    for hd in range(N_GROUPS):
        sl = slice(GROUP * hd, GROUP * (hd + 1))
        qf = q_ref[q_rows, sl].astype(F32)
        parts = []
        for k_ref, v_ref, kv_rows, is_f32 in kv_refs:
            kk, vv = k_ref[kv_rows, sl], v_ref[kv_rows, sl]
            parts.append((kk.astype(BF16), vv.astype(BF16)) if is_f32 else (kk, vv))

        def probs(qm):
            ss = [lax.dot_general(qm, kk, nt, preferred_element_type=F32) for kk, _ in parts]
            m = functools.reduce(jnp.maximum, [jnp.max(s, axis=1, keepdims=True) for s in ss])
            es = [jnp.exp2(s - m) for s in ss]
            den = functools.reduce(lambda u, w: u + w, [jnp.sum(e, axis=1, keepdims=True) for e in es])
            return es, 1.0 / den

        es0, inv0 = probs(jnp.where(lane < QK_DIM, qf, 0.0).astype(BF16))
        es1, inv1 = probs(jnp.where(lane >= QK_DIM, qf, 0.0).astype(BF16))
        o0 = functools.reduce(lambda u, w: u + w,
                              [_dot(e.astype(BF16), vv) for e, (_, vv) in zip(es0, parts)])
        o1 = functools.reduce(lambda u, w: u + w,
                              [_dot(e.astype(BF16), vv) for e, (_, vv) in zip(es1, parts)])
        o = o0 * inv0 - o1 * (lam * inv1)
        pieces.append((_rms(o, sg_ref[...]) * (1.0 - lambda_init)).astype(BF16))
    return jnp.concatenate(pieces, axis=1)


def _even_mix(layer, x, a, q, k, v, cache_k, cache_v, dft_ctx, dft_lat, cs128, lam_q, lam_k, subln_g,
              w_out, mods, gffn, rw_hi, rw_lo, rb):
    j = layer // 2
    lambda_init = 0.8 - 0.6 * math.exp(-0.3 * layer)
    ctx_blocks = CTX_TILES // MIX_TILES
    per_seq = LAT_SEQ // MIX_ROWS
    lat = lambda t: jnp.maximum(t - ctx_blocks, 0)
    mod_row = lambda t: _tile_mod_row(t * MIX_TILES)
    tile = lambda w: pl.BlockSpec((MIX_ROWS, w), lambda t: (t, 0))
    ctx_seq = pl.BlockSpec((MIX_ROWS, HALF), lambda t: (jnp.minimum(t, ctx_blocks - 1), 0))
    lat_seq = pl.BlockSpec((LAT_SEQ, HALF), lambda t: (N_CTX // LAT_SEQ + lat(t) // per_seq, 0))
    cache = pl.BlockSpec((None, None, PAST_LEN, HALF), lambda t: (lat(t) // per_seq, j, 0, 0))
    dft_lat_spec = pl.BlockSpec((MIX_ROWS, LAT_SEQ), lambda t: (lat(t) % per_seq, 0))
    in_specs = [
        tile(D_MODEL), tile(HALF), ctx_seq, ctx_seq, ctx_seq, lat_seq, lat_seq, lat_seq, cache, cache,
        _full_spec((CTX_SEQ, CTX_SEQ)), _full_spec((CTX_SEQ, CTX_SEQ)), dft_lat_spec, dft_lat_spec,
        _full_spec((2 * GROUP, GROUP)),
        _full_spec((2, QK_DIM)), _full_spec((2, QK_DIM)), _full_spec((1, GROUP)),
        _full_spec((D_MODEL, D_MODEL)),
        _mod_spec(layer, 2, mod_row), _full_spec((1, D_MODEL)),
        _mod_spec(layer, 3, mod_row), _mod_spec(layer, 4, mod_row),
        _full_spec((D_MODEL, LANES)), _full_spec((D_MODEL, LANES)), _full_spec((1, LANES)),
    ]
    args = [x, q, a, k, v, a, k, v, cache_k, cache_v, dft_ctx[0], dft_ctx[1], dft_lat[0], dft_lat[1],
            cs128, lam_q, lam_k, subln_g, w_out, mods, gffn, mods, mods, rw_hi, rw_lo, rb]
    return pl.pallas_call(
        functools.partial(_even_mix_kernel, lambda_init),
        grid=(N_TILES // MIX_TILES,),
        in_specs=in_specs,
        out_specs=_route_out_specs(lambda t: t, MIX_TILES),
        out_shape=_ROUTE_OUT_SHAPES,
        compiler_params=pltpu.CompilerParams(dimension_semantics=("arbitrary",),
                                             vmem_limit_bytes=VMEM_LIMIT),
        name="even_mix",
    )(*args)


ODD_TILES = 2


def _odd_kernel(group, has_prev, *refs):
    x_ref = refs[0]
    refs = refs[1:]
    if has_prev:
        prev_refs = refs[:N_PREV]
        refs = refs[N_PREV:]
    (g_ref, shift_ref, scale_ref, w_ref, vg_ref, ws_ref, bs_ref, wo_ref,
     gate_ref, gffn_ref, shift2_ref, scale2_ref, rwhi_ref, rwlo_ref, rb_ref) = refs[:15]
    out_refs = refs[15:]
    if group == 1:
        out_refs = out_refs[N_ROUTE_OUT:]

    x = x_ref[...]
    if has_prev:
        x = _apply_prev(x, prev_refs)
    h = (_rms(x, g_ref[...]) * (1.0 + scale_ref[...]) + shift_ref[...]).astype(BF16)

    def gelu(z):
        return 0.5 * z * (1.0 + lax.erf(z * (2.0 ** -0.5)))

    u = gelu(_dot(h, w_ref[:, 0:D_MODEL]))
    vn = _rms(gelu(_dot(h, w_ref[:, D_MODEL:2 * D_MODEL])), vg_ref[...]).astype(BF16)
    bs = bs_ref[...]
    rows = []
    for c in range(x.shape[0] // CHUNK):
        cols = []
        for g in range(C_GROUPS):
            cols.append(_dot(ws_ref[g], vn[CHUNK * c:CHUNK * (c + 1), GROUP * g:GROUP * (g + 1)]))
        rows.append(jnp.concatenate(cols, axis=1) + bs)
    sv = jnp.concatenate(rows, axis=0)
    mix = _dot((u * sv).astype(BF16), wo_ref[...])
    _post_mixer(x, mix, gate_ref, gffn_ref, shift2_ref, scale2_ref, rwhi_ref, rwlo_ref, rb_ref,
                *out_refs)


def _odd_layer(layer, group, x, prev, ctx_outs, mods, norm_g, w_in, v_norm_g, w_s, b_s, w_out, gffn,
               rw_hi, rw_lo, rb):
    has_prev = prev is not None
    first, n_blocks = _group_blocks(group, ODD_TILES)
    mod_row = lambda t: _tile_mod_row((first + t) * ODD_TILES)
    in_specs = [pl.BlockSpec((ODD_TILES * TM, D_MODEL), lambda t: (first + t, 0))]
    args = [x]
    if has_prev:
        yg, gates = prev
        in_specs += _prev_specs(layer, first, ODD_TILES)
        args += [yg] * TOP_K + [gates, mods]
    in_specs += [
        _full_spec((1, D_MODEL)),
        _mod_spec(layer, 0, mod_row), _mod_spec(layer, 1, mod_row),
        _full_spec((D_MODEL, 2 * D_MODEL)), _full_spec((1, D_MODEL)),
        _full_spec((C_GROUPS, CHUNK, CHUNK)), _full_spec((CHUNK, D_MODEL)),
        _full_spec((D_MODEL, D_MODEL)),
        _mod_spec(layer, 2, mod_row), _full_spec((1, D_MODEL)),
        _mod_spec(layer, 3, mod_row), _mod_spec(layer, 4, mod_row),
        _full_spec((D_MODEL, LANES)), _full_spec((D_MODEL, LANES)), _full_spec((1, LANES)),
    ]
    args += [norm_g, mods, mods, w_in, v_norm_g, w_s, b_s, w_out, mods, gffn, mods, mods, rw_hi, rw_lo, rb]
    aliases = {}
    if group == 1:
        aliases = {len(args) + i: i for i in range(N_ROUTE_OUT)}
        in_specs += [pl.BlockSpec(memory_space=pl.ANY)] * N_ROUTE_OUT
        args += list(ctx_outs)
    return pl.pallas_call(
        functools.partial(_odd_kernel, group, has_prev),
        grid=(n_blocks,),
        in_specs=in_specs,
        out_specs=_route_out_specs(lambda t: first + t, ODD_TILES),
        out_shape=_ROUTE_OUT_SHAPES,
        input_output_aliases=aliases,
        compiler_params=pltpu.CompilerParams(dimension_semantics=("arbitrary",),
                                             vmem_limit_bytes=VMEM_LIMIT),
        name="odd_layer_ctx" if group == 0 else "odd_layer_lat",
    )(*args)


def _slot_kernel(ei_ref, base_ref, tri_ref, o_ref):
    expert = lax.broadcasted_iota(I32, (N_EXPERTS, TM), 0)
    krow = lax.broadcasted_iota(I32, (SUBLANES, TM), 0)
    tri = tri_ref[...]
    for i in range(SLOT_TILES):
        ei = ei_ref[SUBLANES * i:SUBLANES * (i + 1), :]
        onehots = [expert == ei[k:k + 1, :] for k in range(TOP_K)]
        member = functools.reduce(lambda u, w: u + w, [o.astype(F32) for o in onehots])
        before = _dot(member.astype(BF16), tri) + base_ref[N_EXPERTS * i:N_EXPERTS * (i + 1), :]
        slots = jnp.zeros((SUBLANES, TM), F32)
        for k in range(TOP_K):
            s = jnp.sum(jnp.where(onehots[k], before, 0.0), axis=0, keepdims=True)
            slots = slots + jnp.where(krow == k, s, 0.0)
        o_ref[i] = slots.astype(I32)


SLOT_TILES = 4


def _slots(eidx, tile_base, tri):
    return pl.pallas_call(
        _slot_kernel,
        grid=(N_TILES // SLOT_TILES,),
        in_specs=[pl.BlockSpec((SLOT_TILES * SUBLANES, TM), lambda t: (t, 0)),
                  pl.BlockSpec((SLOT_TILES * N_EXPERTS, TM), lambda t: (t, 0)),
                  _full_spec((TM, TM))],
        out_specs=pl.BlockSpec((SLOT_TILES, SUBLANES, TM), lambda t: (t, 0, 0)),
        out_shape=jax.ShapeDtypeStruct((N_TILES, SUBLANES, TM), I32),
        compiler_params=pltpu.CompilerParams(dimension_semantics=("arbitrary",)),
        name="moe_slots",
    )(eidx, tile_base, tri)


def _sc_mesh():
    return plsc.VectorSubcoreMesh(core_axis_name="c", subcore_axis_name="s")


def _sc_worker():
    return lax.axis_index("s") * SC_CORES + lax.axis_index("c")


def _dispatch(hp, idx):
    n_chunks = N_TOK // SC_WORKERS // SC_ROWS

    @functools.partial(
        pl.kernel, mesh=_sc_mesh(),
        out_type=jax.ShapeDtypeStruct((N_SLOTS, HALF), U32),
        scratch_types=[pltpu.VMEM((n_chunks, TOP_K, SC_ROWS), I32), pltpu.VMEM((SC_BUFS, SC_ROWS, HALF), U32),
                       pltpu.SemaphoreType.DMA((SC_BUFS,)), pltpu.SemaphoreType.DMA((SC_BUFS,))],
        name="moe_dispatch",
    )
    def k(x_hbm, idx_hbm, out_hbm, idx_v, rows_v, read_sem, scat_sem):
        wid = _sc_worker()
        base = wid * n_chunks
        pltpu.sync_copy(idx_hbm.at[wid], idx_v)

        def read(j, b):
            return pltpu.make_async_copy(x_hbm.at[pl.ds((base + j) * SC_ROWS, SC_ROWS)], rows_v.at[b],
                                         read_sem.at[b])

        def scatter(j, b, kk):
            return pltpu.make_async_copy(rows_v.at[b], out_hbm.at[idx_v.at[j, kk]], scat_sem.at[b])

        def drain(j, b):
            for kk in range(TOP_K):
                scatter(j, b, kk).wait()

        read(0, 0).start()

        @pl.loop(0, n_chunks, step=SC_BUFS)
        def _(j):
            for b in range(SC_BUFS):
                jj = j + b
                other = (b + 1) % SC_BUFS
                read(jj, b).wait()

                @pl.when(jj >= 1)
                def _():
                    drain(jj - 1, other)

                @pl.when(jj + 1 < n_chunks)
                def _():
                    read(jj + 1, other).start()

                for kk in range(TOP_K):
                    scatter(jj, b, kk).start()

        drain(n_chunks - 1, (n_chunks - 1) % SC_BUFS)

    return k(hp, idx)


def _combine_gather(ys, idx):
    n_chunks = idx.shape[1]

    @functools.partial(
        pl.kernel, mesh=_sc_mesh(),
        out_type=jax.ShapeDtypeStruct((SC_WORKERS * n_chunks * SC_ROWS, HALF), U32),
        scratch_types=[pltpu.VMEM((n_chunks, SC_ROWS), I32), pltpu.VMEM((SC_BUFS, SC_ROWS, HALF), U32),
                       pltpu.SemaphoreType.DMA((SC_BUFS,)), pltpu.SemaphoreType.DMA((SC_BUFS,))],
        name="moe_combine",
    )
    def k(ys_hbm, idx_hbm, out_hbm, idx_v, rows_v, gather_sem, write_sem):
        wid = _sc_worker()
        base = wid * n_chunks
        pltpu.sync_copy(idx_hbm.at[wid], idx_v)

        def gather(j, b):
            return pltpu.make_async_copy(ys_hbm.at[idx_v.at[j]], rows_v.at[b], gather_sem.at[b])

        def write(j, b):
            return pltpu.make_async_copy(rows_v.at[b], out_hbm.at[pl.ds((base + j) * SC_ROWS, SC_ROWS)],
                                         write_sem.at[b])

        gather(0, 0).start()

        @pl.loop(0, n_chunks, step=SC_BUFS)
        def _(j):
            for b in range(SC_BUFS):
                jj = j + b
                other = (b + 1) % SC_BUFS
                gather(jj, b).wait()

                @pl.when(jj >= 1)
                def _():
                    write(jj - 1, other).wait()

                @pl.when(jj + 1 < n_chunks)
                def _():
                    gather(jj + 1, other).start()

                write(jj, b).start()

        write(n_chunks - 1, (n_chunks - 1) % SC_BUFS).wait()

    return k(ys, idx)


N_MATS = 3


def _moe_kernel(layer, first_ref, nblk_ref, xs_hbm, wg_hbm, wu_hbm, wd_hbm, bg_ref, bu_ref, bd_ref,
                ys_hbm, wf32, wbf, xbuf, obuf, wsem, xsem, osem):
    e = pl.program_id(0)
    slot = e % 2
    w_hbm = (wg_hbm, wu_hbm, wd_hbm)
    nb = nblk_ref[e]
    row0 = first_ref[e] * MOE_BLOCK
    n_chunks = nb // BLOCKS_PER_CHUNK
    has_a = nb % BLOCKS_PER_CHUNK >= 2
    has_b = nb % 2 == 1
    row_a = row0 + n_chunks * MOE_CHUNK
    row_b = row_a + jnp.where(has_a, TAIL_A, 0)
    slot_a = n_chunks % 2
    slot_b = (n_chunks + has_a.astype(I32)) % 2
    last_chunk = (row_a - MOE_CHUNK, MOE_CHUNK, 1 - slot_a)

    def w_copy(ee, s, m):
        return pltpu.make_async_copy(w_hbm[m].at[layer, ee], wf32.at[s, m], wsem.at[s, m])

    def x_copy(row, rows, s):
        return pltpu.make_async_copy(xs_hbm.at[pl.ds(row, rows)], xbuf.at[s, pl.ds(0, rows)], xsem.at[s])

    def o_copy(row, rows, s):
        return pltpu.make_async_copy(obuf.at[s, pl.ds(0, rows)], ys_hbm.at[pl.ds(row, rows)], osem.at[s])

    def expert(xp):
        lo, hi = _unpack_halves(xp)
        xb = jnp.concatenate([lo.astype(BF16), hi.astype(BF16)], axis=1)
        gt = jnp.minimum(_dot(xb, wbf[0]) + bg_ref[...], SWIGLU_LIMIT)
        up = jnp.clip(_dot(xb, wbf[1]) + bu_ref[...], -SWIGLU_LIMIT, SWIGLU_LIMIT)
        glu = gt * jax.nn.sigmoid(SWIGLU_ALPHA * gt)
        hmid = ((up + 1.0) * glu).astype(BF16)
        out = _dot(hmid, wbf[2]) + bd_ref[...]
        return _pack_halves(out.astype(BF16).astype(F32))

    @pl.when(e == 0)
    def _():
        for m in range(N_MATS):
            w_copy(0, 0, m).start()

    @pl.when(n_chunks > 0)
    def _():
        x_copy(row0, MOE_CHUNK, 0).start()

    @pl.when((n_chunks == 0) & has_a)
    def _():
        x_copy(row0, TAIL_A, 0).start()

    @pl.when((n_chunks == 0) & jnp.logical_not(has_a) & has_b)
    def _():
        x_copy(row0, TAIL_B, 0).start()

    for m in range(N_MATS):
        w_copy(e, slot, m).wait()

    @pl.when(e + 1 < N_EXPERTS)
    def _():
        for m in range(N_MATS):
            w_copy(e + 1, 1 - slot, m).start()

    for m in range(N_MATS):
        wbf[m] = wf32[slot, m].astype(BF16)

    def chunk(c, carry):
        s = c % 2
        row = row0 + c * MOE_CHUNK
        x_copy(row, MOE_CHUNK, s).wait()

        @pl.when(c + 1 < n_chunks)
        def _():
            x_copy(row + MOE_CHUNK, MOE_CHUNK, 1 - s).start()

        @pl.when((c + 1 == n_chunks) & has_a)
        def _():
            x_copy(row_a, TAIL_A, 1 - s).start()

        @pl.when((c + 1 == n_chunks) & jnp.logical_not(has_a) & has_b)
        def _():
            x_copy(row_b, TAIL_B, 1 - s).start()

        obuf[s] = expert(xbuf[s])

        @pl.when(c >= 1)
        def _():
            o_copy(row - MOE_CHUNK, MOE_CHUNK, 1 - s).wait()

        o_copy(row, MOE_CHUNK, s).start()
        return carry

    lax.fori_loop(0, n_chunks, chunk, 0)

    @pl.when(has_a)
    def _():
        x_copy(row_a, TAIL_A, slot_a).wait()

        @pl.when(has_b)
        def _():
            x_copy(row_b, TAIL_B, 1 - slot_a).start()

        obuf[slot_a, 0:TAIL_A, :] = expert(xbuf[slot_a, 0:TAIL_A, :])

        @pl.when(n_chunks >= 1)
        def _():
            o_copy(*last_chunk).wait()

        o_copy(row_a, TAIL_A, slot_a).start()

    @pl.when(has_b)
    def _():
        x_copy(row_b, TAIL_B, slot_b).wait()
        obuf[slot_b, 0:TAIL_B, :] = expert(xbuf[slot_b, 0:TAIL_B, :])

        @pl.when(has_a)
        def _():
            o_copy(row_a, TAIL_A, slot_a).wait()

        @pl.when(jnp.logical_not(has_a) & (n_chunks >= 1))
        def _():
            o_copy(*last_chunk).wait()

        o_copy(row_b, TAIL_B, slot_b).start()
        o_copy(row_b, TAIL_B, slot_b).wait()

    @pl.when(jnp.logical_not(has_b) & has_a)
    def _():
        o_copy(row_a, TAIL_A, slot_a).wait()

    @pl.when(jnp.logical_not(has_b) & jnp.logical_not(has_a) & (n_chunks >= 1))
    def _():
        o_copy(*last_chunk).wait()


def _moe_experts(layer, xs, first_block, n_blocks, w_gate, b_gate, w_up, b_up, w_down, b_down):
    hbm = pl.BlockSpec(memory_space=pl.ANY)
    bspec = pl.BlockSpec((None, None, 1, D_MODEL), lambda e, fb, nb: (layer, e, 0, 0))
    bias = lambda b: b.reshape(DEPTH, N_EXPERTS, 1, D_MODEL)
    return pl.pallas_call(
        functools.partial(_moe_kernel, layer),
        grid_spec=pltpu.PrefetchScalarGridSpec(
            num_scalar_prefetch=2,
            grid=(N_EXPERTS,),
            in_specs=[hbm, hbm, hbm, hbm, bspec, bspec, bspec],
            out_specs=hbm,
            scratch_shapes=[
                pltpu.VMEM((2, N_MATS, D_MODEL, D_MODEL), F32),
                pltpu.VMEM((N_MATS, D_MODEL, D_MODEL), BF16),
                pltpu.VMEM((2, MOE_CHUNK, HALF), U32),
                pltpu.VMEM((2, MOE_CHUNK, HALF), U32),
                pltpu.SemaphoreType.DMA((2, N_MATS)),
                pltpu.SemaphoreType.DMA((2,)),
                pltpu.SemaphoreType.DMA((2,)),
            ],
        ),
        out_shape=jax.ShapeDtypeStruct((N_SLOTS, HALF), U32),
        compiler_params=pltpu.CompilerParams(dimension_semantics=("arbitrary",),
                                             vmem_limit_bytes=VMEM_LIMIT),
        name="moe_experts",
    )(first_block, n_blocks, xs, w_gate, w_up, w_down, bias(b_gate), bias(b_up), bias(b_down))


def _moe(layer, hp, eidx, cnt, tri, w_gate, b_gate, w_up, b_up, w_down, b_down):
    tile_cnt = cnt[:, 0].reshape(N_TILES, N_EXPERTS).astype(I32)
    counts = jnp.sum(tile_cnt, axis=0)
    padded = (counts + MOE_BLOCK - 1) // MOE_BLOCK * MOE_BLOCK
    pad_ends = jnp.cumsum(padded)
    pad_starts = pad_ends - padded
    tile_base = pad_starts[None, :] + jnp.cumsum(tile_cnt, axis=0) - tile_cnt
    tile_base = jnp.broadcast_to(tile_base.astype(F32).reshape(N_TILES * N_EXPERTS, 1),
                                 (N_TILES * N_EXPERTS, TM))
    first_block = (pad_starts // MOE_BLOCK).astype(I32)
    n_blocks = (padded // MOE_BLOCK).astype(I32)

    slots = _slots(eidx, tile_base, tri)[:, :TOP_K, :]
    per_tile = TM // SC_ROWS
    d_idx = slots.reshape(N_TILES, TOP_K, per_tile, SC_ROWS).transpose(0, 2, 1, 3)
    d_idx = d_idx.reshape(SC_WORKERS, N_TOK // SC_WORKERS // SC_ROWS, TOP_K, SC_ROWS)

    xs = _dispatch(hp, d_idx)
    ys = _moe_experts(layer, xs, first_block, n_blocks, w_gate, b_gate, w_up, b_up, w_down, b_down)
    gathered = []
    for tiles in (slots[:CTX_TILES], slots[CTX_TILES:]):
        n_tok = tiles.shape[0] * TM
        c_idx = tiles.transpose(1, 0, 2).reshape(SC_WORKERS, n_tok * TOP_K // SC_WORKERS // SC_ROWS, SC_ROWS)
        gathered.append(_combine_gather(ys, c_idx).reshape(TOP_K, n_tok, HALF))
    return gathered


FINAL_TILES = 2


def _final_kernel(x_ref, *refs):
    o_ref = refs[N_PREV]
    o_ref[...] = _apply_prev(x_ref[...], refs[:N_PREV])


def _final_residual(group, x, prev, mods):
    rows = FINAL_TILES * TM
    first, n_blocks = _group_blocks(group, FINAL_TILES)
    yg, gates = prev
    return pl.pallas_call(
        _final_kernel,
        grid=(n_blocks,),
        in_specs=[pl.BlockSpec((rows, D_MODEL), lambda t: (first + t, 0))]
        + _prev_specs(DEPTH, first, FINAL_TILES),
        out_specs=pl.BlockSpec((rows, D_MODEL), lambda t: (t, 0)),
        out_shape=jax.ShapeDtypeStruct((n_blocks * rows, D_MODEL), F32),
        compiler_params=pltpu.CompilerParams(dimension_semantics=("arbitrary",)),
        name="final_residual",
    )(x, *([yg] * TOP_K), gates, mods)


def _dft_pair(n):
    idx = np.arange(n)
    ang = 2.0 * np.pi * ((idx[:, None] * idx[None, :]) % n) / n
    return np.cos(ang) / np.sqrt(n), np.sin(ang) / np.sqrt(n)


def _rope_tables():
    lane = np.arange(GROUP)
    within = lane % QK_DIM
    axis = within // 32
    e = within % 32
    inv16 = ROPE_BASE ** (-jnp.arange(16, dtype=F32) / 16)
    pos = np.arange(LAT_SEQ)
    coord = np.where(axis[None, :] == 0, (pos // GRID_W)[:, None], (pos % GRID_W)[:, None])
    ang = jnp.asarray(coord, F32) * inv16[e % 16][None, :]
    first = jnp.asarray((e // 16) == 0)[None, :]
    cos, sin = jnp.cos(ang), jnp.sin(ang)
    return cos, jnp.where(first, -sin, 0.0), jnp.where(first, 0.0, sin)


def kernel(x_prompt, x_sample, cache_k, cache_v, c, c_ctx, ada_w, ada_b, norm_mix_g, norm_ffn_g,
           w_in_even, w_out_even, q_norm_g, k_norm_g, lambda_q, lambda_k, subln_g,
           w_in_odd, v_norm_g, w_spatial, b_spatial, w_out_odd,
           router_w, router_b, w_gate, b_gate, w_up, b_up, w_down, b_down):
    n_even = w_in_even.shape[0]
    x_groups = (x_prompt.reshape(N_CTX, D_MODEL), x_sample.reshape(N_LAT, D_MODEL))
    x = None
    cond = jnp.zeros((COND_ROWS, D_MODEL), F32).at[0].set(c_ctx).at[1:1 + LAT_BATCH].set(c)
    mods = _ada_modulation(cond, ada_w, ada_b).reshape(DEPTH, N_MOD, COND_ROWS, 1, D_MODEL)

    dft_ctx = [jnp.asarray(m, F32).astype(BF16) for m in _dft_pair(CTX_SEQ)]
    dft_lat = [jnp.asarray(m, F32).astype(BF16) for m in _dft_pair(LAT_SEQ)]
    c128, s128 = _dft_pair(GROUP)
    cs128 = jnp.asarray(np.concatenate([c128, -s128], axis=0), F32).astype(BF16)
    grp = np.arange(HALF) // QK_DIM
    blockdiag = jnp.asarray(grp[:, None] == grp[None, :], F32).astype(BF16)
    tri = jnp.asarray(np.arange(TM)[:, None] < np.arange(TM)[None, :], F32).astype(BF16)
    rope_tabs = _rope_tables()
    cache_k2 = cache_k.reshape(LAT_BATCH, n_even, PAST_LEN, HALF)
    cache_v2 = cache_v.reshape(LAT_BATCH, n_even, PAST_LEN, HALF)

    rw = jnp.pad(router_w, ((0, 0), (0, 0), (0, LANES - N_EXPERTS)))
    rw_hi = rw.astype(BF16)
    rw_lo = (rw - rw_hi.astype(F32)).astype(BF16)
    rb = jnp.pad(router_b, ((0, 0), (0, LANES - N_EXPERTS)))[:, None, :]

    prev = None
    caches = None
    for l in range(DEPTH):
        j = l // 2
        gmix = norm_mix_g[l][None, :]
        gffn = norm_ffn_g[l][None, :]
        if l % 2 == 0:
            proj_args = (mods, gmix, w_in_even[j].astype(BF16), blockdiag,
                         jnp.tile(q_norm_g[j], HALF // QK_DIM)[None, :],
                         jnp.tile(k_norm_g[j], HALF // QK_DIM)[None, :], rope_tabs)
            outs = None
            for g in range(2):
                x_in = x_groups[g] if l == 0 else x
                outs_g = _even_proj(l, g, x_in, l == 0, None if prev is None else (prev[0][g], prev[1]),
                                    outs, caches, *proj_args)
                if g == 0:
                    caches = outs_g[N_PROJ_OUT:]
                outs = outs_g[:N_PROJ_OUT]
            x, a, q, k, v = outs
            common = (cs128, lambda_q[j], lambda_k[j], subln_g[j][None, :], w_out_even[j].astype(BF16),
                      mods, gffn, rw_hi[l], rw_lo[l], rb[l])
            x, hp, gates, eidx, cnt = _even_mix(l, x, a, q, k, v, cache_k2, cache_v2, dft_ctx, dft_lat, *common)
        else:
            b_s = jnp.broadcast_to(jnp.transpose(b_spatial[j])[:, :, None],
                                   (CHUNK, C_GROUPS, GROUP)).reshape(CHUNK, D_MODEL)
            odd_args = (mods, gmix, w_in_odd[j].astype(BF16), v_norm_g[j][None, :],
                        w_spatial[j].astype(BF16), b_s, w_out_odd[j].astype(BF16), gffn, rw_hi[l], rw_lo[l], rb[l])
            outs = None
            for g in range(2):
                outs = _odd_layer(l, g, x, None if prev is None else (prev[0][g], prev[1]), outs, *odd_args)
            x, hp, gates, eidx, cnt = outs
        yg = _moe(l, hp, eidx, cnt, tri, w_gate, b_gate, w_up, b_up, w_down, b_down)
        prev = (yg, gates)
    y_ctx = _final_residual(0, x, (prev[0][0], prev[1]), mods)
    y_lat = _final_residual(1, x, (prev[0][1], prev[1]), mods)
    return (y_ctx.reshape(CTX_BATCH, CTX_SEQ, D_MODEL),
            y_lat.reshape(LAT_BATCH, LAT_SEQ, D_MODEL),
            caches[0].reshape(CTX_BATCH, N_EVEN, CTX_SEQ, N_GROUPS, 2, QK_DIM),
            caches[1].reshape(CTX_BATCH, N_EVEN, CTX_SEQ, N_GROUPS, GROUP))
```

```python
import functools
import math

import numpy as np
import jax
import jax.numpy as jnp
from jax import lax
from jax.experimental import pallas as pl
from jax.experimental.pallas import tpu as pltpu
from jax.experimental.pallas import tpu_sc as plsc

F32 = jnp.float32
BF16 = jnp.bfloat16
U32 = jnp.uint32
I32 = jnp.int32

D_MODEL = 1024
DEPTH = 4
N_EVEN = (DEPTH + 1) // 2
N_MOD = 6
EPS = 1e-6
CTX_BATCH, CTX_SEQ = 32, 256
LAT_BATCH, LAT_SEQ = 8, 1024
PAST_LEN = 512
GRID_W = 64
N_CTX = CTX_BATCH * CTX_SEQ
N_LAT = LAT_BATCH * LAT_SEQ
N_TOK = N_CTX + N_LAT
TM = 256
N_TILES = N_TOK // TM
CTX_TILES = N_CTX // TM
LAT_TILES_PER_BATCH = LAT_SEQ // TM
COND_ROWS = 16
GROUP = 128
N_GROUPS = 4
QK_DIM = 64
HALF = 512
CHUNK = 128
C_GROUPS = 8
N_EXPERTS = 32
TOP_K = 4
LANES = 128
SUBLANES = 8
SWIGLU_LIMIT = 7.0
SWIGLU_ALPHA = 1.702
MOE_BLOCK = 128
BLOCKS_PER_CHUNK = 4
MOE_CHUNK = BLOCKS_PER_CHUNK * MOE_BLOCK
TAIL_A = 2 * MOE_BLOCK
TAIL_B = MOE_BLOCK
N_ASSIGN = N_TOK * TOP_K
MOE_BLOCKS = N_ASSIGN // MOE_BLOCK + N_EXPERTS
N_SLOTS = MOE_BLOCKS * MOE_BLOCK
ROPE_BASE = 10000.0
VMEM_LIMIT = 52 * 1024 * 1024
SC_CORES = 2
SC_SUBCORES = 16
SC_WORKERS = SC_CORES * SC_SUBCORES
SC_ROWS = 64
SC_BUFS = 2
HI_MASK = 0xFFFF0000
LOG2E = math.log2(math.e)


def _dot(a, b):
    return jnp.dot(a, b, preferred_element_type=F32)


def _rms(x, g):
    return x * lax.rsqrt(jnp.mean(x * x, axis=-1, keepdims=True) + EPS) * g


def _pack_halves(xf):
    b = lax.bitcast_convert_type(xf, U32)
    return (b[:, :HALF] >> 16) | (b[:, HALF:] & jnp.uint32(HI_MASK))


def _unpack_halves(w):
    lo = lax.bitcast_convert_type(w << 16, F32)
    hi = lax.bitcast_convert_type(w & jnp.uint32(HI_MASK), F32)
    return lo, hi


def _tile_mod_row(t):
    return jnp.where(t < CTX_TILES, 0, 1 + (t - CTX_TILES) // LAT_TILES_PER_BATCH)


def _mod_spec(layer, k, row_fn):
    return pl.BlockSpec((None, None, None, 1, D_MODEL),
                        lambda *g: (layer, k, row_fn(*g), 0, 0))


def _full_spec(shape):
    return pl.BlockSpec(shape, lambda *g: (0,) * len(shape))


def _ada_kernel(cond_ref, w_ref, b_ref, o_ref):
    c = cond_ref[...]
    s = (c * jax.nn.sigmoid(c)).astype(BF16)
    o_ref[...] = _dot(s, w_ref[...].astype(BF16)) + b_ref[...]


def _ada_modulation(cond, ada_w, ada_b):
    return pl.pallas_call(
        _ada_kernel,
        grid=(DEPTH, N_MOD),
        in_specs=[
            _full_spec((COND_ROWS, D_MODEL)),
            pl.BlockSpec((None, D_MODEL, D_MODEL), lambda l, n: (l, 0, n)),
            pl.BlockSpec((None, None, 1, D_MODEL), lambda l, n: (l, n, 0, 0)),
        ],
        out_specs=pl.BlockSpec((None, None, COND_ROWS, D_MODEL), lambda l, n: (l, n, 0, 0)),
        out_shape=jax.ShapeDtypeStruct((DEPTH, N_MOD, COND_ROWS, D_MODEL), F32),
        compiler_params=pltpu.CompilerParams(dimension_semantics=("arbitrary", "arbitrary"),
                                             vmem_limit_bytes=VMEM_LIMIT),
        name="ada_modulation",
    )(cond, ada_w, ada_b.reshape(DEPTH, N_MOD, 1, D_MODEL))


N_PREV = 6


def _group_blocks(group, tiles_per_block):
    ctx = CTX_TILES // tiles_per_block
    return (0, ctx) if group == 0 else (ctx, N_TILES // tiles_per_block - ctx)


def _prev_specs(layer, first_block, tiles_per_block):
    rows = tiles_per_block * TM
    ys = [pl.BlockSpec((None, rows, HALF), functools.partial(lambda k, t: (k, t, 0), k))
          for k in range(TOP_K)]
    return ys + [pl.BlockSpec((rows, LANES), lambda t: (first_block + t, 0)),
                 _mod_spec(layer - 1, 5, lambda t: _tile_mod_row((first_block + t) * tiles_per_block))]


def _apply_prev(x, prev_refs):
    y_refs, gt_ref, gate_ref = prev_refs[:TOP_K], prev_refs[TOP_K], prev_refs[TOP_K + 1]
    gt = gt_ref[...]
    acc_lo = acc_hi = None
    for k in range(TOP_K):
        lo, hi = _unpack_halves(y_refs[k][...])
        g = gt[:, k:k + 1]
        acc_lo = g * lo if acc_lo is None else acc_lo + g * lo
        acc_hi = g * hi if acc_hi is None else acc_hi + g * hi
    return x + gate_ref[...] * jnp.concatenate([acc_lo, acc_hi], axis=1)


N_ROUTE_OUT = 5


def _route_out_specs(tile_fn, tiles_per_block=1):
    tile = lambda w: pl.BlockSpec((tiles_per_block * TM, w), lambda *g: (tile_fn(*g), 0))
    return [tile(D_MODEL), tile(HALF), tile(LANES),
            pl.BlockSpec((tiles_per_block * SUBLANES, TM), lambda *g: (tile_fn(*g), 0)),
            pl.BlockSpec((tiles_per_block * N_EXPERTS, LANES), lambda *g: (tile_fn(*g), 0))]


_ROUTE_OUT_SHAPES = [
    jax.ShapeDtypeStruct((N_TOK, D_MODEL), F32),
    jax.ShapeDtypeStruct((N_TOK, HALF), U32),
    jax.ShapeDtypeStruct((N_TOK, LANES), F32),
    jax.ShapeDtypeStruct((N_TILES * SUBLANES, TM), I32),
    jax.ShapeDtypeStruct((N_TILES * N_EXPERTS, LANES), F32),
]


def _post_mixer(x, mix, gate_ref, gffn_ref, shift_ref, scale_ref, rwhi_ref, rwlo_ref, rb_ref,
                xo_ref, hp_ref, gt_ref, ei_ref, cnt_ref):
    xn = x + gate_ref[...] * mix
    xo_ref[...] = xn
    h2 = _rms(xn, gffn_ref[...]) * (1.0 + scale_ref[...]) + shift_ref[...]
    hi = h2.astype(BF16)
    hif = hi.astype(F32)
    lo = (h2 - hif).astype(BF16)
    hp_ref[...] = _pack_halves(hif)
    rwhi = rwhi_ref[...]
    both = _dot(hi, jnp.concatenate([rwhi, rwlo_ref[...]], axis=1))
    logits = both[:, :LANES] + both[:, LANES:] + _dot(lo, rwhi) + rb_ref[...]

    rows = x.shape[0]
    work = logits.T[0:N_EXPERTS, :]
    expert = lax.broadcasted_iota(I32, (N_EXPERTS, rows), 0).astype(F32)
    krow = lax.broadcasted_iota(I32, (SUBLANES, rows), 0)
    member = jnp.zeros((N_EXPERTS, rows), F32)
    gates = jnp.zeros((SUBLANES, rows), F32)
    ids = jnp.zeros((SUBLANES, rows), F32)
    den = None
    top = None
    for k in range(TOP_K):
        m = jnp.max(work, axis=0, keepdims=True)
        idx = jnp.min(jnp.where(work == m, expert, float(N_EXPERTS)), axis=0, keepdims=True)
        onehot = expert == idx
        work = jnp.where(onehot, -jnp.inf, work)
        member = member + onehot.astype(F32)
        if k == 0:
            top = m
        e = jnp.exp(m - top)
        den = e if den is None else den + e
        gates = gates + jnp.where(krow == k, e, 0.0)
        ids = ids + jnp.where(krow == k, idx, 0.0)
    gates = gates * (1.0 / den)
    gt_ref[...] = jnp.concatenate([gates, jnp.zeros((LANES - SUBLANES, rows), F32)], axis=0).T
    ids = ids.astype(I32)
    for i in range(rows // TM):
        ei_ref[SUBLANES * i:SUBLANES * (i + 1), :] = ids[:, TM * i:TM * (i + 1)]
        cnt_ref[N_EXPERTS * i:N_EXPERTS * (i + 1), :] = jnp.broadcast_to(
            jnp.sum(member[:, TM * i:TM * (i + 1)], axis=1, keepdims=True), (N_EXPERTS, LANES))


def _rope(x, c, sa, sb):
    outs = []
    for h in range(N_GROUPS):
        xs = x[:, GROUP * h:GROUP * (h + 1)]
        outs.append(xs * c + pltpu.roll(xs, GROUP - 16, 1) * sa + pltpu.roll(xs, 16, 1) * sb)
    return jnp.concatenate(outs, axis=1)


PROJ_TILES = 2


N_PROJ_OUT = 5


def _even_proj_kernel(group, has_prev, n_aliased, *refs):
    x_ref = refs[0]
    refs = refs[1:]
    if has_prev:
        prev_refs = refs[:N_PREV]
        refs = refs[N_PREV:]
    (g_ref, shift_ref, scale_ref, w_ref, bd_ref, qg_ref, kg_ref, cos_ref, sa_ref, sb_ref) = refs[:10]
    outs = refs[10 + n_aliased:]
    xo_ref, a_ref, q_ref, k_ref, v_ref = outs[:N_PROJ_OUT]

    x = x_ref[...]
    if has_prev:
        x = _apply_prev(x, prev_refs)
    xo_ref[...] = x
    h = (_rms(x, g_ref[...]) * (1.0 + scale_ref[...]) + shift_ref[...]).astype(BF16)

    a_ref[...] = _dot(h, w_ref[:, 0:HALF]).astype(BF16)
    zv = _dot(h, w_ref[:, 3 * HALF:4 * HALF])
    v_ref[...] = zv.astype(BF16)

    bd = bd_ref[...]

    def qk_norm(z, gain):
        ssq = _dot((z * z).astype(BF16), bd)
        return z * lax.rsqrt(ssq * (1.0 / QK_DIM) + EPS) * gain

    qn = qk_norm(_dot(h, w_ref[:, HALF:2 * HALF]), qg_ref[...]) * (QK_DIM ** -0.5 * LOG2E)
    kn = qk_norm(_dot(h, w_ref[:, 2 * HALF:3 * HALF]), kg_ref[...])

    if group == 0:
        kf_ref, vf_ref = outs[N_PROJ_OUT:]
        kf_ref[...] = kn.reshape(kf_ref.shape)
        vf_ref[...] = zv.reshape(vf_ref.shape)
        q_ref[...] = qn.astype(BF16)
        k_ref[...] = kn.astype(BF16)
    else:
        c, sa, sb = cos_ref[...], sa_ref[...], sb_ref[...]
        q_ref[...] = _rope(qn, c, sa, sb).astype(BF16)
        k_ref[...] = _rope(kn, c, sa, sb).astype(BF16)


def _even_proj(layer, group, x, x_is_group_local, prev, ctx_outs, caches, mods, norm_g, w_in, blockdiag,
               qg, kg, rope_tabs):
    has_prev = prev is not None
    rows = PROJ_TILES * TM
    first, n_blocks = _group_blocks(group, PROJ_TILES)
    mod_row = lambda t: _tile_mod_row((first + t) * PROJ_TILES)
    glob = lambda w: pl.BlockSpec((rows, w), lambda t: (first + t, 0))
    local = lambda w: pl.BlockSpec((rows, w), lambda t: (t, 0))
    rope_spec = pl.BlockSpec((rows, GROUP), lambda t: (t % (LAT_SEQ // rows) if group == 1 else 0, 0))
    in_specs = [local(D_MODEL) if x_is_group_local else glob(D_MODEL)]
    args = [x]
    if has_prev:
        yg, gates = prev
        in_specs += _prev_specs(layer, first, PROJ_TILES)
        args += [yg] * TOP_K + [gates, mods]
    in_specs += [
        _full_spec((1, D_MODEL)),
        _mod_spec(layer, 0, mod_row), _mod_spec(layer, 1, mod_row),
        _full_spec((D_MODEL, 4 * HALF)), _full_spec((HALF, HALF)),
        _full_spec((1, HALF)), _full_spec((1, HALF)),
        rope_spec, rope_spec, rope_spec,
    ]
    args += [norm_g, mods, mods, w_in, blockdiag, qg, kg, *rope_tabs]
    out_specs = [glob(D_MODEL)] + [glob(HALF)] * 4
    out_shape = [jax.ShapeDtypeStruct((N_TOK, D_MODEL), F32)] + [jax.ShapeDtypeStruct((N_TOK, HALF), BF16)] * 4
    aliased = []
    if group == 0:
        seqs = rows // CTX_SEQ
        cache_spec = pl.BlockSpec((seqs, None, CTX_SEQ, HALF), lambda t: (t, layer // 2, 0, 0))
        out_specs += [cache_spec] * 2
        out_shape += [jax.ShapeDtypeStruct((CTX_BATCH, N_EVEN, CTX_SEQ, HALF), F32)] * 2
        if caches is not None:
            aliased = [(caches[i], N_PROJ_OUT + i) for i in range(2)]
    else:
        aliased = [(ctx_outs[i], i) for i in range(N_PROJ_OUT)]
    aliases = {len(args) + i: out_idx for i, (_, out_idx) in enumerate(aliased)}
    in_specs += [pl.BlockSpec(memory_space=pl.ANY)] * len(aliased)
    args += [arr for arr, _ in aliased]
    return pl.pallas_call(
        functools.partial(_even_proj_kernel, group, has_prev, len(aliased)),
        grid=(n_blocks,),
        in_specs=in_specs, out_specs=out_specs, out_shape=out_shape,
        input_output_aliases=aliases,
        compiler_params=pltpu.CompilerParams(dimension_semantics=("arbitrary",),
                                             vmem_limit_bytes=VMEM_LIMIT),
        name="even_proj_ctx" if group == 0 else "even_proj_lat",
    )(*args)


MIX_TILES = 2
MIX_ROWS = MIX_TILES * TM


def _even_mix_kernel(lambda_init, *refs):
    (x_ref, q_ref, ac_ref, kc_ref, vc_ref, al_ref, kl_ref, vl_ref, ck_ref, cv_ref,
     cnc_ref, snc_ref, cnl_ref, snl_ref) = refs[:14]
    shared = refs[14:]
    t = pl.program_id(0)
    everything = slice(None)

    @pl.when(t < CTX_TILES // MIX_TILES)
    def _():
        seqs = []
        for s in range(MIX_ROWS // CTX_SEQ):
            rows = slice(CTX_SEQ * s, CTX_SEQ * (s + 1))
            seqs.append((rows, ac_ref, rows, [(kc_ref, vc_ref, rows, False)], cnc_ref, snc_ref))
        _even_mix_body(lambda_init, x_ref, q_ref, seqs, *shared)

    @pl.when(t >= CTX_TILES // MIX_TILES)
    def _():
        seqs = [(everything, al_ref, everything,
                 [(kl_ref, vl_ref, everything, False), (ck_ref, cv_ref, everything, True)], cnl_ref, snl_ref)]
        _even_mix_body(lambda_init, x_ref, q_ref, seqs, *shared)


def _even_mix_body(lambda_init, x_ref, q_ref, seqs, *refs):
    (cs_ref, lq_ref, lk_ref, sg_ref, wo_ref,
     gate_ref, gffn_ref, shift_ref, scale_ref, rwhi_ref, rwlo_ref, rb_ref) = refs[:12]
    out_refs = refs[12:]
    d = jnp.sum(lq_ref[...] * lk_ref[...], axis=1, keepdims=True)
    ed = jnp.exp(d)
    lam = ed[0:1, :] - ed[1:2, :] + lambda_init
    mixed = [_mix_sequence(lambda_init, lam, q_ref, cs_ref, sg_ref, *seq) for seq in seqs]
    mix = _dot(mixed[0] if len(mixed) == 1 else jnp.concatenate(mixed, axis=0), wo_ref[...])
    _post_mixer(x_ref[...], mix, gate_ref, gffn_ref, shift_ref, scale_ref, rwhi_ref, rwlo_ref, rb_ref,
                *out_refs)


def _mix_sequence(lambda_init, lam, q_ref, cs_ref, sg_ref, q_rows, a_ref, a_rows, kv_refs, cn_ref, sn_ref):
    a = a_ref[a_rows, :]
    y1 = _dot(cn_ref[...], a).astype(BF16)
    y2 = _dot(sn_ref[...], a).astype(BF16)
    cs = cs_ref[...]
    pieces = []
    for g in range(N_GROUPS):
        sl = slice(GROUP * g, GROUP * (g + 1))
        pieces.append(_dot(jnp.concatenate([y1[:, sl], y2[:, sl]], axis=1), cs).astype(BF16))

    lane = lax.broadcasted_iota(I32, (y1.shape[0], GROUP), 1)
    nt = (((1,), (1,)), ((), ()))
    for hd in range(N_GROUPS):
        sl = slice(GROUP * hd, GROUP * (hd + 1))
        qf = q_ref[q_rows, sl].astype(F32)
        parts = []
        for k_ref, v_ref, kv_rows, is_f32 in kv_refs:
            kk, vv = k_ref[kv_rows, sl], v_ref[kv_rows, sl]
            parts.append((kk.astype(BF16), vv.astype(BF16)) if is_f32 else (kk, vv))

        def probs(qm):
            ss = [lax.dot_general(qm, kk, nt, preferred_element_type=F32) for kk, _ in parts]
            m = functools.reduce(jnp.maximum, [jnp.max(s, axis=1, keepdims=True) for s in ss])
            es = [jnp.exp2(s - m) for s in ss]
            den = functools.reduce(lambda u, w: u + w, [jnp.sum(e, axis=1, keepdims=True) for e in es])
            return es, 1.0 / den

        es0, inv0 = probs(jnp.where(lane < QK_DIM, qf, 0.0).astype(BF16))
        es1, inv1 = probs(jnp.where(lane >= QK_DIM, qf, 0.0).astype(BF16))
        o0 = functools.reduce(lambda u, w: u + w,
                              [_dot(e.astype(BF16), vv) for e, (_, vv) in zip(es0, parts)])
        o1 = functools.reduce(lambda u, w: u + w,
                              [_dot(e.astype(BF16), vv) for e, (_, vv) in zip(es1, parts)])
        o = o0 * inv0 - o1 * (lam * inv1)
        pieces.append((_rms(o, sg_ref[...]) * (1.0 - lambda_init)).astype(BF16))
    return jnp.concatenate(pieces, axis=1)


def _even_mix(layer, x, a, q, k, v, cache_k, cache_v, dft_ctx, dft_lat, cs128, lam_q, lam_k, subln_g,
              w_out, mods, gffn, rw_hi, rw_lo, rb):
    j = layer // 2
    lambda_init = 0.8 - 0.6 * math.exp(-0.3 * layer)
    ctx_blocks = CTX_TILES // MIX_TILES
    per_seq = LAT_SEQ // MIX_ROWS
    lat = lambda t: jnp.maximum(t - ctx_blocks, 0)
    mod_row = lambda t: _tile_mod_row(t * MIX_TILES)
    tile = lambda w: pl.BlockSpec((MIX_ROWS, w), lambda t: (t, 0))
    ctx_seq = pl.BlockSpec((MIX_ROWS, HALF), lambda t: (jnp.minimum(t, ctx_blocks - 1), 0))
    lat_seq = pl.BlockSpec((LAT_SEQ, HALF), lambda t: (N_CTX // LAT_SEQ + lat(t) // per_seq, 0))
    cache = pl.BlockSpec((None, None, PAST_LEN, HALF), lambda t: (lat(t) // per_seq, j, 0, 0))
    dft_lat_spec = pl.BlockSpec((MIX_ROWS, LAT_SEQ), lambda t: (lat(t) % per_seq, 0))
    in_specs = [
        tile(D_MODEL), tile(HALF), ctx_seq, ctx_seq, ctx_seq, lat_seq, lat_seq, lat_seq, cache, cache,
        _full_spec((CTX_SEQ, CTX_SEQ)), _full_spec((CTX_SEQ, CTX_SEQ)), dft_lat_spec, dft_lat_spec,
        _full_spec((2 * GROUP, GROUP)),
        _full_spec((2, QK_DIM)), _full_spec((2, QK_DIM)), _full_spec((1, GROUP)),
        _full_spec((D_MODEL, D_MODEL)),
        _mod_spec(layer, 2, mod_row), _full_spec((1, D_MODEL)),
        _mod_spec(layer, 3, mod_row), _mod_spec(layer, 4, mod_row),
        _full_spec((D_MODEL, LANES)), _full_spec((D_MODEL, LANES)), _full_spec((1, LANES)),
    ]
    args = [x, q, a, k, v, a, k, v, cache_k, cache_v, dft_ctx[0], dft_ctx[1], dft_lat[0], dft_lat[1],
            cs128, lam_q, lam_k, subln_g, w_out, mods, gffn, mods, mods, rw_hi, rw_lo, rb]
    return pl.pallas_call(
        functools.partial(_even_mix_kernel, lambda_init),
        grid=(N_TILES // MIX_TILES,),
        in_specs=in_specs,
        out_specs=_route_out_specs(lambda t: t, MIX_TILES),
        out_shape=_ROUTE_OUT_SHAPES,
        compiler_params=pltpu.CompilerParams(dimension_semantics=("arbitrary",),
                                             vmem_limit_bytes=VMEM_LIMIT),
        name="even_mix",
    )(*args)


ODD_TILES = 2


def _odd_kernel(group, has_prev, *refs):
    x_ref = refs[0]
    refs = refs[1:]
    if has_prev:
        prev_refs = refs[:N_PREV]
        refs = refs[N_PREV:]
    (g_ref, shift_ref, scale_ref, w_ref, vg_ref, ws_ref, bs_ref, wo_ref,
     gate_ref, gffn_ref, shift2_ref, scale2_ref, rwhi_ref, rwlo_ref, rb_ref) = refs[:15]
    out_refs = refs[15:]
    if group == 1:
        out_refs = out_refs[N_ROUTE_OUT:]

    x = x_ref[...]
    if has_prev:
        x = _apply_prev(x, prev_refs)
    h = (_rms(x, g_ref[...]) * (1.0 + scale_ref[...]) + shift_ref[...]).astype(BF16)

    def gelu(z):
        return 0.5 * z * (1.0 + lax.erf(z * (2.0 ** -0.5)))

    u = gelu(_dot(h, w_ref[:, 0:D_MODEL]))
    vn = _rms(gelu(_dot(h, w_ref[:, D_MODEL:2 * D_MODEL])), vg_ref[...]).astype(BF16)
    bs = bs_ref[...]
    rows = []
    for c in range(x.shape[0] // CHUNK):
        cols = []
        for g in range(C_GROUPS):
            cols.append(_dot(ws_ref[g], vn[CHUNK * c:CHUNK * (c + 1), GROUP * g:GROUP * (g + 1)]))
        rows.append(jnp.concatenate(cols, axis=1) + bs)
    sv = jnp.concatenate(rows, axis=0)
    mix = _dot((u * sv).astype(BF16), wo_ref[...])
    _post_mixer(x, mix, gate_ref, gffn_ref, shift2_ref, scale2_ref, rwhi_ref, rwlo_ref, rb_ref,
                *out_refs)


def _odd_layer(layer, group, x, prev, ctx_outs, mods, norm_g, w_in, v_norm_g, w_s, b_s, w_out, gffn,
               rw_hi, rw_lo, rb):
    has_prev = prev is not None
    first, n_blocks = _group_blocks(group, ODD_TILES)
    mod_row = lambda t: _tile_mod_row((first + t) * ODD_TILES)
    in_specs = [pl.BlockSpec((ODD_TILES * TM, D_MODEL), lambda t: (first + t, 0))]
    args = [x]
    if has_prev:
        yg, gates = prev
        in_specs += _prev_specs(layer, first, ODD_TILES)
        args += [yg] * TOP_K + [gates, mods]
    in_specs += [
        _full_spec((1, D_MODEL)),
        _mod_spec(layer, 0, mod_row), _mod_spec(layer, 1, mod_row),
        _full_spec((D_MODEL, 2 * D_MODEL)), _full_spec((1, D_MODEL)),
        _full_spec((C_GROUPS, CHUNK, CHUNK)), _full_spec((CHUNK, D_MODEL)),
        _full_spec((D_MODEL, D_MODEL)),
        _mod_spec(layer, 2, mod_row), _full_spec((1, D_MODEL)),
        _mod_spec(layer, 3, mod_row), _mod_spec(layer, 4, mod_row),
        _full_spec((D_MODEL, LANES)), _full_spec((D_MODEL, LANES)), _full_spec((1, LANES)),
    ]
    args += [norm_g, mods, mods, w_in, v_norm_g, w_s, b_s, w_out, mods, gffn, mods, mods, rw_hi, rw_lo, rb]
    aliases = {}
    if group == 1:
        aliases = {len(args) + i: i for i in range(N_ROUTE_OUT)}
        in_specs += [pl.BlockSpec(memory_space=pl.ANY)] * N_ROUTE_OUT
        args += list(ctx_outs)
    return pl.pallas_call(
        functools.partial(_odd_kernel, group, has_prev),
        grid=(n_blocks,),
        in_specs=in_specs,
        out_specs=_route_out_specs(lambda t: first + t, ODD_TILES),
        out_shape=_ROUTE_OUT_SHAPES,
        input_output_aliases=aliases,
        compiler_params=pltpu.CompilerParams(dimension_semantics=("arbitrary",),
                                             vmem_limit_bytes=VMEM_LIMIT),
        name="odd_layer_ctx" if group == 0 else "odd_layer_lat",
    )(*args)


def _slot_kernel(ei_ref, base_ref, tri_ref, o_ref):
    expert = lax.broadcasted_iota(I32, (N_EXPERTS, TM), 0)
    krow = lax.broadcasted_iota(I32, (SUBLANES, TM), 0)
    tri = tri_ref[...]
    for i in range(SLOT_TILES):
        ei = ei_ref[SUBLANES * i:SUBLANES * (i + 1), :]
        onehots = [expert == ei[k:k + 1, :] for k in range(TOP_K)]
        member = functools.reduce(lambda u, w: u + w, [o.astype(F32) for o in onehots])
        before = _dot(member.astype(BF16), tri) + base_ref[N_EXPERTS * i:N_EXPERTS * (i + 1), :]
        slots = jnp.zeros((SUBLANES, TM), F32)
        for k in range(TOP_K):
            s = jnp.sum(jnp.where(onehots[k], before, 0.0), axis=0, keepdims=True)
            slots = slots + jnp.where(krow == k, s, 0.0)
        o_ref[i] = slots.astype(I32)


SLOT_TILES = 4


def _slots(eidx, tile_base, tri):
    return pl.pallas_call(
        _slot_kernel,
        grid=(N_TILES // SLOT_TILES,),
        in_specs=[pl.BlockSpec((SLOT_TILES * SUBLANES, TM), lambda t: (t, 0)),
                  pl.BlockSpec((SLOT_TILES * N_EXPERTS, TM), lambda t: (t, 0)),
                  _full_spec((TM, TM))],
        out_specs=pl.BlockSpec((SLOT_TILES, SUBLANES, TM), lambda t: (t, 0, 0)),
        out_shape=jax.ShapeDtypeStruct((N_TILES, SUBLANES, TM), I32),
        compiler_params=pltpu.CompilerParams(dimension_semantics=("arbitrary",)),
        name="moe_slots",
    )(eidx, tile_base, tri)


def _sc_mesh():
    return plsc.VectorSubcoreMesh(core_axis_name="c", subcore_axis_name="s")


def _sc_worker():
    return lax.axis_index("s") * SC_CORES + lax.axis_index("c")


def _dispatch(hp, idx):
    n_chunks = N_TOK // SC_WORKERS // SC_ROWS

    @functools.partial(
        pl.kernel, mesh=_sc_mesh(),
        out_type=jax.ShapeDtypeStruct((N_SLOTS, HALF), U32),
        scratch_types=[pltpu.VMEM((n_chunks, TOP_K, SC_ROWS), I32), pltpu.VMEM((SC_BUFS, SC_ROWS, HALF), U32),
                       pltpu.SemaphoreType.DMA((SC_BUFS,)), pltpu.SemaphoreType.DMA((SC_BUFS,))],
        name="moe_dispatch",
    )
    def k(x_hbm, idx_hbm, out_hbm, idx_v, rows_v, read_sem, scat_sem):
        wid = _sc_worker()
        base = wid * n_chunks
        pltpu.sync_copy(idx_hbm.at[wid], idx_v)

        def read(j, b):
            return pltpu.make_async_copy(x_hbm.at[pl.ds((base + j) * SC_ROWS, SC_ROWS)], rows_v.at[b],
                                         read_sem.at[b])

        def scatter(j, b, kk):
            return pltpu.make_async_copy(rows_v.at[b], out_hbm.at[idx_v.at[j, kk]], scat_sem.at[b])

        def drain(j, b):
            for kk in range(TOP_K):
                scatter(j, b, kk).wait()

        read(0, 0).start()

        @pl.loop(0, n_chunks, step=SC_BUFS)
        def _(j):
            for b in range(SC_BUFS):
                jj = j + b
                other = (b + 1) % SC_BUFS
                read(jj, b).wait()

                @pl.when(jj >= 1)
                def _():
                    drain(jj - 1, other)

                @pl.when(jj + 1 < n_chunks)
                def _():
                    read(jj + 1, other).start()

                for kk in range(TOP_K):
                    scatter(jj, b, kk).start()

        drain(n_chunks - 1, (n_chunks - 1) % SC_BUFS)

    return k(hp, idx)


def _combine_gather(ys, idx):
    n_chunks = idx.shape[1]

    @functools.partial(
        pl.kernel, mesh=_sc_mesh(),
        out_type=jax.ShapeDtypeStruct((SC_WORKERS * n_chunks * SC_ROWS, HALF), U32),
        scratch_types=[pltpu.VMEM((n_chunks, SC_ROWS), I32), pltpu.VMEM((SC_BUFS, SC_ROWS, HALF), U32),
                       pltpu.SemaphoreType.DMA((SC_BUFS,)), pltpu.SemaphoreType.DMA((SC_BUFS,))],
        name="moe_combine",
    )
    def k(ys_hbm, idx_hbm, out_hbm, idx_v, rows_v, gather_sem, write_sem):
        wid = _sc_worker()
        base = wid * n_chunks
        pltpu.sync_copy(idx_hbm.at[wid], idx_v)

        def gather(j, b):
            return pltpu.make_async_copy(ys_hbm.at[idx_v.at[j]], rows_v.at[b], gather_sem.at[b])

        def write(j, b):
            return pltpu.make_async_copy(rows_v.at[b], out_hbm.at[pl.ds((base + j) * SC_ROWS, SC_ROWS)],
                                         write_sem.at[b])

        gather(0, 0).start()

        @pl.loop(0, n_chunks, step=SC_BUFS)
        def _(j):
            for b in range(SC_BUFS):
                jj = j + b
                other = (b + 1) % SC_BUFS
                gather(jj, b).wait()

                @pl.when(jj >= 1)
                def _():
                    write(jj - 1, other).wait()

                @pl.when(jj + 1 < n_chunks)
                def _():
                    gather(jj + 1, other).start()

                write(jj, b).start()

        write(n_chunks - 1, (n_chunks - 1) % SC_BUFS).wait()

    return k(ys, idx)


N_MATS = 3
WEIGHT_DMA_PRIORITY = 1


def _moe_kernel(layer, first_ref, nblk_ref, xs_hbm, wg_hbm, wu_hbm, wd_hbm, bg_ref, bu_ref, bd_ref,
                ys_hbm, wf32, wbf, xbuf, obuf, wsem, xsem, osem):
    e = pl.program_id(0)
    slot = e % 2
    w_hbm = (wg_hbm, wu_hbm, wd_hbm)
    nb = nblk_ref[e]
    row0 = first_ref[e] * MOE_BLOCK
    n_chunks = nb // BLOCKS_PER_CHUNK
    has_a = nb % BLOCKS_PER_CHUNK >= 2
    has_b = nb % 2 == 1
    row_a = row0 + n_chunks * MOE_CHUNK
    row_b = row_a + jnp.where(has_a, TAIL_A, 0)
    slot_a = n_chunks % 2
    slot_b = (n_chunks + has_a.astype(I32)) % 2
    last_chunk = (row_a - MOE_CHUNK, MOE_CHUNK, 1 - slot_a)

    def w_copy(ee, s, m):
        return pltpu.make_async_copy(w_hbm[m].at[layer, ee], wf32.at[s, m], wsem.at[s, m])

    def x_copy(row, rows, s):
        return pltpu.make_async_copy(xs_hbm.at[pl.ds(row, rows)], xbuf.at[s, pl.ds(0, rows)], xsem.at[s])

    def o_copy(row, rows, s):
        return pltpu.make_async_copy(obuf.at[s, pl.ds(0, rows)], ys_hbm.at[pl.ds(row, rows)], osem.at[s])

    def expert(xp):
        lo, hi = _unpack_halves(xp)
        xb = jnp.concatenate([lo.astype(BF16), hi.astype(BF16)], axis=1)
        gt = jnp.minimum(_dot(xb, wbf[0]) + bg_ref[...], SWIGLU_LIMIT)
        up = jnp.clip(_dot(xb, wbf[1]) + bu_ref[...], -SWIGLU_LIMIT, SWIGLU_LIMIT)
        glu = gt * jax.nn.sigmoid(SWIGLU_ALPHA * gt)
        hmid = ((up + 1.0) * glu).astype(BF16)
        out = _dot(hmid, wbf[2]) + bd_ref[...]
        return _pack_halves(out.astype(BF16).astype(F32))

    @pl.when(e == 0)
    def _():
        for m in range(N_MATS):
            w_copy(0, 0, m).start(priority=WEIGHT_DMA_PRIORITY)

    @pl.when(n_chunks > 0)
    def _():
        x_copy(row0, MOE_CHUNK, 0).start()

    @pl.when((n_chunks == 0) & has_a)
    def _():
        x_copy(row0, TAIL_A, 0).start()

    @pl.when((n_chunks == 0) & jnp.logical_not(has_a) & has_b)
    def _():
        x_copy(row0, TAIL_B, 0).start()

    for m in range(N_MATS):
        w_copy(e, slot, m).wait()

    @pl.when(e + 1 < N_EXPERTS)
    def _():
        for m in range(N_MATS):
            w_copy(e + 1, 1 - slot, m).start(priority=WEIGHT_DMA_PRIORITY)

    for m in range(N_MATS):
        wbf[m] = wf32[slot, m].astype(BF16)

    def chunk(c, carry):
        s = c % 2
        row = row0 + c * MOE_CHUNK
        x_copy(row, MOE_CHUNK, s).wait()

        @pl.when(c + 1 < n_chunks)
        def _():
            x_copy(row + MOE_CHUNK, MOE_CHUNK, 1 - s).start()

        @pl.when((c + 1 == n_chunks) & has_a)
        def _():
            x_copy(row_a, TAIL_A, 1 - s).start()

        @pl.when((c + 1 == n_chunks) & jnp.logical_not(has_a) & has_b)
        def _():
            x_copy(row_b, TAIL_B, 1 - s).start()

        obuf[s] = expert(xbuf[s])

        @pl.when(c >= 1)
        def _():
            o_copy(row - MOE_CHUNK, MOE_CHUNK, 1 - s).wait()

        o_copy(row, MOE_CHUNK, s).start()
        return carry

    lax.fori_loop(0, n_chunks, chunk, 0)

    @pl.when(has_a)
    def _():
        x_copy(row_a, TAIL_A, slot_a).wait()

        @pl.when(has_b)
        def _():
            x_copy(row_b, TAIL_B, 1 - slot_a).start()

        obuf[slot_a, 0:TAIL_A, :] = expert(xbuf[slot_a, 0:TAIL_A, :])

        @pl.when(n_chunks >= 1)
        def _():
            o_copy(*last_chunk).wait()

        o_copy(row_a, TAIL_A, slot_a).start()

    @pl.when(has_b)
    def _():
        x_copy(row_b, TAIL_B, slot_b).wait()
        obuf[slot_b, 0:TAIL_B, :] = expert(xbuf[slot_b, 0:TAIL_B, :])

        @pl.when(has_a)
        def _():
            o_copy(row_a, TAIL_A, slot_a).wait()

        @pl.when(jnp.logical_not(has_a) & (n_chunks >= 1))
        def _():
            o_copy(*last_chunk).wait()

        o_copy(row_b, TAIL_B, slot_b).start()
        o_copy(row_b, TAIL_B, slot_b).wait()

    @pl.when(jnp.logical_not(has_b) & has_a)
    def _():
        o_copy(row_a, TAIL_A, slot_a).wait()

    @pl.when(jnp.logical_not(has_b) & jnp.logical_not(has_a) & (n_chunks >= 1))
    def _():
        o_copy(*last_chunk).wait()


def _moe_experts(layer, xs, first_block, n_blocks, w_gate, b_gate, w_up, b_up, w_down, b_down):
    hbm = pl.BlockSpec(memory_space=pl.ANY)
    bspec = pl.BlockSpec((None, None, 1, D_MODEL), lambda e, fb, nb: (layer, e, 0, 0))
    bias = lambda b: b.reshape(DEPTH, N_EXPERTS, 1, D_MODEL)
    return pl.pallas_call(
        functools.partial(_moe_kernel, layer),
        grid_spec=pltpu.PrefetchScalarGridSpec(
            num_scalar_prefetch=2,
            grid=(N_EXPERTS,),
            in_specs=[hbm, hbm, hbm, hbm, bspec, bspec, bspec],
            out_specs=hbm,
            scratch_shapes=[
                pltpu.VMEM((2, N_MATS, D_MODEL, D_MODEL), F32),
                pltpu.VMEM((N_MATS, D_MODEL, D_MODEL), BF16),
                pltpu.VMEM((2, MOE_CHUNK, HALF), U32),
                pltpu.VMEM((2, MOE_CHUNK, HALF), U32),
                pltpu.SemaphoreType.DMA((2, N_MATS)),
                pltpu.SemaphoreType.DMA((2,)),
                pltpu.SemaphoreType.DMA((2,)),
            ],
        ),
        out_shape=jax.ShapeDtypeStruct((N_SLOTS, HALF), U32),
        compiler_params=pltpu.CompilerParams(dimension_semantics=("arbitrary",),
                                             vmem_limit_bytes=VMEM_LIMIT),
        name="moe_experts",
    )(first_block, n_blocks, xs, w_gate, w_up, w_down, bias(b_gate), bias(b_up), bias(b_down))


def _moe(layer, hp, eidx, cnt, tri, w_gate, b_gate, w_up, b_up, w_down, b_down):
    tile_cnt = cnt[:, 0].reshape(N_TILES, N_EXPERTS).astype(I32)
    counts = jnp.sum(tile_cnt, axis=0)
    padded = (counts + MOE_BLOCK - 1) // MOE_BLOCK * MOE_BLOCK
    pad_ends = jnp.cumsum(padded)
    pad_starts = pad_ends - padded
    tile_base = pad_starts[None, :] + jnp.cumsum(tile_cnt, axis=0) - tile_cnt
    tile_base = jnp.broadcast_to(tile_base.astype(F32).reshape(N_TILES * N_EXPERTS, 1),
                                 (N_TILES * N_EXPERTS, TM))
    first_block = (pad_starts // MOE_BLOCK).astype(I32)
    n_blocks = (padded // MOE_BLOCK).astype(I32)

    slots = _slots(eidx, tile_base, tri)[:, :TOP_K, :]
    per_tile = TM // SC_ROWS
    d_idx = slots.reshape(N_TILES, TOP_K, per_tile, SC_ROWS).transpose(0, 2, 1, 3)
    d_idx = d_idx.reshape(SC_WORKERS, N_TOK // SC_WORKERS // SC_ROWS, TOP_K, SC_ROWS)

    xs = _dispatch(hp, d_idx)
    ys = _moe_experts(layer, xs, first_block, n_blocks, w_gate, b_gate, w_up, b_up, w_down, b_down)
    gathered = []
    for tiles in (slots[:CTX_TILES], slots[CTX_TILES:]):
        n_tok = tiles.shape[0] * TM
        c_idx = tiles.transpose(1, 0, 2).reshape(SC_WORKERS, n_tok * TOP_K // SC_WORKERS // SC_ROWS, SC_ROWS)
        gathered.append(_combine_gather(ys, c_idx).reshape(TOP_K, n_tok, HALF))
    return gathered


FINAL_TILES = 2


def _final_kernel(x_ref, *refs):
    o_ref = refs[N_PREV]
    o_ref[...] = _apply_prev(x_ref[...], refs[:N_PREV])


def _final_residual(group, x, prev, mods):
    rows = FINAL_TILES * TM
    first, n_blocks = _group_blocks(group, FINAL_TILES)
    yg, gates = prev
    return pl.pallas_call(
        _final_kernel,
        grid=(n_blocks,),
        in_specs=[pl.BlockSpec((rows, D_MODEL), lambda t: (first + t, 0))]
        + _prev_specs(DEPTH, first, FINAL_TILES),
        out_specs=pl.BlockSpec((rows, D_MODEL), lambda t: (t, 0)),
        out_shape=jax.ShapeDtypeStruct((n_blocks * rows, D_MODEL), F32),
        compiler_params=pltpu.CompilerParams(dimension_semantics=("arbitrary",)),
        name="final_residual",
    )(x, *([yg] * TOP_K), gates, mods)


def _dft_pair(n):
    idx = np.arange(n)
    ang = 2.0 * np.pi * ((idx[:, None] * idx[None, :]) % n) / n
    return np.cos(ang) / np.sqrt(n), np.sin(ang) / np.sqrt(n)


def _rope_tables():
    lane = np.arange(GROUP)
    within = lane % QK_DIM
    axis = within // 32
    e = within % 32
    inv16 = ROPE_BASE ** (-jnp.arange(16, dtype=F32) / 16)
    pos = np.arange(LAT_SEQ)
    coord = np.where(axis[None, :] == 0, (pos // GRID_W)[:, None], (pos % GRID_W)[:, None])
    ang = jnp.asarray(coord, F32) * inv16[e % 16][None, :]
    first = jnp.asarray((e // 16) == 0)[None, :]
    cos, sin = jnp.cos(ang), jnp.sin(ang)
    return cos, jnp.where(first, -sin, 0.0), jnp.where(first, 0.0, sin)


def kernel(x_prompt, x_sample, cache_k, cache_v, c, c_ctx, ada_w, ada_b, norm_mix_g, norm_ffn_g,
           w_in_even, w_out_even, q_norm_g, k_norm_g, lambda_q, lambda_k, subln_g,
           w_in_odd, v_norm_g, w_spatial, b_spatial, w_out_odd,
           router_w, router_b, w_gate, b_gate, w_up, b_up, w_down, b_down):
    n_even = w_in_even.shape[0]
    x_groups = (x_prompt.reshape(N_CTX, D_MODEL), x_sample.reshape(N_LAT, D_MODEL))
    x = None
    cond = jnp.zeros((COND_ROWS, D_MODEL), F32).at[0].set(c_ctx).at[1:1 + LAT_BATCH].set(c)
    mods = _ada_modulation(cond, ada_w, ada_b).reshape(DEPTH, N_MOD, COND_ROWS, 1, D_MODEL)

    dft_ctx = [jnp.asarray(m, F32).astype(BF16) for m in _dft_pair(CTX_SEQ)]
    dft_lat = [jnp.asarray(m, F32).astype(BF16) for m in _dft_pair(LAT_SEQ)]
    c128, s128 = _dft_pair(GROUP)
    cs128 = jnp.asarray(np.concatenate([c128, -s128], axis=0), F32).astype(BF16)
    grp = np.arange(HALF) // QK_DIM
    blockdiag = jnp.asarray(grp[:, None] == grp[None, :], F32).astype(BF16)
    tri = jnp.asarray(np.arange(TM)[:, None] < np.arange(TM)[None, :], F32).astype(BF16)
    rope_tabs = _rope_tables()
    cache_k2 = cache_k.reshape(LAT_BATCH, n_even, PAST_LEN, HALF)
    cache_v2 = cache_v.reshape(LAT_BATCH, n_even, PAST_LEN, HALF)

    rw = jnp.pad(router_w, ((0, 0), (0, 0), (0, LANES - N_EXPERTS)))
    rw_hi = rw.astype(BF16)
    rw_lo = (rw - rw_hi.astype(F32)).astype(BF16)
    rb = jnp.pad(router_b, ((0, 0), (0, LANES - N_EXPERTS)))[:, None, :]

    prev = None
    caches = None
    for l in range(DEPTH):
        j = l // 2
        gmix = norm_mix_g[l][None, :]
        gffn = norm_ffn_g[l][None, :]
        if l % 2 == 0:
            proj_args = (mods, gmix, w_in_even[j].astype(BF16), blockdiag,
                         jnp.tile(q_norm_g[j], HALF // QK_DIM)[None, :],
                         jnp.tile(k_norm_g[j], HALF // QK_DIM)[None, :], rope_tabs)
            outs = None
            for g in range(2):
                x_in = x_groups[g] if l == 0 else x
                outs_g = _even_proj(l, g, x_in, l == 0, None if prev is None else (prev[0][g], prev[1]),
                                    outs, caches, *proj_args)
                if g == 0:
                    caches = outs_g[N_PROJ_OUT:]
                outs = outs_g[:N_PROJ_OUT]
            x, a, q, k, v = outs
            common = (cs128, lambda_q[j], lambda_k[j], subln_g[j][None, :], w_out_even[j].astype(BF16),
                      mods, gffn, rw_hi[l], rw_lo[l], rb[l])
            x, hp, gates, eidx, cnt = _even_mix(l, x, a, q, k, v, cache_k2, cache_v2, dft_ctx, dft_lat, *common)
        else:
            b_s = jnp.broadcast_to(jnp.transpose(b_spatial[j])[:, :, None],
                                   (CHUNK, C_GROUPS, GROUP)).reshape(CHUNK, D_MODEL)
            odd_args = (mods, gmix, w_in_odd[j].astype(BF16), v_norm_g[j][None, :],
                        w_spatial[j].astype(BF16), b_s, w_out_odd[j].astype(BF16), gffn, rw_hi[l], rw_lo[l], rb[l])
            outs = None
            for g in range(2):
                outs = _odd_layer(l, g, x, None if prev is None else (prev[0][g], prev[1]), outs, *odd_args)
            x, hp, gates, eidx, cnt = outs
        yg = _moe(l, hp, eidx, cnt, tri, w_gate, b_gate, w_up, b_up, w_down, b_down)
        prev = (yg, gates)
    y_ctx = _final_residual(0, x, (prev[0][0], prev[1]), mods)
    y_lat = _final_residual(1, x, (prev[0][1], prev[1]), mods)
    return (y_ctx.reshape(CTX_BATCH, CTX_SEQ, D_MODEL),
            y_lat.reshape(LAT_BATCH, LAT_SEQ, D_MODEL),
            caches[0].reshape(CTX_BATCH, N_EVEN, CTX_SEQ, N_GROUPS, 2, QK_DIM),
            caches[1].reshape(CTX_BATCH, N_EVEN, CTX_SEQ, N_GROUPS, GROUP))
```
